```python
import math
import jax
import jax.numpy as jnp
from jax import lax
import numpy as np

D_MODEL = 1024
BATCH = 32
SEQ = 2048
DEPTH = 2

GRID_W = 64
CTX_LEN = 256
EPS = 1e-6
ROPE_THETA = 10000.0
BLOCK_Q = 128
RET_CHUNK = 128
N_GROUPS = 4
GROUP_W = D_MODEL // N_GROUPS
MIX_W = N_GROUPS * GROUP_W

HY_W = GROUP_W
HY_ORDER = 2
HY_SHORT = 3
HY_EMB = 33
HY_BANDS = (HY_EMB - 1) // 2
HY_FFN = 64
HY_NFILT = HY_ORDER * 2 * HY_W
HY_DECAY_SHIFT = 0.05
HY_FAST_PCT = 0.3
HY_SLOW_PCT = 1.5
HY_TARGET = 1e-2
HY_COLS = (HY_ORDER + 1) * HY_W

MLA_HEADS = 4
MLA_Q_RANK = GROUP_W
MLA_KV_RANK = GROUP_W // 2
MLA_NOPE = 64
MLA_ROPE = 32
MLA_V = GROUP_W // MLA_HEADS
MLA_COLS = MLA_Q_RANK + MLA_KV_RANK + MLA_ROPE
MLA_SCALE = (MLA_NOPE + MLA_ROPE) ** -0.5

RET_HEADS = 4
RET_DK = GROUP_W // RET_HEADS
RET_DV = GROUP_W // RET_HEADS
RET_COLS = 2 * RET_HEADS * RET_DK + RET_HEADS * RET_DV + GROUP_W

GQA_HEADS = 4
GQA_KV_HEADS = 2
GQA_HD = GROUP_W // GQA_HEADS
GQA_COLS = (GQA_HEADS + 2 * GQA_KV_HEADS) * GQA_HD
GQA_SCALE = GQA_HD ** -0.5

IN_COLS = HY_COLS + MLA_COLS + RET_COLS + GQA_COLS
D_FF = ((8 * D_MODEL + 3 * 256 - 1) // (3 * 256)) * 256
F32 = jnp.float32

kernel_name = 'hymba_style_hybrid_dit_block'


def rms_norm(x, g):
    xf = x.astype(F32)
    y = xf * lax.rsqrt(jnp.mean(xf * xf, axis=-1, keepdims=True) + EPS)
    return (y * g.astype(F32)).astype(x.dtype)


def modulated_norm(x, g, shift, scale):
    return rms_norm(x, g) * (1.0 + scale) + shift


def ada_mod(cond, w, b):
    return jnp.split(jax.nn.silu(cond) @ w + b, 6, axis=-1)


def axial_rope_tables(rows, cols, dim):
    quarter = dim // 4
    inv = ROPE_THETA ** (-jnp.arange(quarter, dtype=F32) / quarter)
    ang = jnp.concatenate([rows.astype(F32)[:, None] * inv, cols.astype(F32)[:, None] * inv], axis=-1)
    return jnp.cos(ang), jnp.sin(ang)


def apply_rope(x, cos, sin):
    xf = x.astype(F32)
    x1, x2 = jnp.split(xf, 2, axis=-1)
    c = cos[None, :, None, :]
    s = sin[None, :, None, :]
    return jnp.concatenate([x1 * c - x2 * s, x2 * c + x1 * s], axis=-1).astype(x.dtype)


def block_attention(q, k, v, scale):
    b, lq, h, dk = q.shape
    g = k.shape[2]
    r = h // g
    dv = v.shape[-1]
    nb = lq // BLOCK_Q
    qb = q.reshape(b, nb, BLOCK_Q, g, r, dk).transpose(1, 0, 2, 3, 4, 5)

    def one_block(qblk):
        s = jnp.einsum('bqgrd,bkgd->bgrqk', qblk, k, preferred_element_type=F32) * scale
        p = jax.nn.softmax(s, axis=-1)
        return jnp.einsum('bgrqk,bkge->bqgre', p.astype(v.dtype), v)

    o = lax.map(one_block, qb)
    return o.transpose(1, 0, 2, 3, 4, 5).reshape(b, lq, h, dv)


def split_projection(p):
    sizes = (HY_COLS, MLA_Q_RANK, MLA_KV_RANK, MLA_ROPE, RET_COLS, GQA_COLS)
    offs = np.cumsum((0,) + sizes)
    return tuple(p[..., int(offs[i]):int(offs[i + 1])] for i in range(len(sizes)))


def short_conv(u, w, b):
    ch = u.shape[-1]
    y = lax.conv_general_dilated(u, w.astype(u.dtype)[:, None, :], window_strides=(1,),
                                 padding=((HY_SHORT // 2, HY_SHORT // 2),),
                                 dimension_numbers=('NWC', 'WIO', 'NWC'), feature_group_count=ch)
    return y + b.astype(u.dtype)


def hyena_filters(n_tok, w1, b1, w2, b2, w3, b3, freq, w4):
    t = jnp.linspace(0.0, 1.0, n_tok, dtype=F32)[:, None]
    wpos = 2.0 * math.pi * jnp.arange(n_tok, dtype=F32)[:, None] / n_tok
    fb = jnp.linspace(1e-4, HY_BANDS - 1, HY_BANDS, dtype=F32)[None, :]
    z = jnp.concatenate([t, jnp.cos(fb * wpos), -jnp.sin(fb * wpos)], axis=-1)
    fr = freq.astype(F32)
    hdn = jnp.sin(fr * (z @ w1.astype(F32) + b1.astype(F32)))
    hdn = jnp.sin(fr * (hdn @ w2.astype(F32) + b2.astype(F32)))
    hdn = jnp.sin(fr * (hdn @ w3.astype(F32) + b3.astype(F32)))
    filt = hdn @ w4.astype(F32)
    deltas = jnp.linspace(math.log(HY_TARGET) / HY_FAST_PCT, math.log(HY_TARGET) / HY_SLOW_PCT, HY_NFILT, dtype=F32)
    filt = filt * (jnp.exp(-t * jnp.abs(deltas)) + HY_DECAY_SHIFT)
    filt = filt.reshape(n_tok, HY_ORDER, 2, HY_W)
    k_full = jnp.concatenate([filt[:, :, 0], jnp.zeros((1, HY_ORDER, HY_W), F32), filt[:0:-1, :, 1]], axis=0)
    return k_full / jnp.sum(jnp.abs(k_full), axis=0, keepdims=True)


def fft_long_conv(u, kf, d):
    n_tok = u.shape[1]
    y = jnp.fft.irfft(jnp.fft.rfft(u, n=2 * n_tok, axis=1) * kf[None], n=2 * n_tok, axis=1)[:, :n_tok]
    return y + u * d.astype(F32)


def hyena_mixer(u, conv_w, conv_b, w1, b1, w2, b2, w3, b3, freq, w4, bias):
    n_tok = u.shape[1]
    uc = short_conv(u, conv_w, conv_b).astype(F32)
    v, x1, x2 = jnp.split(uc, 3, axis=-1)
    kf = jnp.fft.rfft(hyena_filters(n_tok, w1, b1, w2, b2, w3, b3, freq, w4), axis=0)
    zz = x1 * fft_long_conv(v, kf[:, 0], bias[0])
    y = x2 * fft_long_conv(zz, kf[:, 1], bias[1])
    return y.astype(u.dtype)


def mla_qkv(cq, ckv, kr, gq, wuq, gkv, wukv, rope):
    b, n_tok, _ = cq.shape
    q = (rms_norm(cq, gq) @ wuq).reshape(b, n_tok, MLA_HEADS, MLA_NOPE + MLA_ROPE)
    kv = (rms_norm(ckv, gkv) @ wukv).reshape(b, n_tok, MLA_HEADS, MLA_NOPE + MLA_V)
    q_nope, q_pe = q[..., :MLA_NOPE], q[..., MLA_NOPE:]
    k_nope, v = kv[..., :MLA_NOPE], kv[..., MLA_NOPE:]
    k_pe = kr[:, :, None, :]
    if rope is not None:
        q_pe = apply_rope(q_pe, *rope)
        k_pe = apply_rope(k_pe, *rope)
    q = jnp.concatenate([q_nope, q_pe], axis=-1)
    k = jnp.concatenate([k_nope, jnp.broadcast_to(k_pe, (b, n_tok, MLA_HEADS, MLA_ROPE))], axis=-1)
    return q, k, v


def retention_qkvg(u, rope):
    b, n_tok, _ = u.shape
    qk = RET_HEADS * RET_DK
    q = u[..., :qk].reshape(b, n_tok, RET_HEADS, RET_DK)
    k = u[..., qk:2 * qk].reshape(b, n_tok, RET_HEADS, RET_DK) * (RET_DK ** -0.5)
    v = u[..., 2 * qk:2 * qk + RET_HEADS * RET_DV].reshape(b, n_tok, RET_HEADS, RET_DV)
    g = u[..., 2 * qk + RET_HEADS * RET_DV:]
    if rope is not None:
        q = apply_rope(q, *rope)
        k = apply_rope(k, *rope)
    return q, k, v, g


def retention_chunked(q, k, v, log_g, state0):
    b, n_tok, h, _ = q.shape
    dv = v.shape[-1]
    n_chunks = n_tok // RET_CHUNK
    idx = jnp.arange(RET_CHUNK, dtype=F32)
    diff = idx[:, None] - idx[None, :]
    inner = jnp.where(diff >= 0, jnp.exp(log_g[:, None, None] * jnp.maximum(diff, 0.0)), 0.0)
    q_dec = jnp.exp((idx[:, None] + 1.0) * log_g[None, :])[None, :, :, None]
    k_dec = jnp.exp((RET_CHUNK - 1.0 - idx[:, None]) * log_g[None, :])[None, :, :, None]
    c_dec = jnp.exp(RET_CHUNK * log_g)[None, :, None, None]

    def chunks(t):
        return t.astype(F32).reshape(b, n_chunks, RET_CHUNK, h, t.shape[-1]).transpose(1, 0, 2, 3, 4)

    def step(state, qkv):
        qc, kc, vc = qkv
        s = jnp.einsum('bihd,bjhd->bhij', qc, kc) * inner
        o = jnp.einsum('bhij,bjhe->bihe', s, vc) + jnp.einsum('bihd,bhde->bihe', qc, state) * q_dec
        state = state * c_dec + jnp.einsum('bjhd,bjhe->bhde', kc * k_dec, vc)
        return state, o

    state, o = lax.scan(step, state0, (chunks(q), chunks(k), chunks(v)))
    return o.transpose(1, 0, 2, 3, 4).reshape(b, n_tok, h, dv), state


def retention_final_state(k, v, log_g):
    n_tok = k.shape[1]
    w = jnp.exp((n_tok - 1.0 - jnp.arange(n_tok, dtype=F32))[:, None] * log_g[None, :])
    return jnp.einsum('bjhd,bjhe,jh->bhde', k.astype(F32), v.astype(F32), w)


def bidir_retention(q, k, v, log_g, s_f0, s_b0):
    o_f, s_f = retention_chunked(q, k, v, log_g[0], s_f0)
    o_b, s_b = retention_chunked(q[:, ::-1], k[:, ::-1], v[:, ::-1], log_g[1], s_b0)
    return o_f + o_b[:, ::-1], s_f, s_b


def retention_output(o, g, norm_g):
    b, n_tok = o.shape[0], o.shape[1]
    o = rms_norm(o, norm_g.reshape(RET_HEADS, RET_DV)).reshape(b, n_tok, GROUP_W)
    return (o * jax.nn.silu(g.astype(F32))).astype(g.dtype)


def gqa_qkv(u, gq, gk, rope):
    b, n_tok, _ = u.shape
    nq = GQA_HEADS * GQA_HD
    nk = GQA_KV_HEADS * GQA_HD
    q = rms_norm(u[..., :nq].reshape(b, n_tok, GQA_HEADS, GQA_HD), gq)
    k = rms_norm(u[..., nq:nq + nk].reshape(b, n_tok, GQA_KV_HEADS, GQA_HD), gk)
    v = u[..., nq + nk:].reshape(b, n_tok, GQA_KV_HEADS, GQA_HD)
    if rope is not None:
        q = apply_rope(q, *rope)
        k = apply_rope(k, *rope)
    return q, k, v


def merge_groups(ya, yb, yc, yd, g_out, w_out):
    b, n_tok = ya.shape[0], ya.shape[1]
    y = jnp.stack([ya, yb.astype(ya.dtype), yc.astype(ya.dtype), yd.astype(ya.dtype)], axis=-2)
    y = rms_norm(y, g_out.reshape(N_GROUPS, GROUP_W)).reshape(b, n_tok, MIX_W)
    return y @ w_out


def swiglu(h, w_in, w_out):
    a, g = jnp.split(h @ w_in, 2, axis=-1)
    return (jax.nn.silu(a) * g) @ w_out


def setup_inputs(seed: int = 0) -> dict:
    key = jax.random.key(seed)
    keys = iter(jax.random.split(key, 48))

    def nrm(shape, scale):
        return jax.random.normal(next(keys), shape, F32) * scale

    def gain(shape):
        return 1.0 + nrm(shape, 0.02)

    ret_logit = jnp.asarray(np.log(2.0 ** (5.0 + np.arange(RET_HEADS)) - 1.0), F32)
    return {
        'x': nrm((BATCH, SEQ, D_MODEL), 1.0),
        'c': nrm((BATCH, D_MODEL), 1.0),
        'ctx': nrm((BATCH, CTX_LEN, D_MODEL), 1.0),
        'c_ctx': nrm((D_MODEL,), 1.0),
        'w_mod': nrm((DEPTH, D_MODEL, 6 * D_MODEL), 0.5 * D_MODEL ** -0.5),
        'b_mod': nrm((DEPTH, 6 * D_MODEL), 0.01),
        'norm_attn_g': gain((DEPTH, D_MODEL)),
        'norm_ffn_g': gain((DEPTH, D_MODEL)),
        'w_in': nrm((DEPTH, D_MODEL, IN_COLS), D_MODEL ** -0.5),
        'hy_conv_w': nrm((DEPTH, HY_SHORT, HY_COLS), HY_SHORT ** -0.5),
        'hy_conv_b': nrm((DEPTH, HY_COLS), 0.01),
        'hy_w1': nrm((DEPTH, HY_EMB, HY_FFN), HY_EMB ** -0.5),
        'hy_b1': nrm((DEPTH, HY_FFN), 0.1),
        'hy_w2': nrm((DEPTH, HY_FFN, HY_FFN), HY_FFN ** -0.5),
        'hy_b2': nrm((DEPTH, HY_FFN), 0.1),
        'hy_w3': nrm((DEPTH, HY_FFN, HY_FFN), HY_FFN ** -0.5),
        'hy_b3': nrm((DEPTH, HY_FFN), 0.1),
        'hy_freq': 1.0 + nrm((DEPTH, HY_FFN), 0.1),
        'hy_w4': nrm((DEPTH, HY_FFN, HY_NFILT), HY_FFN ** -0.5),
        'hy_bias': nrm((DEPTH, HY_ORDER, HY_W), 0.1),
        'mla_q_norm_g': gain((DEPTH, MLA_Q_RANK)),
        'mla_w_uq': nrm((DEPTH, MLA_Q_RANK, MLA_HEADS * (MLA_NOPE + MLA_ROPE)), MLA_Q_RANK ** -0.5),
        'mla_kv_norm_g': gain((DEPTH, MLA_KV_RANK)),
        'mla_w_ukv': nrm((DEPTH, MLA_KV_RANK, MLA_HEADS * (MLA_NOPE + MLA_V)), MLA_KV_RANK ** -0.5),
        'ret_decay': ret_logit + nrm((DEPTH, 2, RET_HEADS), 0.05),
        'ret_norm_g': gain((DEPTH, RET_HEADS * RET_DV)),
        'gqa_q_norm_g': gain((DEPTH, GQA_HD)),
        'gqa_k_norm_g': gain((DEPTH, GQA_HD)),
        'out_norm_g': gain((DEPTH, MIX_W)),
        'w_out': nrm((DEPTH, MIX_W, D_MODEL), MIX_W ** -0.5),
        'w_ffn_in': nrm((DEPTH, D_MODEL, 2 * D_FF), D_MODEL ** -0.5),
        'w_ffn_out': nrm((DEPTH, D_FF, D_MODEL), D_FF ** -0.5),
        'final_norm_g': gain((D_MODEL,)),
    }


def reference(x, c, ctx, c_ctx, w_mod, b_mod, norm_attn_g, norm_ffn_g, w_in, hy_conv_w, hy_conv_b,
              hy_w1, hy_b1, hy_w2, hy_b2, hy_w3, hy_b3, hy_freq, hy_w4, hy_bias,
              mla_q_norm_g, mla_w_uq, mla_kv_norm_g, mla_w_ukv, ret_decay, ret_norm_g,
              gqa_q_norm_g, gqa_k_norm_g, out_norm_g, w_out, w_ffn_in, w_ffn_out, final_norm_g):
    b, n_lat = x.shape[0], x.shape[1]
    ROWS = n_lat // GRID_W
    rows = jnp.repeat(jnp.arange(ROWS, dtype=jnp.int32), GRID_W)
    cols = jnp.tile(jnp.arange(GRID_W, dtype=jnp.int32), ROWS)
    rope_mla = axial_rope_tables(rows, cols, MLA_ROPE)
    rope_ret = axial_rope_tables(rows, cols, RET_DK)
    rope_gqa = axial_rope_tables(rows, cols, GQA_HD)
    zero_state = jnp.zeros((b, RET_HEADS, RET_DK, RET_DV), F32)

    xl, xc = x, ctx
    for l in range(DEPTH):
        update_ctx = l < DEPTH - 1
        sh_a, sc_a, g_a, sh_f, sc_f, g_f = ada_mod(c[:, None, :], w_mod[l], b_mod[l])
        csh_a, csc_a, cg_a, csh_f, csc_f, cg_f = ada_mod(c_ctx[None, None, :], w_mod[l], b_mod[l])
        pl = modulated_norm(xl, norm_attn_g[l], sh_a, sc_a) @ w_in[l]
        pc = modulated_norm(xc, norm_attn_g[l], csh_a, csc_a) @ w_in[l]
        hy_l, cq_l, ckv_l, kr_l, ret_in_l, gqa_in_l = split_projection(pl)
        hy_c, cq_c, ckv_c, kr_c, ret_in_c, gqa_in_c = split_projection(pc)

        hy_args = (hy_conv_w[l], hy_conv_b[l], hy_w1[l], hy_b1[l], hy_w2[l], hy_b2[l],
                   hy_w3[l], hy_b3[l], hy_freq[l], hy_w4[l], hy_bias[l])
        ya_l = hyena_mixer(hy_l, *hy_args)

        mla_args = (mla_q_norm_g[l], mla_w_uq[l], mla_kv_norm_g[l], mla_w_ukv[l])
        q_l, k_l, v_l = mla_qkv(cq_l, ckv_l, kr_l, *mla_args, rope_mla)
        q_c, k_c, v_c = mla_qkv(cq_c, ckv_c, kr_c, *mla_args, None)
        yb_l = block_attention(q_l, jnp.concatenate([k_c, k_l], axis=1), jnp.concatenate([v_c, v_l], axis=1),
                               MLA_SCALE).reshape(b, n_lat, GROUP_W)

        log_g = jax.nn.log_sigmoid(ret_decay[l].astype(F32))
        rq_l, rk_l, rv_l, rg_l = retention_qkvg(ret_in_l, rope_ret)
        rq_c, rk_c, rv_c, rg_c = retention_qkvg(ret_in_c, None)
        if update_ctx:
            ro_c, s_f, s_b = bidir_retention(rq_c, rk_c, rv_c, log_g, zero_state, zero_state)
        else:
            s_f = retention_final_state(rk_c, rv_c, log_g[0])
            s_b = retention_final_state(rk_c[:, ::-1], rv_c[:, ::-1], log_g[1])
        ro_l, _, _ = bidir_retention(rq_l, rk_l, rv_l, log_g, s_f, s_b)
        yc_l = retention_output(ro_l, rg_l, ret_norm_g[l])

        gq_l, gk_l, gv_l = gqa_qkv(gqa_in_l, gqa_q_norm_g[l], gqa_k_norm_g[l], rope_gqa)
        gq_c, gk_c, gv_c = gqa_qkv(gqa_in_c, gqa_q_norm_g[l], gqa_k_norm_g[l], None)
        yd_l = block_attention(gq_l, jnp.concatenate([gk_c, gk_l], axis=1), jnp.concatenate([gv_c, gv_l], axis=1),
                               GQA_SCALE).reshape(b, n_lat, GROUP_W)

        xl = xl + g_a * merge_groups(ya_l, yb_l, yc_l, yd_l, out_norm_g[l], w_out[l])
        xl = xl + g_f * swiglu(modulated_norm(xl, norm_ffn_g[l], sh_f, sc_f), w_ffn_in[l], w_ffn_out[l])

        if update_ctx:
            n_ctx = xc.shape[1]
            ya_c = hyena_mixer(hy_c, *hy_args)
            yb_c = block_attention(q_c, k_c, v_c, MLA_SCALE).reshape(b, n_ctx, GROUP_W)
            yc_c = retention_output(ro_c, rg_c, ret_norm_g[l])
            yd_c = block_attention(gq_c, gk_c, gv_c, GQA_SCALE).reshape(b, n_ctx, GROUP_W)
            xc = xc + cg_a * merge_groups(ya_c, yb_c, yc_c, yd_c, out_norm_g[l], w_out[l])
            xc = xc + cg_f * swiglu(modulated_norm(xc, norm_ffn_g[l], csh_f, csc_f), w_ffn_in[l], w_ffn_out[l])

    return rms_norm(xl, final_norm_g)
```

```python
import functools
import math

import jax
import jax.numpy as jnp
from jax import lax
from jax.experimental import pallas as pl
from jax.experimental.pallas import tpu as pltpu

F32 = jnp.float32
BF16 = jnp.bfloat16
HI = lax.Precision.HIGHEST

D_MODEL = 1024
GRID_W = 64
EPS = 1e-6
ROPE_THETA = 10000.0
GROUP_W = 256
LANES = 128

HY_EMB = 33
HY_BANDS = 16
HY_FFN = 64
HY_NFILT = 1024
HY_DECAY_SHIFT = 0.05
HY_FAST_PCT = 0.3
HY_SLOW_PCT = 1.5
HY_TARGET = 1e-2

MLA_NOPE = 64
MLA_ROPE = 32
MLA_V = 64
MLA_KV_RANK = 128
MLA_SCALE = (MLA_NOPE + MLA_ROPE) ** -0.5

HEAD_D = 64
RET_CHUNK = 128
GQA_SCALE = HEAD_D ** -0.5
RET_K_SCALE = HEAD_D ** -0.5

D_FF = 2816
FF_TILE = 256

PROJ_SPLITS = (768, 256, 256, 1024, 512)

VMEM_LIMIT = 56 * 1024 * 1024


def _params(*sem):
    return pltpu.CompilerParams(dimension_semantics=sem, vmem_limit_bytes=VMEM_LIMIT)


def _rms(x):
    return x * lax.rsqrt(jnp.mean(x * x, axis=-1, keepdims=True) + EPS)


def _rope(x, cf, s1, s2, half):
    w = x.shape[-1]
    return x * cf + pltpu.roll(x, w - half, 1) * s1 + pltpu.roll(x, half, 1) * s2


def _seg_mean(sq, seg):
    hi = sq.astype(BF16)
    lo = (sq - hi.astype(F32)).astype(BF16)
    return (jnp.dot(hi, seg, preferred_element_type=F32) + jnp.dot(lo, seg, preferred_element_type=F32))


def _log_sigmoid(x):
    return -(jnp.maximum(-x, 0.0) + jnp.log(1.0 + jnp.exp(-jnp.abs(x))))


def _ada_kernel(c_ref, w_ref, b_ref, o_ref):
    c = c_ref[...]
    s = c * jax.nn.sigmoid(c)
    o_ref[...] = jnp.dot(s, w_ref[...], precision=HI, preferred_element_type=F32) + b_ref[...]


def _ada_mod(cond, w_mod, b_mod):
    depth, d, n = w_mod.shape
    r = cond.shape[0]
    tn = 1024
    return pl.pallas_call(
        _ada_kernel,
        grid=(depth, n // tn),
        in_specs=[pl.BlockSpec((r, d), lambda l, j: (0, 0)),
                  pl.BlockSpec((None, d, tn), lambda l, j: (l, 0, j)),
                  pl.BlockSpec((None, 1, tn), lambda l, j: (l, 0, j))],
        out_specs=pl.BlockSpec((None, r, tn), lambda l, j: (l, 0, j)),
        out_shape=jax.ShapeDtypeStruct((depth, r, n), F32),
        compiler_params=_params("arbitrary", "arbitrary"),
        name="ada_mod",
    )(cond, w_mod, b_mod.reshape(depth, 1, n))


def _proj_kernel(x_ref, g_ref, sh_ref, sc_ref, w_ref, *out_refs):
    y = _rms(x_ref[...]) * g_ref[...]
    h = (y * (1.0 + sc_ref[...]) + sh_ref[...]).astype(BF16)
    off = 0
    for o_ref in out_refs:
        wd = o_ref.shape[-1]
        o_ref[...] = jnp.dot(h, w_ref[:, off:off + wd], preferred_element_type=F32)
        off += wd


def _in_proj(x, g, sh, sc, w):
    b, l, d = x.shape
    tm = min(512, l)
    tok = lambda wd: pl.BlockSpec((None, tm, wd), lambda i, j: (i, j, 0))
    vec = pl.BlockSpec((None, 1, d), lambda i, j: (i, 0, 0))
    return pl.pallas_call(
        _proj_kernel,
        grid=(b, l // tm),
        in_specs=[tok(d), pl.BlockSpec((1, d), lambda i, j: (0, 0)), vec, vec,
                  pl.BlockSpec(w.shape, lambda i, j: (0, 0))],
        out_specs=[tok(wd) for wd in PROJ_SPLITS],
        out_shape=[jax.ShapeDtypeStruct((b, l, wd), F32) for wd in PROJ_SPLITS],
        compiler_params=_params("parallel", "parallel"),
        name="in_proj",
    )(x, g, sh, sc, w)


def _hy_filter_kernel(z_ref, t_ref, w1_ref, b1_ref, w2_ref, b2_ref, w3_ref, b3_ref, fr_ref,
                      w4_ref, dl_ref, o_ref):
    fr = fr_ref[...]
    dot = functools.partial(jnp.dot, precision=HI, preferred_element_type=F32)
    h = jnp.sin(fr * (dot(z_ref[...], w1_ref[...]) + b1_ref[...]))
    h = jnp.sin(fr * (dot(h, w2_ref[...]) + b2_ref[...]))
    h = jnp.sin(fr * (dot(h, w3_ref[...]) + b3_ref[...]))
    filt = dot(h, w4_ref[...]) * (jnp.exp(-t_ref[...] * dl_ref[...]) + HY_DECAY_SHIFT)
    fwd = filt[:, :GROUP_W]
    bwd = filt[:, GROUP_W:]
    row = lax.broadcasted_iota(jnp.int32, bwd.shape, 0)
    bwd = jnp.where(row == 0, 0.0, bwd)
    nrm = (jnp.sum(jnp.abs(fwd), axis=0, keepdims=True) + jnp.sum(jnp.abs(bwd), axis=0, keepdims=True))
    o_ref[:, :GROUP_W] = fwd / nrm
    o_ref[:, GROUP_W:] = bwd / nrm


def _hy_filters(l, z, t, w1, b1, w2, b2, w3, b3, fr, w4, dl):
    full = lambda a: pl.BlockSpec(a.shape, lambda o: (0, 0))
    return pl.pallas_call(
        _hy_filter_kernel,
        grid=(2,),
        in_specs=[full(z), full(t), full(w1), full(b1), full(w2), full(b2), full(w3), full(b3), full(fr),
                  pl.BlockSpec((HY_FFN, 2 * GROUP_W), lambda o: (0, o)),
                  pl.BlockSpec((1, 2 * GROUP_W), lambda o: (0, o))],
        out_specs=pl.BlockSpec((l, 2 * GROUP_W), lambda o: (0, o)),
        out_shape=jax.ShapeDtypeStruct((l, HY_NFILT), F32),
        compiler_params=_params("arbitrary"),
        name="hy_filters",
    )(z, t, w1, b1, w2, b2, w3, b3, fr, w4, dl)


def _hy_spec_kernel(c_ref, s_ref, h_ref, sign_ref, a_ref, b_ref, d_ref):
    dot = functools.partial(jnp.dot, precision=HI, preferred_element_type=F32)
    row = lax.broadcasted_iota(jnp.int32, (c_ref.shape[0], GROUP_W), 0)
    first = jnp.logical_and(pl.program_id(0) == 0, row == 0)
    for o in range(2):
        hf = h_ref[:, 2 * o * GROUP_W:(2 * o + 1) * GROUP_W]
        hb = h_ref[:, (2 * o + 1) * GROUP_W:(2 * o + 2) * GROUP_W]
        hs = hf + hb
        re = dot(c_ref[...], hs)
        im = dot(s_ref[...], hf - hb)
        ny = jnp.sum(sign_ref[...] * hs, axis=0, keepdims=True)
        cols = slice(o * GROUP_W, (o + 1) * GROUP_W)
        a_ref[:, cols] = re
        b_ref[:, cols] = jnp.where(first, 0.0, im)
        d_ref[:, cols] = jnp.where(first, ny, re)


def _hy_spectrum(h, c32, s32, sign):
    l = h.shape[0]
    fb = min(256, l)
    out = jax.ShapeDtypeStruct((l, 2 * GROUP_W), F32)
    blk = pl.BlockSpec((fb, 2 * GROUP_W), lambda f: (f, 0))
    return pl.pallas_call(
        _hy_spec_kernel,
        grid=(l // fb,),
        in_specs=[pl.BlockSpec((fb, l), lambda f: (f, 0)), pl.BlockSpec((fb, l), lambda f: (f, 0)),
                  pl.BlockSpec(h.shape, lambda f: (0, 0)), pl.BlockSpec((l, 1), lambda f: (0, 0))],
        out_specs=[blk, blk, blk],
        out_shape=[out, out, out],
        compiler_params=_params("arbitrary"),
        name="hy_spectrum",
    )(c32, s32, h, sign)


def _short_conv(u, w, b):
    n = u.shape[0]
    row = lax.broadcasted_iota(jnp.int32, u.shape, 0)
    up = jnp.where(row == 0, 0.0, pltpu.roll(u, 1, 0))
    dn = jnp.where(row == n - 1, 0.0, pltpu.roll(u, n - 1, 0))
    return w[0:1] * up + w[1:2] * u + w[2:3] * dn + b


def _hy_conv_kernel(sig_ref, gate_ref, cw_ref, cb_ref, a_ref, b_ref, d_ref, bias_ref,
                    c_ref, s_ref, ci_ref, si_ref, o_ref, u_scr, ub_scr, acc_scr,
                    *, nb, sig_col, gate_col):
    f = pl.program_id(1)
    w = GROUP_W

    @pl.when(f == 0)
    def _():
        for i in range(nb):
            u = sig_ref[i]
            if sig_col is not None:
                u = _short_conv(u, cw_ref[:, sig_col * w:(sig_col + 1) * w], cb_ref[:, sig_col * w:(sig_col + 1) * w])
            u_scr[:, i * w:(i + 1) * w] = u
            ub_scr[:, i * w:(i + 1) * w] = u.astype(BF16)
        acc_scr[...] = jnp.zeros_like(acc_scr)

    ub = ub_scr[...]
    xr = jnp.dot(c_ref[...], ub, preferred_element_type=F32)
    xi = jnp.dot(s_ref[...], ub, preferred_element_type=F32)
    ka, kb, kd = a_ref[...], b_ref[...], d_ref[...]
    yr, yi = [], []
    for i in range(nb):
        r = xr[:, i * w:(i + 1) * w]
        m = xi[:, i * w:(i + 1) * w]
        yr.append((r * ka - m * kb).astype(BF16))
        yi.append((r * kb + m * kd).astype(BF16))
    yr = jnp.concatenate(yr, axis=1) if nb > 1 else yr[0]
    yi = jnp.concatenate(yi, axis=1) if nb > 1 else yi[0]
    acc_scr[...] += (jnp.dot(ci_ref[...], yr, preferred_element_type=F32)
                     + jnp.dot(si_ref[...], yi, preferred_element_type=F32))

    @pl.when(f == pl.num_programs(1) - 1)
    def _():
        for i in range(nb):
            y = acc_scr[:, i * w:(i + 1) * w] + u_scr[:, i * w:(i + 1) * w] * bias_ref[...]
            g = _short_conv(gate_ref[i], cw_ref[:, gate_col * w:(gate_col + 1) * w],
                            cb_ref[:, gate_col * w:(gate_col + 1) * w])
            o_ref[i] = g * y


def _hy_conv(sig, sig_col, hy, gate_col, cw, cb, spec, order, bias, dft):
    b, l, _ = hy.shape
    w = GROUP_W
    nb = 2 if l >= 2048 else math.gcd(b, 8)
    fb = min(256, l)
    c16, s16, ci16, si16 = dft
    a, bm, d = spec
    sig_arr = hy if sig_col is not None else sig
    sig_blk = sig_col if sig_col is not None else 0
    spec_blk = pl.BlockSpec((fb, w), lambda i, f: (f, order))
    kernel = functools.partial(_hy_conv_kernel, nb=nb, sig_col=sig_col, gate_col=gate_col)
    return pl.pallas_call(
        kernel,
        grid=(b // nb, l // fb),
        in_specs=[pl.BlockSpec((nb, l, w), lambda i, f: (i, 0, sig_blk)),
                  pl.BlockSpec((nb, l, w), lambda i, f: (i, 0, gate_col)),
                  pl.BlockSpec(cw.shape, lambda i, f: (0, 0)),
                  pl.BlockSpec(cb.shape, lambda i, f: (0, 0)),
                  spec_blk, spec_blk, spec_blk,
                  pl.BlockSpec((1, w), lambda i, f: (0, 0)),
                  pl.BlockSpec((fb, l), lambda i, f: (f, 0)),
                  pl.BlockSpec((fb, l), lambda i, f: (f, 0)),
                  pl.BlockSpec((l, fb), lambda i, f: (0, f)),
                  pl.BlockSpec((l, fb), lambda i, f: (0, f))],
        out_specs=pl.BlockSpec((nb, l, w), lambda i, f: (i, 0, 0)),
        out_shape=jax.ShapeDtypeStruct((b, l, w), F32),
        scratch_shapes=[pltpu.VMEM((l, nb * w), F32), pltpu.VMEM((l, nb * w), BF16),
                        pltpu.VMEM((l, nb * w), F32)],
        compiler_params=_params("parallel", "arbitrary"),
        name="hy_conv",
    )(sig_arr, hy, cw, cb, a, bm, d, bias, c16, s16, ci16, si16)


def _dft_tables(l):
    n = 2 * l
    k = jnp.arange(l, dtype=jnp.int32)
    ang = ((k[:, None] * k[None, :]) % n).astype(F32) * (2.0 * math.pi / n)
    sign = jnp.where(k % 2 == 0, 1.0, -1.0).astype(F32)
    c = jnp.cos(ang)
    s = -jnp.sin(ang)
    s = s.at[0, :].set(sign)
    wk = jnp.where(k == 0, 1.0, 2.0).astype(F32) / n
    ci = c.T * wk[None, :]
    si = s.T * wk[None, :]
    return (c, s, sign[:, None]), (c.astype(BF16), s.astype(BF16), ci.astype(BF16), si.astype(BF16))


def _hy_features(l):
    t = jnp.linspace(0.0, 1.0, l, dtype=F32)[:, None]
    wpos = 2.0 * math.pi * jnp.arange(l, dtype=F32)[:, None] / l
    fb = jnp.linspace(1e-4, HY_BANDS - 1, HY_BANDS, dtype=F32)[None, :]
    z = jnp.concatenate([t, jnp.cos(fb * wpos), -jnp.sin(fb * wpos)], axis=-1)
    z = jnp.pad(z, ((0, 0), (0, HY_FFN - HY_EMB)))
    deltas = jnp.linspace(math.log(HY_TARGET) / HY_FAST_PCT, math.log(HY_TARGET) / HY_SLOW_PCT, HY_NFILT, dtype=F32)
    return z, t, jnp.abs(deltas)[None, :]


def _hyena(hy, feats, dft32, dft16, cw, cb, w1, b1, w2, b2, w3, b3, fr, w4, bias):
    l = hy.shape[1]
    z, t, dl = feats
    w1p = jnp.pad(w1, ((0, HY_FFN - HY_EMB), (0, 0)))
    h = _hy_filters(l, z, t, w1p, b1[None], w2, b2[None], w3, b3[None], fr[None], w4, dl)
    spec = _hy_spectrum(h, *dft32)
    cb2 = cb[None]
    zz = _hy_conv(None, 0, hy, 1, cw, cb2, spec, 0, bias[0:1], dft16)
    return _hy_conv(zz, None, hy, 2, cw, cb2, spec, 1, bias[1:2], dft16)


def _rope_tables(l, dim, width, off, reps):
    rows = jnp.repeat(jnp.arange(l // GRID_W, dtype=jnp.int32), GRID_W).astype(F32)
    cols = jnp.tile(jnp.arange(GRID_W, dtype=jnp.int32), l // GRID_W).astype(F32)
    quarter = dim // 4
    half = dim // 2
    inv = ROPE_THETA ** (-jnp.arange(quarter, dtype=F32) / quarter)
    ang = jnp.concatenate([rows[:, None] * inv, cols[:, None] * inv], axis=-1)
    c, s = jnp.cos(ang), jnp.sin(ang)
    cf = jnp.ones((l, width), F32).at[:, off:off + dim].set(jnp.concatenate([c, c], axis=-1))
    s1 = jnp.zeros((l, width), F32).at[:, off:off + half].set(-s)
    s2 = jnp.zeros((l, width), F32).at[:, off + half:off + dim].set(s)
    return tuple(jnp.tile(a, (1, reps)) for a in (cf, s1, s2))


def _mla_prep_kernel(cq_ref, ckvr_ref, gq_ref, gkv_ref, wuq_ref, wk_ref, wv_ref, *rest, rope):
    if rope:
        cf_ref, s1_ref, s2_ref, q_ref, k_ref, v_ref = rest
        cf, s1, s2 = cf_ref[...], s1_ref[...], s2_ref[...]
    else:
        q_ref, k_ref, v_ref = rest
    half = MLA_ROPE // 2
    qn = (_rms(cq_ref[...]) * gq_ref[...]).astype(BF16)
    q = jnp.dot(qn, wuq_ref[...], preferred_element_type=F32)
    x = ckvr_ref[...]
    kvn = (_rms(x[:, :MLA_KV_RANK]) * gkv_ref[...]).astype(BF16)
    k = jnp.dot(kvn, wk_ref[...], preferred_element_type=F32)
    v = jnp.dot(kvn, wv_ref[...], preferred_element_type=F32)
    kpe = pltpu.roll(x[:, MLA_KV_RANK:], MLA_NOPE, 1)
    if rope:
        kpe = _rope(kpe, cf, s1, s2, half)
    for h in range(q_ref.shape[0]):
        cols = slice(h * LANES, (h + 1) * LANES)
        qh = q[:, cols]
        if rope:
            qh = _rope(qh, cf, s1, s2, half)
        q_ref[h] = (qh * MLA_SCALE).astype(BF16)
        k_ref[h] = (k[:, cols] + kpe).astype(BF16)
        v_ref[h] = v[:, cols].astype(BF16)


def _mla_prep(cq, ckvr, gq, gkv, wuq, wk, wv, tables):
    b, l, _ = cq.shape
    tm = min(512, l)
    rope = tables is not None
    tok = lambda wd: pl.BlockSpec((None, tm, wd), lambda i, j: (i, j, 0))
    full = lambda a: pl.BlockSpec(a.shape, lambda i, j: (0, 0))
    head = pl.BlockSpec((None, 4, tm, LANES), lambda i, j: (i, 0, j, 0))
    args = [cq, ckvr, gq, gkv, wuq, wk, wv]
    specs = [tok(256), tok(256), full(gq), full(gkv), full(wuq), full(wk), full(wv)]
    if rope:
        args += list(tables)
        specs += [pl.BlockSpec((tm, LANES), lambda i, j: (j, 0))] * 3
    hshape = jax.ShapeDtypeStruct((b, 4, l, LANES), BF16)
    return pl.pallas_call(
        functools.partial(_mla_prep_kernel, rope=rope),
        grid=(b, l // tm),
        in_specs=specs,
        out_specs=[head, head, head],
        out_shape=[hshape, hshape, hshape],
        compiler_params=_params("parallel", "parallel"),
        name="mla_prep",
    )(*args)


def _gqa_prep_kernel(x_ref, gq_ref, gk_ref, seg_ref, *rest, rope):
    if rope:
        cf_ref, s1_ref, s2_ref, q_ref, k_ref, v_ref = rest
        cf, s1, s2 = cf_ref[...], s1_ref[...], s2_ref[...]
    else:
        q_ref, k_ref, v_ref = rest
    half = HEAD_D // 2
    seg = seg_ref[...]
    tm = x_ref.shape[0]
    lo = lax.broadcasted_iota(jnp.int32, (tm, LANES), 1) < HEAD_D

    def normed(a, g):
        a = a * lax.rsqrt(_seg_mean(a * a, seg) + EPS) * g
        return _rope(a, cf, s1, s2, half) if rope else a

    def dup(a, g):
        a = jnp.where(lo if g == 0 else jnp.logical_not(lo), a, 0.0)
        return (a + pltpu.roll(a, HEAD_D, 1)).astype(BF16)

    for p in range(2):
        qb = normed(x_ref[:, p * LANES:(p + 1) * LANES], gq_ref[...]) * GQA_SCALE
        q_ref[2 * p] = jnp.where(lo, qb, 0.0).astype(BF16)
        q_ref[2 * p + 1] = jnp.where(lo, 0.0, qb).astype(BF16)
    kb = normed(x_ref[:, 2 * LANES:3 * LANES], gk_ref[...])
    vb = x_ref[:, 3 * LANES:4 * LANES]
    for g in range(2):
        k_ref[g] = dup(kb, g)
        v_ref[g] = dup(vb, g)


def _gqa_prep(x, gq, gk, seg, tables):
    b, l, _ = x.shape
    tm = min(512, l)
    rope = tables is not None
    full = lambda a: pl.BlockSpec(a.shape, lambda i, j: (0, 0))
    args = [x, gq, gk, seg]
    specs = [pl.BlockSpec((None, tm, 512), lambda i, j: (i, j, 0)), full(gq), full(gk), full(seg)]
    if rope:
        args += list(tables)
        specs += [pl.BlockSpec((tm, LANES), lambda i, j: (j, 0))] * 3
    return pl.pallas_call(
        functools.partial(_gqa_prep_kernel, rope=rope),
        grid=(b, l // tm),
        in_specs=specs,
        out_specs=[pl.BlockSpec((None, 4, tm, LANES), lambda i, j: (i, 0, j, 0)),
                   pl.BlockSpec((None, 2, tm, LANES), lambda i, j: (i, 0, j, 0)),
                   pl.BlockSpec((None, 2, tm, LANES), lambda i, j: (i, 0, j, 0))],
        out_shape=[jax.ShapeDtypeStruct((b, 4, l, LANES), BF16),
                   jax.ShapeDtypeStruct((b, 2, l, LANES), BF16),
                   jax.ShapeDtypeStruct((b, 2, l, LANES), BF16)],
        compiler_params=_params("parallel", "parallel"),
        name="gqa_prep",
    )(*args)


def _attn_kernel(q_ref, *rest, rep):
    o_ref = rest[-1]
    kv = rest[:-1]
    ks, vs = kv[0::2], kv[1::2]
    tq = q_ref.shape[1]
    lo = lax.broadcasted_iota(jnp.int32, (tq, LANES), 1) < HEAD_D
    nt = (((1,), (1,)), ((), ()))
    for p in range(2):
        outs = []
        for r in range(2):
            h = 2 * p + r
            g = h // rep
            q = q_ref[h]
            ss = [lax.dot_general(q, k_ref[g], nt, preferred_element_type=F32) for k_ref in ks]
            m = ss[0].max(axis=-1, keepdims=True)
            for s in ss[1:]:
                m = jnp.maximum(m, s.max(axis=-1, keepdims=True))
            den = 0.0
            acc = 0.0
            for s, v_ref in zip(ss, vs):
                e = jnp.exp(s - m)
                den = den + e.sum(axis=-1, keepdims=True)
                acc = acc + jnp.dot(e.astype(BF16), v_ref[g], preferred_element_type=F32)
            outs.append(acc / den)
        o_ref[:, p * LANES:(p + 1) * LANES] = jnp.where(lo, outs[0], outs[1])


def _attention(q, kvs, rep):
    b, nh, l, _ = q.shape
    tq = min(256, l)
    specs = [pl.BlockSpec((None, nh, tq, LANES), lambda i, j: (i, 0, j, 0))]
    args = [q]
    for k, v in kvs:
        spec = pl.BlockSpec((None,) + k.shape[1:], lambda i, j: (i, 0, 0, 0))
        specs += [spec, spec]
        args += [k, v]
    return pl.pallas_call(
        functools.partial(_attn_kernel, rep=rep),
        grid=(b, l // tq),
        in_specs=specs,
        out_specs=pl.BlockSpec((None, tq, GROUP_W), lambda i, j: (i, j, 0)),
        out_shape=jax.ShapeDtypeStruct((b, l, GROUP_W), F32),
        compiler_params=_params("parallel", "parallel"),
        name="attention",
    )(*args)


def _ret_kernel(q_ref, k_ref, v_ref, g_ref, dl_ref, dc_ref, ng_ref, seg_ref, sf0_ref, sb0_ref, *rest, rope, chunk):
    if rope:
        cf_ref, s1_ref, s2_ref, y_ref, sf_ref, sb_ref, qs, ks, sball = rest
    else:
        y_ref, sf_ref, sb_ref, qs, ks, sball = rest
    c_ = chunk
    w = GROUP_W
    n_chunks = q_ref.shape[0] // c_
    half = HEAD_D // 2
    shift_c = int(math.log2(c_))
    shift_h = int(math.log2(HEAD_D))
    tn = (((0,), (0,)), ((), ()))
    nt = (((1,), (1,)), ((), ()))

    lgf = _log_sigmoid(dl_ref[0:1, :])
    lgb = _log_sigmoid(dl_ref[1:2, :])
    ii = lax.broadcasted_iota(jnp.int32, (c_, 1), 0).astype(F32)
    qdf = jnp.exp((ii + 1.0) * lgf)
    kdf = jnp.exp((c_ - 1.0 - ii) * lgf)
    cdf = jnp.exp(float(c_) * lgf)
    qdb = jnp.exp((c_ - ii) * lgb)
    kdb = jnp.exp(ii * lgb)
    cdb = jnp.exp(float(c_) * lgb)
    r_h = lax.shift_right_logical(lax.broadcasted_iota(jnp.int32, (w, w), 0), shift_h)
    c_h = lax.shift_right_logical(lax.broadcasted_iota(jnp.int32, (w, w), 1), shift_h)
    blockdiag = r_h == c_h
    lgc = _log_sigmoid(dc_ref[...])
    i4 = jnp.bitwise_and(lax.broadcasted_iota(jnp.int32, (4 * c_, c_), 0), c_ - 1)
    j4 = lax.broadcasted_iota(jnp.int32, (4 * c_, c_), 1)
    diff = (i4 - j4).astype(F32)
    dmat = (jnp.where(diff >= 0, jnp.exp(lgc[:, 0:1] * jnp.maximum(diff, 0.0)), 0.0)
            + jnp.where(diff <= 0, jnp.exp(lgc[:, 1:2] * jnp.maximum(-diff, 0.0)), 0.0))
    row_h = lax.shift_right_logical(lax.broadcasted_iota(jnp.int32, (4 * c_, w), 0), shift_c)
    lane_h = lax.shift_right_logical(lax.broadcasted_iota(jnp.int32, (4 * c_, w), 1), shift_h)
    headmask = row_h == lane_h
    seg = seg_ref[...]

    def rows(c):
        return pl.ds(pl.multiple_of(c * c_, c_), c_)

    def prep(c, carry):
        sl = rows(c)
        q = q_ref[sl, :]
        k = k_ref[sl, :] * RET_K_SCALE
        if rope:
            cf, s1, s2 = cf_ref[sl, :], s1_ref[sl, :], s2_ref[sl, :]
            q = _rope(q, cf, s1, s2, half)
            k = _rope(k, cf, s1, s2, half)
        qs[sl, :] = q
        ks[sl, :] = k
        return carry

    lax.fori_loop(0, n_chunks, prep, 0)

    def kv_outer(k, v, dec):
        a = lax.dot_general((k * dec).astype(BF16), v.astype(BF16), tn, preferred_element_type=F32)
        return jnp.where(blockdiag, a, 0.0)

    def bwd(t, s):
        c = n_chunks - 1 - t
        sball[c] = s
        sl = rows(c)
        return s * cdb + kv_outer(ks[sl, :], v_ref[sl, :], kdb)

    sb_ref[...] = lax.fori_loop(0, n_chunks, bwd, sb0_ref[...])

    def fwd(c, s):
        sl = rows(c)
        q, k, v = qs[sl, :], ks[sl, :], v_ref[sl, :]
        vb = v.astype(BF16)
        q4 = jnp.where(headmask, jnp.concatenate([q, q, q, q], axis=0), 0.0).astype(BF16)
        sc = lax.dot_general(q4, k.astype(BF16), nt, preferred_element_type=F32)
        o4 = jnp.dot((sc * dmat).astype(BF16), vb, preferred_element_type=F32)
        o4 = jnp.where(headmask, o4, 0.0)
        o = o4[0:c_] + o4[c_:2 * c_] + o4[2 * c_:3 * c_] + o4[3 * c_:4 * c_]
        o = o + jnp.dot((q * qdf).astype(BF16), s.astype(BF16), preferred_element_type=F32)
        o = o + jnp.dot((q * qdb).astype(BF16), sball[c].astype(BF16), preferred_element_type=F32)
        on = o * lax.rsqrt(_seg_mean(o * o, seg) + EPS) * ng_ref[...]
        g = g_ref[sl, :]
        y_ref[sl, :] = on * (g * jax.nn.sigmoid(g))
        return s * cdf + kv_outer(k, v, kdf)

    sf_ref[...] = lax.fori_loop(0, n_chunks, fwd, sf0_ref[...])


def _retention(ret, dl, dc, ng, seg, sf0, sb0, tables):
    b, l, _ = ret.shape
    w = GROUP_W
    rope = tables is not None
    col = lambda j: pl.BlockSpec((None, l, w), lambda i: (i, 0, j))
    full = lambda a: pl.BlockSpec(a.shape, lambda i: (0, 0))
    state = pl.BlockSpec((None, w, w), lambda i: (i, 0, 0))
    args = [ret, ret, ret, ret, dl, dc, ng, seg, sf0, sb0]
    specs = [col(0), col(1), col(2), col(3), full(dl), full(dc), full(ng), full(seg), state, state]
    if rope:
        args += list(tables)
        specs += [pl.BlockSpec((l, w), lambda i: (0, 0))] * 3
    sshape = jax.ShapeDtypeStruct((b, w, w), F32)
    return pl.pallas_call(
        functools.partial(_ret_kernel, rope=rope, chunk=RET_CHUNK),
        grid=(b,),
        in_specs=specs,
        out_specs=[pl.BlockSpec((None, l, w), lambda i: (i, 0, 0)), state, state],
        out_shape=[jax.ShapeDtypeStruct((b, l, w), F32), sshape, sshape],
        scratch_shapes=[pltpu.VMEM((l, w), F32), pltpu.VMEM((l, w), F32),
                        pltpu.VMEM((l // RET_CHUNK, w, w), F32)],
        compiler_params=_params("parallel"),
        name="retention",
    )(*args)


def _merge_kernel(x_ref, ya_ref, yb_ref, yc_ref, yd_ref, g_ref, gate_ref, w_ref, o_ref):
    acc = 0.0
    for i, y_ref in enumerate((ya_ref, yb_ref, yc_ref, yd_ref)):
        rows = slice(i * GROUP_W, (i + 1) * GROUP_W)
        yn = (_rms(y_ref[...]) * g_ref[:, rows]).astype(BF16)
        acc = acc + jnp.dot(yn, w_ref[rows, :], preferred_element_type=F32)
    o_ref[...] = x_ref[...] + gate_ref[...] * acc


def _merge(x, ys, g_out, gate, w):
    b, l, d = x.shape
    tm = min(512, l)
    tok = lambda wd: pl.BlockSpec((None, tm, wd), lambda i, j: (i, j, 0))
    return pl.pallas_call(
        _merge_kernel,
        grid=(b, l // tm),
        in_specs=[tok(d)] + [tok(GROUP_W)] * 4 + [
            pl.BlockSpec((1, d), lambda i, j: (0, 0)),
            pl.BlockSpec((None, 1, d), lambda i, j: (i, 0, 0)),
            pl.BlockSpec(w.shape, lambda i, j: (0, 0))],
        out_specs=tok(d),
        out_shape=jax.ShapeDtypeStruct((b, l, d), F32),
        compiler_params=_params("parallel", "parallel"),
        name="merge",
    )(x, *ys, g_out, gate, w)


def _ffn_kernel(x_ref, g_ref, sh_ref, sc_ref, gate_ref, wa_ref, wg_ref, wo_ref, fg_ref, o_ref,
                h_scr, acc_scr, *, final_norm):
    j = pl.program_id(2)

    @pl.when(j == 0)
    def _():
        y = _rms(x_ref[...]) * g_ref[...]
        h_scr[...] = (y * (1.0 + sc_ref[...]) + sh_ref[...]).astype(BF16)
        acc_scr[...] = jnp.zeros_like(acc_scr)

    h = h_scr[...]
    a = jnp.dot(h, wa_ref[...], preferred_element_type=F32)
    g = jnp.dot(h, wg_ref[...], preferred_element_type=F32)
    u = (a * jax.nn.sigmoid(a) * g).astype(BF16)
    acc_scr[...] += jnp.dot(u, wo_ref[...], preferred_element_type=F32)

    @pl.when(j == pl.num_programs(2) - 1)
    def _():
        out = x_ref[...] + gate_ref[...] * acc_scr[...]
        if final_norm:
            out = _rms(out) * fg_ref[...]
        o_ref[...] = out


def _ffn(x, g, sh, sc, gate, w_in, w_out, final_g, final_norm):
    b, l, d = x.shape
    tm = min(1024, l)
    nff = D_FF // FF_TILE
    tok = pl.BlockSpec((None, tm, d), lambda i, t, j: (i, t, 0))
    vec = pl.BlockSpec((None, 1, d), lambda i, t, j: (i, 0, 0))
    row = pl.BlockSpec((1, d), lambda i, t, j: (0, 0))
    return pl.pallas_call(
        functools.partial(_ffn_kernel, final_norm=final_norm),
        grid=(b, l // tm, nff),
        in_specs=[tok, row, vec, vec, vec,
                  pl.BlockSpec((d, FF_TILE), lambda i, t, j: (0, j)),
                  pl.BlockSpec((d, FF_TILE), lambda i, t, j: (0, j + nff)),
                  pl.BlockSpec((FF_TILE, d), lambda i, t, j: (j, 0)),
                  row],
        out_specs=tok,
        out_shape=jax.ShapeDtypeStruct((b, l, d), F32),
        scratch_shapes=[pltpu.VMEM((tm, d), BF16), pltpu.VMEM((tm, d), F32)],
        compiler_params=_params("parallel", "parallel", "arbitrary"),
        name="ffn",
    )(x, g, sh, sc, gate, w_in, w_in, w_out, final_g)


def _pad_proj(w):
    z = jnp.zeros((w.shape[0], LANES - MLA_ROPE), w.dtype)
    return jnp.concatenate([w[:, :1184], z, w[:, 1184:]], axis=1).astype(BF16)


def _mla_weights(wuq, wukv):
    r = wuq.shape[0]
    q = wuq.reshape(r, 4, MLA_NOPE + MLA_ROPE)
    q = jnp.pad(q, ((0, 0), (0, 0), (0, LANES - MLA_NOPE - MLA_ROPE))).reshape(r, 4 * LANES)
    kv = wukv.reshape(wukv.shape[0], 4, MLA_NOPE + MLA_V)
    k = jnp.pad(kv[..., :MLA_NOPE], ((0, 0), (0, 0), (0, LANES - MLA_NOPE))).reshape(-1, 4 * LANES)
    v = jnp.concatenate([kv[..., MLA_NOPE:], kv[..., MLA_NOPE:]], axis=-1).reshape(-1, 4 * LANES)
    return q.astype(BF16), k.astype(BF16), v.astype(BF16)


def _seg_matrix(width):
    i = jnp.arange(width) // HEAD_D
    return jnp.where(i[:, None] == i[None, :], 1.0 / HEAD_D, 0.0).astype(BF16)


def kernel(x, c, ctx, c_ctx, w_mod, b_mod, norm_attn_g, norm_ffn_g, w_in, hy_conv_w, hy_conv_b, hy_w1, hy_b1, hy_w2, hy_b2, hy_w3, hy_b3, hy_freq, hy_w4, hy_bias, mla_q_norm_g, mla_w_uq, mla_kv_norm_g, mla_w_ukv, ret_decay, ret_norm_g, gqa_q_norm_g, gqa_k_norm_g, out_norm_g, w_out, w_ffn_in, w_ffn_out, final_norm_g):
    b, n_lat, d = x.shape
    n_ctx = ctx.shape[1]
    depth = w_mod.shape[0]

    rope_mla = _rope_tables(n_lat, MLA_ROPE, LANES, MLA_NOPE, 1)
    rope_hd2 = _rope_tables(n_lat, HEAD_D, HEAD_D, 0, 2)
    rope_hd4 = tuple(jnp.tile(a, (1, 2)) for a in rope_hd2)
    seg2, seg4 = _seg_matrix(LANES), _seg_matrix(GROUP_W)
    hy_consts = {n: (_hy_features(n),) + _dft_tables(n) for n in {n_lat, n_ctx}}
    zero_state = jnp.zeros((b, GROUP_W, GROUP_W), F32)

    rows = -(-(b + 1) // 8) * 8
    cond = jnp.zeros((rows, d), F32).at[:b].set(c).at[b].set(c_ctx)
    mod = _ada_mod(cond, w_mod, b_mod)

    xl, xc = x, ctx
    for l in range(depth):
        update_ctx = l < depth - 1
        m_lat = [m[:b, None, :] for m in jnp.split(mod[l], 6, axis=-1)]
        m_ctx = [jnp.broadcast_to(m[b][None, None, :], (b, 1, d)) for m in jnp.split(mod[l], 6, axis=-1)]
        wp = _pad_proj(w_in[l])
        g_attn = norm_attn_g[l][None]
        hy_l, cq_l, ckvr_l, ret_l, gqa_l = _in_proj(xl, g_attn, m_lat[0], m_lat[1], wp)
        hy_c, cq_c, ckvr_c, ret_c, gqa_c = _in_proj(xc, g_attn, m_ctx[0], m_ctx[1], wp)

        hy_args = (hy_conv_w[l], hy_conv_b[l], hy_w1[l], hy_b1[l], hy_w2[l], hy_b2[l], hy_w3[l], hy_b3[l],
                   hy_freq[l], hy_w4[l], hy_bias[l])
        ya_l = _hyena(hy_l, *hy_consts[n_lat], *hy_args)

        wuq, wk, wv = _mla_weights(mla_w_uq[l], mla_w_ukv[l])
        gq, gkv = mla_q_norm_g[l][None], mla_kv_norm_g[l][None]
        q_l, k_l, v_l = _mla_prep(cq_l, ckvr_l, gq, gkv, wuq, wk, wv, rope_mla)
        q_c, k_c, v_c = _mla_prep(cq_c, ckvr_c, gq, gkv, wuq, wk, wv, None)
        yb_l = _attention(q_l, [(k_c, v_c), (k_l, v_l)], 1)

        dl = jnp.repeat(ret_decay[l], HEAD_D, axis=1)
        dc = jnp.repeat(ret_decay[l].T, RET_CHUNK, axis=0)
        ng = ret_norm_g[l][None]
        yc_c, s_f, s_b = _retention(ret_c, dl, dc, ng, seg4, zero_state, zero_state, None)
        yc_l, _, _ = _retention(ret_l, dl, dc, ng, seg4, s_f, s_b, rope_hd4)

        ggq = jnp.tile(gqa_q_norm_g[l], 2)[None]
        ggk = jnp.tile(gqa_k_norm_g[l], 2)[None]
        gq_l, gk_l, gv_l = _gqa_prep(gqa_l, ggq, ggk, seg2, rope_hd2)
        gq_c, gk_c, gv_c = _gqa_prep(gqa_c, ggq, ggk, seg2, None)
        yd_l = _attention(gq_l, [(gk_c, gv_c), (gk_l, gv_l)], 2)

        wo = w_out[l].astype(BF16)
        wfi, wfo = w_ffn_in[l].astype(BF16), w_ffn_out[l].astype(BF16)
        g_out, g_ffn, g_fin = out_norm_g[l][None], norm_ffn_g[l][None], final_norm_g[None]
        xl = _merge(xl, (ya_l, yb_l, yc_l, yd_l), g_out, m_lat[2], wo)
        xl = _ffn(xl, g_ffn, m_lat[3], m_lat[4], m_lat[5], wfi, wfo, g_fin, l == depth - 1)

        if update_ctx:
            ya_c = _hyena(hy_c, *hy_consts[n_ctx], *hy_args)
            yb_c = _attention(q_c, [(k_c, v_c)], 1)
            yd_c = _attention(gq_c, [(gk_c, gv_c)], 2)
            xc = _merge(xc, (ya_c, yb_c, yc_c, yd_c), g_out, m_ctx[2], wo)
            xc = _ffn(xc, g_ffn, m_ctx[3], m_ctx[4], m_ctx[5], wfi, wfo, g_fin, False)

    return xl
```

```python
import functools
import math

import numpy as np
import jax
import jax.numpy as jnp
from jax import lax
from jax.experimental import pallas as pl
from jax.experimental.pallas import tpu as pltpu

F32 = jnp.float32
BF16 = jnp.bfloat16
HI = lax.Precision.HIGHEST

D_MODEL = 1024
GRID_W = 64
EPS = 1e-6
ROPE_THETA = 10000.0
GROUP_W = 256
LANES = 128

HY_EMB = 33
HY_BANDS = 16
HY_FFN = 64
HY_NFILT = 1024
HY_DECAY_SHIFT = 0.05
HY_FAST_PCT = 0.3
HY_SLOW_PCT = 1.5
HY_TARGET = 1e-2

MLA_NOPE = 64
MLA_ROPE = 32
MLA_V = 64
MLA_KV_RANK = 128
MLA_SCALE = (MLA_NOPE + MLA_ROPE) ** -0.5

HEAD_D = 64
RET_CHUNK = 128
GQA_SCALE = HEAD_D ** -0.5
RET_K_SCALE = HEAD_D ** -0.5

D_FF = 2816
FF_TILE = 256

PROJ_SPLITS = (768, 256, 256, 1024, 512)
PROJ_OFFSETS = (0, 768, 1024, 1280, 2304, 2816)

VMEM_LIMIT = 56 * 1024 * 1024


def _params(*sem):
    return pltpu.CompilerParams(dimension_semantics=sem, vmem_limit_bytes=VMEM_LIMIT)


def _const_spec(a):
    nd = a.ndim
    return pl.BlockSpec(a.shape, lambda *_: (0,) * nd, pipeline_mode=pl.Buffered(1))


def _rms(x):
    return x * lax.rsqrt(jnp.mean(x * x, axis=-1, keepdims=True) + EPS)


def _rope(x, cf, s1, s2, half):
    w = x.shape[-1]
    return x * cf + pltpu.roll(x, w - half, 1) * s1 + pltpu.roll(x, half, 1) * s2


def _seg_mean(sq, seg):
    hi = sq.astype(BF16)
    lo = (sq - hi.astype(F32)).astype(BF16)
    return (jnp.dot(hi, seg, preferred_element_type=F32) + jnp.dot(lo, seg, preferred_element_type=F32))


def _log_sigmoid(x):
    return -(jnp.maximum(-x, 0.0) + jnp.log(1.0 + jnp.exp(-jnp.abs(x))))


def _ada_kernel(c_ref, w_ref, b_ref, o_ref):
    c = c_ref[...]
    s = c * jax.nn.sigmoid(c)
    o_ref[...] = jnp.dot(s, w_ref[...], precision=HI, preferred_element_type=F32) + b_ref[...]


def _ada_mod(cond, w_mod, b_mod):
    depth, d, n = w_mod.shape
    r = cond.shape[0]
    tn = 1024
    return pl.pallas_call(
        _ada_kernel,
        grid=(depth, n // tn),
        in_specs=[pl.BlockSpec((r, d), lambda l, j: (0, 0)),
                  pl.BlockSpec((None, d, tn), lambda l, j: (l, 0, j)),
                  pl.BlockSpec((None, 1, tn), lambda l, j: (l, 0, j))],
        out_specs=pl.BlockSpec((None, r, tn), lambda l, j: (l, 0, j)),
        out_shape=jax.ShapeDtypeStruct((depth, r, n), F32),
        compiler_params=_params("arbitrary", "arbitrary"),
        name="ada_mod",
    )(cond, w_mod, b_mod.reshape(depth, 1, n))


def _mla_heads(cq, ckvr, gq, gkv, wuq_ref, wk_ref, wv_ref, tables, q_ref, k_ref, v_ref):
    half = MLA_ROPE // 2
    qn = (_rms(cq) * gq).astype(BF16)
    q = jnp.dot(qn, wuq_ref[...], preferred_element_type=F32)
    kvn = (_rms(ckvr[:, :MLA_KV_RANK]) * gkv).astype(BF16)
    k = jnp.dot(kvn, wk_ref[...], preferred_element_type=F32)
    v = jnp.dot(kvn, wv_ref[...], preferred_element_type=F32)
    kpe = pltpu.roll(ckvr[:, MLA_KV_RANK:], MLA_NOPE, 1)
    if tables is not None:
        kpe = _rope(kpe, *tables, half)
    for h in range(q_ref.shape[0]):
        cols = slice(h * LANES, (h + 1) * LANES)
        qh = q[:, cols]
        if tables is not None:
            qh = _rope(qh, *tables, half)
        q_ref[h] = (qh * MLA_SCALE).astype(BF16)
        k_ref[h] = (k[:, cols] + kpe).astype(BF16)
        v_ref[h] = v[:, cols].astype(BF16)


def _gqa_heads(x, gq, gk, seg, tables, q_ref, k_ref, v_ref):
    half = HEAD_D // 2
    lo = lax.broadcasted_iota(jnp.int32, (x.shape[0], LANES), 1) < HEAD_D

    def normed(a, g):
        a = a * lax.rsqrt(_seg_mean(a * a, seg) + EPS) * g
        return _rope(a, *tables, half) if tables is not None else a

    def dup(a, g):
        a = jnp.where(lo if g == 0 else jnp.logical_not(lo), a, 0.0)
        return (a + pltpu.roll(a, HEAD_D, 1)).astype(BF16)

    for p in range(2):
        qb = normed(x[:, p * LANES:(p + 1) * LANES], gq) * GQA_SCALE
        q_ref[2 * p] = jnp.where(lo, qb, 0.0).astype(BF16)
        q_ref[2 * p + 1] = jnp.where(lo, 0.0, qb).astype(BF16)
    kb = normed(x[:, 2 * LANES:3 * LANES], gk)
    vb = x[:, 3 * LANES:4 * LANES]
    for g in range(2):
        k_ref[g] = dup(kb, g)
        v_ref[g] = dup(vb, g)


def _front_kernel(x_ref, g_ref, sh_ref, sc_ref, w_ref, gq_ref, gkv_ref, wuq_ref, wk_ref, wv_ref,
                  ggq_ref, ggk_ref, seg_ref, *rest, rope):
    if rope:
        mla_t = tuple(r[...] for r in rest[0:3])
        hd_t = tuple(r[...] for r in rest[3:6])
        rest = rest[6:]
    else:
        mla_t = hd_t = None
    hy_ref, ret_ref, mq_ref, mk_ref, mv_ref, gq_out, gk_out, gv_out = rest
    y = _rms(x_ref[...]) * g_ref[...]
    h = (y * (1.0 + sc_ref[...]) + sh_ref[...]).astype(BF16)
    o0, o1, o2, o3, o4, o5 = PROJ_OFFSETS

    def proj(a, b):
        return jnp.dot(h, w_ref[:, a:b], preferred_element_type=F32)

    hy_ref[...] = proj(o0, o1)
    ret_ref[...] = proj(o3, o4)
    _mla_heads(proj(o1, o2), proj(o2, o3), gq_ref[...], gkv_ref[...], wuq_ref, wk_ref, wv_ref, mla_t,
               mq_ref, mk_ref, mv_ref)
    _gqa_heads(proj(o4, o5), ggq_ref[...], ggk_ref[...], seg_ref[...], hd_t, gq_out, gk_out, gv_out)


def _front(x, g, sh, sc, w, mla_w, gqa_w, tables):
    b, l, d = x.shape
    tm = min(512, l)
    nt = l // tm
    rope = tables is not None
    tok = lambda wd: pl.BlockSpec((None, tm, wd), lambda i, j: (i, j, 0))
    vec = pl.BlockSpec((None, 1, d), lambda i, j: (i, 0, 0))
    head = lambda nh: pl.BlockSpec((nh, tm, LANES), lambda i, j: (0, i * nt + j, 0))
    hshape = lambda nh: jax.ShapeDtypeStruct((nh, b * l, LANES), BF16)
    consts = [w, *mla_w, *gqa_w]
    args = [x, g, sh, sc, *consts]
    specs = [tok(d), _const_spec(g), vec, vec] + [_const_spec(a) for a in consts]
    if rope:
        args += [*tables[0], *tables[1]]
        specs += [pl.BlockSpec((tm, LANES), lambda i, j: (j, 0))] * 6
    return pl.pallas_call(
        functools.partial(_front_kernel, rope=rope),
        grid=(b, nt),
        in_specs=specs,
        out_specs=[tok(PROJ_SPLITS[0]), tok(PROJ_SPLITS[3]), head(4), head(4), head(4), head(4), head(2), head(2)],
        out_shape=[jax.ShapeDtypeStruct((b, l, PROJ_SPLITS[0]), F32),
                   jax.ShapeDtypeStruct((b, l, PROJ_SPLITS[3]), F32),
                   hshape(4), hshape(4), hshape(4), hshape(4), hshape(2), hshape(2)],
        compiler_params=_params("parallel", "parallel"),
        name="front",
    )(*args)


def _hy_filter_kernel(z_ref, t_ref, w1_ref, b1_ref, w2_ref, b2_ref, w3_ref, b3_ref, fr_ref,
                      w4_ref, dl_ref, o_ref):
    fr = fr_ref[...]
    dot = functools.partial(jnp.dot, precision=HI, preferred_element_type=F32)
    h = jnp.sin(fr * (dot(z_ref[...], w1_ref[...]) + b1_ref[...]))
    h = jnp.sin(fr * (dot(h, w2_ref[...]) + b2_ref[...]))
    h = jnp.sin(fr * (dot(h, w3_ref[...]) + b3_ref[...]))
    filt = dot(h, w4_ref[...]) * (jnp.exp(-t_ref[...] * dl_ref[...]) + HY_DECAY_SHIFT)
    fwd = filt[:, :GROUP_W]
    bwd = filt[:, GROUP_W:]
    row = lax.broadcasted_iota(jnp.int32, bwd.shape, 0)
    bwd = jnp.where(row == 0, 0.0, bwd)
    nrm = (jnp.sum(jnp.abs(fwd), axis=0, keepdims=True) + jnp.sum(jnp.abs(bwd), axis=0, keepdims=True))
    o_ref[:, :GROUP_W] = fwd / nrm
    o_ref[:, GROUP_W:] = bwd / nrm


def _hy_filters(l, z, t, w1, b1, w2, b2, w3, b3, fr, w4, dl):
    full = lambda a: pl.BlockSpec(a.shape, lambda o: (0, 0))
    return pl.pallas_call(
        _hy_filter_kernel,
        grid=(2,),
        in_specs=[full(z), full(t), full(w1), full(b1), full(w2), full(b2), full(w3), full(b3), full(fr),
                  pl.BlockSpec((HY_FFN, 2 * GROUP_W), lambda o: (0, o)),
                  pl.BlockSpec((1, 2 * GROUP_W), lambda o: (0, o))],
        out_specs=pl.BlockSpec((l, 2 * GROUP_W), lambda o: (0, o)),
        out_shape=jax.ShapeDtypeStruct((l, HY_NFILT), F32),
        compiler_params=_params("arbitrary"),
        name="hy_filters",
    )(z, t, w1, b1, w2, b2, w3, b3, fr, w4, dl)


def _hy_spec_kernel(c_ref, s_ref, h_ref, sign_ref, a_ref, b_ref, d_ref):
    dot = functools.partial(jnp.dot, precision=HI, preferred_element_type=F32)
    row = lax.broadcasted_iota(jnp.int32, (c_ref.shape[0], GROUP_W), 0)
    first = jnp.logical_and(pl.program_id(0) == 0, row == 0)
    for o in range(2):
        hf = h_ref[:, 2 * o * GROUP_W:(2 * o + 1) * GROUP_W]
        hb = h_ref[:, (2 * o + 1) * GROUP_W:(2 * o + 2) * GROUP_W]
        hs = hf + hb
        re = dot(c_ref[...], hs)
        im = dot(s_ref[...], hf - hb)
        ny = jnp.sum(sign_ref[...] * hs, axis=0, keepdims=True)
        cols = slice(o * GROUP_W, (o + 1) * GROUP_W)
        a_ref[:, cols] = re
        b_ref[:, cols] = jnp.where(first, 0.0, im)
        d_ref[:, cols] = jnp.where(first, ny, re)


def _hy_spectrum(h, c32, s32, sign):
    l = h.shape[0]
    fb = min(256, l)
    out = jax.ShapeDtypeStruct((l, 2 * GROUP_W), F32)
    blk = pl.BlockSpec((fb, 2 * GROUP_W), lambda f: (f, 0))
    return pl.pallas_call(
        _hy_spec_kernel,
        grid=(l // fb,),
        in_specs=[pl.BlockSpec((fb, l), lambda f: (f, 0)), pl.BlockSpec((fb, l), lambda f: (f, 0)),
                  pl.BlockSpec(h.shape, lambda f: (0, 0)), pl.BlockSpec((l, 1), lambda f: (0, 0))],
        out_specs=[blk, blk, blk],
        out_shape=[out, out, out],
        compiler_params=_params("arbitrary"),
        name="hy_spectrum",
    )(c32, s32, h, sign)


def _short_conv(u, w, b):
    n = u.shape[0]
    row = lax.broadcasted_iota(jnp.int32, u.shape, 0)
    up = jnp.where(row == 0, 0.0, pltpu.roll(u, 1, 0))
    dn = jnp.where(row == n - 1, 0.0, pltpu.roll(u, n - 1, 0))
    return w[0:1] * up + w[1:2] * u + w[2:3] * dn + b


def _hy_conv_kernel(sig_ref, gate_ref, cw_ref, cb_ref, a_ref, b_ref, d_ref, bias_ref,
                    c_ref, s_ref, ci_ref, si_ref, o_ref, u_scr, ub_scr, acc_scr,
                    *, nb, sig_col, gate_col):
    f = pl.program_id(1)
    w = GROUP_W

    @pl.when(f == 0)
    def _():
        for i in range(nb):
            u = sig_ref[i]
            if sig_col is not None:
                u = _short_conv(u, cw_ref[:, sig_col * w:(sig_col + 1) * w], cb_ref[:, sig_col * w:(sig_col + 1) * w])
            u_scr[:, i * w:(i + 1) * w] = u
            ub_scr[:, i * w:(i + 1) * w] = u.astype(BF16)
        acc_scr[...] = jnp.zeros_like(acc_scr)

    ub = ub_scr[...]
    xr = jnp.dot(c_ref[...], ub, preferred_element_type=F32)
    xi = jnp.dot(s_ref[...], ub, preferred_element_type=F32)
    ka, kb, kd = a_ref[...], b_ref[...], d_ref[...]
    yr, yi = [], []
    for i in range(nb):
        r = xr[:, i * w:(i + 1) * w]
        m = xi[:, i * w:(i + 1) * w]
        yr.append((r * ka - m * kb).astype(BF16))
        yi.append((r * kb + m * kd).astype(BF16))
    yr = jnp.concatenate(yr, axis=1) if nb > 1 else yr[0]
    yi = jnp.concatenate(yi, axis=1) if nb > 1 else yi[0]
    acc_scr[...] += (jnp.dot(ci_ref[...], yr, preferred_element_type=F32)
                     + jnp.dot(si_ref[...], yi, preferred_element_type=F32))

    @pl.when(f == pl.num_programs(1) - 1)
    def _():
        for i in range(nb):
            y = acc_scr[:, i * w:(i + 1) * w] + u_scr[:, i * w:(i + 1) * w] * bias_ref[...]
            g = _short_conv(gate_ref[i], cw_ref[:, gate_col * w:(gate_col + 1) * w],
                            cb_ref[:, gate_col * w:(gate_col + 1) * w])
            o_ref[i] = g * y


def _hy_conv(sig, sig_col, hy, gate_col, cw, cb, spec, order, bias, dft):
    b, l, _ = hy.shape
    w = GROUP_W
    nb = 2 if l >= 2048 else math.gcd(b, 8)
    fb = min(256, l)
    c16, s16, ci16, si16 = dft
    a, bm, d = spec
    sig_arr = hy if sig_col is not None else sig
    sig_blk = sig_col if sig_col is not None else 0
    spec_blk = pl.BlockSpec((fb, w), lambda i, f: (f, order))
    kernel = functools.partial(_hy_conv_kernel, nb=nb, sig_col=sig_col, gate_col=gate_col)
    return pl.pallas_call(
        kernel,
        grid=(b // nb, l // fb),
        in_specs=[pl.BlockSpec((nb, l, w), lambda i, f: (i, 0, sig_blk)),
                  pl.BlockSpec((nb, l, w), lambda i, f: (i, 0, gate_col)),
                  pl.BlockSpec(cw.shape, lambda i, f: (0, 0)),
                  pl.BlockSpec(cb.shape, lambda i, f: (0, 0)),
                  spec_blk, spec_blk, spec_blk,
                  pl.BlockSpec((1, w), lambda i, f: (0, 0)),
                  pl.BlockSpec((fb, l), lambda i, f: (f, 0)),
                  pl.BlockSpec((fb, l), lambda i, f: (f, 0)),
                  pl.BlockSpec((l, fb), lambda i, f: (0, f)),
                  pl.BlockSpec((l, fb), lambda i, f: (0, f))],
        out_specs=pl.BlockSpec((nb, l, w), lambda i, f: (i, 0, 0)),
        out_shape=jax.ShapeDtypeStruct((b, l, w), F32),
        scratch_shapes=[pltpu.VMEM((l, nb * w), F32), pltpu.VMEM((l, nb * w), BF16),
                        pltpu.VMEM((l, nb * w), F32)],
        compiler_params=_params("parallel", "arbitrary"),
        name="hy_conv",
    )(sig_arr, hy, cw, cb, a, bm, d, bias, c16, s16, ci16, si16)


@functools.lru_cache(maxsize=None)
def _dft_tables(l):
    n = 2 * l
    k = np.arange(l, dtype=np.int64)
    ang = ((k[:, None] * k[None, :]) % n).astype(np.float64) * (2.0 * math.pi / n)
    sign = np.where(k % 2 == 0, 1.0, -1.0)
    c = np.cos(ang)
    s = -np.sin(ang)
    s[0, :] = sign
    wk = np.where(k == 0, 1.0, 2.0) / n
    ci = c.T * wk[None, :]
    si = s.T * wk[None, :]
    f32 = lambda a: np.ascontiguousarray(a, dtype=np.float32)
    b16 = lambda a: f32(a).astype(BF16)
    return (f32(c), f32(s), f32(sign[:, None])), (b16(c), b16(s), b16(ci), b16(si))


def _hy_features(l):
    t = jnp.linspace(0.0, 1.0, l, dtype=F32)[:, None]
    wpos = 2.0 * math.pi * jnp.arange(l, dtype=F32)[:, None] / l
    fb = jnp.linspace(1e-4, HY_BANDS - 1, HY_BANDS, dtype=F32)[None, :]
    z = jnp.concatenate([t, jnp.cos(fb * wpos), -jnp.sin(fb * wpos)], axis=-1)
    z = jnp.pad(z, ((0, 0), (0, HY_FFN - HY_EMB)))
    deltas = jnp.linspace(math.log(HY_TARGET) / HY_FAST_PCT, math.log(HY_TARGET) / HY_SLOW_PCT, HY_NFILT, dtype=F32)
    return z, t, jnp.abs(deltas)[None, :]


def _hyena(hy, feats, dft32, dft16, cw, cb, w1, b1, w2, b2, w3, b3, fr, w4, bias):
    l = hy.shape[1]
    z, t, dl = feats
    w1p = jnp.pad(w1, ((0, HY_FFN - HY_EMB), (0, 0)))
    h = _hy_filters(l, z, t, w1p, b1[None], w2, b2[None], w3, b3[None], fr[None], w4, dl)
    spec = _hy_spectrum(h, *dft32)
    cb2 = cb[None]
    zz = _hy_conv(None, 0, hy, 1, cw, cb2, spec, 0, bias[0:1], dft16)
    return _hy_conv(zz, None, hy, 2, cw, cb2, spec, 1, bias[1:2], dft16)


def _rope_tables(l, dim, width, off, reps):
    rows = jnp.repeat(jnp.arange(l // GRID_W, dtype=jnp.int32), GRID_W).astype(F32)
    cols = jnp.tile(jnp.arange(GRID_W, dtype=jnp.int32), l // GRID_W).astype(F32)
    quarter = dim // 4
    half = dim // 2
    inv = ROPE_THETA ** (-jnp.arange(quarter, dtype=F32) / quarter)
    ang = jnp.concatenate([rows[:, None] * inv, cols[:, None] * inv], axis=-1)
    c, s = jnp.cos(ang), jnp.sin(ang)
    cf = jnp.ones((l, width), F32).at[:, off:off + dim].set(jnp.concatenate([c, c], axis=-1))
    s1 = jnp.zeros((l, width), F32).at[:, off:off + half].set(-s)
    s2 = jnp.zeros((l, width), F32).at[:, off + half:off + dim].set(s)
    return tuple(jnp.tile(a, (1, reps)) for a in (cf, s1, s2))


def _attn_kernel(q_ref, *rest, rep):
    o_ref = rest[-1]
    kv = rest[:-1]
    ks, vs = kv[0::2], kv[1::2]
    tq = q_ref.shape[1]
    lo = lax.broadcasted_iota(jnp.int32, (tq, LANES), 1) < HEAD_D
    nt = (((1,), (1,)), ((), ()))
    for p in range(2):
        outs = []
        for r in range(2):
            h = 2 * p + r
            g = h // rep
            q = q_ref[h]
            ss = [lax.dot_general(q, k_ref[g], nt, preferred_element_type=F32) for k_ref in ks]
            m = ss[0].max(axis=-1, keepdims=True)
            for s in ss[1:]:
                m = jnp.maximum(m, s.max(axis=-1, keepdims=True))
            den = 0.0
            acc = 0.0
            for s, v_ref in zip(ss, vs):
                e = jnp.exp(s - m)
                den = den + e.sum(axis=-1, keepdims=True)
                acc = acc + jnp.dot(e.astype(BF16), v_ref[g], preferred_element_type=F32)
            outs.append(acc / den)
        o_ref[:, p * LANES:(p + 1) * LANES] = jnp.where(lo, outs[0], outs[1])


def _attention(q, kvs, rep, b, l):
    nh = q.shape[0]
    tq = min(512, l)
    nq = l // tq
    specs = [pl.BlockSpec((nh, tq, LANES), lambda i, j: (0, i * nq + j, 0))]
    args = [q]
    for k, v, lk in kvs:
        spec = pl.BlockSpec((k.shape[0], lk, LANES), lambda i, j: (0, i, 0))
        specs += [spec, spec]
        args += [k, v]
    return pl.pallas_call(
        functools.partial(_attn_kernel, rep=rep),
        grid=(b, nq),
        in_specs=specs,
        out_specs=pl.BlockSpec((None, tq, GROUP_W), lambda i, j: (i, j, 0)),
        out_shape=jax.ShapeDtypeStruct((b, l, GROUP_W), F32),
        compiler_params=_params("parallel", "parallel"),
        name="attention",
    )(*args)


def _ret_kernel(q_ref, k_ref, v_ref, g_ref, dl_ref, dc_ref, ng_ref, seg_ref, sf0_ref, sb0_ref, *rest, rope, chunk):
    if rope:
        cf_ref, s1_ref, s2_ref, y_ref, sf_ref, sb_ref, qs, ks, sball = rest
    else:
        y_ref, sf_ref, sb_ref, qs, ks, sball = rest
    c_ = chunk
    w = GROUP_W
    n_chunks = q_ref.shape[0] // c_
    half = HEAD_D // 2
    shift_c = int(math.log2(c_))
    shift_h = int(math.log2(HEAD_D))
    tn = (((0,), (0,)), ((), ()))
    nt = (((1,), (1,)), ((), ()))

    lgf = _log_sigmoid(dl_ref[0:1, :])
    lgb = _log_sigmoid(dl_ref[1:2, :])
    ii = lax.broadcasted_iota(jnp.int32, (c_, 1), 0).astype(F32)
    qdf = jnp.exp((ii + 1.0) * lgf)
    kdf = jnp.exp((c_ - 1.0 - ii) * lgf)
    cdf = jnp.exp(float(c_) * lgf)
    qdb = jnp.exp((c_ - ii) * lgb)
    kdb = jnp.exp(ii * lgb)
    cdb = jnp.exp(float(c_) * lgb)
    r_h = lax.shift_right_logical(lax.broadcasted_iota(jnp.int32, (w, w), 0), shift_h)
    c_h = lax.shift_right_logical(lax.broadcasted_iota(jnp.int32, (w, w), 1), shift_h)
    blockdiag = r_h == c_h
    lgc = _log_sigmoid(dc_ref[...])
    i4 = jnp.bitwise_and(lax.broadcasted_iota(jnp.int32, (4 * c_, c_), 0), c_ - 1)
    j4 = lax.broadcasted_iota(jnp.int32, (4 * c_, c_), 1)
    diff = (i4 - j4).astype(F32)
    dmat = (jnp.where(diff >= 0, jnp.exp(lgc[:, 0:1] * jnp.maximum(diff, 0.0)), 0.0)
            + jnp.where(diff <= 0, jnp.exp(lgc[:, 1:2] * jnp.maximum(-diff, 0.0)), 0.0))
    row_h = lax.shift_right_logical(lax.broadcasted_iota(jnp.int32, (4 * c_, w), 0), shift_c)
    lane_h = lax.shift_right_logical(lax.broadcasted_iota(jnp.int32, (4 * c_, w), 1), shift_h)
    headmask = row_h == lane_h
    seg = seg_ref[...]

    def rows(c):
        return pl.ds(pl.multiple_of(c * c_, c_), c_)

    def prep(c, carry):
        sl = rows(c)
        q = q_ref[sl, :]
        k = k_ref[sl, :] * RET_K_SCALE
        if rope:
            cf, s1, s2 = cf_ref[sl, :], s1_ref[sl, :], s2_ref[sl, :]
            q = _rope(q, cf, s1, s2, half)
            k = _rope(k, cf, s1, s2, half)
        qs[sl, :] = q
        ks[sl, :] = k
        return carry

    lax.fori_loop(0, n_chunks, prep, 0)

    def kv_outer(k, v, dec):
        a = lax.dot_general((k * dec).astype(BF16), v.astype(BF16), tn, preferred_element_type=F32)
        return jnp.where(blockdiag, a, 0.0)

    def bwd(t, s):
        c = n_chunks - 1 - t
        sball[c] = s
        sl = rows(c)
        return s * cdb + kv_outer(ks[sl, :], v_ref[sl, :], kdb)

    sb_ref[...] = lax.fori_loop(0, n_chunks, bwd, sb0_ref[...])

    def fwd(c, s):
        sl = rows(c)
        q, k, v = qs[sl, :], ks[sl, :], v_ref[sl, :]
        vb = v.astype(BF16)
        q4 = jnp.where(headmask, jnp.concatenate([q, q, q, q], axis=0), 0.0).astype(BF16)
        sc = lax.dot_general(q4, k.astype(BF16), nt, preferred_element_type=F32)
        o4 = jnp.dot((sc * dmat).astype(BF16), vb, preferred_element_type=F32)
        o4 = jnp.where(headmask, o4, 0.0)
        o = o4[0:c_] + o4[c_:2 * c_] + o4[2 * c_:3 * c_] + o4[3 * c_:4 * c_]
        o = o + jnp.dot((q * qdf).astype(BF16), s.astype(BF16), preferred_element_type=F32)
        o = o + jnp.dot((q * qdb).astype(BF16), sball[c].astype(BF16), preferred_element_type=F32)
        on = o * lax.rsqrt(_seg_mean(o * o, seg) + EPS) * ng_ref[...]
        g = g_ref[sl, :]
        y_ref[sl, :] = on * (g * jax.nn.sigmoid(g))
        return s * cdf + kv_outer(k, v, kdf)

    sf_ref[...] = lax.fori_loop(0, n_chunks, fwd, sf0_ref[...])


def _retention(ret, dl, dc, ng, seg, sf0, sb0, tables):
    b, l, _ = ret.shape
    w = GROUP_W
    rope = tables is not None
    col = lambda j: pl.BlockSpec((None, l, w), lambda i: (i, 0, j))
    full = lambda a: pl.BlockSpec(a.shape, lambda i: (0, 0))
    state = pl.BlockSpec((None, w, w), lambda i: (i, 0, 0))
    args = [ret, ret, ret, ret, dl, dc, ng, seg, sf0, sb0]
    specs = [col(0), col(1), col(2), col(3), full(dl), full(dc), full(ng), full(seg), state, state]
    if rope:
        args += list(tables)
        specs += [_const_spec(t) for t in tables]
    sshape = jax.ShapeDtypeStruct((b, w, w), F32)
    return pl.pallas_call(
        functools.partial(_ret_kernel, rope=rope, chunk=RET_CHUNK),
        grid=(b,),
        in_specs=specs,
        out_specs=[pl.BlockSpec((None, l, w), lambda i: (i, 0, 0)), state, state],
        out_shape=[jax.ShapeDtypeStruct((b, l, w), F32), sshape, sshape],
        scratch_shapes=[pltpu.VMEM((l, w), F32), pltpu.VMEM((l, w), F32),
                        pltpu.VMEM((l // RET_CHUNK, w, w), F32)],
        compiler_params=_params("parallel"),
        name="retention",
    )(*args)


def _mlp_kernel(x_ref, ya_ref, yb_ref, yc_ref, yd_ref, gout_ref, ga_ref, wout_ref,
                gffn_ref, sh_ref, sc_ref, gf_ref, wfi_ref, wfo_ref, fg_ref, o_ref,
                x1_scr, h_scr, acc_scr, *, final_norm):
    acc = 0.0
    for i, y_ref in enumerate((ya_ref, yb_ref, yc_ref, yd_ref)):
        rows = slice(i * GROUP_W, (i + 1) * GROUP_W)
        yn = (_rms(y_ref[...]) * gout_ref[:, rows]).astype(BF16)
        acc = acc + jnp.dot(yn, wout_ref[rows, :], preferred_element_type=F32)
    x1 = x_ref[...] + ga_ref[...] * acc
    x1_scr[...] = x1
    h_scr[...] = (_rms(x1) * gffn_ref[...] * (1.0 + sc_ref[...]) + sh_ref[...]).astype(BF16)
    acc_scr[...] = jnp.zeros_like(acc_scr)
    nff = wfo_ref.shape[0]

    def body(c, carry):
        h = h_scr[...]
        a = jnp.dot(h, wfi_ref[c], preferred_element_type=F32)
        g = jnp.dot(h, wfi_ref[c + nff], preferred_element_type=F32)
        u = (a * jax.nn.sigmoid(a) * g).astype(BF16)
        acc_scr[...] += jnp.dot(u, wfo_ref[c], preferred_element_type=F32)
        return carry

    lax.fori_loop(0, nff, body, 0)
    out = x1_scr[...] + gf_ref[...] * acc_scr[...]
    if final_norm:
        out = _rms(out) * fg_ref[...]
    o_ref[...] = out


def _mlp(x, ys, g_out, g_a, w_out, g_ffn, sh, sc, g_f, wfi, wfo, final_g, final_norm):
    b, l, d = x.shape
    tm = min(512, l)
    tok = lambda wd: pl.BlockSpec((None, tm, wd), lambda i, j: (i, j, 0))
    vec = pl.BlockSpec((None, 1, d), lambda i, j: (i, 0, 0))
    return pl.pallas_call(
        functools.partial(_mlp_kernel, final_norm=final_norm),
        grid=(b, l // tm),
        in_specs=[tok(d)] + [tok(GROUP_W)] * 4 + [
            _const_spec(g_out), vec, _const_spec(w_out), _const_spec(g_ffn), vec, vec, vec,
            _const_spec(wfi), _const_spec(wfo), _const_spec(final_g)],
        out_specs=tok(d),
        out_shape=jax.ShapeDtypeStruct((b, l, d), F32),
        scratch_shapes=[pltpu.VMEM((tm, d), F32), pltpu.VMEM((tm, d), BF16), pltpu.VMEM((tm, d), F32)],
        compiler_params=_params("parallel", "parallel"),
        name="mlp",
    )(x, *ys, g_out, g_a, w_out, g_ffn, sh, sc, g_f, wfi, wfo, final_g)


def _pad_proj(w):
    z = jnp.zeros((w.shape[0], LANES - MLA_ROPE), w.dtype)
    return jnp.concatenate([w[:, :1184], z, w[:, 1184:]], axis=1).astype(BF16)


def _mla_weights(wuq, wukv):
    r = wuq.shape[0]
    q = wuq.reshape(r, 4, MLA_NOPE + MLA_ROPE)
    q = jnp.pad(q, ((0, 0), (0, 0), (0, LANES - MLA_NOPE - MLA_ROPE))).reshape(r, 4 * LANES)
    kv = wukv.reshape(wukv.shape[0], 4, MLA_NOPE + MLA_V)
    k = jnp.pad(kv[..., :MLA_NOPE], ((0, 0), (0, 0), (0, LANES - MLA_NOPE))).reshape(-1, 4 * LANES)
    v = jnp.concatenate([kv[..., MLA_NOPE:], kv[..., MLA_NOPE:]], axis=-1).reshape(-1, 4 * LANES)
    return q.astype(BF16), k.astype(BF16), v.astype(BF16)


def _ffn_weights(w_in, w_out):
    d = w_in.shape[0]
    nff = D_FF // FF_TILE
    wfi = w_in.astype(BF16).reshape(d, 2 * nff, FF_TILE).transpose(1, 0, 2)
    return wfi, w_out.astype(BF16).reshape(nff, FF_TILE, d)


def _seg_matrix(width):
    i = jnp.arange(width) // HEAD_D
    return jnp.where(i[:, None] == i[None, :], 1.0 / HEAD_D, 0.0).astype(BF16)


def kernel(x, c, ctx, c_ctx, w_mod, b_mod, norm_attn_g, norm_ffn_g, w_in, hy_conv_w, hy_conv_b, hy_w1, hy_b1, hy_w2, hy_b2, hy_w3, hy_b3, hy_freq, hy_w4, hy_bias, mla_q_norm_g, mla_w_uq, mla_kv_norm_g, mla_w_ukv, ret_decay, ret_norm_g, gqa_q_norm_g, gqa_k_norm_g, out_norm_g, w_out, w_ffn_in, w_ffn_out, final_norm_g):
    b, n_lat, d = x.shape
    n_ctx = ctx.shape[1]
    depth = w_mod.shape[0]

    rope_mla = _rope_tables(n_lat, MLA_ROPE, LANES, MLA_NOPE, 1)
    rope_hd2 = _rope_tables(n_lat, HEAD_D, HEAD_D, 0, 2)
    rope_hd4 = tuple(jnp.tile(a, (1, 2)) for a in rope_hd2)
    seg2, seg4 = _seg_matrix(LANES), _seg_matrix(GROUP_W)
    hy_consts = {n: (_hy_features(n),) + _dft_tables(n) for n in {n_lat, n_ctx}}
    zero_state = jnp.zeros((b, GROUP_W, GROUP_W), F32)

    rows = -(-(b + 1) // 8) * 8
    cond = jnp.zeros((rows, d), F32).at[:b].set(c).at[b].set(c_ctx)
    mod = _ada_mod(cond, w_mod, b_mod)

    xl, xc = x, ctx.reshape(1, b * n_ctx, d)
    for l in range(depth):
        update_ctx = l < depth - 1
        m_lat = [m[:b, None, :] for m in jnp.split(mod[l], 6, axis=-1)]
        m_ctx = [m[b][None, None, :] for m in jnp.split(mod[l], 6, axis=-1)]
        wp = _pad_proj(w_in[l])
        g_attn = norm_attn_g[l][None]
        mla_w = (mla_q_norm_g[l][None], mla_kv_norm_g[l][None]) + _mla_weights(mla_w_uq[l], mla_w_ukv[l])
        gqa_w = (jnp.tile(gqa_q_norm_g[l], 2)[None], jnp.tile(gqa_k_norm_g[l], 2)[None], seg2)
        hy_l, ret_l, q_l, k_l, v_l, gq_l, gk_l, gv_l = _front(
            xl, g_attn, m_lat[0], m_lat[1], wp, mla_w, gqa_w, (rope_mla, rope_hd2))
        hy_c, ret_c, q_c, k_c, v_c, gq_c, gk_c, gv_c = _front(
            xc, g_attn, m_ctx[0], m_ctx[1], wp, mla_w, gqa_w, None)
        hy_c = hy_c.reshape(b, n_ctx, -1)
        ret_c = ret_c.reshape(b, n_ctx, -1)

        hy_args = (hy_conv_w[l], hy_conv_b[l], hy_w1[l], hy_b1[l], hy_w2[l], hy_b2[l], hy_w3[l], hy_b3[l],
                   hy_freq[l], hy_w4[l], hy_bias[l])
        ya_l = _hyena(hy_l, *hy_consts[n_lat], *hy_args)
        yb_l = _attention(q_l, [(k_c, v_c, n_ctx), (k_l, v_l, n_lat)], 1, b, n_lat)

        dl = jnp.repeat(ret_decay[l], HEAD_D, axis=1)
        dc = jnp.repeat(ret_decay[l].T, RET_CHUNK, axis=0)
        ng = ret_norm_g[l][None]
        yc_c, s_f, s_b = _retention(ret_c, dl, dc, ng, seg4, zero_state, zero_state, None)
        yc_l, _, _ = _retention(ret_l, dl, dc, ng, seg4, s_f, s_b, rope_hd4)
        yd_l = _attention(gq_l, [(gk_c, gv_c, n_ctx), (gk_l, gv_l, n_lat)], 2, b, n_lat)

        wo = w_out[l].astype(BF16)
        wfi, wfo = _ffn_weights(w_ffn_in[l], w_ffn_out[l])
        g_out, g_ffn, g_fin = out_norm_g[l][None], norm_ffn_g[l][None], final_norm_g[None]
        xl = _mlp(xl, (ya_l, yb_l, yc_l, yd_l), g_out, m_lat[2], wo, g_ffn, m_lat[3], m_lat[4], m_lat[5],
                  wfi, wfo, g_fin, l == depth - 1)

        if update_ctx:
            flat = lambda a: a.reshape(1, b * n_ctx, GROUP_W)
            ya_c = _hyena(hy_c, *hy_consts[n_ctx], *hy_args)
            yb_c = _attention(q_c, [(k_c, v_c, n_ctx)], 1, b, n_ctx)
            yd_c = _attention(gq_c, [(gk_c, gv_c, n_ctx)], 2, b, n_ctx)
            xc = _mlp(xc, tuple(flat(a) for a in (ya_c, yb_c, yc_c, yd_c)), g_out, m_ctx[2], wo,
                      g_ffn, m_ctx[3], m_ctx[4], m_ctx[5], wfi, wfo, g_fin, False)

    return xl
```

```python
import functools
import math

import numpy as np
import jax
import jax.numpy as jnp
from jax import lax
from jax.experimental import pallas as pl
from jax.experimental.pallas import tpu as pltpu

F32 = jnp.float32
BF16 = jnp.bfloat16
HI = lax.Precision.HIGHEST

D_MODEL = 1024
GRID_W = 64
EPS = 1e-6
ROPE_THETA = 10000.0
GROUP_W = 256
LANES = 128

HY_EMB = 33
HY_BANDS = 16
HY_FFN = 64
HY_NFILT = 1024
HY_DECAY_SHIFT = 0.05
HY_FAST_PCT = 0.3
HY_SLOW_PCT = 1.5
HY_TARGET = 1e-2

MLA_NOPE = 64
MLA_ROPE = 32
MLA_V = 64
MLA_KV_RANK = 128
MLA_SCALE = (MLA_NOPE + MLA_ROPE) ** -0.5

HEAD_D = 64
RET_CHUNK = 128
RET_UNROLL = 4
GQA_SCALE = HEAD_D ** -0.5
RET_K_SCALE = HEAD_D ** -0.5

D_FF = 2816
FF_TILE = 256

PROJ_SPLITS = (768, 256, 256, 1024, 512)
PROJ_OFFSETS = (0, 768, 1024, 1280, 2304, 2816)

VMEM_LIMIT = 56 * 1024 * 1024


def _params(*sem):
    return pltpu.CompilerParams(dimension_semantics=sem, vmem_limit_bytes=VMEM_LIMIT)


def _const_spec(a):
    nd = a.ndim
    return pl.BlockSpec(a.shape, lambda *_: (0,) * nd, pipeline_mode=pl.Buffered(1))


def _rms(x):
    return x * lax.rsqrt(jnp.mean(x * x, axis=-1, keepdims=True) + EPS)


def _rope(x, cf, s1, s2, half):
    w = x.shape[-1]
    return x * cf + pltpu.roll(x, w - half, 1) * s1 + pltpu.roll(x, half, 1) * s2


def _seg_mean(sq, seg):
    hi = sq.astype(BF16)
    lo = (sq - hi.astype(F32)).astype(BF16)
    return (jnp.dot(hi, seg, preferred_element_type=F32) + jnp.dot(lo, seg, preferred_element_type=F32))


def _log_sigmoid(x):
    return -(jnp.maximum(-x, 0.0) + jnp.log(1.0 + jnp.exp(-jnp.abs(x))))


def _ada_kernel(c_ref, w_ref, b_ref, o_ref):
    c = c_ref[...]
    s = c * jax.nn.sigmoid(c)
    o_ref[...] = jnp.dot(s, w_ref[...], precision=HI, preferred_element_type=F32) + b_ref[...]


def _ada_mod(cond, w_mod, b_mod):
    depth, d, n = w_mod.shape
    r = cond.shape[0]
    tn = 1024
    return pl.pallas_call(
        _ada_kernel,
        grid=(depth, n // tn),
        in_specs=[pl.BlockSpec((r, d), lambda l, j: (0, 0)),
                  pl.BlockSpec((None, d, tn), lambda l, j: (l, 0, j)),
                  pl.BlockSpec((None, 1, tn), lambda l, j: (l, 0, j))],
        out_specs=pl.BlockSpec((None, r, tn), lambda l, j: (l, 0, j)),
        out_shape=jax.ShapeDtypeStruct((depth, r, n), F32),
        compiler_params=_params("arbitrary", "arbitrary"),
        name="ada_mod",
    )(cond, w_mod, b_mod.reshape(depth, 1, n))


def _mla_heads(cq, ckvr, gq, gkv, wuq_ref, wk_ref, wv_ref, tables, q_ref, k_ref, v_ref):
    half = MLA_ROPE // 2
    qn = (_rms(cq) * gq).astype(BF16)
    q = jnp.dot(qn, wuq_ref[...], preferred_element_type=F32)
    kvn = (_rms(ckvr[:, :MLA_KV_RANK]) * gkv).astype(BF16)
    k = jnp.dot(kvn, wk_ref[...], preferred_element_type=F32)
    v = jnp.dot(kvn, wv_ref[...], preferred_element_type=F32)
    kpe = pltpu.roll(ckvr[:, MLA_KV_RANK:], MLA_NOPE, 1)
    if tables is not None:
        kpe = _rope(kpe, *tables, half)
    for h in range(q_ref.shape[0]):
        cols = slice(h * LANES, (h + 1) * LANES)
        qh = q[:, cols]
        if tables is not None:
            qh = _rope(qh, *tables, half)
        q_ref[h] = (qh * MLA_SCALE).astype(BF16)
        k_ref[h] = (k[:, cols] + kpe).astype(BF16)
        v_ref[h] = v[:, cols].astype(BF16)


def _gqa_heads(x, gq, gk, seg, tables, q_ref, k_ref, v_ref):
    half = HEAD_D // 2
    lo = lax.broadcasted_iota(jnp.int32, (x.shape[0], LANES), 1) < HEAD_D

    def normed(a, g):
        a = a * lax.rsqrt(_seg_mean(a * a, seg) + EPS) * g
        return _rope(a, *tables, half) if tables is not None else a

    def dup(a, g):
        a = jnp.where(lo if g == 0 else jnp.logical_not(lo), a, 0.0)
        return (a + pltpu.roll(a, HEAD_D, 1)).astype(BF16)

    for p in range(2):
        qb = normed(x[:, p * LANES:(p + 1) * LANES], gq) * GQA_SCALE
        q_ref[2 * p] = jnp.where(lo, qb, 0.0).astype(BF16)
        q_ref[2 * p + 1] = jnp.where(lo, 0.0, qb).astype(BF16)
    kb = normed(x[:, 2 * LANES:3 * LANES], gk)
    vb = x[:, 3 * LANES:4 * LANES]
    for g in range(2):
        k_ref[g] = dup(kb, g)
        v_ref[g] = dup(vb, g)


def _front_kernel(x_ref, g_ref, sh_ref, sc_ref, w_ref, gq_ref, gkv_ref, wuq_ref, wk_ref, wv_ref,
                  ggq_ref, ggk_ref, seg_ref, *rest, rope):
    if rope:
        mla_t = tuple(r[...] for r in rest[0:3])
        hd_t = tuple(r[...] for r in rest[3:6])
        rest = rest[6:]
    else:
        mla_t = hd_t = None
    hy_ref, ret_ref, mq_ref, mk_ref, mv_ref, gq_out, gk_out, gv_out = rest
    y = _rms(x_ref[...]) * g_ref[...]
    h = (y * (1.0 + sc_ref[...]) + sh_ref[...]).astype(BF16)
    o0, o1, o2, o3, o4, o5 = PROJ_OFFSETS

    def proj(a, b):
        return jnp.dot(h, w_ref[:, a:b], preferred_element_type=F32)

    hy_ref[...] = proj(o0, o1)
    for i in range(2 * GROUP_W // LANES):
        cols = slice(i * LANES, (i + 1) * LANES)
        a = proj(o3 + i * LANES, o3 + (i + 1) * LANES)
        if i * LANES >= GROUP_W:
            a = a * RET_K_SCALE
        ret_ref[:, cols] = _rope(a, *hd_t, HEAD_D // 2) if rope else a
    ret_ref[:, 2 * GROUP_W:] = proj(o3 + 2 * GROUP_W, o4)
    _mla_heads(proj(o1, o2), proj(o2, o3), gq_ref[...], gkv_ref[...], wuq_ref, wk_ref, wv_ref, mla_t,
               mq_ref, mk_ref, mv_ref)
    _gqa_heads(proj(o4, o5), ggq_ref[...], ggk_ref[...], seg_ref[...], hd_t, gq_out, gk_out, gv_out)


def _front(x, g, sh, sc, w, mla_w, gqa_w, tables):
    b, l, d = x.shape
    tm = min(512, l)
    nt = l // tm
    rope = tables is not None
    tok = lambda wd: pl.BlockSpec((None, tm, wd), lambda i, j: (i, j, 0))
    vec = pl.BlockSpec((None, 1, d), lambda i, j: (i, 0, 0))
    head = lambda nh: pl.BlockSpec((nh, tm, LANES), lambda i, j: (0, i * nt + j, 0))
    hshape = lambda nh: jax.ShapeDtypeStruct((nh, b * l, LANES), BF16)
    consts = [w, *mla_w, *gqa_w]
    args = [x, g, sh, sc, *consts]
    specs = [tok(d), _const_spec(g), vec, vec] + [_const_spec(a) for a in consts]
    if rope:
        args += [*tables[0], *tables[1]]
        specs += [pl.BlockSpec((tm, LANES), lambda i, j: (j, 0))] * 6
    return pl.pallas_call(
        functools.partial(_front_kernel, rope=rope),
        grid=(b, nt),
        in_specs=specs,
        out_specs=[tok(PROJ_SPLITS[0]), tok(PROJ_SPLITS[3]), head(4), head(4), head(4), head(4), head(2), head(2)],
        out_shape=[jax.ShapeDtypeStruct((b, l, PROJ_SPLITS[0]), F32),
                   jax.ShapeDtypeStruct((b, l, PROJ_SPLITS[3]), F32),
                   hshape(4), hshape(4), hshape(4), hshape(4), hshape(2), hshape(2)],
        compiler_params=_params("parallel", "parallel"),
        name="front",
    )(*args)


def _hy_filter_kernel(z_ref, t_ref, w1_ref, b1_ref, w2_ref, b2_ref, w3_ref, b3_ref, fr_ref,
                      w4_ref, dl_ref, o_ref):
    fr = fr_ref[...]
    dot = functools.partial(jnp.dot, precision=HI, preferred_element_type=F32)
    h = jnp.sin(fr * (dot(z_ref[...], w1_ref[...]) + b1_ref[...]))
    h = jnp.sin(fr * (dot(h, w2_ref[...]) + b2_ref[...]))
    h = jnp.sin(fr * (dot(h, w3_ref[...]) + b3_ref[...]))
    filt = dot(h, w4_ref[...]) * (jnp.exp(-t_ref[...] * dl_ref[...]) + HY_DECAY_SHIFT)
    fwd = filt[:, :GROUP_W]
    bwd = filt[:, GROUP_W:]
    row = lax.broadcasted_iota(jnp.int32, bwd.shape, 0)
    bwd = jnp.where(row == 0, 0.0, bwd)
    nrm = (jnp.sum(jnp.abs(fwd), axis=0, keepdims=True) + jnp.sum(jnp.abs(bwd), axis=0, keepdims=True))
    o_ref[:, :GROUP_W] = fwd / nrm
    o_ref[:, GROUP_W:] = bwd / nrm


def _hy_filters(l, z, t, w1, b1, w2, b2, w3, b3, fr, w4, dl):
    full = lambda a: pl.BlockSpec(a.shape, lambda o: (0, 0))
    return pl.pallas_call(
        _hy_filter_kernel,
        grid=(2,),
        in_specs=[full(z), full(t), full(w1), full(b1), full(w2), full(b2), full(w3), full(b3), full(fr),
                  pl.BlockSpec((HY_FFN, 2 * GROUP_W), lambda o: (0, o)),
                  pl.BlockSpec((1, 2 * GROUP_W), lambda o: (0, o))],
        out_specs=pl.BlockSpec((l, 2 * GROUP_W), lambda o: (0, o)),
        out_shape=jax.ShapeDtypeStruct((l, HY_NFILT), F32),
        compiler_params=_params("arbitrary"),
        name="hy_filters",
    )(z, t, w1, b1, w2, b2, w3, b3, fr, w4, dl)


def _hy_spec_kernel(c_ref, s_ref, h_ref, sign_ref, a_ref, b_ref, d_ref):
    dot = functools.partial(jnp.dot, precision=HI, preferred_element_type=F32)
    row = lax.broadcasted_iota(jnp.int32, (c_ref.shape[0], GROUP_W), 0)
    first = jnp.logical_and(pl.program_id(0) == 0, row == 0)
    for o in range(2):
        hf = h_ref[:, 2 * o * GROUP_W:(2 * o + 1) * GROUP_W]
        hb = h_ref[:, (2 * o + 1) * GROUP_W:(2 * o + 2) * GROUP_W]
        hs = hf + hb
        re = dot(c_ref[...], hs)
        im = dot(s_ref[...], hf - hb)
        ny = jnp.sum(sign_ref[...] * hs, axis=0, keepdims=True)
        cols = slice(o * GROUP_W, (o + 1) * GROUP_W)
        a_ref[:, cols] = re
        b_ref[:, cols] = jnp.where(first, 0.0, im)
        d_ref[:, cols] = jnp.where(first, ny, re)


def _hy_spectrum(h, c32, s32, sign):
    l = h.shape[0]
    fb = min(256, l)
    out = jax.ShapeDtypeStruct((l, 2 * GROUP_W), F32)
    blk = pl.BlockSpec((fb, 2 * GROUP_W), lambda f: (f, 0))
    return pl.pallas_call(
        _hy_spec_kernel,
        grid=(l // fb,),
        in_specs=[pl.BlockSpec((fb, l), lambda f: (f, 0)), pl.BlockSpec((fb, l), lambda f: (f, 0)),
                  pl.BlockSpec(h.shape, lambda f: (0, 0)), pl.BlockSpec((l, 1), lambda f: (0, 0))],
        out_specs=[blk, blk, blk],
        out_shape=[out, out, out],
        compiler_params=_params("arbitrary"),
        name="hy_spectrum",
    )(c32, s32, h, sign)


def _short_conv(u, w, b):
    n = u.shape[0]
    row = lax.broadcasted_iota(jnp.int32, u.shape, 0)
    up = jnp.where(row == 0, 0.0, pltpu.roll(u, 1, 0))
    dn = jnp.where(row == n - 1, 0.0, pltpu.roll(u, n - 1, 0))
    return w[0:1] * up + w[1:2] * u + w[2:3] * dn + b


def _hy_conv_kernel(sig_ref, gate_ref, cw_ref, cb_ref, a_ref, b_ref, d_ref, bias_ref,
                    c_ref, s_ref, ci_ref, si_ref, o_ref, u_scr, ub_scr, acc_scr,
                    *, nb, sig_col, gate_col):
    f = pl.program_id(1)
    w = GROUP_W

    @pl.when(f == 0)
    def _():
        for i in range(nb):
            u = sig_ref[i]
            if sig_col is not None:
                u = _short_conv(u, cw_ref[:, sig_col * w:(sig_col + 1) * w], cb_ref[:, sig_col * w:(sig_col + 1) * w])
            u_scr[:, i * w:(i + 1) * w] = u
            ub_scr[:, i * w:(i + 1) * w] = u.astype(BF16)
        acc_scr[...] = jnp.zeros_like(acc_scr)

    ub = ub_scr[...]
    xr = jnp.dot(c_ref[...], ub, preferred_element_type=F32)
    xi = jnp.dot(s_ref[...], ub, preferred_element_type=F32)
    ka, kb, kd = a_ref[...], b_ref[...], d_ref[...]
    yr, yi = [], []
    for i in range(nb):
        r = xr[:, i * w:(i + 1) * w]
        m = xi[:, i * w:(i + 1) * w]
        yr.append((r * ka - m * kb).astype(BF16))
        yi.append((r * kb + m * kd).astype(BF16))
    yr = jnp.concatenate(yr, axis=1) if nb > 1 else yr[0]
    yi = jnp.concatenate(yi, axis=1) if nb > 1 else yi[0]
    acc_scr[...] += (jnp.dot(ci_ref[...], yr, preferred_element_type=F32)
                     + jnp.dot(si_ref[...], yi, preferred_element_type=F32))

    @pl.when(f == pl.num_programs(1) - 1)
    def _():
        for i in range(nb):
            y = acc_scr[:, i * w:(i + 1) * w] + u_scr[:, i * w:(i + 1) * w] * bias_ref[...]
            g = _short_conv(gate_ref[i], cw_ref[:, gate_col * w:(gate_col + 1) * w],
                            cb_ref[:, gate_col * w:(gate_col + 1) * w])
            o_ref[i] = g * y


def _hy_conv(sig, sig_col, hy, gate_col, cw, cb, spec, order, bias, dft):
    b, l, _ = hy.shape
    w = GROUP_W
    nb = 2 if l >= 2048 else math.gcd(b, 8)
    fb = min(256, l)
    c16, s16, ci16, si16 = dft
    a, bm, d = spec
    sig_arr = hy if sig_col is not None else sig
    sig_blk = sig_col if sig_col is not None else 0
    spec_blk = pl.BlockSpec((fb, w), lambda i, f: (f, order))
    kernel = functools.partial(_hy_conv_kernel, nb=nb, sig_col=sig_col, gate_col=gate_col)
    return pl.pallas_call(
        kernel,
        grid=(b // nb, l // fb),
        in_specs=[pl.BlockSpec((nb, l, w), lambda i, f: (i, 0, sig_blk)),
                  pl.BlockSpec((nb, l, w), lambda i, f: (i, 0, gate_col)),
                  pl.BlockSpec(cw.shape, lambda i, f: (0, 0)),
                  pl.BlockSpec(cb.shape, lambda i, f: (0, 0)),
                  spec_blk, spec_blk, spec_blk,
                  pl.BlockSpec((1, w), lambda i, f: (0, 0)),
                  pl.BlockSpec((fb, l), lambda i, f: (f, 0)),
                  pl.BlockSpec((fb, l), lambda i, f: (f, 0)),
                  pl.BlockSpec((l, fb), lambda i, f: (0, f)),
                  pl.BlockSpec((l, fb), lambda i, f: (0, f))],
        out_specs=pl.BlockSpec((nb, l, w), lambda i, f: (i, 0, 0)),
        out_shape=jax.ShapeDtypeStruct((b, l, w), F32),
        scratch_shapes=[pltpu.VMEM((l, nb * w), F32), pltpu.VMEM((l, nb * w), BF16),
                        pltpu.VMEM((l, nb * w), F32)],
        compiler_params=_params("parallel", "arbitrary"),
        name="hy_conv",
    )(sig_arr, hy, cw, cb, a, bm, d, bias, c16, s16, ci16, si16)


@functools.lru_cache(maxsize=None)
def _dft_tables(l):
    n = 2 * l
    k = np.arange(l, dtype=np.int64)
    ang = ((k[:, None] * k[None, :]) % n).astype(np.float64) * (2.0 * math.pi / n)
    sign = np.where(k % 2 == 0, 1.0, -1.0)
    c = np.cos(ang)
    s = -np.sin(ang)
    s[0, :] = sign
    wk = np.where(k == 0, 1.0, 2.0) / n
    ci = c.T * wk[None, :]
    si = s.T * wk[None, :]
    f32 = lambda a: np.ascontiguousarray(a, dtype=np.float32)
    b16 = lambda a: f32(a).astype(BF16)
    return (f32(c), f32(s), f32(sign[:, None])), (b16(c), b16(s), b16(ci), b16(si))


def _hy_features(l):
    t = jnp.linspace(0.0, 1.0, l, dtype=F32)[:, None]
    wpos = 2.0 * math.pi * jnp.arange(l, dtype=F32)[:, None] / l
    fb = jnp.linspace(1e-4, HY_BANDS - 1, HY_BANDS, dtype=F32)[None, :]
    z = jnp.concatenate([t, jnp.cos(fb * wpos), -jnp.sin(fb * wpos)], axis=-1)
    z = jnp.pad(z, ((0, 0), (0, HY_FFN - HY_EMB)))
    deltas = jnp.linspace(math.log(HY_TARGET) / HY_FAST_PCT, math.log(HY_TARGET) / HY_SLOW_PCT, HY_NFILT, dtype=F32)
    return z, t, jnp.abs(deltas)[None, :]


def _hyena(hy, feats, dft32, dft16, cw, cb, w1, b1, w2, b2, w3, b3, fr, w4, bias):
    l = hy.shape[1]
    z, t, dl = feats
    w1p = jnp.pad(w1, ((0, HY_FFN - HY_EMB), (0, 0)))
    h = _hy_filters(l, z, t, w1p, b1[None], w2, b2[None], w3, b3[None], fr[None], w4, dl)
    spec = _hy_spectrum(h, *dft32)
    cb2 = cb[None]
    zz = _hy_conv(None, 0, hy, 1, cw, cb2, spec, 0, bias[0:1], dft16)
    return _hy_conv(zz, None, hy, 2, cw, cb2, spec, 1, bias[1:2], dft16)


def _rope_tables(l, dim, width, off, reps):
    rows = jnp.repeat(jnp.arange(l // GRID_W, dtype=jnp.int32), GRID_W).astype(F32)
    cols = jnp.tile(jnp.arange(GRID_W, dtype=jnp.int32), l // GRID_W).astype(F32)
    quarter = dim // 4
    half = dim // 2
    inv = ROPE_THETA ** (-jnp.arange(quarter, dtype=F32) / quarter)
    ang = jnp.concatenate([rows[:, None] * inv, cols[:, None] * inv], axis=-1)
    c, s = jnp.cos(ang), jnp.sin(ang)
    cf = jnp.ones((l, width), F32).at[:, off:off + dim].set(jnp.concatenate([c, c], axis=-1))
    s1 = jnp.zeros((l, width), F32).at[:, off:off + half].set(-s)
    s2 = jnp.zeros((l, width), F32).at[:, off + half:off + dim].set(s)
    return tuple(jnp.tile(a, (1, reps)) for a in (cf, s1, s2))


def _attn_kernel(q_ref, *rest, rep):
    o_ref = rest[-1]
    kv = rest[:-1]
    ks, vs = kv[0::2], kv[1::2]
    tq = q_ref.shape[1]
    lo = lax.broadcasted_iota(jnp.int32, (tq, LANES), 1) < HEAD_D
    nt = (((1,), (1,)), ((), ()))
    for p in range(2):
        outs = []
        for r in range(2):
            h = 2 * p + r
            g = h // rep
            q = q_ref[h]
            ss = [lax.dot_general(q, k_ref[g], nt, preferred_element_type=F32) for k_ref in ks]
            m = ss[0].max(axis=-1, keepdims=True)
            for s in ss[1:]:
                m = jnp.maximum(m, s.max(axis=-1, keepdims=True))
            den = 0.0
            acc = 0.0
            for s, v_ref in zip(ss, vs):
                e = jnp.exp(s - m)
                den = den + e.sum(axis=-1, keepdims=True)
                acc = acc + jnp.dot(e.astype(BF16), v_ref[g], preferred_element_type=F32)
            outs.append(acc / den)
        o_ref[:, p * LANES:(p + 1) * LANES] = jnp.where(lo, outs[0], outs[1])


def _attention(q, kvs, rep, b, l):
    nh = q.shape[0]
    tq = min(512, l)
    nq = l // tq
    specs = [pl.BlockSpec((nh, tq, LANES), lambda i, j: (0, i * nq + j, 0))]
    args = [q]
    for k, v, lk in kvs:
        spec = pl.BlockSpec((k.shape[0], lk, LANES), lambda i, j: (0, i, 0))
        specs += [spec, spec]
        args += [k, v]
    return pl.pallas_call(
        functools.partial(_attn_kernel, rep=rep),
        grid=(b, nq),
        in_specs=specs,
        out_specs=pl.BlockSpec((None, tq, GROUP_W), lambda i, j: (i, j, 0)),
        out_shape=jax.ShapeDtypeStruct((b, l, GROUP_W), F32),
        compiler_params=_params("parallel", "parallel"),
        name="attention",
    )(*args)


def _ret_kernel(qs, ks, v_ref, g_ref, dl_ref, dc_ref, ng_ref, seg_ref, sf0_ref, sb0_ref,
                y_ref, sf_ref, sb_ref, sball, *, chunk, unroll):
    c_ = chunk
    w = GROUP_W
    n_chunks = qs.shape[0] // c_
    unroll = min(unroll, n_chunks)
    shift_c = int(math.log2(c_))
    shift_h = int(math.log2(HEAD_D))
    tn = (((0,), (0,)), ((), ()))
    nt = (((1,), (1,)), ((), ()))

    lgf = _log_sigmoid(dl_ref[0:1, :])
    lgb = _log_sigmoid(dl_ref[1:2, :])
    ii = lax.broadcasted_iota(jnp.int32, (c_, 1), 0).astype(F32)
    qdf = jnp.exp((ii + 1.0) * lgf)
    kdf = jnp.exp((c_ - 1.0 - ii) * lgf)
    cdf = jnp.exp(float(c_) * lgf)
    qdb = jnp.exp((c_ - ii) * lgb)
    kdb = jnp.exp(ii * lgb)
    cdb = jnp.exp(float(c_) * lgb)
    r_h = lax.shift_right_logical(lax.broadcasted_iota(jnp.int32, (w, w), 0), shift_h)
    c_h = lax.shift_right_logical(lax.broadcasted_iota(jnp.int32, (w, w), 1), shift_h)
    blockdiag = r_h == c_h
    lgc = _log_sigmoid(dc_ref[...])
    i4 = jnp.bitwise_and(lax.broadcasted_iota(jnp.int32, (4 * c_, c_), 0), c_ - 1)
    j4 = lax.broadcasted_iota(jnp.int32, (4 * c_, c_), 1)
    diff = (i4 - j4).astype(F32)
    dmat = (jnp.where(diff >= 0, jnp.exp(lgc[:, 0:1] * jnp.maximum(diff, 0.0)), 0.0)
            + jnp.where(diff <= 0, jnp.exp(lgc[:, 1:2] * jnp.maximum(-diff, 0.0)), 0.0))
    row_h = lax.shift_right_logical(lax.broadcasted_iota(jnp.int32, (4 * c_, w), 0), shift_c)
    lane_h = lax.shift_right_logical(lax.broadcasted_iota(jnp.int32, (4 * c_, w), 1), shift_h)
    headmask = row_h == lane_h
    seg = seg_ref[...]

    def rows(c):
        return pl.ds(pl.multiple_of(c * c_, c_), c_)

    def kv_outer(k, v, dec):
        a = lax.dot_general((k * dec).astype(BF16), v.astype(BF16), tn, preferred_element_type=F32)
        return jnp.where(blockdiag, a, 0.0)

    def bwd(t, s):
        c = n_chunks - 1 - t
        sball[c] = s
        sl = rows(c)
        return s * cdb + kv_outer(ks[sl, :], v_ref[sl, :], kdb)

    sb_ref[...] = lax.fori_loop(0, n_chunks, bwd, sb0_ref[...], unroll=unroll)

    def fwd(c, s):
        sl = rows(c)
        q, k, v = qs[sl, :], ks[sl, :], v_ref[sl, :]
        vb = v.astype(BF16)
        q4 = jnp.where(headmask, jnp.concatenate([q, q, q, q], axis=0), 0.0).astype(BF16)
        sc = lax.dot_general(q4, k.astype(BF16), nt, preferred_element_type=F32)
        o4 = jnp.dot((sc * dmat).astype(BF16), vb, preferred_element_type=F32)
        o4 = jnp.where(headmask, o4, 0.0)
        o = o4[0:c_] + o4[c_:2 * c_] + o4[2 * c_:3 * c_] + o4[3 * c_:4 * c_]
        o = o + jnp.dot((q * qdf).astype(BF16), s.astype(BF16), preferred_element_type=F32)
        o = o + jnp.dot((q * qdb).astype(BF16), sball[c].astype(BF16), preferred_element_type=F32)
        on = o * lax.rsqrt(_seg_mean(o * o, seg) + EPS) * ng_ref[...]
        g = g_ref[sl, :]
        y_ref[sl, :] = on * (g * jax.nn.sigmoid(g))
        return s * cdf + kv_outer(k, v, kdf)

    sf_ref[...] = lax.fori_loop(0, n_chunks, fwd, sf0_ref[...], unroll=unroll)


def _retention(ret, dl, dc, ng, seg, sf0, sb0):
    b, l, _ = ret.shape
    w = GROUP_W
    col = lambda j: pl.BlockSpec((None, l, w), lambda i: (i, 0, j))
    state = pl.BlockSpec((None, w, w), lambda i: (i, 0, 0))
    sshape = jax.ShapeDtypeStruct((b, w, w), F32)
    return pl.pallas_call(
        functools.partial(_ret_kernel, chunk=RET_CHUNK, unroll=RET_UNROLL),
        grid=(b,),
        in_specs=[col(0), col(1), col(2), col(3), _const_spec(dl), _const_spec(dc), _const_spec(ng),
                  _const_spec(seg), state, state],
        out_specs=[pl.BlockSpec((None, l, w), lambda i: (i, 0, 0)), state, state],
        out_shape=[jax.ShapeDtypeStruct((b, l, w), F32), sshape, sshape],
        scratch_shapes=[pltpu.VMEM((l // RET_CHUNK, w, w), F32)],
        compiler_params=_params("parallel"),
        name="retention",
    )(ret, ret, ret, ret, dl, dc, ng, seg, sf0, sb0)


def _mlp_kernel(x_ref, ya_ref, yb_ref, yc_ref, yd_ref, gout_ref, ga_ref, wout_ref,
                gffn_ref, sh_ref, sc_ref, gf_ref, wfi_ref, wfo_ref, fg_ref, o_ref,
                x1_scr, h_scr, acc_scr, *, final_norm):
    acc = 0.0
    for i, y_ref in enumerate((ya_ref, yb_ref, yc_ref, yd_ref)):
        rows = slice(i * GROUP_W, (i + 1) * GROUP_W)
        yn = (_rms(y_ref[...]) * gout_ref[:, rows]).astype(BF16)
        acc = acc + jnp.dot(yn, wout_ref[rows, :], preferred_element_type=F32)
    x1 = x_ref[...] + ga_ref[...] * acc
    x1_scr[...] = x1
    h_scr[...] = (_rms(x1) * gffn_ref[...] * (1.0 + sc_ref[...]) + sh_ref[...]).astype(BF16)
    acc_scr[...] = jnp.zeros_like(acc_scr)
    nff = wfo_ref.shape[0]

    def body(c, carry):
        h = h_scr[...]
        a = jnp.dot(h, wfi_ref[c], preferred_element_type=F32)
        g = jnp.dot(h, wfi_ref[c + nff], preferred_element_type=F32)
        u = (a * jax.nn.sigmoid(a) * g).astype(BF16)
        acc_scr[...] += jnp.dot(u, wfo_ref[c], preferred_element_type=F32)
        return carry

    lax.fori_loop(0, nff, body, 0)
    out = x1_scr[...] + gf_ref[...] * acc_scr[...]
    if final_norm:
        out = _rms(out) * fg_ref[...]
    o_ref[...] = out


def _mlp(x, ys, g_out, g_a, w_out, g_ffn, sh, sc, g_f, wfi, wfo, final_g, final_norm):
    b, l, d = x.shape
    tm = min(512, l)
    tok = lambda wd: pl.BlockSpec((None, tm, wd), lambda i, j: (i, j, 0))
    vec = pl.BlockSpec((None, 1, d), lambda i, j: (i, 0, 0))
    return pl.pallas_call(
        functools.partial(_mlp_kernel, final_norm=final_norm),
        grid=(b, l // tm),
        in_specs=[tok(d)] + [tok(GROUP_W)] * 4 + [
            _const_spec(g_out), vec, _const_spec(w_out), _const_spec(g_ffn), vec, vec, vec,
            _const_spec(wfi), _const_spec(wfo), _const_spec(final_g)],
        out_specs=tok(d),
        out_shape=jax.ShapeDtypeStruct((b, l, d), F32),
        scratch_shapes=[pltpu.VMEM((tm, d), F32), pltpu.VMEM((tm, d), BF16), pltpu.VMEM((tm, d), F32)],
        compiler_params=_params("parallel", "parallel"),
        name="mlp",
    )(x, *ys, g_out, g_a, w_out, g_ffn, sh, sc, g_f, wfi, wfo, final_g)


def _pad_proj(w):
    z = jnp.zeros((w.shape[0], LANES - MLA_ROPE), w.dtype)
    return jnp.concatenate([w[:, :1184], z, w[:, 1184:]], axis=1).astype(BF16)


def _mla_weights(wuq, wukv):
    r = wuq.shape[0]
    q = wuq.reshape(r, 4, MLA_NOPE + MLA_ROPE)
    q = jnp.pad(q, ((0, 0), (0, 0), (0, LANES - MLA_NOPE - MLA_ROPE))).reshape(r, 4 * LANES)
    kv = wukv.reshape(wukv.shape[0], 4, MLA_NOPE + MLA_V)
    k = jnp.pad(kv[..., :MLA_NOPE], ((0, 0), (0, 0), (0, LANES - MLA_NOPE))).reshape(-1, 4 * LANES)
    v = jnp.concatenate([kv[..., MLA_NOPE:], kv[..., MLA_NOPE:]], axis=-1).reshape(-1, 4 * LANES)
    return q.astype(BF16), k.astype(BF16), v.astype(BF16)


def _ffn_weights(w_in, w_out):
    d = w_in.shape[0]
    nff = D_FF // FF_TILE
    wfi = w_in.astype(BF16).reshape(d, 2 * nff, FF_TILE).transpose(1, 0, 2)
    return wfi, w_out.astype(BF16).reshape(nff, FF_TILE, d)


def _seg_matrix(width):
    i = jnp.arange(width) // HEAD_D
    return jnp.where(i[:, None] == i[None, :], 1.0 / HEAD_D, 0.0).astype(BF16)


def kernel(x, c, ctx, c_ctx, w_mod, b_mod, norm_attn_g, norm_ffn_g, w_in, hy_conv_w, hy_conv_b, hy_w1, hy_b1, hy_w2, hy_b2, hy_w3, hy_b3, hy_freq, hy_w4, hy_bias, mla_q_norm_g, mla_w_uq, mla_kv_norm_g, mla_w_ukv, ret_decay, ret_norm_g, gqa_q_norm_g, gqa_k_norm_g, out_norm_g, w_out, w_ffn_in, w_ffn_out, final_norm_g):
    b, n_lat, d = x.shape
    n_ctx = ctx.shape[1]
    depth = w_mod.shape[0]

    rope_mla = _rope_tables(n_lat, MLA_ROPE, LANES, MLA_NOPE, 1)
    rope_hd2 = _rope_tables(n_lat, HEAD_D, HEAD_D, 0, 2)
    seg2, seg4 = _seg_matrix(LANES), _seg_matrix(GROUP_W)
    hy_consts = {n: (_hy_features(n),) + _dft_tables(n) for n in {n_lat, n_ctx}}
    zero_state = jnp.zeros((b, GROUP_W, GROUP_W), F32)

    rows = -(-(b + 1) // 8) * 8
    cond = jnp.zeros((rows, d), F32).at[:b].set(c).at[b].set(c_ctx)
    mod = _ada_mod(cond, w_mod, b_mod)

    xl, xc = x, ctx.reshape(1, b * n_ctx, d)
    for l in range(depth):
        update_ctx = l < depth - 1
        m_lat = [m[:b, None, :] for m in jnp.split(mod[l], 6, axis=-1)]
        m_ctx = [m[b][None, None, :] for m in jnp.split(mod[l], 6, axis=-1)]
        wp = _pad_proj(w_in[l])
        g_attn = norm_attn_g[l][None]
        mla_w = (mla_q_norm_g[l][None], mla_kv_norm_g[l][None]) + _mla_weights(mla_w_uq[l], mla_w_ukv[l])
        gqa_w = (jnp.tile(gqa_q_norm_g[l], 2)[None], jnp.tile(gqa_k_norm_g[l], 2)[None], seg2)
        hy_l, ret_l, q_l, k_l, v_l, gq_l, gk_l, gv_l = _front(
            xl, g_attn, m_lat[0], m_lat[1], wp, mla_w, gqa_w, (rope_mla, rope_hd2))
        hy_c, ret_c, q_c, k_c, v_c, gq_c, gk_c, gv_c = _front(
            xc, g_attn, m_ctx[0], m_ctx[1], wp, mla_w, gqa_w, None)
        hy_c = hy_c.reshape(b, n_ctx, -1)
        ret_c = ret_c.reshape(b, n_ctx, -1)

        hy_args = (hy_conv_w[l], hy_conv_b[l], hy_w1[l], hy_b1[l], hy_w2[l], hy_b2[l], hy_w3[l], hy_b3[l],
                   hy_freq[l], hy_w4[l], hy_bias[l])
        ya_l = _hyena(hy_l, *hy_consts[n_lat], *hy_args)
        yb_l = _attention(q_l, [(k_c, v_c, n_ctx), (k_l, v_l, n_lat)], 1, b, n_lat)

        dl = jnp.repeat(ret_decay[l], HEAD_D, axis=1)
        dc = jnp.repeat(ret_decay[l].T, RET_CHUNK, axis=0)
        ng = ret_norm_g[l][None]
        yc_c, s_f, s_b = _retention(ret_c, dl, dc, ng, seg4, zero_state, zero_state)
        yc_l, _, _ = _retention(ret_l, dl, dc, ng, seg4, s_f, s_b)
        yd_l = _attention(gq_l, [(gk_c, gv_c, n_ctx), (gk_l, gv_l, n_lat)], 2, b, n_lat)

        wo = w_out[l].astype(BF16)
        wfi, wfo = _ffn_weights(w_ffn_in[l], w_ffn_out[l])
        g_out, g_ffn, g_fin = out_norm_g[l][None], norm_ffn_g[l][None], final_norm_g[None]
        xl = _mlp(xl, (ya_l, yb_l, yc_l, yd_l), g_out, m_lat[2], wo, g_ffn, m_lat[3], m_lat[4], m_lat[5],
                  wfi, wfo, g_fin, l == depth - 1)

        if update_ctx:
            flat = lambda a: a.reshape(1, b * n_ctx, GROUP_W)
            ya_c = _hyena(hy_c, *hy_consts[n_ctx], *hy_args)
            yb_c = _attention(q_c, [(k_c, v_c, n_ctx)], 1, b, n_ctx)
            yd_c = _attention(gq_c, [(gk_c, gv_c, n_ctx)], 2, b, n_ctx)
            xc = _mlp(xc, tuple(flat(a) for a in (ya_c, yb_c, yc_c, yd_c)), g_out, m_ctx[2], wo,
                      g_ffn, m_ctx[3], m_ctx[4], m_ctx[5], wfi, wfo, g_fin, False)

    return xl
```

```python
import functools
import math

import numpy as np
import jax
import jax.numpy as jnp
from jax import lax
from jax.experimental import pallas as pl
from jax.experimental.pallas import tpu as pltpu

F32 = jnp.float32
BF16 = jnp.bfloat16
HI = lax.Precision.HIGHEST

D_MODEL = 1024
GRID_W = 64
EPS = 1e-6
ROPE_THETA = 10000.0
GROUP_W = 256
LANES = 128

HY_EMB = 33
HY_BANDS = 16
HY_FFN = 64
HY_NFILT = 1024
HY_DECAY_SHIFT = 0.05
HY_FAST_PCT = 0.3
HY_SLOW_PCT = 1.5
HY_TARGET = 1e-2

MLA_NOPE = 64
MLA_ROPE = 32
MLA_V = 64
MLA_KV_RANK = 128
MLA_SCALE = (MLA_NOPE + MLA_ROPE) ** -0.5

HEAD_D = 64
RET_CHUNK = 128
RET_UNROLL = 4
GQA_SCALE = HEAD_D ** -0.5
RET_K_SCALE = HEAD_D ** -0.5

D_FF = 2816
FF_TILE = 256

PROJ_SPLITS = (768, 256, 256, 1024, 512)
PROJ_OFFSETS = (0, 768, 1024, 1280, 2304, 2816)

VMEM_LIMIT = 56 * 1024 * 1024


def _params(*sem):
    return pltpu.CompilerParams(dimension_semantics=sem, vmem_limit_bytes=VMEM_LIMIT)


def _const_spec(a):
    nd = a.ndim
    return pl.BlockSpec(a.shape, lambda *_: (0,) * nd, pipeline_mode=pl.Buffered(1))


def _rms(x):
    return x * lax.rsqrt(jnp.mean(x * x, axis=-1, keepdims=True) + EPS)


def _rope(x, cf, s1, s2, half):
    w = x.shape[-1]
    return x * cf + pltpu.roll(x, w - half, 1) * s1 + pltpu.roll(x, half, 1) * s2


def _seg_mean(sq, seg):
    hi = sq.astype(BF16)
    lo = (sq - hi.astype(F32)).astype(BF16)
    return (jnp.dot(hi, seg, preferred_element_type=F32) + jnp.dot(lo, seg, preferred_element_type=F32))


def _log_sigmoid(x):
    return -(jnp.maximum(-x, 0.0) + jnp.log(1.0 + jnp.exp(-jnp.abs(x))))


def _ada_kernel(c_ref, w_ref, b_ref, o_ref):
    c = c_ref[...]
    s = c * jax.nn.sigmoid(c)
    o_ref[...] = jnp.dot(s, w_ref[...], precision=HI, preferred_element_type=F32) + b_ref[...]


def _ada_mod(cond, w_mod, b_mod):
    depth, d, n = w_mod.shape
    r = cond.shape[0]
    tn = 1024
    return pl.pallas_call(
        _ada_kernel,
        grid=(depth, n // tn),
        in_specs=[pl.BlockSpec((r, d), lambda l, j: (0, 0)),
                  pl.BlockSpec((None, d, tn), lambda l, j: (l, 0, j)),
                  pl.BlockSpec((None, 1, tn), lambda l, j: (l, 0, j))],
        out_specs=pl.BlockSpec((None, r, tn), lambda l, j: (l, 0, j)),
        out_shape=jax.ShapeDtypeStruct((depth, r, n), F32),
        compiler_params=_params("arbitrary", "arbitrary"),
        name="ada_mod",
    )(cond, w_mod, b_mod.reshape(depth, 1, n))


def _mla_heads(cq, ckvr, gq, gkv, wuq_ref, wk_ref, wv_ref, tables, q_ref, k_ref, v_ref):
    half = MLA_ROPE // 2
    qn = (_rms(cq) * gq).astype(BF16)
    q = jnp.dot(qn, wuq_ref[...], preferred_element_type=F32)
    kvn = (_rms(ckvr[:, :MLA_KV_RANK]) * gkv).astype(BF16)
    k = jnp.dot(kvn, wk_ref[...], preferred_element_type=F32)
    v = jnp.dot(kvn, wv_ref[...], preferred_element_type=F32)
    kpe = pltpu.roll(ckvr[:, MLA_KV_RANK:], MLA_NOPE, 1)
    if tables is not None:
        kpe = _rope(kpe, *tables, half)
    for h in range(q_ref.shape[0]):
        cols = slice(h * LANES, (h + 1) * LANES)
        qh = q[:, cols]
        if tables is not None:
            qh = _rope(qh, *tables, half)
        q_ref[h] = (qh * MLA_SCALE).astype(BF16)
        k_ref[h] = (k[:, cols] + kpe).astype(BF16)
        v_ref[h] = v[:, cols].astype(BF16)


def _gqa_heads(x, gq, gk, seg, tables, q_ref, k_ref, v_ref):
    half = HEAD_D // 2
    lo = lax.broadcasted_iota(jnp.int32, (x.shape[0], LANES), 1) < HEAD_D

    def normed(a, g):
        a = a * lax.rsqrt(_seg_mean(a * a, seg) + EPS) * g
        return _rope(a, *tables, half) if tables is not None else a

    def dup(a, g):
        a = jnp.where(lo if g == 0 else jnp.logical_not(lo), a, 0.0)
        return (a + pltpu.roll(a, HEAD_D, 1)).astype(BF16)

    for p in range(2):
        qb = normed(x[:, p * LANES:(p + 1) * LANES], gq) * GQA_SCALE
        q_ref[2 * p] = jnp.where(lo, qb, 0.0).astype(BF16)
        q_ref[2 * p + 1] = jnp.where(lo, 0.0, qb).astype(BF16)
    kb = normed(x[:, 2 * LANES:3 * LANES], gk)
    vb = x[:, 3 * LANES:4 * LANES]
    for g in range(2):
        k_ref[g] = dup(kb, g)
        v_ref[g] = dup(vb, g)


def _front_kernel(x_ref, g_ref, sh_ref, sc_ref, w_ref, gq_ref, gkv_ref, wuq_ref, wk_ref, wv_ref,
                  ggq_ref, ggk_ref, seg_ref, *rest, rope):
    if rope:
        mla_t = tuple(r[...] for r in rest[0:3])
        hd_t = tuple(r[...] for r in rest[3:6])
        rest = rest[6:]
    else:
        mla_t = hd_t = None
    hy_ref, ret_ref, mq_ref, mk_ref, mv_ref, gq_out, gk_out, gv_out = rest
    y = _rms(x_ref[...]) * g_ref[...]
    h = (y * (1.0 + sc_ref[...]) + sh_ref[...]).astype(BF16)
    o0, o1, o2, o3, o4, o5 = PROJ_OFFSETS

    def proj(a, b):
        return jnp.dot(h, w_ref[:, a:b], preferred_element_type=F32)

    _mla_heads(proj(o1, o2), proj(o2, o3), gq_ref[...], gkv_ref[...], wuq_ref, wk_ref, wv_ref, mla_t,
               mq_ref, mk_ref, mv_ref)
    _gqa_heads(proj(o4, o5), ggq_ref[...], ggk_ref[...], seg_ref[...], hd_t, gq_out, gk_out, gv_out)
    for i in range(2 * GROUP_W // LANES):
        cols = slice(i * LANES, (i + 1) * LANES)
        a = proj(o3 + i * LANES, o3 + (i + 1) * LANES)
        if i * LANES >= GROUP_W:
            a = a * RET_K_SCALE
        ret_ref[:, cols] = _rope(a, *hd_t, HEAD_D // 2) if rope else a
    ret_ref[:, 2 * GROUP_W:] = proj(o3 + 2 * GROUP_W, o4)
    hy_ref[...] = proj(o0, o1)


def _front(x, g, sh, sc, w, mla_w, gqa_w, tables):
    b, l, d = x.shape
    tm = min(512, l)
    nt = l // tm
    rope = tables is not None
    tok = lambda wd: pl.BlockSpec((None, tm, wd), lambda i, j: (i, j, 0))
    vec = pl.BlockSpec((None, 1, d), lambda i, j: (i, 0, 0))
    head = lambda nh: pl.BlockSpec((nh, tm, LANES), lambda i, j: (0, i * nt + j, 0))
    hshape = lambda nh: jax.ShapeDtypeStruct((nh, b * l, LANES), BF16)
    consts = [w, *mla_w, *gqa_w]
    args = [x, g, sh, sc, *consts]
    specs = [tok(d), _const_spec(g), vec, vec] + [_const_spec(a) for a in consts]
    if rope:
        args += [*tables[0], *tables[1]]
        specs += [pl.BlockSpec((tm, LANES), lambda i, j: (j, 0))] * 6
    return pl.pallas_call(
        functools.partial(_front_kernel, rope=rope),
        grid=(b, nt),
        in_specs=specs,
        out_specs=[tok(PROJ_SPLITS[0]), tok(PROJ_SPLITS[3]), head(4), head(4), head(4), head(4), head(2), head(2)],
        out_shape=[jax.ShapeDtypeStruct((b, l, PROJ_SPLITS[0]), F32),
                   jax.ShapeDtypeStruct((b, l, PROJ_SPLITS[3]), F32),
                   hshape(4), hshape(4), hshape(4), hshape(4), hshape(2), hshape(2)],
        compiler_params=_params("parallel", "parallel"),
        name="front",
    )(*args)


def _hy_filter_kernel(z_ref, t_ref, w1_ref, b1_ref, w2_ref, b2_ref, w3_ref, b3_ref, fr_ref,
                      w4_ref, dl_ref, o_ref):
    fr = fr_ref[...]
    dot = functools.partial(jnp.dot, precision=HI, preferred_element_type=F32)
    h = jnp.sin(fr * (dot(z_ref[...], w1_ref[...]) + b1_ref[...]))
    h = jnp.sin(fr * (dot(h, w2_ref[...]) + b2_ref[...]))
    h = jnp.sin(fr * (dot(h, w3_ref[...]) + b3_ref[...]))
    filt = dot(h, w4_ref[...]) * (jnp.exp(-t_ref[...] * dl_ref[...]) + HY_DECAY_SHIFT)
    fwd = filt[:, :GROUP_W]
    bwd = filt[:, GROUP_W:]
    row = lax.broadcasted_iota(jnp.int32, bwd.shape, 0)
    bwd = jnp.where(row == 0, 0.0, bwd)
    nrm = (jnp.sum(jnp.abs(fwd), axis=0, keepdims=True) + jnp.sum(jnp.abs(bwd), axis=0, keepdims=True))
    o_ref[:, :GROUP_W] = fwd / nrm
    o_ref[:, GROUP_W:] = bwd / nrm


def _hy_filters(l, z, t, w1, b1, w2, b2, w3, b3, fr, w4, dl):
    full = lambda a: pl.BlockSpec(a.shape, lambda o: (0, 0))
    return pl.pallas_call(
        _hy_filter_kernel,
        grid=(2,),
        in_specs=[full(z), full(t), full(w1), full(b1), full(w2), full(b2), full(w3), full(b3), full(fr),
                  pl.BlockSpec((HY_FFN, 2 * GROUP_W), lambda o: (0, o)),
                  pl.BlockSpec((1, 2 * GROUP_W), lambda o: (0, o))],
        out_specs=pl.BlockSpec((l, 2 * GROUP_W), lambda o: (0, o)),
        out_shape=jax.ShapeDtypeStruct((l, HY_NFILT), F32),
        compiler_params=_params("arbitrary"),
        name="hy_filters",
    )(z, t, w1, b1, w2, b2, w3, b3, fr, w4, dl)


def _hy_spec_kernel(c_ref, s_ref, h_ref, sign_ref, a_ref, b_ref, d_ref):
    dot = functools.partial(jnp.dot, precision=HI, preferred_element_type=F32)
    row = lax.broadcasted_iota(jnp.int32, (c_ref.shape[0], GROUP_W), 0)
    first = jnp.logical_and(pl.program_id(0) == 0, row == 0)
    for o in range(2):
        hf = h_ref[:, 2 * o * GROUP_W:(2 * o + 1) * GROUP_W]
        hb = h_ref[:, (2 * o + 1) * GROUP_W:(2 * o + 2) * GROUP_W]
        hs = hf + hb
        re = dot(c_ref[...], hs)
        im = dot(s_ref[...], hf - hb)
        ny = jnp.sum(sign_ref[...] * hs, axis=0, keepdims=True)
        cols = slice(o * GROUP_W, (o + 1) * GROUP_W)
        a_ref[:, cols] = re
        b_ref[:, cols] = jnp.where(first, 0.0, im)
        d_ref[:, cols] = jnp.where(first, ny, re)


def _hy_spectrum(h, c32, s32, sign):
    l = h.shape[0]
    fb = min(256, l)
    out = jax.ShapeDtypeStruct((l, 2 * GROUP_W), F32)
    blk = pl.BlockSpec((fb, 2 * GROUP_W), lambda f: (f, 0))
    return pl.pallas_call(
        _hy_spec_kernel,
        grid=(l // fb,),
        in_specs=[pl.BlockSpec((fb, l), lambda f: (f, 0)), pl.BlockSpec((fb, l), lambda f: (f, 0)),
                  pl.BlockSpec(h.shape, lambda f: (0, 0)), pl.BlockSpec((l, 1), lambda f: (0, 0))],
        out_specs=[blk, blk, blk],
        out_shape=[out, out, out],
        compiler_params=_params("arbitrary"),
        name="hy_spectrum",
    )(c32, s32, h, sign)


def _short_conv(u, w, b):
    n = u.shape[0]
    row = lax.broadcasted_iota(jnp.int32, u.shape, 0)
    up = jnp.where(row == 0, 0.0, pltpu.roll(u, 1, 0))
    dn = jnp.where(row == n - 1, 0.0, pltpu.roll(u, n - 1, 0))
    return w[0:1] * up + w[1:2] * u + w[2:3] * dn + b


def _hy_conv_kernel(sig_ref, gate_ref, cw_ref, cb_ref, a_ref, b_ref, d_ref, bias_ref,
                    c_ref, s_ref, ci_ref, si_ref, o_ref, u_scr, ub_scr, acc_scr,
                    *, nb, sig_col, gate_col):
    f = pl.program_id(1)
    w = GROUP_W

    @pl.when(f == 0)
    def _():
        for i in range(nb):
            u = sig_ref[i]
            if sig_col is not None:
                u = _short_conv(u, cw_ref[:, sig_col * w:(sig_col + 1) * w], cb_ref[:, sig_col * w:(sig_col + 1) * w])
            u_scr[:, i * w:(i + 1) * w] = u
            ub_scr[:, i * w:(i + 1) * w] = u.astype(BF16)
        acc_scr[...] = jnp.zeros_like(acc_scr)

    ub = ub_scr[...]
    xr = jnp.dot(c_ref[...], ub, preferred_element_type=F32)
    xi = jnp.dot(s_ref[...], ub, preferred_element_type=F32)
    ka, kb, kd = a_ref[...], b_ref[...], d_ref[...]
    yr, yi = [], []
    for i in range(nb):
        r = xr[:, i * w:(i + 1) * w]
        m = xi[:, i * w:(i + 1) * w]
        yr.append((r * ka - m * kb).astype(BF16))
        yi.append((r * kb + m * kd).astype(BF16))
    yr = jnp.concatenate(yr, axis=1) if nb > 1 else yr[0]
    yi = jnp.concatenate(yi, axis=1) if nb > 1 else yi[0]
    acc_scr[...] += (jnp.dot(ci_ref[...], yr, preferred_element_type=F32)
                     + jnp.dot(si_ref[...], yi, preferred_element_type=F32))

    @pl.when(f == pl.num_programs(1) - 1)
    def _():
        for i in range(nb):
            y = acc_scr[:, i * w:(i + 1) * w] + u_scr[:, i * w:(i + 1) * w] * bias_ref[...]
            g = _short_conv(gate_ref[i], cw_ref[:, gate_col * w:(gate_col + 1) * w],
                            cb_ref[:, gate_col * w:(gate_col + 1) * w])
            o_ref[i] = g * y


def _hy_conv(sig, sig_col, hy, gate_col, cw, cb, spec, order, bias, dft):
    b, l, _ = hy.shape
    w = GROUP_W
    nb = 2 if l >= 2048 else math.gcd(b, 8)
    fb = min(256, l)
    c16, s16, ci16, si16 = dft
    a, bm, d = spec
    sig_arr = hy if sig_col is not None else sig
    sig_blk = sig_col if sig_col is not None else 0
    spec_blk = pl.BlockSpec((fb, w), lambda i, f: (f, order))
    kernel = functools.partial(_hy_conv_kernel, nb=nb, sig_col=sig_col, gate_col=gate_col)
    return pl.pallas_call(
        kernel,
        grid=(b // nb, l // fb),
        in_specs=[pl.BlockSpec((nb, l, w), lambda i, f: (i, 0, sig_blk)),
                  pl.BlockSpec((nb, l, w), lambda i, f: (i, 0, gate_col)),
                  pl.BlockSpec(cw.shape, lambda i, f: (0, 0)),
                  pl.BlockSpec(cb.shape, lambda i, f: (0, 0)),
                  spec_blk, spec_blk, spec_blk,
                  pl.BlockSpec((1, w), lambda i, f: (0, 0)),
                  pl.BlockSpec((fb, l), lambda i, f: (f, 0)),
                  pl.BlockSpec((fb, l), lambda i, f: (f, 0)),
                  pl.BlockSpec((l, fb), lambda i, f: (0, f)),
                  pl.BlockSpec((l, fb), lambda i, f: (0, f))],
        out_specs=pl.BlockSpec((nb, l, w), lambda i, f: (i, 0, 0)),
        out_shape=jax.ShapeDtypeStruct((b, l, w), F32),
        scratch_shapes=[pltpu.VMEM((l, nb * w), F32), pltpu.VMEM((l, nb * w), BF16),
                        pltpu.VMEM((l, nb * w), F32)],
        compiler_params=_params("parallel", "arbitrary"),
        name="hy_conv",
    )(sig_arr, hy, cw, cb, a, bm, d, bias, c16, s16, ci16, si16)


@functools.lru_cache(maxsize=None)
def _dft_tables(l):
    n = 2 * l
    k = np.arange(l, dtype=np.int64)
    ang = ((k[:, None] * k[None, :]) % n).astype(np.float64) * (2.0 * math.pi / n)
    sign = np.where(k % 2 == 0, 1.0, -1.0)
    c = np.cos(ang)
    s = -np.sin(ang)
    s[0, :] = sign
    wk = np.where(k == 0, 1.0, 2.0) / n
    ci = c.T * wk[None, :]
    si = s.T * wk[None, :]
    f32 = lambda a: np.ascontiguousarray(a, dtype=np.float32)
    b16 = lambda a: f32(a).astype(BF16)
    return (f32(c), f32(s), f32(sign[:, None])), (b16(c), b16(s), b16(ci), b16(si))


def _hy_features(l):
    t = jnp.linspace(0.0, 1.0, l, dtype=F32)[:, None]
    wpos = 2.0 * math.pi * jnp.arange(l, dtype=F32)[:, None] / l
    fb = jnp.linspace(1e-4, HY_BANDS - 1, HY_BANDS, dtype=F32)[None, :]
    z = jnp.concatenate([t, jnp.cos(fb * wpos), -jnp.sin(fb * wpos)], axis=-1)
    z = jnp.pad(z, ((0, 0), (0, HY_FFN - HY_EMB)))
    deltas = jnp.linspace(math.log(HY_TARGET) / HY_FAST_PCT, math.log(HY_TARGET) / HY_SLOW_PCT, HY_NFILT, dtype=F32)
    return z, t, jnp.abs(deltas)[None, :]


def _hyena(hy, feats, dft32, dft16, cw, cb, w1, b1, w2, b2, w3, b3, fr, w4, bias):
    l = hy.shape[1]
    z, t, dl = feats
    w1p = jnp.pad(w1, ((0, HY_FFN - HY_EMB), (0, 0)))
    h = _hy_filters(l, z, t, w1p, b1[None], w2, b2[None], w3, b3[None], fr[None], w4, dl)
    spec = _hy_spectrum(h, *dft32)
    cb2 = cb[None]
    zz = _hy_conv(None, 0, hy, 1, cw, cb2, spec, 0, bias[0:1], dft16)
    return _hy_conv(zz, None, hy, 2, cw, cb2, spec, 1, bias[1:2], dft16)


def _rope_tables(l, dim, width, off, reps):
    rows = jnp.repeat(jnp.arange(l // GRID_W, dtype=jnp.int32), GRID_W).astype(F32)
    cols = jnp.tile(jnp.arange(GRID_W, dtype=jnp.int32), l // GRID_W).astype(F32)
    quarter = dim // 4
    half = dim // 2
    inv = ROPE_THETA ** (-jnp.arange(quarter, dtype=F32) / quarter)
    ang = jnp.concatenate([rows[:, None] * inv, cols[:, None] * inv], axis=-1)
    c, s = jnp.cos(ang), jnp.sin(ang)
    cf = jnp.ones((l, width), F32).at[:, off:off + dim].set(jnp.concatenate([c, c], axis=-1))
    s1 = jnp.zeros((l, width), F32).at[:, off:off + half].set(-s)
    s2 = jnp.zeros((l, width), F32).at[:, off + half:off + dim].set(s)
    return tuple(jnp.tile(a, (1, reps)) for a in (cf, s1, s2))


def _attn_kernel(q_ref, *rest, rep):
    o_ref = rest[-1]
    kv = rest[:-1]
    ks, vs = kv[0::2], kv[1::2]
    tq = q_ref.shape[1]
    lo = lax.broadcasted_iota(jnp.int32, (tq, LANES), 1) < HEAD_D
    nt = (((1,), (1,)), ((), ()))
    n_heads = q_ref.shape[0]

    def scores(h):
        return [lax.dot_general(q_ref[h], k_ref[h // rep], nt, preferred_element_type=F32) for k_ref in ks]

    def weighted(h, ss):
        m = ss[0].max(axis=-1, keepdims=True)
        for s in ss[1:]:
            m = jnp.maximum(m, s.max(axis=-1, keepdims=True))
        den = 0.0
        acc = 0.0
        for s, v_ref in zip(ss, vs):
            e = jnp.exp(s - m)
            den = den + e.sum(axis=-1, keepdims=True)
            acc = acc + jnp.dot(e.astype(BF16), v_ref[h // rep], preferred_element_type=F32)
        return acc / den

    outs = []
    ss_next = scores(0)
    for h in range(n_heads):
        ss = ss_next
        if h + 1 < n_heads:
            ss_next = scores(h + 1)
        outs.append(weighted(h, ss))
    for p in range(n_heads // 2):
        o_ref[:, p * LANES:(p + 1) * LANES] = jnp.where(lo, outs[2 * p], outs[2 * p + 1])


def _attention(q, kvs, rep, b, l):
    nh = q.shape[0]
    tq = min(512, l)
    nq = l // tq
    specs = [pl.BlockSpec((nh, tq, LANES), lambda i, j: (0, i * nq + j, 0))]
    args = [q]
    for k, v, lk in kvs:
        spec = pl.BlockSpec((k.shape[0], lk, LANES), lambda i, j: (0, i, 0))
        specs += [spec, spec]
        args += [k, v]
    return pl.pallas_call(
        functools.partial(_attn_kernel, rep=rep),
        grid=(b, nq),
        in_specs=specs,
        out_specs=pl.BlockSpec((None, tq, GROUP_W), lambda i, j: (i, j, 0)),
        out_shape=jax.ShapeDtypeStruct((b, l, GROUP_W), F32),
        compiler_params=_params("parallel", "parallel"),
        name="attention",
    )(*args)


def _ret_kernel(qs, ks, v_ref, g_ref, dl_ref, dc_ref, ng_ref, seg_ref, sf0_ref, sb0_ref,
                y_ref, sf_ref, sb_ref, sball, *, chunk, unroll):
    c_ = chunk
    w = GROUP_W
    n_chunks = qs.shape[0] // c_
    unroll = min(unroll, n_chunks)
    shift_c = int(math.log2(c_))
    shift_h = int(math.log2(HEAD_D))
    tn = (((0,), (0,)), ((), ()))
    nt = (((1,), (1,)), ((), ()))

    lgf = _log_sigmoid(dl_ref[0:1, :])
    lgb = _log_sigmoid(dl_ref[1:2, :])
    ii = lax.broadcasted_iota(jnp.int32, (c_, 1), 0).astype(F32)
    qdf = jnp.exp((ii + 1.0) * lgf)
    kdf = jnp.exp((c_ - 1.0 - ii) * lgf)
    cdf = jnp.exp(float(c_) * lgf)
    qdb = jnp.exp((c_ - ii) * lgb)
    kdb = jnp.exp(ii * lgb)
    cdb = jnp.exp(float(c_) * lgb)
    r_h = lax.shift_right_logical(lax.broadcasted_iota(jnp.int32, (w, w), 0), shift_h)
    c_h = lax.shift_right_logical(lax.broadcasted_iota(jnp.int32, (w, w), 1), shift_h)
    blockdiag = r_h == c_h
    lgc = _log_sigmoid(dc_ref[...])
    i4 = jnp.bitwise_and(lax.broadcasted_iota(jnp.int32, (4 * c_, c_), 0), c_ - 1)
    j4 = lax.broadcasted_iota(jnp.int32, (4 * c_, c_), 1)
    diff = (i4 - j4).astype(F32)
    dmat = (jnp.where(diff >= 0, jnp.exp(lgc[:, 0:1] * jnp.maximum(diff, 0.0)), 0.0)
            + jnp.where(diff <= 0, jnp.exp(lgc[:, 1:2] * jnp.maximum(-diff, 0.0)), 0.0))
    row_h = lax.shift_right_logical(lax.broadcasted_iota(jnp.int32, (4 * c_, w), 0), shift_c)
    lane_h = lax.shift_right_logical(lax.broadcasted_iota(jnp.int32, (4 * c_, w), 1), shift_h)
    headmask = row_h == lane_h
    seg = seg_ref[...]

    def rows(c):
        return pl.ds(pl.multiple_of(c * c_, c_), c_)

    def kv_outer(k, v, dec):
        a = lax.dot_general((k * dec).astype(BF16), v.astype(BF16), tn, preferred_element_type=F32)
        return jnp.where(blockdiag, a, 0.0)

    def bwd(t, s):
        c = n_chunks - 1 - t
        sball[c] = s
        sl = rows(c)
        return s * cdb + kv_outer(ks[sl, :], v_ref[sl, :], kdb)

    sb_ref[...] = lax.fori_loop(0, n_chunks, bwd, sb0_ref[...], unroll=unroll)

    def fwd(c, s):
        sl = rows(c)
        q, k, v = qs[sl, :], ks[sl, :], v_ref[sl, :]
        vb = v.astype(BF16)
        q4 = jnp.where(headmask, jnp.concatenate([q, q, q, q], axis=0), 0.0).astype(BF16)
        sc = lax.dot_general(q4, k.astype(BF16), nt, preferred_element_type=F32)
        o4 = jnp.dot((sc * dmat).astype(BF16), vb, preferred_element_type=F32)
        o4 = jnp.where(headmask, o4, 0.0)
        o = o4[0:c_] + o4[c_:2 * c_] + o4[2 * c_:3 * c_] + o4[3 * c_:4 * c_]
        o = o + jnp.dot((q * qdf).astype(BF16), s.astype(BF16), preferred_element_type=F32)
        o = o + jnp.dot((q * qdb).astype(BF16), sball[c].astype(BF16), preferred_element_type=F32)
        on = o * lax.rsqrt(_seg_mean(o * o, seg) + EPS) * ng_ref[...]
        g = g_ref[sl, :]
        y_ref[sl, :] = on * (g * jax.nn.sigmoid(g))
        return s * cdf + kv_outer(k, v, kdf)

    sf_ref[...] = lax.fori_loop(0, n_chunks, fwd, sf0_ref[...], unroll=unroll)


def _retention(ret, dl, dc, ng, seg, sf0, sb0):
    b, l, _ = ret.shape
    w = GROUP_W
    col = lambda j: pl.BlockSpec((None, l, w), lambda i: (i, 0, j))
    state = pl.BlockSpec((None, w, w), lambda i: (i, 0, 0))
    sshape = jax.ShapeDtypeStruct((b, w, w), F32)
    return pl.pallas_call(
        functools.partial(_ret_kernel, chunk=RET_CHUNK, unroll=RET_UNROLL),
        grid=(b,),
        in_specs=[col(0), col(1), col(2), col(3), _const_spec(dl), _const_spec(dc), _const_spec(ng),
                  _const_spec(seg), state, state],
        out_specs=[pl.BlockSpec((None, l, w), lambda i: (i, 0, 0)), state, state],
        out_shape=[jax.ShapeDtypeStruct((b, l, w), F32), sshape, sshape],
        scratch_shapes=[pltpu.VMEM((l // RET_CHUNK, w, w), F32)],
        compiler_params=_params("parallel"),
        name="retention",
    )(ret, ret, ret, ret, dl, dc, ng, seg, sf0, sb0)


def _mlp_kernel(x_ref, ya_ref, yb_ref, yc_ref, yd_ref, gout_ref, ga_ref, wout_ref,
                gffn_ref, sh_ref, sc_ref, gf_ref, wfi_ref, wfo_ref, fg_ref, o_ref,
                x1_a, h_a, x1_b, h_b, *, final_norm):
    t = pl.program_id(0)
    nff = wfo_ref.shape[0]

    @pl.when(t == 0)
    def _():
        x1_b[...] = jnp.zeros_like(x1_b)
        h_b[...] = jnp.zeros_like(h_b)

    def step(x1_w, h_w, x1_r, h_r):
        acc = 0.0
        for i, y_ref in enumerate((ya_ref, yb_ref, yc_ref, yd_ref)):
            rows = slice(i * GROUP_W, (i + 1) * GROUP_W)
            yn = (_rms(y_ref[...]) * gout_ref[:, rows]).astype(BF16)
            acc = acc + jnp.dot(yn, wout_ref[rows, :], preferred_element_type=F32)
        x1 = x_ref[...] + ga_ref[...] * acc
        x1_w[...] = x1
        h_w[...] = (_rms(x1) * gffn_ref[...] * (1.0 + sc_ref[...]) + sh_ref[...]).astype(BF16)

        h = h_r[...]
        ff = 0.0
        for c in range(nff):
            a = jnp.dot(h, wfi_ref[c], preferred_element_type=F32)
            g = jnp.dot(h, wfi_ref[c + nff], preferred_element_type=F32)
            u = (a * jax.nn.sigmoid(a) * g).astype(BF16)
            ff = ff + jnp.dot(u, wfo_ref[c], preferred_element_type=F32)
        out = x1_r[...] + gf_ref[...] * ff
        if final_norm:
            out = _rms(out) * fg_ref[...]
        o_ref[...] = out

    @pl.when(t % 2 == 0)
    def _():
        step(x1_a, h_a, x1_b, h_b)

    @pl.when(t % 2 == 1)
    def _():
        step(x1_b, h_b, x1_a, h_a)


def _mlp(x, ys, g_out, g_a, w_out, g_ffn, sh, sc, g_f, wfi, wfo, final_g, final_norm):
    b, l, d = x.shape
    tm = min(512, l)
    nt = l // tm
    n = b * nt
    cur = lambda t: jnp.minimum(t, n - 1)
    prv = lambda t: jnp.maximum(t - 1, 0)
    tok = lambda wd: pl.BlockSpec((None, tm, wd), lambda t: (cur(t) // nt, cur(t) % nt, 0))
    vec = pl.BlockSpec((None, 1, d), lambda t: (cur(t) // nt, 0, 0))
    return pl.pallas_call(
        functools.partial(_mlp_kernel, final_norm=final_norm),
        grid=(n + 1,),
        in_specs=[tok(d)] + [tok(GROUP_W)] * 4 + [
            _const_spec(g_out), vec, _const_spec(w_out), _const_spec(g_ffn), vec, vec,
            pl.BlockSpec((None, 1, d), lambda t: (prv(t) // nt, 0, 0)),
            _const_spec(wfi), _const_spec(wfo), _const_spec(final_g)],
        out_specs=pl.BlockSpec((None, tm, d), lambda t: (prv(t) // nt, prv(t) % nt, 0)),
        out_shape=jax.ShapeDtypeStruct((b, l, d), F32),
        scratch_shapes=[pltpu.VMEM((tm, d), F32), pltpu.VMEM((tm, d), BF16),
                        pltpu.VMEM((tm, d), F32), pltpu.VMEM((tm, d), BF16)],
        compiler_params=_params("arbitrary"),
        name="mlp",
    )(x, *ys, g_out, g_a, w_out, g_ffn, sh, sc, g_f, wfi, wfo, final_g)


def _pad_proj(w):
    z = jnp.zeros((w.shape[0], LANES - MLA_ROPE), w.dtype)
    return jnp.concatenate([w[:, :1184], z, w[:, 1184:]], axis=1).astype(BF16)


def _mla_weights(wuq, wukv):
    r = wuq.shape[0]
    q = wuq.reshape(r, 4, MLA_NOPE + MLA_ROPE)
    q = jnp.pad(q, ((0, 0), (0, 0), (0, LANES - MLA_NOPE - MLA_ROPE))).reshape(r, 4 * LANES)
    kv = wukv.reshape(wukv.shape[0], 4, MLA_NOPE + MLA_V)
    k = jnp.pad(kv[..., :MLA_NOPE], ((0, 0), (0, 0), (0, LANES - MLA_NOPE))).reshape(-1, 4 * LANES)
    v = jnp.concatenate([kv[..., MLA_NOPE:], kv[..., MLA_NOPE:]], axis=-1).reshape(-1, 4 * LANES)
    return q.astype(BF16), k.astype(BF16), v.astype(BF16)


def _ffn_weights(w_in, w_out):
    d = w_in.shape[0]
    nff = D_FF // FF_TILE
    wfi = w_in.astype(BF16).reshape(d, 2 * nff, FF_TILE).transpose(1, 0, 2)
    return wfi, w_out.astype(BF16).reshape(nff, FF_TILE, d)


def _seg_matrix(width):
    i = jnp.arange(width) // HEAD_D
    return jnp.where(i[:, None] == i[None, :], 1.0 / HEAD_D, 0.0).astype(BF16)


def kernel(x, c, ctx, c_ctx, w_mod, b_mod, norm_attn_g, norm_ffn_g, w_in, hy_conv_w, hy_conv_b, hy_w1, hy_b1, hy_w2, hy_b2, hy_w3, hy_b3, hy_freq, hy_w4, hy_bias, mla_q_norm_g, mla_w_uq, mla_kv_norm_g, mla_w_ukv, ret_decay, ret_norm_g, gqa_q_norm_g, gqa_k_norm_g, out_norm_g, w_out, w_ffn_in, w_ffn_out, final_norm_g):
    b, n_lat, d = x.shape
    n_ctx = ctx.shape[1]
    depth = w_mod.shape[0]

    rope_mla = _rope_tables(n_lat, MLA_ROPE, LANES, MLA_NOPE, 1)
    rope_hd2 = _rope_tables(n_lat, HEAD_D, HEAD_D, 0, 2)
    seg2, seg4 = _seg_matrix(LANES), _seg_matrix(GROUP_W)
    hy_consts = {n: (_hy_features(n),) + _dft_tables(n) for n in {n_lat, n_ctx}}
    zero_state = jnp.zeros((b, GROUP_W, GROUP_W), F32)

    rows = -(-(b + 1) // 8) * 8
    cond = jnp.zeros((rows, d), F32).at[:b].set(c).at[b].set(c_ctx)
    mod = _ada_mod(cond, w_mod, b_mod)

    xl, xc = x, ctx.reshape(1, b * n_ctx, d)
    for l in range(depth):
        update_ctx = l < depth - 1
        m_lat = [m[:b, None, :] for m in jnp.split(mod[l], 6, axis=-1)]
        m_ctx = [m[b][None, None, :] for m in jnp.split(mod[l], 6, axis=-1)]
        wp = _pad_proj(w_in[l])
        g_attn = norm_attn_g[l][None]
        mla_w = (mla_q_norm_g[l][None], mla_kv_norm_g[l][None]) + _mla_weights(mla_w_uq[l], mla_w_ukv[l])
        gqa_w = (jnp.tile(gqa_q_norm_g[l], 2)[None], jnp.tile(gqa_k_norm_g[l], 2)[None], seg2)
        hy_l, ret_l, q_l, k_l, v_l, gq_l, gk_l, gv_l = _front(
            xl, g_attn, m_lat[0], m_lat[1], wp, mla_w, gqa_w, (rope_mla, rope_hd2))
        hy_c, ret_c, q_c, k_c, v_c, gq_c, gk_c, gv_c = _front(
            xc, g_attn, m_ctx[0], m_ctx[1], wp, mla_w, gqa_w, None)
        hy_c = hy_c.reshape(b, n_ctx, -1)
        ret_c = ret_c.reshape(b, n_ctx, -1)

        hy_args = (hy_conv_w[l], hy_conv_b[l], hy_w1[l], hy_b1[l], hy_w2[l], hy_b2[l], hy_w3[l], hy_b3[l],
                   hy_freq[l], hy_w4[l], hy_bias[l])
        ya_l = _hyena(hy_l, *hy_consts[n_lat], *hy_args)
        yb_l = _attention(q_l, [(k_c, v_c, n_ctx), (k_l, v_l, n_lat)], 1, b, n_lat)

        dl = jnp.repeat(ret_decay[l], HEAD_D, axis=1)
        dc = jnp.repeat(ret_decay[l].T, RET_CHUNK, axis=0)
        ng = ret_norm_g[l][None]
        yc_c, s_f, s_b = _retention(ret_c, dl, dc, ng, seg4, zero_state, zero_state)
        yc_l, _, _ = _retention(ret_l, dl, dc, ng, seg4, s_f, s_b)
        yd_l = _attention(gq_l, [(gk_c, gv_c, n_ctx), (gk_l, gv_l, n_lat)], 2, b, n_lat)

        wo = w_out[l].astype(BF16)
        wfi, wfo = _ffn_weights(w_ffn_in[l], w_ffn_out[l])
        g_out, g_ffn, g_fin = out_norm_g[l][None], norm_ffn_g[l][None], final_norm_g[None]
        xl = _mlp(xl, (ya_l, yb_l, yc_l, yd_l), g_out, m_lat[2], wo, g_ffn, m_lat[3], m_lat[4], m_lat[5],
                  wfi, wfo, g_fin, l == depth - 1)

        if update_ctx:
            flat = lambda a: a.reshape(1, b * n_ctx, GROUP_W)
            ya_c = _hyena(hy_c, *hy_consts[n_ctx], *hy_args)
            yb_c = _attention(q_c, [(k_c, v_c, n_ctx)], 1, b, n_ctx)
            yd_c = _attention(gq_c, [(gk_c, gv_c, n_ctx)], 2, b, n_ctx)
            xc = _mlp(xc, tuple(flat(a) for a in (ya_c, yb_c, yc_c, yd_c)), g_out, m_ctx[2], wo,
                      g_ffn, m_ctx[3], m_ctx[4], m_ctx[5], wfi, wfo, g_fin, False)

    return xl
```

```python
import functools
import math

import numpy as np
import jax
import jax.numpy as jnp
from jax import lax
from jax.experimental import pallas as pl
from jax.experimental.pallas import tpu as pltpu

F32 = jnp.float32
BF16 = jnp.bfloat16
HI = lax.Precision.HIGHEST

D_MODEL = 1024
GRID_W = 64
EPS = 1e-6
ROPE_THETA = 10000.0
GROUP_W = 256
LANES = 128

HY_EMB = 33
HY_BANDS = 16
HY_FFN = 64
HY_NFILT = 1024
HY_DECAY_SHIFT = 0.05
HY_FAST_PCT = 0.3
HY_SLOW_PCT = 1.5
HY_TARGET = 1e-2
FFT_RADIX = 16
FFT_BINS = FFT_RADIX // 2 + 1
FFT_PIECE_ROWS = 32

MLA_NOPE = 64
MLA_ROPE = 32
MLA_V = 64
MLA_KV_RANK = 128
MLA_SCALE = (MLA_NOPE + MLA_ROPE) ** -0.5

HEAD_D = 64
RET_CHUNK = 128
RET_UNROLL = 4
GQA_SCALE = HEAD_D ** -0.5
RET_K_SCALE = HEAD_D ** -0.5

D_FF = 2816
FF_TILE = 256

PROJ_SPLITS = (768, 256, 256, 1024, 512)
PROJ_OFFSETS = (0, 768, 1024, 1280, 2304, 2816)

VMEM_LIMIT = 56 * 1024 * 1024


def _params(*sem):
    return pltpu.CompilerParams(dimension_semantics=sem, vmem_limit_bytes=VMEM_LIMIT)


def _const_spec(a):
    nd = a.ndim
    return pl.BlockSpec(a.shape, lambda *_: (0,) * nd, pipeline_mode=pl.Buffered(1))


def _rms(x):
    return x * lax.rsqrt(jnp.mean(x * x, axis=-1, keepdims=True) + EPS)


def _rope(x, cf, s1, s2, half):
    w = x.shape[-1]
    return x * cf + pltpu.roll(x, w - half, 1) * s1 + pltpu.roll(x, half, 1) * s2


def _seg_mean(sq, seg):
    hi = sq.astype(BF16)
    lo = (sq - hi.astype(F32)).astype(BF16)
    return (jnp.dot(hi, seg, preferred_element_type=F32) + jnp.dot(lo, seg, preferred_element_type=F32))


def _log_sigmoid(x):
    return -(jnp.maximum(-x, 0.0) + jnp.log(1.0 + jnp.exp(-jnp.abs(x))))


def _ada_kernel(c_ref, w_ref, b_ref, o_ref):
    c = c_ref[...]
    s = c * jax.nn.sigmoid(c)
    o_ref[...] = jnp.dot(s, w_ref[...], precision=HI, preferred_element_type=F32) + b_ref[...]


def _ada_mod(cond, w_mod, b_mod):
    depth, d, n = w_mod.shape
    r = cond.shape[0]
    tn = 1024
    return pl.pallas_call(
        _ada_kernel,
        grid=(depth, n // tn),
        in_specs=[pl.BlockSpec((r, d), lambda l, j: (0, 0)),
                  pl.BlockSpec((None, d, tn), lambda l, j: (l, 0, j)),
                  pl.BlockSpec((None, 1, tn), lambda l, j: (l, 0, j))],
        out_specs=pl.BlockSpec((None, r, tn), lambda l, j: (l, 0, j)),
        out_shape=jax.ShapeDtypeStruct((depth, r, n), F32),
        compiler_params=_params("arbitrary", "arbitrary"),
        name="ada_mod",
    )(cond, w_mod, b_mod.reshape(depth, 1, n))


def _mla_heads(cq, ckvr, gq, gkv, wuq_ref, wk_ref, wv_ref, tables, q_ref, k_ref, v_ref):
    half = MLA_ROPE // 2
    qn = (_rms(cq) * gq).astype(BF16)
    q = jnp.dot(qn, wuq_ref[...], preferred_element_type=F32)
    kvn = (_rms(ckvr[:, :MLA_KV_RANK]) * gkv).astype(BF16)
    k = jnp.dot(kvn, wk_ref[...], preferred_element_type=F32)
    v = jnp.dot(kvn, wv_ref[...], preferred_element_type=F32)
    kpe = pltpu.roll(ckvr[:, MLA_KV_RANK:], MLA_NOPE, 1)
    if tables is not None:
        kpe = _rope(kpe, *tables, half)
    for h in range(q_ref.shape[0]):
        cols = slice(h * LANES, (h + 1) * LANES)
        qh = q[:, cols]
        if tables is not None:
            qh = _rope(qh, *tables, half)
        q_ref[h] = (qh * MLA_SCALE).astype(BF16)
        k_ref[h] = (k[:, cols] + kpe).astype(BF16)
        v_ref[h] = v[:, cols].astype(BF16)


def _gqa_heads(x, gq, gk, seg, tables, q_ref, k_ref, v_ref):
    half = HEAD_D // 2
    lo = lax.broadcasted_iota(jnp.int32, (x.shape[0], LANES), 1) < HEAD_D

    def normed(a, g):
        a = a * lax.rsqrt(_seg_mean(a * a, seg) + EPS) * g
        return _rope(a, *tables, half) if tables is not None else a

    def dup(a, g):
        a = jnp.where(lo if g == 0 else jnp.logical_not(lo), a, 0.0)
        return (a + pltpu.roll(a, HEAD_D, 1)).astype(BF16)

    for p in range(2):
        qb = normed(x[:, p * LANES:(p + 1) * LANES], gq) * GQA_SCALE
        q_ref[2 * p] = jnp.where(lo, qb, 0.0).astype(BF16)
        q_ref[2 * p + 1] = jnp.where(lo, 0.0, qb).astype(BF16)
    kb = normed(x[:, 2 * LANES:3 * LANES], gk)
    vb = x[:, 3 * LANES:4 * LANES]
    for g in range(2):
        k_ref[g] = dup(kb, g)
        v_ref[g] = dup(vb, g)


def _front_kernel(x_ref, g_ref, sh_ref, sc_ref, w_ref, gq_ref, gkv_ref, wuq_ref, wk_ref, wv_ref,
                  ggq_ref, ggk_ref, seg_ref, *rest, rope):
    if rope:
        mla_t = tuple(r[...] for r in rest[0:3])
        hd_t = tuple(r[...] for r in rest[3:6])
        rest = rest[6:]
    else:
        mla_t = hd_t = None
    hy_ref, ret_ref, mq_ref, mk_ref, mv_ref, gq_out, gk_out, gv_out = rest
    y = _rms(x_ref[...]) * g_ref[...]
    h = (y * (1.0 + sc_ref[...]) + sh_ref[...]).astype(BF16)
    o0, o1, o2, o3, o4, o5 = PROJ_OFFSETS

    def proj(a, b):
        return jnp.dot(h, w_ref[:, a:b], preferred_element_type=F32)

    _mla_heads(proj(o1, o2), proj(o2, o3), gq_ref[...], gkv_ref[...], wuq_ref, wk_ref, wv_ref, mla_t,
               mq_ref, mk_ref, mv_ref)
    _gqa_heads(proj(o4, o5), ggq_ref[...], ggk_ref[...], seg_ref[...], hd_t, gq_out, gk_out, gv_out)
    for i in range(2 * GROUP_W // LANES):
        cols = slice(i * LANES, (i + 1) * LANES)
        a = proj(o3 + i * LANES, o3 + (i + 1) * LANES)
        if i * LANES >= GROUP_W:
            a = a * RET_K_SCALE
        ret_ref[:, cols] = _rope(a, *hd_t, HEAD_D // 2) if rope else a
    ret_ref[:, 2 * GROUP_W:] = proj(o3 + 2 * GROUP_W, o4)
    hy_ref[...] = proj(o0, o1)


def _front(x, g, sh, sc, w, mla_w, gqa_w, tables):
    b, l, d = x.shape
    tm = min(512, l)
    nt = l // tm
    rope = tables is not None
    tok = lambda wd: pl.BlockSpec((None, tm, wd), lambda i, j: (i, j, 0))
    vec = pl.BlockSpec((None, 1, d), lambda i, j: (i, 0, 0))
    head = lambda nh: pl.BlockSpec((nh, tm, LANES), lambda i, j: (0, i * nt + j, 0))
    hshape = lambda nh: jax.ShapeDtypeStruct((nh, b * l, LANES), BF16)
    consts = [w, *mla_w, *gqa_w]
    args = [x, g, sh, sc, *consts]
    specs = [tok(d), _const_spec(g), vec, vec] + [_const_spec(a) for a in consts]
    if rope:
        args += [*tables[0], *tables[1]]
        specs += [pl.BlockSpec((tm, LANES), lambda i, j: (j, 0))] * 6
    return pl.pallas_call(
        functools.partial(_front_kernel, rope=rope),
        grid=(b, nt),
        in_specs=specs,
        out_specs=[tok(PROJ_SPLITS[0]), tok(PROJ_SPLITS[3]), head(4), head(4), head(4), head(4), head(2), head(2)],
        out_shape=[jax.ShapeDtypeStruct((b, l, PROJ_SPLITS[0]), F32),
                   jax.ShapeDtypeStruct((b, l, PROJ_SPLITS[3]), F32),
                   hshape(4), hshape(4), hshape(4), hshape(4), hshape(2), hshape(2)],
        compiler_params=_params("parallel", "parallel"),
        name="front",
    )(*args)


def _hy_filter_kernel(z_ref, t_ref, w1_ref, b1_ref, w2_ref, b2_ref, w3_ref, b3_ref, fr_ref,
                      w4_ref, dl_ref, o_ref):
    fr = fr_ref[...]
    dot = functools.partial(jnp.dot, precision=HI, preferred_element_type=F32)
    h = jnp.sin(fr * (dot(z_ref[...], w1_ref[...]) + b1_ref[...]))
    h = jnp.sin(fr * (dot(h, w2_ref[...]) + b2_ref[...]))
    h = jnp.sin(fr * (dot(h, w3_ref[...]) + b3_ref[...]))
    filt = dot(h, w4_ref[...]) * (jnp.exp(-t_ref[...] * dl_ref[...]) + HY_DECAY_SHIFT)
    fwd = filt[:, :GROUP_W]
    bwd = filt[:, GROUP_W:]
    row = lax.broadcasted_iota(jnp.int32, bwd.shape, 0)
    bwd = jnp.where(row == 0, 0.0, bwd)
    nrm = (jnp.sum(jnp.abs(fwd), axis=0, keepdims=True) + jnp.sum(jnp.abs(bwd), axis=0, keepdims=True))
    o_ref[:, :GROUP_W] = fwd / nrm
    o_ref[:, GROUP_W:] = bwd / nrm


def _hy_filters(l, z, t, w1, b1, w2, b2, w3, b3, fr, w4, dl):
    full = lambda a: pl.BlockSpec(a.shape, lambda o: (0, 0))
    return pl.pallas_call(
        _hy_filter_kernel,
        grid=(2,),
        in_specs=[full(z), full(t), full(w1), full(b1), full(w2), full(b2), full(w3), full(b3), full(fr),
                  pl.BlockSpec((HY_FFN, 2 * GROUP_W), lambda o: (0, o)),
                  pl.BlockSpec((1, 2 * GROUP_W), lambda o: (0, o))],
        out_specs=pl.BlockSpec((l, 2 * GROUP_W), lambda o: (0, o)),
        out_shape=jax.ShapeDtypeStruct((l, HY_NFILT), F32),
        compiler_params=_params("arbitrary"),
        name="hy_filters",
    )(z, t, w1, b1, w2, b2, w3, b3, fr, w4, dl)


def _hy_spec_kernel(c_ref, s_ref, h_ref, re_ref, im_ref):
    dot = functools.partial(jnp.dot, precision=HI, preferred_element_type=F32)
    for o in range(2):
        hf = h_ref[:, 2 * o * GROUP_W:(2 * o + 1) * GROUP_W]
        hb = h_ref[:, (2 * o + 1) * GROUP_W:(2 * o + 2) * GROUP_W]
        cols = slice(o * GROUP_W, (o + 1) * GROUP_W)
        re_ref[:, cols] = dot(c_ref[...], hf + hb)
        im_ref[:, cols] = -dot(s_ref[...], hf - hb)


def _hy_spectrum(h, cf, sf):
    rows, l = cf.shape
    fb = rows // FFT_BINS if rows // FFT_BINS >= 256 else rows
    out = jax.ShapeDtypeStruct((rows, 2 * GROUP_W), F32)
    blk = pl.BlockSpec((fb, 2 * GROUP_W), lambda f: (f, 0))
    return pl.pallas_call(
        _hy_spec_kernel,
        grid=(rows // fb,),
        in_specs=[pl.BlockSpec((fb, l), lambda f: (f, 0)), pl.BlockSpec((fb, l), lambda f: (f, 0)),
                  _const_spec(h)],
        out_specs=[blk, blk],
        out_shape=[out, out],
        compiler_params=_params("arbitrary"),
        name="hy_spectrum",
    )(cf, sf, h)


def _short_conv(u, w, b):
    n = u.shape[0]
    row = lax.broadcasted_iota(jnp.int32, u.shape, 0)
    up = jnp.where(row == 0, 0.0, pltpu.roll(u, 1, 0))
    dn = jnp.where(row == n - 1, 0.0, pltpu.roll(u, n - 1, 0))
    return w[0:1] * up + w[1:2] * u + w[2:3] * dn + b


_R2 = math.sqrt(0.5)


def _dft8_real4(x0, x1, x2, x3):
    s02, d02 = x0 + x2, x0 - x2
    s13, d13 = x1 + x3, x1 - x3
    p, q = _R2 * d13, _R2 * s13
    re = [s02 + s13, x0 + p, d02, x0 - p, s02 - s13]
    im = [None, -(q + x2), -d13, x2 - q, None]
    return re, im


def _fwd16(a):
    er, ei = _dft8_real4(a[0], a[2], a[4], a[6])
    orr, oi = _dft8_real4(a[1], a[3], a[5], a[7])

    def ext(re, im, k):
        if k <= 4:
            return re[k], im[k]
        return re[8 - k], (None if im[8 - k] is None else -im[8 - k])

    out_r, out_i = [], []
    for k in range(FFT_BINS):
        e_r, e_i = ext(er, ei, k)
        o_r, o_i = ext(orr, oi, k)
        c, s = math.cos(math.pi * k / 8), math.sin(math.pi * k / 8)
        if k == 0:
            out_r.append(e_r + o_r)
            out_i.append(None)
        elif k == 8:
            out_r.append(e_r - o_r)
            out_i.append(None)
        elif k == 4:
            out_r.append(e_r)
            out_i.append(-o_r)
        else:
            out_r.append(e_r + (c * o_r + s * o_i))
            out_i.append(e_i + (c * o_i - s * o_r))
    return out_r, out_i


def _dft8_real4_t(kr, ki):
    a = kr[0] + kr[4]
    b = kr[0] - kr[4]
    x0 = a + kr[2] + kr[1] + kr[3]
    x2 = a - kr[2] - ki[1] + ki[3]
    t1 = _R2 * (kr[1] - kr[3])
    t2 = _R2 * (ki[1] + ki[3])
    return x0, b - ki[2] + t1 - t2, x2, b + ki[2] - t1 - t2


def _inv16(br, bi):
    o_r, o_i = [None] * FFT_BINS, [None] * FFT_BINS
    for k in range(FFT_BINS):
        c, s = math.cos(math.pi * k / 8), math.sin(math.pi * k / 8)
        if k == 0:
            o_r[k] = br[k]
        elif k == 8:
            o_r[k] = -br[k]
        elif k == 4:
            o_r[k] = -bi[k]
        else:
            o_r[k] = c * br[k] - s * bi[k]
            o_i[k] = c * bi[k] + s * br[k]

    def fold(re, im):
        fr = [re[0] + re[8], re[1] + re[7], re[2] + re[6], re[3] + re[5], re[4]]
        fi = [None, im[1] - im[7], im[2] - im[6], im[3] - im[5], None]
        return fr, fi

    xe = _dft8_real4_t(*fold(br, bi))
    xo = _dft8_real4_t(*fold(o_r, o_i))
    return [xe[0], xo[0], xe[1], xo[1], xe[2], xo[2], xe[3], xo[3]]


def _hy_conv_kernel(sig_ref, gate_ref, cw_ref, cb_ref, kr_ref, ki_ref, bias_ref, m_ref, mt_ref,
                    twc_ref, tws_ref, twci_ref, twsi_ref, o_ref, u_scr, g_scr, z_scr, v_scr,
                    *, sig_col, gate_col, rows):
    w = GROUP_W
    l = u_scr.shape[0]
    n2 = l // (FFT_RADIX // 2)
    u = sig_ref[...]
    if sig_col is not None:
        u = _short_conv(u, cw_ref[:, sig_col * w:(sig_col + 1) * w], cb_ref[:, sig_col * w:(sig_col + 1) * w])
    u_scr[...] = u
    g_scr[...] = _short_conv(gate_ref[...], cw_ref[:, gate_col * w:(gate_col + 1) * w],
                             cb_ref[:, gate_col * w:(gate_col + 1) * w])

    def piece(base, r0):
        return pl.ds(pl.multiple_of(base + r0, rows), rows)

    def fwd_piece(i, carry):
        r0 = i * rows
        for hh in range(w // LANES):
            lanes = slice(hh * LANES, (hh + 1) * LANES)
            fr, fi = _fwd16([u_scr[piece(n1 * n2, r0), lanes] for n1 in range(FFT_RADIX // 2)])
            for k in range(FFT_BINS):
                cols = slice(k * w + hh * LANES, k * w + (hh + 1) * LANES)
                if k == 0:
                    zr, zi = fr[0], jnp.zeros_like(fr[0])
                else:
                    c, s = twc_ref[k, piece(0, r0), :], tws_ref[k, piece(0, r0), :]
                    if fi[k] is None:
                        zr, zi = fr[k] * c, -(fr[k] * s)
                    else:
                        zr, zi = fr[k] * c + fi[k] * s, fi[k] * c - fr[k] * s
                z_scr[piece(0, r0), cols] = zr.astype(BF16)
                z_scr[piece(n2, r0), cols] = zi.astype(BF16)
        return carry

    lax.fori_loop(0, n2 // rows, fwd_piece, 0)

    m, mt = m_ref[...], mt_ref[...]
    for k in range(FFT_BINS):
        cols = slice(k * w, (k + 1) * w)
        x = jnp.dot(m, z_scr[:, cols], preferred_element_type=F32)
        xr, xi = x[:n2], x[n2:]
        kr, ki = kr_ref[k * n2:(k + 1) * n2, :], ki_ref[k * n2:(k + 1) * n2, :]
        y = jnp.concatenate([xr * kr - xi * ki, xr * ki + xi * kr], axis=0).astype(BF16)
        v_scr[:, cols] = jnp.dot(mt, y, preferred_element_type=F32)

    def inv_piece(i, carry):
        r0 = i * rows
        for hh in range(w // LANES):
            lanes = slice(hh * LANES, (hh + 1) * LANES)
            br, bi = [], []
            for k in range(FFT_BINS):
                cols = slice(k * w + hh * LANES, k * w + (hh + 1) * LANES)
                vr, vi = v_scr[piece(0, r0), cols], v_scr[piece(n2, r0), cols]
                if k == 0:
                    br.append(vr * (0.5 / l))
                    bi.append(None)
                else:
                    c, s = twci_ref[k, piece(0, r0), :], twsi_ref[k, piece(0, r0), :]
                    br.append(vr * c - vi * s)
                    bi.append(None if k == FFT_BINS - 1 else vr * s + vi * c)
            y = _inv16(br, bi)
            for n1 in range(FFT_RADIX // 2):
                rws = piece(n1 * n2, r0)
                o_ref[rws, lanes] = g_scr[rws, lanes] * (y[n1] + u_scr[rws, lanes] * bias_ref[:, lanes])
        return carry

    lax.fori_loop(0, n2 // rows, inv_piece, 0)


def _hy_conv(sig, sig_col, hy, gate_col, cw, cb, spec, order, bias, tabs):
    b, l, _ = hy.shape
    w = GROUP_W
    n2 = 2 * l // FFT_RADIX
    rows = min(FFT_PIECE_ROWS, n2)
    kr, ki = spec
    sig_arr = hy if sig_col is not None else sig
    sig_blk = sig_col if sig_col is not None else 0
    seq = lambda col: pl.BlockSpec((None, l, w), lambda i: (i, 0, col))
    spec_blk = pl.BlockSpec((FFT_BINS * n2, w), lambda i: (0, order), pipeline_mode=pl.Buffered(1))
    kernel = functools.partial(_hy_conv_kernel, sig_col=sig_col, gate_col=gate_col, rows=rows)
    return pl.pallas_call(
        kernel,
        grid=(b,),
        in_specs=[seq(sig_blk), seq(gate_col), _const_spec(cw), _const_spec(cb), spec_blk, spec_blk,
                  _const_spec(bias)] + [_const_spec(t) for t in tabs],
        out_specs=seq(0),
        out_shape=jax.ShapeDtypeStruct((b, l, w), F32),
        scratch_shapes=[pltpu.VMEM((l, w), F32), pltpu.VMEM((l, w), F32),
                        pltpu.VMEM((2 * n2, FFT_BINS * w), BF16), pltpu.VMEM((2 * n2, FFT_BINS * w), F32)],
        compiler_params=_params("parallel"),
        name="hy_conv",
    )(sig_arr, hy, cw, cb, kr, ki, bias, *tabs)


@functools.lru_cache(maxsize=None)
def _fft_tables(l):
    n = 2 * l
    n2 = n // FFT_RADIX
    j = np.arange(n2, dtype=np.int64)
    k1 = np.arange(FFT_BINS, dtype=np.int64)
    f32 = lambda a: np.ascontiguousarray(a, dtype=np.float32)
    b16 = lambda a: f32(a).astype(BF16)
    ang2 = ((j[:, None] * j[None, :]) % n2).astype(np.float64) * (2.0 * math.pi / n2)
    c2, s2 = np.cos(ang2), np.sin(ang2)
    m = np.block([[c2, s2], [-s2, c2]])
    angt = (k1[:, None] * j[None, :]).astype(np.float64) * (2.0 * math.pi / n)
    wk = np.where((k1 == 0) | (k1 == FFT_BINS - 1), 1.0, 2.0)[:, None] / n
    lane = lambda a: f32(np.broadcast_to(a[:, :, None], a.shape + (LANES,)))
    bins = (k1[:, None] + FFT_RADIX * j[None, :]).reshape(-1)
    angf = ((bins[:, None] * np.arange(l, dtype=np.int64)[None, :]) % n).astype(np.float64) * (2.0 * math.pi / n)
    conv_tabs = (b16(m), b16(m.T), lane(np.cos(angt)), lane(np.sin(angt)),
                 lane(np.cos(angt) * wk), lane(np.sin(angt) * wk))
    return (f32(np.cos(angf)), f32(np.sin(angf))), conv_tabs


def _hy_features(l):
    t = jnp.linspace(0.0, 1.0, l, dtype=F32)[:, None]
    wpos = 2.0 * math.pi * jnp.arange(l, dtype=F32)[:, None] / l
    fb = jnp.linspace(1e-4, HY_BANDS - 1, HY_BANDS, dtype=F32)[None, :]
    z = jnp.concatenate([t, jnp.cos(fb * wpos), -jnp.sin(fb * wpos)], axis=-1)
    z = jnp.pad(z, ((0, 0), (0, HY_FFN - HY_EMB)))
    deltas = jnp.linspace(math.log(HY_TARGET) / HY_FAST_PCT, math.log(HY_TARGET) / HY_SLOW_PCT, HY_NFILT, dtype=F32)
    return z, t, jnp.abs(deltas)[None, :]


def _hyena(hy, feats, spec_tabs, conv_tabs, cw, cb, w1, b1, w2, b2, w3, b3, fr, w4, bias):
    l = hy.shape[1]
    z, t, dl = feats
    w1p = jnp.pad(w1, ((0, HY_FFN - HY_EMB), (0, 0)))
    h = _hy_filters(l, z, t, w1p, b1[None], w2, b2[None], w3, b3[None], fr[None], w4, dl)
    spec = _hy_spectrum(h, *spec_tabs)
    cb2 = cb[None]
    zz = _hy_conv(None, 0, hy, 1, cw, cb2, spec, 0, bias[0:1], conv_tabs)
    return _hy_conv(zz, None, hy, 2, cw, cb2, spec, 1, bias[1:2], conv_tabs)


def _rope_tables(l, dim, width, off, reps):
    rows = jnp.repeat(jnp.arange(l // GRID_W, dtype=jnp.int32), GRID_W).astype(F32)
    cols = jnp.tile(jnp.arange(GRID_W, dtype=jnp.int32), l // GRID_W).astype(F32)
    quarter = dim // 4
    half = dim // 2
    inv = ROPE_THETA ** (-jnp.arange(quarter, dtype=F32) / quarter)
    ang = jnp.concatenate([rows[:, None] * inv, cols[:, None] * inv], axis=-1)
    c, s = jnp.cos(ang), jnp.sin(ang)
    cf = jnp.ones((l, width), F32).at[:, off:off + dim].set(jnp.concatenate([c, c], axis=-1))
    s1 = jnp.zeros((l, width), F32).at[:, off:off + half].set(-s)
    s2 = jnp.zeros((l, width), F32).at[:, off + half:off + dim].set(s)
    return tuple(jnp.tile(a, (1, reps)) for a in (cf, s1, s2))


def _attn_kernel(q_ref, *rest, rep):
    o_ref = rest[-1]
    kv = rest[:-1]
    ks, vs = kv[0::2], kv[1::2]
    tq = q_ref.shape[1]
    lo = lax.broadcasted_iota(jnp.int32, (tq, LANES), 1) < HEAD_D
    nt = (((1,), (1,)), ((), ()))
    n_heads = q_ref.shape[0]

    def scores(h):
        return [lax.dot_general(q_ref[h], k_ref[h // rep], nt, preferred_element_type=F32) for k_ref in ks]

    def weighted(h, ss):
        m = ss[0].max(axis=-1, keepdims=True)
        for s in ss[1:]:
            m = jnp.maximum(m, s.max(axis=-1, keepdims=True))
        den = 0.0
        acc = 0.0
        for s, v_ref in zip(ss, vs):
            e = jnp.exp(s - m)
            den = den + e.sum(axis=-1, keepdims=True)
            acc = acc + jnp.dot(e.astype(BF16), v_ref[h // rep], preferred_element_type=F32)
        return acc / den

    outs = []
    ss_next = scores(0)
    for h in range(n_heads):
        ss = ss_next
        if h + 1 < n_heads:
            ss_next = scores(h + 1)
        outs.append(weighted(h, ss))
    for p in range(n_heads // 2):
        o_ref[:, p * LANES:(p + 1) * LANES] = jnp.where(lo, outs[2 * p], outs[2 * p + 1])


def _attention(q, kvs, rep, b, l):
    nh = q.shape[0]
    tq = min(512, l)
    nq = l // tq
    specs = [pl.BlockSpec((nh, tq, LANES), lambda i, j: (0, i * nq + j, 0))]
    args = [q]
    for k, v, lk in kvs:
        spec = pl.BlockSpec((k.shape[0], lk, LANES), lambda i, j: (0, i, 0))
        specs += [spec, spec]
        args += [k, v]
    return pl.pallas_call(
        functools.partial(_attn_kernel, rep=rep),
        grid=(b, nq),
        in_specs=specs,
        out_specs=pl.BlockSpec((None, tq, GROUP_W), lambda i, j: (i, j, 0)),
        out_shape=jax.ShapeDtypeStruct((b, l, GROUP_W), F32),
        compiler_params=_params("parallel", "parallel"),
        name="attention",
    )(*args)


def _ret_kernel(qs, ks, v_ref, g_ref, dl_ref, dc_ref, ng_ref, seg_ref, sf0_ref, sb0_ref,
                y_ref, sf_ref, sb_ref, sball, *, chunk, unroll):
    c_ = chunk
    w = GROUP_W
    n_chunks = qs.shape[0] // c_
    unroll = min(unroll, n_chunks)
    shift_c = int(math.log2(c_))
    shift_h = int(math.log2(HEAD_D))
    tn = (((0,), (0,)), ((), ()))
    nt = (((1,), (1,)), ((), ()))

    lgf = _log_sigmoid(dl_ref[0:1, :])
    lgb = _log_sigmoid(dl_ref[1:2, :])
    ii = lax.broadcasted_iota(jnp.int32, (c_, 1), 0).astype(F32)
    qdf = jnp.exp((ii + 1.0) * lgf)
    kdf = jnp.exp((c_ - 1.0 - ii) * lgf)
    cdf = jnp.exp(float(c_) * lgf)
    qdb = jnp.exp((c_ - ii) * lgb)
    kdb = jnp.exp(ii * lgb)
    cdb = jnp.exp(float(c_) * lgb)
    r_h = lax.shift_right_logical(lax.broadcasted_iota(jnp.int32, (w, w), 0), shift_h)
    c_h = lax.shift_right_logical(lax.broadcasted_iota(jnp.int32, (w, w), 1), shift_h)
    blockdiag = r_h == c_h
    lgc = _log_sigmoid(dc_ref[...])
    i4 = jnp.bitwise_and(lax.broadcasted_iota(jnp.int32, (4 * c_, c_), 0), c_ - 1)
    j4 = lax.broadcasted_iota(jnp.int32, (4 * c_, c_), 1)
    diff = (i4 - j4).astype(F32)
    dmat = (jnp.where(diff >= 0, jnp.exp(lgc[:, 0:1] * jnp.maximum(diff, 0.0)), 0.0)
            + jnp.where(diff <= 0, jnp.exp(lgc[:, 1:2] * jnp.maximum(-diff, 0.0)), 0.0))
    row_h = lax.shift_right_logical(lax.broadcasted_iota(jnp.int32, (4 * c_, w), 0), shift_c)
    lane_h = lax.shift_right_logical(lax.broadcasted_iota(jnp.int32, (4 * c_, w), 1), shift_h)
    headmask = row_h == lane_h
    seg = seg_ref[...]

    def rows(c):
        return pl.ds(pl.multiple_of(c * c_, c_), c_)

    def kv_outer(k, v, dec):
        a = lax.dot_general((k * dec).astype(BF16), v.astype(BF16), tn, preferred_element_type=F32)
        return jnp.where(blockdiag, a, 0.0)

    def bwd(t, s):
        c = n_chunks - 1 - t
        sball[c] = s
        sl = rows(c)
        return s * cdb + kv_outer(ks[sl, :], v_ref[sl, :], kdb)

    sb_ref[...] = lax.fori_loop(0, n_chunks, bwd, sb0_ref[...], unroll=unroll)

    def fwd(c, s):
        sl = rows(c)
        q, k, v = qs[sl, :], ks[sl, :], v_ref[sl, :]
        vb = v.astype(BF16)
        q4 = jnp.where(headmask, jnp.concatenate([q, q, q, q], axis=0), 0.0).astype(BF16)
        sc = lax.dot_general(q4, k.astype(BF16), nt, preferred_element_type=F32)
        o4 = jnp.dot((sc * dmat).astype(BF16), vb, preferred_element_type=F32)
        o4 = jnp.where(headmask, o4, 0.0)
        o = o4[0:c_] + o4[c_:2 * c_] + o4[2 * c_:3 * c_] + o4[3 * c_:4 * c_]
        o = o + jnp.dot((q * qdf).astype(BF16), s.astype(BF16), preferred_element_type=F32)
        o = o + jnp.dot((q * qdb).astype(BF16), sball[c].astype(BF16), preferred_element_type=F32)
        on = o * lax.rsqrt(_seg_mean(o * o, seg) + EPS) * ng_ref[...]
        g = g_ref[sl, :]
        y_ref[sl, :] = on * (g * jax.nn.sigmoid(g))
        return s * cdf + kv_outer(k, v, kdf)

    sf_ref[...] = lax.fori_loop(0, n_chunks, fwd, sf0_ref[...], unroll=unroll)


def _retention(ret, dl, dc, ng, seg, sf0, sb0):
    b, l, _ = ret.shape
    w = GROUP_W
    col = lambda j: pl.BlockSpec((None, l, w), lambda i: (i, 0, j))
    state = pl.BlockSpec((None, w, w), lambda i: (i, 0, 0))
    sshape = jax.ShapeDtypeStruct((b, w, w), F32)
    return pl.pallas_call(
        functools.partial(_ret_kernel, chunk=RET_CHUNK, unroll=RET_UNROLL),
        grid=(b,),
        in_specs=[col(0), col(1), col(2), col(3), _const_spec(dl), _const_spec(dc), _const_spec(ng),
                  _const_spec(seg), state, state],
        out_specs=[pl.BlockSpec((None, l, w), lambda i: (i, 0, 0)), state, state],
        out_shape=[jax.ShapeDtypeStruct((b, l, w), F32), sshape, sshape],
        scratch_shapes=[pltpu.VMEM((l // RET_CHUNK, w, w), F32)],
        compiler_params=_params("parallel"),
        name="retention",
    )(ret, ret, ret, ret, dl, dc, ng, seg, sf0, sb0)


def _mlp_kernel(x_ref, ya_ref, yb_ref, yc_ref, yd_ref, gout_ref, ga_ref, wout_ref,
                gffn_ref, sh_ref, sc_ref, gf_ref, wfi_ref, wfo_ref, fg_ref, o_ref,
                x1_a, h_a, x1_b, h_b, *, final_norm):
    t = pl.program_id(0)
    nff = wfo_ref.shape[0]

    @pl.when(t == 0)
    def _():
        x1_b[...] = jnp.zeros_like(x1_b)
        h_b[...] = jnp.zeros_like(h_b)

    def step(x1_w, h_w, x1_r, h_r):
        acc = 0.0
        for i, y_ref in enumerate((ya_ref, yb_ref, yc_ref, yd_ref)):
            rows = slice(i * GROUP_W, (i + 1) * GROUP_W)
            yn = (_rms(y_ref[...]) * gout_ref[:, rows]).astype(BF16)
            acc = acc + jnp.dot(yn, wout_ref[rows, :], preferred_element_type=F32)
        x1 = x_ref[...] + ga_ref[...] * acc
        x1_w[...] = x1
        h_w[...] = (_rms(x1) * gffn_ref[...] * (1.0 + sc_ref[...]) + sh_ref[...]).astype(BF16)

        h = h_r[...]
        ff = 0.0
        for c in range(nff):
            a = jnp.dot(h, wfi_ref[c], preferred_element_type=F32)
            g = jnp.dot(h, wfi_ref[c + nff], preferred_element_type=F32)
            u = (a * jax.nn.sigmoid(a) * g).astype(BF16)
            ff = ff + jnp.dot(u, wfo_ref[c], preferred_element_type=F32)
        out = x1_r[...] + gf_ref[...] * ff
        if final_norm:
            out = _rms(out) * fg_ref[...]
        o_ref[...] = out

    @pl.when(t % 2 == 0)
    def _():
        step(x1_a, h_a, x1_b, h_b)

    @pl.when(t % 2 == 1)
    def _():
        step(x1_b, h_b, x1_a, h_a)


def _mlp(x, ys, g_out, g_a, w_out, g_ffn, sh, sc, g_f, wfi, wfo, final_g, final_norm):
    b, l, d = x.shape
    tm = min(512, l)
    nt = l // tm
    n = b * nt
    cur = lambda t: jnp.minimum(t, n - 1)
    prv = lambda t: jnp.maximum(t - 1, 0)
    tok = lambda wd: pl.BlockSpec((None, tm, wd), lambda t: (cur(t) // nt, cur(t) % nt, 0))
    vec = pl.BlockSpec((None, 1, d), lambda t: (cur(t) // nt, 0, 0))
    return pl.pallas_call(
        functools.partial(_mlp_kernel, final_norm=final_norm),
        grid=(n + 1,),
        in_specs=[tok(d)] + [tok(GROUP_W)] * 4 + [
            _const_spec(g_out), vec, _const_spec(w_out), _const_spec(g_ffn), vec, vec,
            pl.BlockSpec((None, 1, d), lambda t: (prv(t) // nt, 0, 0)),
            _const_spec(wfi), _const_spec(wfo), _const_spec(final_g)],
        out_specs=pl.BlockSpec((None, tm, d), lambda t: (prv(t) // nt, prv(t) % nt, 0)),
        out_shape=jax.ShapeDtypeStruct((b, l, d), F32),
        scratch_shapes=[pltpu.VMEM((tm, d), F32), pltpu.VMEM((tm, d), BF16),
                        pltpu.VMEM((tm, d), F32), pltpu.VMEM((tm, d), BF16)],
        compiler_params=_params("arbitrary"),
        name="mlp",
    )(x, *ys, g_out, g_a, w_out, g_ffn, sh, sc, g_f, wfi, wfo, final_g)


def _pad_proj(w):
    z = jnp.zeros((w.shape[0], LANES - MLA_ROPE), w.dtype)
    return jnp.concatenate([w[:, :1184], z, w[:, 1184:]], axis=1).astype(BF16)


def _mla_weights(wuq, wukv):
    r = wuq.shape[0]
    q = wuq.reshape(r, 4, MLA_NOPE + MLA_ROPE)
    q = jnp.pad(q, ((0, 0), (0, 0), (0, LANES - MLA_NOPE - MLA_ROPE))).reshape(r, 4 * LANES)
    kv = wukv.reshape(wukv.shape[0], 4, MLA_NOPE + MLA_V)
    k = jnp.pad(kv[..., :MLA_NOPE], ((0, 0), (0, 0), (0, LANES - MLA_NOPE))).reshape(-1, 4 * LANES)
    v = jnp.concatenate([kv[..., MLA_NOPE:], kv[..., MLA_NOPE:]], axis=-1).reshape(-1, 4 * LANES)
    return q.astype(BF16), k.astype(BF16), v.astype(BF16)


def _ffn_weights(w_in, w_out):
    d = w_in.shape[0]
    nff = D_FF // FF_TILE
    wfi = w_in.astype(BF16).reshape(d, 2 * nff, FF_TILE).transpose(1, 0, 2)
    return wfi, w_out.astype(BF16).reshape(nff, FF_TILE, d)


def _seg_matrix(width):
    i = jnp.arange(width) // HEAD_D
    return jnp.where(i[:, None] == i[None, :], 1.0 / HEAD_D, 0.0).astype(BF16)


def kernel(x, c, ctx, c_ctx, w_mod, b_mod, norm_attn_g, norm_ffn_g, w_in, hy_conv_w, hy_conv_b, hy_w1, hy_b1, hy_w2, hy_b2, hy_w3, hy_b3, hy_freq, hy_w4, hy_bias, mla_q_norm_g, mla_w_uq, mla_kv_norm_g, mla_w_ukv, ret_decay, ret_norm_g, gqa_q_norm_g, gqa_k_norm_g, out_norm_g, w_out, w_ffn_in, w_ffn_out, final_norm_g):
    b, n_lat, d = x.shape
    n_ctx = ctx.shape[1]
    depth = w_mod.shape[0]

    rope_mla = _rope_tables(n_lat, MLA_ROPE, LANES, MLA_NOPE, 1)
    rope_hd2 = _rope_tables(n_lat, HEAD_D, HEAD_D, 0, 2)
    seg2, seg4 = _seg_matrix(LANES), _seg_matrix(GROUP_W)
    hy_consts = {n: (_hy_features(n),) + _fft_tables(n) for n in {n_lat, n_ctx}}
    zero_state = jnp.zeros((b, GROUP_W, GROUP_W), F32)

    rows = -(-(b + 1) // 8) * 8
    cond = jnp.zeros((rows, d), F32).at[:b].set(c).at[b].set(c_ctx)
    mod = _ada_mod(cond, w_mod, b_mod)

    xl, xc = x, ctx.reshape(1, b * n_ctx, d)
    for l in range(depth):
        update_ctx = l < depth - 1
        m_lat = [m[:b, None, :] for m in jnp.split(mod[l], 6, axis=-1)]
        m_ctx = [m[b][None, None, :] for m in jnp.split(mod[l], 6, axis=-1)]
        wp = _pad_proj(w_in[l])
        g_attn = norm_attn_g[l][None]
        mla_w = (mla_q_norm_g[l][None], mla_kv_norm_g[l][None]) + _mla_weights(mla_w_uq[l], mla_w_ukv[l])
        gqa_w = (jnp.tile(gqa_q_norm_g[l], 2)[None], jnp.tile(gqa_k_norm_g[l], 2)[None], seg2)
        hy_l, ret_l, q_l, k_l, v_l, gq_l, gk_l, gv_l = _front(
            xl, g_attn, m_lat[0], m_lat[1], wp, mla_w, gqa_w, (rope_mla, rope_hd2))
        hy_c, ret_c, q_c, k_c, v_c, gq_c, gk_c, gv_c = _front(
            xc, g_attn, m_ctx[0], m_ctx[1], wp, mla_w, gqa_w, None)
        hy_c = hy_c.reshape(b, n_ctx, -1)
        ret_c = ret_c.reshape(b, n_ctx, -1)

        hy_args = (hy_conv_w[l], hy_conv_b[l], hy_w1[l], hy_b1[l], hy_w2[l], hy_b2[l], hy_w3[l], hy_b3[l],
                   hy_freq[l], hy_w4[l], hy_bias[l])
        ya_l = _hyena(hy_l, *hy_consts[n_lat], *hy_args)
        yb_l = _attention(q_l, [(k_c, v_c, n_ctx), (k_l, v_l, n_lat)], 1, b, n_lat)

        dl = jnp.repeat(ret_decay[l], HEAD_D, axis=1)
        dc = jnp.repeat(ret_decay[l].T, RET_CHUNK, axis=0)
        ng = ret_norm_g[l][None]
        yc_c, s_f, s_b = _retention(ret_c, dl, dc, ng, seg4, zero_state, zero_state)
        yc_l, _, _ = _retention(ret_l, dl, dc, ng, seg4, s_f, s_b)
        yd_l = _attention(gq_l, [(gk_c, gv_c, n_ctx), (gk_l, gv_l, n_lat)], 2, b, n_lat)

        wo = w_out[l].astype(BF16)
        wfi, wfo = _ffn_weights(w_ffn_in[l], w_ffn_out[l])
        g_out, g_ffn, g_fin = out_norm_g[l][None], norm_ffn_g[l][None], final_norm_g[None]
        xl = _mlp(xl, (ya_l, yb_l, yc_l, yd_l), g_out, m_lat[2], wo, g_ffn, m_lat[3], m_lat[4], m_lat[5],
                  wfi, wfo, g_fin, l == depth - 1)

        if update_ctx:
            flat = lambda a: a.reshape(1, b * n_ctx, GROUP_W)
            ya_c = _hyena(hy_c, *hy_consts[n_ctx], *hy_args)
            yb_c = _attention(q_c, [(k_c, v_c, n_ctx)], 1, b, n_ctx)
            yd_c = _attention(gq_c, [(gk_c, gv_c, n_ctx)], 2, b, n_ctx)
            xc = _mlp(xc, tuple(flat(a) for a in (ya_c, yb_c, yc_c, yd_c)), g_out, m_ctx[2], wo,
                      g_ffn, m_ctx[3], m_ctx[4], m_ctx[5], wfi, wfo, g_fin, False)

    return xl
```

```python
import functools
import math

import numpy as np
import jax
import jax.numpy as jnp
from jax import lax
from jax.experimental import pallas as pl
from jax.experimental.pallas import tpu as pltpu

F32 = jnp.float32
BF16 = jnp.bfloat16
HI = lax.Precision.HIGHEST

D_MODEL = 1024
GRID_W = 64
EPS = 1e-6
ROPE_THETA = 10000.0
GROUP_W = 256
LANES = 128

HY_EMB = 33
HY_BANDS = 16
HY_FFN = 64
HY_NFILT = 1024
HY_DECAY_SHIFT = 0.05
HY_FAST_PCT = 0.3
HY_SLOW_PCT = 1.5
HY_TARGET = 1e-2
FFT_RADIX = 16
FFT_BINS = FFT_RADIX // 2 + 1
FFT_PIECE_ROWS = 32

MLA_NOPE = 64
MLA_ROPE = 32
MLA_V = 64
MLA_KV_RANK = 128
LOG2E = math.log2(math.e)
MLA_SCALE = (MLA_NOPE + MLA_ROPE) ** -0.5 * LOG2E

HEAD_D = 64
RET_CHUNK = 128
RET_UNROLL = 8
GQA_SCALE = HEAD_D ** -0.5 * LOG2E
RET_K_SCALE = HEAD_D ** -0.5

D_FF = 2816
FF_TILE = 256

PROJ_SPLITS = (768, 256, 256, 1024, 512)
PROJ_OFFSETS = (0, 768, 1024, 1280, 2304, 2816)

VMEM_LIMIT = 56 * 1024 * 1024


def _params(*sem):
    return pltpu.CompilerParams(dimension_semantics=sem, vmem_limit_bytes=VMEM_LIMIT)


def _const_spec(a):
    nd = a.ndim
    return pl.BlockSpec(a.shape, lambda *_: (0,) * nd, pipeline_mode=pl.Buffered(1))


def _rms(x):
    return x * lax.rsqrt(jnp.mean(x * x, axis=-1, keepdims=True) + EPS)


def _rope(x, cf, s1, s2, half):
    w = x.shape[-1]
    return x * cf + pltpu.roll(x, w - half, 1) * s1 + pltpu.roll(x, half, 1) * s2


def _seg_mean(sq, seg):
    hi = sq.astype(BF16)
    lo = (sq - hi.astype(F32)).astype(BF16)
    return (jnp.dot(hi, seg, preferred_element_type=F32) + jnp.dot(lo, seg, preferred_element_type=F32))


def _log_sigmoid(x):
    return -(jnp.maximum(-x, 0.0) + jnp.log(1.0 + jnp.exp(-jnp.abs(x))))


def _ada_kernel(c_ref, w_ref, b_ref, o_ref):
    c = c_ref[...]
    s = c * jax.nn.sigmoid(c)
    o_ref[...] = jnp.dot(s, w_ref[...], precision=HI, preferred_element_type=F32) + b_ref[...]


def _ada_mod(cond, w_mod, b_mod):
    depth, d, n = w_mod.shape
    r = cond.shape[0]
    tn = 1024
    return pl.pallas_call(
        _ada_kernel,
        grid=(depth, n // tn),
        in_specs=[pl.BlockSpec((r, d), lambda l, j: (0, 0)),
                  pl.BlockSpec((None, d, tn), lambda l, j: (l, 0, j)),
                  pl.BlockSpec((None, 1, tn), lambda l, j: (l, 0, j))],
        out_specs=pl.BlockSpec((None, r, tn), lambda l, j: (l, 0, j)),
        out_shape=jax.ShapeDtypeStruct((depth, r, n), F32),
        compiler_params=_params("arbitrary", "arbitrary"),
        name="ada_mod",
    )(cond, w_mod, b_mod.reshape(depth, 1, n))


def _mla_heads(cq, ckvr, gq, gkv, wuq_ref, wk_ref, wv_ref, tables, q_ref, k_ref, v_ref):
    half = MLA_ROPE // 2
    qn = (_rms(cq) * gq).astype(BF16)
    q = jnp.dot(qn, wuq_ref[...], preferred_element_type=F32)
    kvn = (_rms(ckvr[:, :MLA_KV_RANK]) * gkv).astype(BF16)
    k = jnp.dot(kvn, wk_ref[...], preferred_element_type=F32)
    v = jnp.dot(kvn, wv_ref[...], preferred_element_type=F32)
    kpe = pltpu.roll(ckvr[:, MLA_KV_RANK:], MLA_NOPE, 1)
    if tables is not None:
        kpe = _rope(kpe, *tables, half)
    for h in range(q_ref.shape[0]):
        cols = slice(h * LANES, (h + 1) * LANES)
        qh = q[:, cols]
        if tables is not None:
            qh = _rope(qh, *tables, half)
        q_ref[h] = (qh * MLA_SCALE).astype(BF16)
        k_ref[h] = (k[:, cols] + kpe).astype(BF16)
        v_ref[h] = v[:, cols].astype(BF16)


def _gqa_heads(x, gq, gk, seg, tables, q_ref, k_ref, v_ref):
    half = HEAD_D // 2
    lo = lax.broadcasted_iota(jnp.int32, (x.shape[0], LANES), 1) < HEAD_D

    def normed(a, g):
        a = a * lax.rsqrt(_seg_mean(a * a, seg) + EPS) * g
        return _rope(a, *tables, half) if tables is not None else a

    def dup(a, g):
        a = jnp.where(lo if g == 0 else jnp.logical_not(lo), a, 0.0)
        return (a + pltpu.roll(a, HEAD_D, 1)).astype(BF16)

    for p in range(2):
        qb = normed(x[:, p * LANES:(p + 1) * LANES], gq) * GQA_SCALE
        q_ref[2 * p] = jnp.where(lo, qb, 0.0).astype(BF16)
        q_ref[2 * p + 1] = jnp.where(lo, 0.0, qb).astype(BF16)
    kb = normed(x[:, 2 * LANES:3 * LANES], gk)
    vb = x[:, 3 * LANES:4 * LANES]
    for g in range(2):
        k_ref[g] = dup(kb, g)
        v_ref[g] = dup(vb, g)


def _front_kernel(x_ref, g_ref, sh_ref, sc_ref, w_ref, gq_ref, gkv_ref, wuq_ref, wk_ref, wv_ref,
                  ggq_ref, ggk_ref, seg_ref, *rest, rope):
    if rope:
        mla_t = tuple(r[...] for r in rest[0:3])
        hd_t = tuple(r[...] for r in rest[3:6])
        rest = rest[6:]
    else:
        mla_t = hd_t = None
    hy_ref, ret_ref, mq_ref, mk_ref, mv_ref, gq_out, gk_out, gv_out = rest
    y = _rms(x_ref[...]) * g_ref[...]
    h = (y * (1.0 + sc_ref[...]) + sh_ref[...]).astype(BF16)
    o0, o1, o2, o3, o4, o5 = PROJ_OFFSETS

    def proj(a, b):
        return jnp.dot(h, w_ref[:, a:b], preferred_element_type=F32)

    _mla_heads(proj(o1, o2), proj(o2, o3), gq_ref[...], gkv_ref[...], wuq_ref, wk_ref, wv_ref, mla_t,
               mq_ref, mk_ref, mv_ref)
    _gqa_heads(proj(o4, o5), ggq_ref[...], ggk_ref[...], seg_ref[...], hd_t, gq_out, gk_out, gv_out)
    for i in range(2 * GROUP_W // LANES):
        cols = slice(i * LANES, (i + 1) * LANES)
        a = proj(o3 + i * LANES, o3 + (i + 1) * LANES)
        if i * LANES >= GROUP_W:
            a = a * RET_K_SCALE
        ret_ref[:, cols] = _rope(a, *hd_t, HEAD_D // 2) if rope else a
    ret_ref[:, 2 * GROUP_W:] = proj(o3 + 2 * GROUP_W, o4)
    hy_ref[...] = proj(o0, o1)


def _front(x, g, sh, sc, w, mla_w, gqa_w, tables):
    b, l, d = x.shape
    tm = min(512, l)
    nt = l // tm
    rope = tables is not None
    tok = lambda wd: pl.BlockSpec((None, tm, wd), lambda i, j: (i, j, 0))
    vec = pl.BlockSpec((None, 1, d), lambda i, j: (i, 0, 0))
    head = lambda nh: pl.BlockSpec((nh, tm, LANES), lambda i, j: (0, i * nt + j, 0))
    hshape = lambda nh: jax.ShapeDtypeStruct((nh, b * l, LANES), BF16)
    consts = [w, *mla_w, *gqa_w]
    args = [x, g, sh, sc, *consts]
    specs = [tok(d), _const_spec(g), vec, vec] + [_const_spec(a) for a in consts]
    if rope:
        args += [*tables[0], *tables[1]]
        specs += [pl.BlockSpec((tm, LANES), lambda i, j: (j, 0))] * 6
    return pl.pallas_call(
        functools.partial(_front_kernel, rope=rope),
        grid=(b, nt),
        in_specs=specs,
        out_specs=[tok(PROJ_SPLITS[0]), tok(PROJ_SPLITS[3]), head(4), head(4), head(4), head(4), head(2), head(2)],
        out_shape=[jax.ShapeDtypeStruct((b, l, PROJ_SPLITS[0]), F32),
                   jax.ShapeDtypeStruct((b, l, PROJ_SPLITS[3]), F32),
                   hshape(4), hshape(4), hshape(4), hshape(4), hshape(2), hshape(2)],
        compiler_params=_params("parallel", "parallel"),
        name="front",
    )(*args)


def _hy_filter_kernel(z_ref, t_ref, w1_ref, b1_ref, w2_ref, b2_ref, w3_ref, b3_ref, fr_ref,
                      w4_ref, dl_ref, o_ref):
    fr = fr_ref[...]
    dot = functools.partial(jnp.dot, precision=HI, preferred_element_type=F32)
    h = jnp.sin(fr * (dot(z_ref[...], w1_ref[...]) + b1_ref[...]))
    h = jnp.sin(fr * (dot(h, w2_ref[...]) + b2_ref[...]))
    h = jnp.sin(fr * (dot(h, w3_ref[...]) + b3_ref[...]))
    filt = dot(h, w4_ref[...]) * (jnp.exp(-t_ref[...] * dl_ref[...]) + HY_DECAY_SHIFT)
    fwd = filt[:, :GROUP_W]
    bwd = filt[:, GROUP_W:]
    row = lax.broadcasted_iota(jnp.int32, bwd.shape, 0)
    bwd = jnp.where(row == 0, 0.0, bwd)
    nrm = (jnp.sum(jnp.abs(fwd), axis=0, keepdims=True) + jnp.sum(jnp.abs(bwd), axis=0, keepdims=True))
    o_ref[:, :GROUP_W] = fwd / nrm
    o_ref[:, GROUP_W:] = bwd / nrm


def _hy_filters(l, z, t, w1, b1, w2, b2, w3, b3, fr, w4, dl):
    full = lambda a: pl.BlockSpec(a.shape, lambda o: (0, 0))
    return pl.pallas_call(
        _hy_filter_kernel,
        grid=(2,),
        in_specs=[full(z), full(t), full(w1), full(b1), full(w2), full(b2), full(w3), full(b3), full(fr),
                  pl.BlockSpec((HY_FFN, 2 * GROUP_W), lambda o: (0, o)),
                  pl.BlockSpec((1, 2 * GROUP_W), lambda o: (0, o))],
        out_specs=pl.BlockSpec((l, 2 * GROUP_W), lambda o: (0, o)),
        out_shape=jax.ShapeDtypeStruct((l, HY_NFILT), F32),
        compiler_params=_params("arbitrary"),
        name="hy_filters",
    )(z, t, w1, b1, w2, b2, w3, b3, fr, w4, dl)


def _hy_spec_kernel(c_ref, s_ref, h_ref, re_ref, im_ref):
    dot = functools.partial(jnp.dot, precision=HI, preferred_element_type=F32)
    for o in range(2):
        hf = h_ref[:, 2 * o * GROUP_W:(2 * o + 1) * GROUP_W]
        hb = h_ref[:, (2 * o + 1) * GROUP_W:(2 * o + 2) * GROUP_W]
        cols = slice(o * GROUP_W, (o + 1) * GROUP_W)
        re_ref[:, cols] = dot(c_ref[...], hf + hb)
        im_ref[:, cols] = -dot(s_ref[...], hf - hb)


def _hy_spectrum(h, cf, sf):
    rows, l = cf.shape
    fb = rows // FFT_BINS if rows // FFT_BINS >= 256 else rows
    out = jax.ShapeDtypeStruct((rows, 2 * GROUP_W), F32)
    blk = pl.BlockSpec((fb, 2 * GROUP_W), lambda f: (f, 0))
    return pl.pallas_call(
        _hy_spec_kernel,
        grid=(rows // fb,),
        in_specs=[pl.BlockSpec((fb, l), lambda f: (f, 0)), pl.BlockSpec((fb, l), lambda f: (f, 0)),
                  _const_spec(h)],
        out_specs=[blk, blk],
        out_shape=[out, out],
        compiler_params=_params("arbitrary"),
        name="hy_spectrum",
    )(cf, sf, h)


def _short_conv(u, w, b):
    n = u.shape[0]
    row = lax.broadcasted_iota(jnp.int32, u.shape, 0)
    up = jnp.where(row == 0, 0.0, pltpu.roll(u, 1, 0))
    dn = jnp.where(row == n - 1, 0.0, pltpu.roll(u, n - 1, 0))
    return w[0:1] * up + w[1:2] * u + w[2:3] * dn + b


_R2 = math.sqrt(0.5)


def _dft8_real4(x0, x1, x2, x3):
    s02, d02 = x0 + x2, x0 - x2
    s13, d13 = x1 + x3, x1 - x3
    p, q = _R2 * d13, _R2 * s13
    re = [s02 + s13, x0 + p, d02, x0 - p, s02 - s13]
    im = [None, -(q + x2), -d13, x2 - q, None]
    return re, im


def _fwd16(a):
    er, ei = _dft8_real4(a[0], a[2], a[4], a[6])
    orr, oi = _dft8_real4(a[1], a[3], a[5], a[7])

    def ext(re, im, k):
        if k <= 4:
            return re[k], im[k]
        return re[8 - k], (None if im[8 - k] is None else -im[8 - k])

    out_r, out_i = [], []
    for k in range(FFT_BINS):
        e_r, e_i = ext(er, ei, k)
        o_r, o_i = ext(orr, oi, k)
        c, s = math.cos(math.pi * k / 8), math.sin(math.pi * k / 8)
        if k == 0:
            out_r.append(e_r + o_r)
            out_i.append(None)
        elif k == 8:
            out_r.append(e_r - o_r)
            out_i.append(None)
        elif k == 4:
            out_r.append(e_r)
            out_i.append(-o_r)
        else:
            out_r.append(e_r + (c * o_r + s * o_i))
            out_i.append(e_i + (c * o_i - s * o_r))
    return out_r, out_i


def _dft8_real4_t(kr, ki):
    a = kr[0] + kr[4]
    b = kr[0] - kr[4]
    x0 = a + kr[2] + kr[1] + kr[3]
    x2 = a - kr[2] - ki[1] + ki[3]
    t1 = _R2 * (kr[1] - kr[3])
    t2 = _R2 * (ki[1] + ki[3])
    return x0, b - ki[2] + t1 - t2, x2, b + ki[2] - t1 - t2


def _inv16(br, bi):
    o_r, o_i = [None] * FFT_BINS, [None] * FFT_BINS
    for k in range(FFT_BINS):
        c, s = math.cos(math.pi * k / 8), math.sin(math.pi * k / 8)
        if k == 0:
            o_r[k] = br[k]
        elif k == 8:
            o_r[k] = -br[k]
        elif k == 4:
            o_r[k] = -bi[k]
        else:
            o_r[k] = c * br[k] - s * bi[k]
            o_i[k] = c * bi[k] + s * br[k]

    def fold(re, im):
        fr = [re[0] + re[8], re[1] + re[7], re[2] + re[6], re[3] + re[5], re[4]]
        fi = [None, im[1] - im[7], im[2] - im[6], im[3] - im[5], None]
        return fr, fi

    xe = _dft8_real4_t(*fold(br, bi))
    xo = _dft8_real4_t(*fold(o_r, o_i))
    return [xe[0], xo[0], xe[1], xo[1], xe[2], xo[2], xe[3], xo[3]]


def _hy_conv_kernel(sig_ref, gate_ref, cw_ref, cb_ref, kr_ref, ki_ref, bias_ref, m_ref, mt_ref,
                    twc_ref, tws_ref, twci_ref, twsi_ref, o_ref, u_scr, g_scr, z_scr, v_scr,
                    *, sig_col, gate_col, rows):
    w = GROUP_W
    l = u_scr.shape[0]
    n2 = l // (FFT_RADIX // 2)
    u = sig_ref[...]
    if sig_col is not None:
        u = _short_conv(u, cw_ref[:, sig_col * w:(sig_col + 1) * w], cb_ref[:, sig_col * w:(sig_col + 1) * w])
    u_scr[...] = u
    g_scr[...] = _short_conv(gate_ref[...], cw_ref[:, gate_col * w:(gate_col + 1) * w],
                             cb_ref[:, gate_col * w:(gate_col + 1) * w])

    def piece(base, r0):
        return pl.ds(pl.multiple_of(base + r0, rows), rows)

    def fwd_piece(i, carry):
        r0 = i * rows
        for hh in range(w // LANES):
            lanes = slice(hh * LANES, (hh + 1) * LANES)
            fr, fi = _fwd16([u_scr[piece(n1 * n2, r0), lanes] for n1 in range(FFT_RADIX // 2)])
            for k in range(FFT_BINS):
                cols = slice(k * w + hh * LANES, k * w + (hh + 1) * LANES)
                if k == 0:
                    zr, zi = fr[0], jnp.zeros_like(fr[0])
                else:
                    c, s = twc_ref[k, piece(0, r0), :], tws_ref[k, piece(0, r0), :]
                    if fi[k] is None:
                        zr, zi = fr[k] * c, -(fr[k] * s)
                    else:
                        zr, zi = fr[k] * c + fi[k] * s, fi[k] * c - fr[k] * s
                z_scr[piece(0, r0), cols] = zr.astype(BF16)
                z_scr[piece(n2, r0), cols] = zi.astype(BF16)
        return carry

    lax.fori_loop(0, n2 // rows, fwd_piece, 0)

    m, mt = m_ref[...], mt_ref[...]
    for k in range(FFT_BINS):
        cols = slice(k * w, (k + 1) * w)
        x = jnp.dot(m, z_scr[:, cols], preferred_element_type=F32)
        xr, xi = x[:n2], x[n2:]
        kr, ki = kr_ref[k * n2:(k + 1) * n2, :], ki_ref[k * n2:(k + 1) * n2, :]
        y = jnp.concatenate([xr * kr - xi * ki, xr * ki + xi * kr], axis=0).astype(BF16)
        v_scr[:, cols] = jnp.dot(mt, y, preferred_element_type=F32)

    def inv_piece(i, carry):
        r0 = i * rows
        for hh in range(w // LANES):
            lanes = slice(hh * LANES, (hh + 1) * LANES)
            br, bi = [], []
            for k in range(FFT_BINS):
                cols = slice(k * w + hh * LANES, k * w + (hh + 1) * LANES)
                vr, vi = v_scr[piece(0, r0), cols], v_scr[piece(n2, r0), cols]
                if k == 0:
                    br.append(vr * (0.5 / l))
                    bi.append(None)
                else:
                    c, s = twci_ref[k, piece(0, r0), :], twsi_ref[k, piece(0, r0), :]
                    br.append(vr * c - vi * s)
                    bi.append(None if k == FFT_BINS - 1 else vr * s + vi * c)
            y = _inv16(br, bi)
            for n1 in range(FFT_RADIX // 2):
                rws = piece(n1 * n2, r0)
                o_ref[rws, lanes] = g_scr[rws, lanes] * (y[n1] + u_scr[rws, lanes] * bias_ref[:, lanes])
        return carry

    lax.fori_loop(0, n2 // rows, inv_piece, 0)


def _hy_conv(sig, sig_col, hy, gate_col, cw, cb, spec, order, bias, tabs):
    b, l, _ = hy.shape
    w = GROUP_W
    n2 = 2 * l // FFT_RADIX
    rows = min(FFT_PIECE_ROWS, n2)
    kr, ki = spec
    sig_arr = hy if sig_col is not None else sig
    sig_blk = sig_col if sig_col is not None else 0
    seq = lambda col: pl.BlockSpec((None, l, w), lambda i: (i, 0, col))
    spec_blk = pl.BlockSpec((FFT_BINS * n2, w), lambda i: (0, order), pipeline_mode=pl.Buffered(1))
    kernel = functools.partial(_hy_conv_kernel, sig_col=sig_col, gate_col=gate_col, rows=rows)
    return pl.pallas_call(
        kernel,
        grid=(b,),
        in_specs=[seq(sig_blk), seq(gate_col), _const_spec(cw), _const_spec(cb), spec_blk, spec_blk,
                  _const_spec(bias)] + [_const_spec(t) for t in tabs],
        out_specs=seq(0),
        out_shape=jax.ShapeDtypeStruct((b, l, w), F32),
        scratch_shapes=[pltpu.VMEM((l, w), F32), pltpu.VMEM((l, w), F32),
                        pltpu.VMEM((2 * n2, FFT_BINS * w), BF16), pltpu.VMEM((2 * n2, FFT_BINS * w), F32)],
        compiler_params=_params("parallel"),
        name="hy_conv",
    )(sig_arr, hy, cw, cb, kr, ki, bias, *tabs)


@functools.lru_cache(maxsize=None)
def _fft_tables(l):
    n = 2 * l
    n2 = n // FFT_RADIX
    j = np.arange(n2, dtype=np.int64)
    k1 = np.arange(FFT_BINS, dtype=np.int64)
    f32 = lambda a: np.ascontiguousarray(a, dtype=np.float32)
    b16 = lambda a: f32(a).astype(BF16)
    ang2 = ((j[:, None] * j[None, :]) % n2).astype(np.float64) * (2.0 * math.pi / n2)
    c2, s2 = np.cos(ang2), np.sin(ang2)
    m = np.block([[c2, s2], [-s2, c2]])
    angt = (k1[:, None] * j[None, :]).astype(np.float64) * (2.0 * math.pi / n)
    wk = np.where((k1 == 0) | (k1 == FFT_BINS - 1), 1.0, 2.0)[:, None] / n
    lane = lambda a: f32(np.broadcast_to(a[:, :, None], a.shape + (LANES,)))
    bins = (k1[:, None] + FFT_RADIX * j[None, :]).reshape(-1)
    angf = ((bins[:, None] * np.arange(l, dtype=np.int64)[None, :]) % n).astype(np.float64) * (2.0 * math.pi / n)
    conv_tabs = (b16(m), b16(m.T), lane(np.cos(angt)), lane(np.sin(angt)),
                 lane(np.cos(angt) * wk), lane(np.sin(angt) * wk))
    return (f32(np.cos(angf)), f32(np.sin(angf))), conv_tabs


def _hy_features(l):
    t = jnp.linspace(0.0, 1.0, l, dtype=F32)[:, None]
    wpos = 2.0 * math.pi * jnp.arange(l, dtype=F32)[:, None] / l
    fb = jnp.linspace(1e-4, HY_BANDS - 1, HY_BANDS, dtype=F32)[None, :]
    z = jnp.concatenate([t, jnp.cos(fb * wpos), -jnp.sin(fb * wpos)], axis=-1)
    z = jnp.pad(z, ((0, 0), (0, HY_FFN - HY_EMB)))
    deltas = jnp.linspace(math.log(HY_TARGET) / HY_FAST_PCT, math.log(HY_TARGET) / HY_SLOW_PCT, HY_NFILT, dtype=F32)
    return z, t, jnp.abs(deltas)[None, :]


def _hyena(hy, feats, spec_tabs, conv_tabs, cw, cb, w1, b1, w2, b2, w3, b3, fr, w4, bias):
    l = hy.shape[1]
    z, t, dl = feats
    w1p = jnp.pad(w1, ((0, HY_FFN - HY_EMB), (0, 0)))
    h = _hy_filters(l, z, t, w1p, b1[None], w2, b2[None], w3, b3[None], fr[None], w4, dl)
    spec = _hy_spectrum(h, *spec_tabs)
    cb2 = cb[None]
    zz = _hy_conv(None, 0, hy, 1, cw, cb2, spec, 0, bias[0:1], conv_tabs)
    return _hy_conv(zz, None, hy, 2, cw, cb2, spec, 1, bias[1:2], conv_tabs)


def _rope_tables(l, dim, width, off, reps):
    rows = jnp.repeat(jnp.arange(l // GRID_W, dtype=jnp.int32), GRID_W).astype(F32)
    cols = jnp.tile(jnp.arange(GRID_W, dtype=jnp.int32), l // GRID_W).astype(F32)
    quarter = dim // 4
    half = dim // 2
    inv = ROPE_THETA ** (-jnp.arange(quarter, dtype=F32) / quarter)
    ang = jnp.concatenate([rows[:, None] * inv, cols[:, None] * inv], axis=-1)
    c, s = jnp.cos(ang), jnp.sin(ang)
    cf = jnp.ones((l, width), F32).at[:, off:off + dim].set(jnp.concatenate([c, c], axis=-1))
    s1 = jnp.zeros((l, width), F32).at[:, off:off + half].set(-s)
    s2 = jnp.zeros((l, width), F32).at[:, off + half:off + dim].set(s)
    return tuple(jnp.tile(a, (1, reps)) for a in (cf, s1, s2))


def _attn_kernel(q_ref, *rest, rep):
    o_ref = rest[-1]
    kv = rest[:-1]
    ks, vs = kv[0::2], kv[1::2]
    tq = q_ref.shape[1]
    lo = lax.broadcasted_iota(jnp.int32, (tq, LANES), 1) < HEAD_D
    nt = (((1,), (1,)), ((), ()))
    n_heads = q_ref.shape[0]

    def scores(h):
        return [lax.dot_general(q_ref[h], k_ref[h // rep], nt, preferred_element_type=F32) for k_ref in ks]

    def weighted(h, ss):
        m = ss[0].max(axis=-1, keepdims=True)
        for s in ss[1:]:
            m = jnp.maximum(m, s.max(axis=-1, keepdims=True))
        den = 0.0
        acc = 0.0
        for s, v_ref in zip(ss, vs):
            e = jnp.exp2(s - m)
            den = den + e.sum(axis=-1, keepdims=True)
            acc = acc + jnp.dot(e.astype(BF16), v_ref[h // rep], preferred_element_type=F32)
        return acc / den

    outs = []
    ss_next = scores(0)
    for h in range(n_heads):
        ss = ss_next
        if h + 1 < n_heads:
            ss_next = scores(h + 1)
        outs.append(weighted(h, ss))
    for p in range(n_heads // 2):
        o_ref[:, p * LANES:(p + 1) * LANES] = jnp.where(lo, outs[2 * p], outs[2 * p + 1])


def _attention(q, kvs, rep, b, l):
    nh = q.shape[0]
    tq = min(512, l)
    nq = l // tq
    specs = [pl.BlockSpec((nh, tq, LANES), lambda i, j: (0, i * nq + j, 0))]
    args = [q]
    for k, v, lk in kvs:
        spec = pl.BlockSpec((k.shape[0], lk, LANES), lambda i, j: (0, i, 0))
        specs += [spec, spec]
        args += [k, v]
    return pl.pallas_call(
        functools.partial(_attn_kernel, rep=rep),
        grid=(b, nq),
        in_specs=specs,
        out_specs=pl.BlockSpec((None, tq, GROUP_W), lambda i, j: (i, j, 0)),
        out_shape=jax.ShapeDtypeStruct((b, l, GROUP_W), F32),
        compiler_params=_params("parallel", "parallel"),
        name="attention",
    )(*args)


def _ret_kernel(qs, ks, v_ref, g_ref, dl_ref, dc_ref, ng_ref, seg_ref, sf0_ref, sb0_ref,
                y_ref, sf_ref, sb_ref, sball, *, chunk, unroll):
    c_ = chunk
    w = GROUP_W
    n_chunks = qs.shape[0] // c_
    unroll = min(unroll, n_chunks)
    shift_c = int(math.log2(c_))
    shift_h = int(math.log2(HEAD_D))
    tn = (((0,), (0,)), ((), ()))
    nt = (((1,), (1,)), ((), ()))

    lgf = _log_sigmoid(dl_ref[0:1, :])
    lgb = _log_sigmoid(dl_ref[1:2, :])
    ii = lax.broadcasted_iota(jnp.int32, (c_, 1), 0).astype(F32)
    qdf = jnp.exp((ii + 1.0) * lgf)
    kdf = jnp.exp((c_ - 1.0 - ii) * lgf)
    cdf = jnp.exp(float(c_) * lgf)
    qdb = jnp.exp((c_ - ii) * lgb)
    kdb = jnp.exp(ii * lgb)
    cdb = jnp.exp(float(c_) * lgb)
    r_h = lax.shift_right_logical(lax.broadcasted_iota(jnp.int32, (w, w), 0), shift_h)
    c_h = lax.shift_right_logical(lax.broadcasted_iota(jnp.int32, (w, w), 1), shift_h)
    blockdiag = r_h == c_h
    lgc = _log_sigmoid(dc_ref[...])
    i4 = jnp.bitwise_and(lax.broadcasted_iota(jnp.int32, (4 * c_, c_), 0), c_ - 1)
    j4 = lax.broadcasted_iota(jnp.int32, (4 * c_, c_), 1)
    diff = (i4 - j4).astype(F32)
    dmat = (jnp.where(diff >= 0, jnp.exp(lgc[:, 0:1] * jnp.maximum(diff, 0.0)), 0.0)
            + jnp.where(diff <= 0, jnp.exp(lgc[:, 1:2] * jnp.maximum(-diff, 0.0)), 0.0))
    row_h = lax.shift_right_logical(lax.broadcasted_iota(jnp.int32, (4 * c_, w), 0), shift_c)
    lane_h = lax.shift_right_logical(lax.broadcasted_iota(jnp.int32, (4 * c_, w), 1), shift_h)
    headmask = row_h == lane_h
    seg = seg_ref[...]

    def rows(c):
        return pl.ds(pl.multiple_of(c * c_, c_), c_)

    def kv_outer(k, v, dec):
        a = lax.dot_general((k * dec).astype(BF16), v.astype(BF16), tn, preferred_element_type=F32)
        return jnp.where(blockdiag, a, 0.0)

    def bwd(t, s):
        c = n_chunks - 1 - t
        sball[c] = s
        sl = rows(c)
        return s * cdb + kv_outer(ks[sl, :], v_ref[sl, :], kdb)

    sb_ref[...] = lax.fori_loop(0, n_chunks, bwd, sb0_ref[...], unroll=unroll)

    def fwd(c, s):
        sl = rows(c)
        q, k, v = qs[sl, :], ks[sl, :], v_ref[sl, :]
        vb = v.astype(BF16)
        q4 = jnp.where(headmask, jnp.concatenate([q, q, q, q], axis=0), 0.0).astype(BF16)
        sc = lax.dot_general(q4, k.astype(BF16), nt, preferred_element_type=F32)
        o4 = jnp.dot((sc * dmat).astype(BF16), vb, preferred_element_type=F32)
        o4 = jnp.where(headmask, o4, 0.0)
        o = o4[0:c_] + o4[c_:2 * c_] + o4[2 * c_:3 * c_] + o4[3 * c_:4 * c_]
        o = o + jnp.dot((q * qdf).astype(BF16), s.astype(BF16), preferred_element_type=F32)
        o = o + jnp.dot((q * qdb).astype(BF16), sball[c].astype(BF16), preferred_element_type=F32)
        on = o * lax.rsqrt(_seg_mean(o * o, seg) + EPS) * ng_ref[...]
        g = g_ref[sl, :]
        y_ref[sl, :] = on * (g * jax.nn.sigmoid(g))
        return s * cdf + kv_outer(k, v, kdf)

    sf_ref[...] = lax.fori_loop(0, n_chunks, fwd, sf0_ref[...], unroll=unroll)


def _retention(ret, dl, dc, ng, seg, sf0, sb0):
    b, l, _ = ret.shape
    w = GROUP_W
    col = lambda j: pl.BlockSpec((None, l, w), lambda i: (i, 0, j))
    state = pl.BlockSpec((None, w, w), lambda i: (i, 0, 0))
    sshape = jax.ShapeDtypeStruct((b, w, w), F32)
    return pl.pallas_call(
        functools.partial(_ret_kernel, chunk=RET_CHUNK, unroll=RET_UNROLL),
        grid=(b,),
        in_specs=[col(0), col(1), col(2), col(3), _const_spec(dl), _const_spec(dc), _const_spec(ng),
                  _const_spec(seg), state, state],
        out_specs=[pl.BlockSpec((None, l, w), lambda i: (i, 0, 0)), state, state],
        out_shape=[jax.ShapeDtypeStruct((b, l, w), F32), sshape, sshape],
        scratch_shapes=[pltpu.VMEM((l // RET_CHUNK, w, w), F32)],
        compiler_params=_params("parallel"),
        name="retention",
    )(ret, ret, ret, ret, dl, dc, ng, seg, sf0, sb0)


def _mlp_kernel(x_ref, ya_ref, yb_ref, yc_ref, yd_ref, gout_ref, ga_ref, wout_ref,
                gffn_ref, sh_ref, sc_ref, gf_ref, wfi_ref, wfo_ref, fg_ref, o_ref,
                x1_a, h_a, x1_b, h_b, *, final_norm):
    t = pl.program_id(0)
    nff = wfo_ref.shape[0]

    @pl.when(t == 0)
    def _():
        x1_b[...] = jnp.zeros_like(x1_b)
        h_b[...] = jnp.zeros_like(h_b)

    def step(x1_w, h_w, x1_r, h_r):
        acc = 0.0
        for i, y_ref in enumerate((ya_ref, yb_ref, yc_ref, yd_ref)):
            rows = slice(i * GROUP_W, (i + 1) * GROUP_W)
            yn = (_rms(y_ref[...]) * gout_ref[:, rows]).astype(BF16)
            acc = acc + jnp.dot(yn, wout_ref[rows, :], preferred_element_type=F32)
        x1 = x_ref[...] + ga_ref[...] * acc
        x1_w[...] = x1
        h_w[...] = (_rms(x1) * gffn_ref[...] * (1.0 + sc_ref[...]) + sh_ref[...]).astype(BF16)

        h = h_r[...]
        ff = 0.0
        for c in range(nff):
            a = jnp.dot(h, wfi_ref[c], preferred_element_type=F32)
            g = jnp.dot(h, wfi_ref[c + nff], preferred_element_type=F32)
            u = (a * jax.nn.sigmoid(a) * g).astype(BF16)
            ff = ff + jnp.dot(u, wfo_ref[c], preferred_element_type=F32)
        out = x1_r[...] + gf_ref[...] * ff
        if final_norm:
            out = _rms(out) * fg_ref[...]
        o_ref[...] = out

    @pl.when(t % 2 == 0)
    def _():
        step(x1_a, h_a, x1_b, h_b)

    @pl.when(t % 2 == 1)
    def _():
        step(x1_b, h_b, x1_a, h_a)


def _mlp(x, ys, g_out, g_a, w_out, g_ffn, sh, sc, g_f, wfi, wfo, final_g, final_norm):
    b, l, d = x.shape
    tm = min(512, l)
    nt = l // tm
    n = b * nt
    cur = lambda t: jnp.minimum(t, n - 1)
    prv = lambda t: jnp.maximum(t - 1, 0)
    tok = lambda wd: pl.BlockSpec((None, tm, wd), lambda t: (cur(t) // nt, cur(t) % nt, 0))
    vec = pl.BlockSpec((None, 1, d), lambda t: (cur(t) // nt, 0, 0))
    return pl.pallas_call(
        functools.partial(_mlp_kernel, final_norm=final_norm),
        grid=(n + 1,),
        in_specs=[tok(d)] + [tok(GROUP_W)] * 4 + [
            _const_spec(g_out), vec, _const_spec(w_out), _const_spec(g_ffn), vec, vec,
            pl.BlockSpec((None, 1, d), lambda t: (prv(t) // nt, 0, 0)),
            _const_spec(wfi), _const_spec(wfo), _const_spec(final_g)],
        out_specs=pl.BlockSpec((None, tm, d), lambda t: (prv(t) // nt, prv(t) % nt, 0)),
        out_shape=jax.ShapeDtypeStruct((b, l, d), F32),
        scratch_shapes=[pltpu.VMEM((tm, d), F32), pltpu.VMEM((tm, d), BF16),
                        pltpu.VMEM((tm, d), F32), pltpu.VMEM((tm, d), BF16)],
        compiler_params=_params("arbitrary"),
        name="mlp",
    )(x, *ys, g_out, g_a, w_out, g_ffn, sh, sc, g_f, wfi, wfo, final_g)


def _pad_proj(w):
    z = jnp.zeros((w.shape[0], LANES - MLA_ROPE), w.dtype)
    return jnp.concatenate([w[:, :1184], z, w[:, 1184:]], axis=1).astype(BF16)


def _mla_weights(wuq, wukv):
    r = wuq.shape[0]
    q = wuq.reshape(r, 4, MLA_NOPE + MLA_ROPE)
    q = jnp.pad(q, ((0, 0), (0, 0), (0, LANES - MLA_NOPE - MLA_ROPE))).reshape(r, 4 * LANES)
    kv = wukv.reshape(wukv.shape[0], 4, MLA_NOPE + MLA_V)
    k = jnp.pad(kv[..., :MLA_NOPE], ((0, 0), (0, 0), (0, LANES - MLA_NOPE))).reshape(-1, 4 * LANES)
    v = jnp.concatenate([kv[..., MLA_NOPE:], kv[..., MLA_NOPE:]], axis=-1).reshape(-1, 4 * LANES)
    return q.astype(BF16), k.astype(BF16), v.astype(BF16)


def _ffn_weights(w_in, w_out):
    d = w_in.shape[0]
    nff = D_FF // FF_TILE
    wfi = w_in.astype(BF16).reshape(d, 2 * nff, FF_TILE).transpose(1, 0, 2)
    return wfi, w_out.astype(BF16).reshape(nff, FF_TILE, d)


def _seg_matrix(width):
    i = jnp.arange(width) // HEAD_D
    return jnp.where(i[:, None] == i[None, :], 1.0 / HEAD_D, 0.0).astype(BF16)


def kernel(x, c, ctx, c_ctx, w_mod, b_mod, norm_attn_g, norm_ffn_g, w_in, hy_conv_w, hy_conv_b, hy_w1, hy_b1, hy_w2, hy_b2, hy_w3, hy_b3, hy_freq, hy_w4, hy_bias, mla_q_norm_g, mla_w_uq, mla_kv_norm_g, mla_w_ukv, ret_decay, ret_norm_g, gqa_q_norm_g, gqa_k_norm_g, out_norm_g, w_out, w_ffn_in, w_ffn_out, final_norm_g):
    b, n_lat, d = x.shape
    n_ctx = ctx.shape[1]
    depth = w_mod.shape[0]

    rope_mla = _rope_tables(n_lat, MLA_ROPE, LANES, MLA_NOPE, 1)
    rope_hd2 = _rope_tables(n_lat, HEAD_D, HEAD_D, 0, 2)
    seg2, seg4 = _seg_matrix(LANES), _seg_matrix(GROUP_W)
    hy_consts = {n: (_hy_features(n),) + _fft_tables(n) for n in {n_lat, n_ctx}}
    zero_state = jnp.zeros((b, GROUP_W, GROUP_W), F32)

    rows = -(-(b + 1) // 8) * 8
    cond = jnp.zeros((rows, d), F32).at[:b].set(c).at[b].set(c_ctx)
    mod = _ada_mod(cond, w_mod, b_mod)

    xl, xc = x, ctx.reshape(1, b * n_ctx, d)
    for l in range(depth):
        update_ctx = l < depth - 1
        m_lat = [m[:b, None, :] for m in jnp.split(mod[l], 6, axis=-1)]
        m_ctx = [m[b][None, None, :] for m in jnp.split(mod[l], 6, axis=-1)]
        wp = _pad_proj(w_in[l])
        g_attn = norm_attn_g[l][None]
        mla_w = (mla_q_norm_g[l][None], mla_kv_norm_g[l][None]) + _mla_weights(mla_w_uq[l], mla_w_ukv[l])
        gqa_w = (jnp.tile(gqa_q_norm_g[l], 2)[None], jnp.tile(gqa_k_norm_g[l], 2)[None], seg2)
        hy_l, ret_l, q_l, k_l, v_l, gq_l, gk_l, gv_l = _front(
            xl, g_attn, m_lat[0], m_lat[1], wp, mla_w, gqa_w, (rope_mla, rope_hd2))
        hy_c, ret_c, q_c, k_c, v_c, gq_c, gk_c, gv_c = _front(
            xc, g_attn, m_ctx[0], m_ctx[1], wp, mla_w, gqa_w, None)
        hy_c = hy_c.reshape(b, n_ctx, -1)
        ret_c = ret_c.reshape(b, n_ctx, -1)

        hy_args = (hy_conv_w[l], hy_conv_b[l], hy_w1[l], hy_b1[l], hy_w2[l], hy_b2[l], hy_w3[l], hy_b3[l],
                   hy_freq[l], hy_w4[l], hy_bias[l])
        ya_l = _hyena(hy_l, *hy_consts[n_lat], *hy_args)
        yb_l = _attention(q_l, [(k_c, v_c, n_ctx), (k_l, v_l, n_lat)], 1, b, n_lat)

        dl = jnp.repeat(ret_decay[l], HEAD_D, axis=1)
        dc = jnp.repeat(ret_decay[l].T, RET_CHUNK, axis=0)
        ng = ret_norm_g[l][None]
        yc_c, s_f, s_b = _retention(ret_c, dl, dc, ng, seg4, zero_state, zero_state)
        yc_l, _, _ = _retention(ret_l, dl, dc, ng, seg4, s_f, s_b)
        yd_l = _attention(gq_l, [(gk_c, gv_c, n_ctx), (gk_l, gv_l, n_lat)], 2, b, n_lat)

        wo = w_out[l].astype(BF16)
        wfi, wfo = _ffn_weights(w_ffn_in[l], w_ffn_out[l])
        g_out, g_ffn, g_fin = out_norm_g[l][None], norm_ffn_g[l][None], final_norm_g[None]
        xl = _mlp(xl, (ya_l, yb_l, yc_l, yd_l), g_out, m_lat[2], wo, g_ffn, m_lat[3], m_lat[4], m_lat[5],
                  wfi, wfo, g_fin, l == depth - 1)

        if update_ctx:
            flat = lambda a: a.reshape(1, b * n_ctx, GROUP_W)
            ya_c = _hyena(hy_c, *hy_consts[n_ctx], *hy_args)
            yb_c = _attention(q_c, [(k_c, v_c, n_ctx)], 1, b, n_ctx)
            yd_c = _attention(gq_c, [(gk_c, gv_c, n_ctx)], 2, b, n_ctx)
            xc = _mlp(xc, tuple(flat(a) for a in (ya_c, yb_c, yc_c, yd_c)), g_out, m_ctx[2], wo,
                      g_ffn, m_ctx[3], m_ctx[4], m_ctx[5], wfi, wfo, g_fin, False)

    return xl
```

```python
import functools
import math

import numpy as np
import jax
import jax.numpy as jnp
from jax import lax
from jax.experimental import pallas as pl
from jax.experimental.pallas import tpu as pltpu

F32 = jnp.float32
BF16 = jnp.bfloat16
HI = lax.Precision.HIGHEST

D_MODEL = 1024
GRID_W = 64
EPS = 1e-6
ROPE_THETA = 10000.0
GROUP_W = 256
LANES = 128

HY_EMB = 33
HY_BANDS = 16
HY_FFN = 64
HY_NFILT = 1024
HY_DECAY_SHIFT = 0.05
HY_FAST_PCT = 0.3
HY_SLOW_PCT = 1.5
HY_TARGET = 1e-2
FFT_RADIX = 16
FFT_BINS = FFT_RADIX // 2 + 1
FFT_PIECE_ROWS = 32

MLA_NOPE = 64
MLA_ROPE = 32
MLA_V = 64
MLA_KV_RANK = 128
LOG2E = math.log2(math.e)
MLA_SCALE = (MLA_NOPE + MLA_ROPE) ** -0.5 * LOG2E

HEAD_D = 64
RET_CHUNK = 128
RET_UNROLL = 8
GQA_SCALE = HEAD_D ** -0.5 * LOG2E
RET_K_SCALE = HEAD_D ** -0.5

D_FF = 2816
FF_TILE = 256

PROJ_SPLITS = (768, 256, 256, 1024, 512)
PROJ_OFFSETS = (0, 768, 1024, 1280, 2304, 2816)

VMEM_LIMIT = 56 * 1024 * 1024


def _params(*sem):
    return pltpu.CompilerParams(dimension_semantics=sem, vmem_limit_bytes=VMEM_LIMIT)


def _const_spec(a):
    nd = a.ndim
    return pl.BlockSpec(a.shape, lambda *_: (0,) * nd, pipeline_mode=pl.Buffered(1))


def _rms(x):
    return x * lax.rsqrt(jnp.mean(x * x, axis=-1, keepdims=True) + EPS)


def _rope(x, cf, s1, s2, half):
    w = x.shape[-1]
    return x * cf + pltpu.roll(x, w - half, 1) * s1 + pltpu.roll(x, half, 1) * s2


def _seg_mean(sq, seg):
    hi = sq.astype(BF16)
    lo = (sq - hi.astype(F32)).astype(BF16)
    return (jnp.dot(hi, seg, preferred_element_type=F32) + jnp.dot(lo, seg, preferred_element_type=F32))


def _log_sigmoid(x):
    return -(jnp.maximum(-x, 0.0) + jnp.log(1.0 + jnp.exp(-jnp.abs(x))))


def _ada_kernel(c_ref, w_ref, b_ref, o_ref):
    c = c_ref[...]
    s = c * jax.nn.sigmoid(c)
    o_ref[...] = jnp.dot(s, w_ref[...], precision=HI, preferred_element_type=F32) + b_ref[...]


def _ada_mod(cond, w_mod, b_mod):
    depth, d, n = w_mod.shape
    r = cond.shape[0]
    tn = 1024
    return pl.pallas_call(
        _ada_kernel,
        grid=(depth, n // tn),
        in_specs=[pl.BlockSpec((r, d), lambda l, j: (0, 0)),
                  pl.BlockSpec((None, d, tn), lambda l, j: (l, 0, j)),
                  pl.BlockSpec((None, 1, tn), lambda l, j: (l, 0, j))],
        out_specs=pl.BlockSpec((None, r, tn), lambda l, j: (l, 0, j)),
        out_shape=jax.ShapeDtypeStruct((depth, r, n), F32),
        compiler_params=_params("arbitrary", "arbitrary"),
        name="ada_mod",
    )(cond, w_mod, b_mod.reshape(depth, 1, n))


def _value_with_ones(v2, head):
    lo = lax.broadcasted_iota(jnp.int32, v2.shape, 1) < HEAD_D
    keep = lo if head % 2 == 0 else jnp.logical_not(lo)
    return jnp.where(keep, v2, 1.0).astype(BF16)


def _mla_heads(cq, ckvr, gq, gkv, wuq_ref, wk_ref, wv_ref, tables, q_ref, k_ref, v_ref):
    half = MLA_ROPE // 2
    qn = (_rms(cq) * gq).astype(BF16)
    q = jnp.dot(qn, wuq_ref[...], preferred_element_type=F32)
    kvn = (_rms(ckvr[:, :MLA_KV_RANK]) * gkv).astype(BF16)
    k = jnp.dot(kvn, wk_ref[...], preferred_element_type=F32)
    v = jnp.dot(kvn, wv_ref[...], preferred_element_type=F32)
    kpe = pltpu.roll(ckvr[:, MLA_KV_RANK:], MLA_NOPE, 1)
    if tables is not None:
        kpe = _rope(kpe, *tables, half)
    for h in range(q_ref.shape[0]):
        cols = slice(h * LANES, (h + 1) * LANES)
        qh = q[:, cols]
        if tables is not None:
            qh = _rope(qh, *tables, half)
        q_ref[h] = (qh * MLA_SCALE).astype(BF16)
        k_ref[h] = (k[:, cols] + kpe).astype(BF16)
        v_ref[h] = _value_with_ones(v[:, cols], h)


def _gqa_heads(x, gq, gk, seg, tables, q_ref, k_ref, v_ref):
    half = HEAD_D // 2
    lo = lax.broadcasted_iota(jnp.int32, (x.shape[0], LANES), 1) < HEAD_D

    def normed(a, g):
        a = a * lax.rsqrt(_seg_mean(a * a, seg) + EPS) * g
        return _rope(a, *tables, half) if tables is not None else a

    def dup(a, g):
        a = jnp.where(lo if g == 0 else jnp.logical_not(lo), a, 0.0)
        return a + pltpu.roll(a, HEAD_D, 1)

    for p in range(2):
        qb = normed(x[:, p * LANES:(p + 1) * LANES], gq) * GQA_SCALE
        q_ref[2 * p] = jnp.where(lo, qb, 0.0).astype(BF16)
        q_ref[2 * p + 1] = jnp.where(lo, 0.0, qb).astype(BF16)
    kb = normed(x[:, 2 * LANES:3 * LANES], gk)
    vb = x[:, 3 * LANES:4 * LANES]
    for g in range(2):
        k_ref[g] = dup(kb, g).astype(BF16)
        for r in range(2):
            v_ref[2 * g + r] = _value_with_ones(dup(vb, g), 2 * g + r)


def _front_kernel(x_ref, g_ref, sh_ref, sc_ref, w_ref, gq_ref, gkv_ref, wuq_ref, wk_ref, wv_ref,
                  ggq_ref, ggk_ref, seg_ref, *rest, rope):
    if rope:
        mla_t = tuple(r[...] for r in rest[0:3])
        hd_t = tuple(r[...] for r in rest[3:6])
        rest = rest[6:]
    else:
        mla_t = hd_t = None
    hy_ref, ret_ref, mq_ref, mk_ref, mv_ref, gq_out, gk_out, gv_out = rest
    y = _rms(x_ref[...]) * g_ref[...]
    h = (y * (1.0 + sc_ref[...]) + sh_ref[...]).astype(BF16)
    o0, o1, o2, o3, o4, o5 = PROJ_OFFSETS

    def proj(a, b):
        return jnp.dot(h, w_ref[:, a:b], preferred_element_type=F32)

    _mla_heads(proj(o1, o2), proj(o2, o3), gq_ref[...], gkv_ref[...], wuq_ref, wk_ref, wv_ref, mla_t,
               mq_ref, mk_ref, mv_ref)
    _gqa_heads(proj(o4, o5), ggq_ref[...], ggk_ref[...], seg_ref[...], hd_t, gq_out, gk_out, gv_out)
    for i in range(2 * GROUP_W // LANES):
        cols = slice(i * LANES, (i + 1) * LANES)
        a = proj(o3 + i * LANES, o3 + (i + 1) * LANES)
        if i * LANES >= GROUP_W:
            a = a * RET_K_SCALE
        ret_ref[:, cols] = _rope(a, *hd_t, HEAD_D // 2) if rope else a
    ret_ref[:, 2 * GROUP_W:] = proj(o3 + 2 * GROUP_W, o4)
    hy_ref[...] = proj(o0, o1)


def _front(x, g, sh, sc, w, mla_w, gqa_w, tables):
    b, l, d = x.shape
    tm = min(512, l)
    nt = l // tm
    rope = tables is not None
    tok = lambda wd: pl.BlockSpec((None, tm, wd), lambda i, j: (i, j, 0))
    vec = pl.BlockSpec((None, 1, d), lambda i, j: (i, 0, 0))
    head = lambda nh: pl.BlockSpec((nh, tm, LANES), lambda i, j: (0, i * nt + j, 0))
    hshape = lambda nh: jax.ShapeDtypeStruct((nh, b * l, LANES), BF16)
    consts = [w, *mla_w, *gqa_w]
    args = [x, g, sh, sc, *consts]
    specs = [tok(d), _const_spec(g), vec, vec] + [_const_spec(a) for a in consts]
    if rope:
        args += [*tables[0], *tables[1]]
        specs += [pl.BlockSpec((tm, LANES), lambda i, j: (j, 0))] * 6
    return pl.pallas_call(
        functools.partial(_front_kernel, rope=rope),
        grid=(b, nt),
        in_specs=specs,
        out_specs=[tok(PROJ_SPLITS[0]), tok(PROJ_SPLITS[3]), head(4), head(4), head(4), head(4), head(2), head(4)],
        out_shape=[jax.ShapeDtypeStruct((b, l, PROJ_SPLITS[0]), F32),
                   jax.ShapeDtypeStruct((b, l, PROJ_SPLITS[3]), F32),
                   hshape(4), hshape(4), hshape(4), hshape(4), hshape(2), hshape(4)],
        compiler_params=_params("parallel", "parallel"),
        name="front",
    )(*args)


def _hy_filter_kernel(z_ref, t_ref, w1_ref, b1_ref, w2_ref, b2_ref, w3_ref, b3_ref, fr_ref,
                      w4_ref, dl_ref, o_ref):
    fr = fr_ref[...]
    dot = functools.partial(jnp.dot, precision=HI, preferred_element_type=F32)
    h = jnp.sin(fr * (dot(z_ref[...], w1_ref[...]) + b1_ref[...]))
    h = jnp.sin(fr * (dot(h, w2_ref[...]) + b2_ref[...]))
    h = jnp.sin(fr * (dot(h, w3_ref[...]) + b3_ref[...]))
    filt = dot(h, w4_ref[...]) * (jnp.exp(-t_ref[...] * dl_ref[...]) + HY_DECAY_SHIFT)
    fwd = filt[:, :GROUP_W]
    bwd = filt[:, GROUP_W:]
    row = lax.broadcasted_iota(jnp.int32, bwd.shape, 0)
    bwd = jnp.where(row == 0, 0.0, bwd)
    nrm = (jnp.sum(jnp.abs(fwd), axis=0, keepdims=True) + jnp.sum(jnp.abs(bwd), axis=0, keepdims=True))
    o_ref[:, :GROUP_W] = fwd / nrm
    o_ref[:, GROUP_W:] = bwd / nrm


def _hy_filters(l, z, t, w1, b1, w2, b2, w3, b3, fr, w4, dl):
    full = lambda a: pl.BlockSpec(a.shape, lambda o: (0, 0))
    return pl.pallas_call(
        _hy_filter_kernel,
        grid=(2,),
        in_specs=[full(z), full(t), full(w1), full(b1), full(w2), full(b2), full(w3), full(b3), full(fr),
                  pl.BlockSpec((HY_FFN, 2 * GROUP_W), lambda o: (0, o)),
                  pl.BlockSpec((1, 2 * GROUP_W), lambda o: (0, o))],
        out_specs=pl.BlockSpec((l, 2 * GROUP_W), lambda o: (0, o)),
        out_shape=jax.ShapeDtypeStruct((l, HY_NFILT), F32),
        compiler_params=_params("arbitrary"),
        name="hy_filters",
    )(z, t, w1, b1, w2, b2, w3, b3, fr, w4, dl)


def _split_bf16(x):
    hi = x.astype(BF16)
    return hi, (x - hi.astype(F32)).astype(BF16)


def _dot_3pass(a_hi, a_lo, x):
    x_hi, x_lo = _split_bf16(x)
    dot = functools.partial(jnp.dot, preferred_element_type=F32)
    return dot(a_hi, x_hi) + (dot(a_hi, x_lo) + dot(a_lo, x_hi))


def _hy_spec_kernel(ch_ref, cl_ref, sh_ref, sl_ref, h_ref, re_ref, im_ref):
    for o in range(2):
        hf = h_ref[:, 2 * o * GROUP_W:(2 * o + 1) * GROUP_W]
        hb = h_ref[:, (2 * o + 1) * GROUP_W:(2 * o + 2) * GROUP_W]
        cols = slice(o * GROUP_W, (o + 1) * GROUP_W)
        re_ref[:, cols] = _dot_3pass(ch_ref[...], cl_ref[...], hf + hb)
        im_ref[:, cols] = -_dot_3pass(sh_ref[...], sl_ref[...], hf - hb)


def _hy_spectrum(h, tabs):
    rows, l = tabs[0].shape
    fb = rows // FFT_BINS if rows // FFT_BINS >= 256 else rows
    out = jax.ShapeDtypeStruct((rows, 2 * GROUP_W), F32)
    blk = pl.BlockSpec((fb, 2 * GROUP_W), lambda f: (f, 0))
    return pl.pallas_call(
        _hy_spec_kernel,
        grid=(rows // fb,),
        in_specs=[pl.BlockSpec((fb, l), lambda f: (f, 0))] * 4 + [_const_spec(h)],
        out_specs=[blk, blk],
        out_shape=[out, out],
        compiler_params=_params("arbitrary"),
        name="hy_spectrum",
    )(*tabs, h)


def _short_conv(u, w, b):
    n = u.shape[0]
    row = lax.broadcasted_iota(jnp.int32, u.shape, 0)
    up = jnp.where(row == 0, 0.0, pltpu.roll(u, 1, 0))
    dn = jnp.where(row == n - 1, 0.0, pltpu.roll(u, n - 1, 0))
    return w[0:1] * up + w[1:2] * u + w[2:3] * dn + b


_R2 = math.sqrt(0.5)


def _dft8_real4(x0, x1, x2, x3):
    s02, d02 = x0 + x2, x0 - x2
    s13, d13 = x1 + x3, x1 - x3
    p, q = _R2 * d13, _R2 * s13
    re = [s02 + s13, x0 + p, d02, x0 - p, s02 - s13]
    im = [None, -(q + x2), -d13, x2 - q, None]
    return re, im


def _fwd16(a):
    er, ei = _dft8_real4(a[0], a[2], a[4], a[6])
    orr, oi = _dft8_real4(a[1], a[3], a[5], a[7])

    def ext(re, im, k):
        if k <= 4:
            return re[k], im[k]
        return re[8 - k], (None if im[8 - k] is None else -im[8 - k])

    out_r, out_i = [], []
    for k in range(FFT_BINS):
        e_r, e_i = ext(er, ei, k)
        o_r, o_i = ext(orr, oi, k)
        c, s = math.cos(math.pi * k / 8), math.sin(math.pi * k / 8)
        if k == 0:
            out_r.append(e_r + o_r)
            out_i.append(None)
        elif k == 8:
            out_r.append(e_r - o_r)
            out_i.append(None)
        elif k == 4:
            out_r.append(e_r)
            out_i.append(-o_r)
        else:
            out_r.append(e_r + (c * o_r + s * o_i))
            out_i.append(e_i + (c * o_i - s * o_r))
    return out_r, out_i


def _dft8_real4_t(kr, ki):
    a = kr[0] + kr[4]
    b = kr[0] - kr[4]
    x0 = a + kr[2] + kr[1] + kr[3]
    x2 = a - kr[2] - ki[1] + ki[3]
    t1 = _R2 * (kr[1] - kr[3])
    t2 = _R2 * (ki[1] + ki[3])
    return x0, b - ki[2] + t1 - t2, x2, b + ki[2] - t1 - t2


def _inv16(br, bi):
    o_r, o_i = [None] * FFT_BINS, [None] * FFT_BINS
    for k in range(FFT_BINS):
        c, s = math.cos(math.pi * k / 8), math.sin(math.pi * k / 8)
        if k == 0:
            o_r[k] = br[k]
        elif k == 8:
            o_r[k] = -br[k]
        elif k == 4:
            o_r[k] = -bi[k]
        else:
            o_r[k] = c * br[k] - s * bi[k]
            o_i[k] = c * bi[k] + s * br[k]

    def fold(re, im):
        fr = [re[0] + re[8], re[1] + re[7], re[2] + re[6], re[3] + re[5], re[4]]
        fi = [None, im[1] - im[7], im[2] - im[6], im[3] - im[5], None]
        return fr, fi

    xe = _dft8_real4_t(*fold(br, bi))
    xo = _dft8_real4_t(*fold(o_r, o_i))
    return [xe[0], xo[0], xe[1], xo[1], xe[2], xo[2], xe[3], xo[3]]


def _hy_conv_kernel(sig_ref, gate_ref, cw_ref, cb_ref, kr_ref, ki_ref, bias_ref, m_ref, mt_ref,
                    twc_ref, tws_ref, twci_ref, twsi_ref, o_ref, u_scr, g_scr, z_scr, v_scr,
                    *, sig_col, gate_col, rows):
    w = GROUP_W
    l = u_scr.shape[0]
    n2 = l // (FFT_RADIX // 2)
    u = sig_ref[...]
    if sig_col is not None:
        u = _short_conv(u, cw_ref[:, sig_col * w:(sig_col + 1) * w], cb_ref[:, sig_col * w:(sig_col + 1) * w])
    u_scr[...] = u
    g_scr[...] = _short_conv(gate_ref[...], cw_ref[:, gate_col * w:(gate_col + 1) * w],
                             cb_ref[:, gate_col * w:(gate_col + 1) * w])

    def piece(base, r0):
        return pl.ds(pl.multiple_of(base + r0, rows), rows)

    def fwd_piece(i, carry):
        r0 = i * rows
        for hh in range(w // LANES):
            lanes = slice(hh * LANES, (hh + 1) * LANES)
            fr, fi = _fwd16([u_scr[piece(n1 * n2, r0), lanes] for n1 in range(FFT_RADIX // 2)])
            for k in range(FFT_BINS):
                cols = slice(k * w + hh * LANES, k * w + (hh + 1) * LANES)
                if k == 0:
                    zr, zi = fr[0], jnp.zeros_like(fr[0])
                else:
                    c, s = twc_ref[k, piece(0, r0), :], tws_ref[k, piece(0, r0), :]
                    if fi[k] is None:
                        zr, zi = fr[k] * c, -(fr[k] * s)
                    else:
                        zr, zi = fr[k] * c + fi[k] * s, fi[k] * c - fr[k] * s
                z_scr[piece(0, r0), cols] = zr.astype(BF16)
                z_scr[piece(n2, r0), cols] = zi.astype(BF16)
        return carry

    lax.fori_loop(0, n2 // rows, fwd_piece, 0)

    m, mt = m_ref[...], mt_ref[...]
    for k in range(FFT_BINS):
        cols = slice(k * w, (k + 1) * w)
        x = jnp.dot(m, z_scr[:, cols], preferred_element_type=F32)
        xr, xi = x[:n2], x[n2:]
        kr, ki = kr_ref[k * n2:(k + 1) * n2, :], ki_ref[k * n2:(k + 1) * n2, :]
        y = jnp.concatenate([xr * kr - xi * ki, xr * ki + xi * kr], axis=0).astype(BF16)
        v_scr[:, cols] = jnp.dot(mt, y, preferred_element_type=F32)

    def inv_piece(i, carry):
        r0 = i * rows
        for hh in range(w // LANES):
            lanes = slice(hh * LANES, (hh + 1) * LANES)
            br, bi = [], []
            for k in range(FFT_BINS):
                cols = slice(k * w + hh * LANES, k * w + (hh + 1) * LANES)
                vr, vi = v_scr[piece(0, r0), cols], v_scr[piece(n2, r0), cols]
                if k == 0:
                    br.append(vr * (0.5 / l))
                    bi.append(None)
                else:
                    c, s = twci_ref[k, piece(0, r0), :], twsi_ref[k, piece(0, r0), :]
                    br.append(vr * c - vi * s)
                    bi.append(None if k == FFT_BINS - 1 else vr * s + vi * c)
            y = _inv16(br, bi)
            for n1 in range(FFT_RADIX // 2):
                rws = piece(n1 * n2, r0)
                o_ref[rws, lanes] = g_scr[rws, lanes] * (y[n1] + u_scr[rws, lanes] * bias_ref[:, lanes])
        return carry

    lax.fori_loop(0, n2 // rows, inv_piece, 0)


def _hy_conv(sig, sig_col, hy, gate_col, cw, cb, spec, order, bias, tabs):
    b, l, _ = hy.shape
    w = GROUP_W
    n2 = 2 * l // FFT_RADIX
    rows = min(FFT_PIECE_ROWS, n2)
    kr, ki = spec
    sig_arr = hy if sig_col is not None else sig
    sig_blk = sig_col if sig_col is not None else 0
    seq = lambda col: pl.BlockSpec((None, l, w), lambda i: (i, 0, col))
    spec_blk = pl.BlockSpec((FFT_BINS * n2, w), lambda i: (0, order), pipeline_mode=pl.Buffered(1))
    kernel = functools.partial(_hy_conv_kernel, sig_col=sig_col, gate_col=gate_col, rows=rows)
    return pl.pallas_call(
        kernel,
        grid=(b,),
        in_specs=[seq(sig_blk), seq(gate_col), _const_spec(cw), _const_spec(cb), spec_blk, spec_blk,
                  _const_spec(bias)] + [_const_spec(t) for t in tabs],
        out_specs=seq(0),
        out_shape=jax.ShapeDtypeStruct((b, l, w), F32),
        scratch_shapes=[pltpu.VMEM((l, w), F32), pltpu.VMEM((l, w), F32),
                        pltpu.VMEM((2 * n2, FFT_BINS * w), BF16), pltpu.VMEM((2 * n2, FFT_BINS * w), F32)],
        compiler_params=_params("parallel"),
        name="hy_conv",
    )(sig_arr, hy, cw, cb, kr, ki, bias, *tabs)


@functools.lru_cache(maxsize=None)
def _fft_tables(l):
    n = 2 * l
    n2 = n // FFT_RADIX
    j = np.arange(n2, dtype=np.int64)
    k1 = np.arange(FFT_BINS, dtype=np.int64)
    f32 = lambda a: np.ascontiguousarray(a, dtype=np.float32)
    b16 = lambda a: f32(a).astype(BF16)
    ang2 = ((j[:, None] * j[None, :]) % n2).astype(np.float64) * (2.0 * math.pi / n2)
    c2, s2 = np.cos(ang2), np.sin(ang2)
    m = np.block([[c2, s2], [-s2, c2]])
    angt = (k1[:, None] * j[None, :]).astype(np.float64) * (2.0 * math.pi / n)
    wk = np.where((k1 == 0) | (k1 == FFT_BINS - 1), 1.0, 2.0)[:, None] / n
    lane = lambda a: f32(np.broadcast_to(a[:, :, None], a.shape + (LANES,)))
    bins = (k1[:, None] + FFT_RADIX * j[None, :]).reshape(-1)
    angf = ((bins[:, None] * np.arange(l, dtype=np.int64)[None, :]) % n).astype(np.float64) * (2.0 * math.pi / n)
    conv_tabs = (b16(m), b16(m.T), lane(np.cos(angt)), lane(np.sin(angt)),
                 lane(np.cos(angt) * wk), lane(np.sin(angt) * wk))
    def split(a):
        hi = b16(a)
        return hi, b16(a - hi.astype(np.float64))

    return split(np.cos(angf)) + split(np.sin(angf)), conv_tabs


@functools.lru_cache(maxsize=None)
def _hy_features(l):
    f = np.float32
    t = np.linspace(0.0, 1.0, l, dtype=f)[:, None]
    wpos = (f(2.0 * math.pi) * np.arange(l, dtype=f)[:, None] / f(l)).astype(f)
    fb = np.linspace(1e-4, HY_BANDS - 1, HY_BANDS, dtype=f)[None, :]
    arg = (fb * wpos).astype(f).astype(np.float64)
    z = np.concatenate([t, np.cos(arg), -np.sin(arg), np.zeros((l, HY_FFN - HY_EMB))], axis=-1).astype(f)
    deltas = np.linspace(math.log(HY_TARGET) / HY_FAST_PCT, math.log(HY_TARGET) / HY_SLOW_PCT, HY_NFILT, dtype=f)
    return z, t, np.abs(deltas)[None, :]


def _hyena(hy, feats, spec_tabs, conv_tabs, cw, cb, w1, b1, w2, b2, w3, b3, fr, w4, bias):
    l = hy.shape[1]
    z, t, dl = feats
    w1p = jnp.pad(w1, ((0, HY_FFN - HY_EMB), (0, 0)))
    h = _hy_filters(l, z, t, w1p, b1[None], w2, b2[None], w3, b3[None], fr[None], w4, dl)
    spec = _hy_spectrum(h, spec_tabs)
    cb2 = cb[None]
    zz = _hy_conv(None, 0, hy, 1, cw, cb2, spec, 0, bias[0:1], conv_tabs)
    return _hy_conv(zz, None, hy, 2, cw, cb2, spec, 1, bias[1:2], conv_tabs)


@functools.lru_cache(maxsize=None)
def _rope_tables(l, dim, width, off, reps):
    f = np.float32
    rows = np.repeat(np.arange(l // GRID_W), GRID_W).astype(f)
    cols = np.tile(np.arange(GRID_W), l // GRID_W).astype(f)
    quarter = dim // 4
    half = dim // 2
    inv = (f(ROPE_THETA) ** (-np.arange(quarter, dtype=f) / f(quarter))).astype(f)
    ang = np.concatenate([rows[:, None] * inv, cols[:, None] * inv], axis=-1).astype(f).astype(np.float64)
    c, s = np.cos(ang), np.sin(ang)
    cf = np.ones((l, width))
    s1 = np.zeros((l, width))
    s2 = np.zeros((l, width))
    cf[:, off:off + dim] = np.concatenate([c, c], axis=-1)
    s1[:, off:off + half] = -s
    s2[:, off + half:off + dim] = s
    return tuple(np.tile(a, (1, reps)).astype(f) for a in (cf, s1, s2))


def _attn_kernel(q_ref, *rest, rep):
    o_ref = rest[-1]
    kv = rest[:-1]
    ks, vs = kv[0::2], kv[1::2]
    tq = q_ref.shape[1]
    lo = lax.broadcasted_iota(jnp.int32, (tq, LANES), 1) < HEAD_D
    nt = (((1,), (1,)), ((), ()))
    n_heads = q_ref.shape[0]

    def scores(h):
        return [lax.dot_general(q_ref[h], k_ref[h // rep], nt, preferred_element_type=F32) for k_ref in ks]

    def weighted(h, ss):
        m = ss[0].max(axis=-1, keepdims=True)
        for s in ss[1:]:
            m = jnp.maximum(m, s.max(axis=-1, keepdims=True))
        acc = 0.0
        for s, v_ref in zip(ss, vs):
            acc = acc + jnp.dot(jnp.exp2(s - m).astype(BF16), v_ref[h], preferred_element_type=F32)
        return acc / pltpu.roll(acc, HEAD_D, 1)

    outs = []
    ss_next = scores(0)
    for h in range(n_heads):
        ss = ss_next
        if h + 1 < n_heads:
            ss_next = scores(h + 1)
        outs.append(weighted(h, ss))
    for p in range(n_heads // 2):
        o_ref[:, p * LANES:(p + 1) * LANES] = jnp.where(lo, outs[2 * p], outs[2 * p + 1])


def _attention(q, kvs, rep, b, l):
    nh = q.shape[0]
    tq = min(512, l)
    nq = l // tq
    specs = [pl.BlockSpec((nh, tq, LANES), lambda i, j: (0, i * nq + j, 0))]
    args = [q]
    for k, v, lk in kvs:
        specs += [pl.BlockSpec((k.shape[0], lk, LANES), lambda i, j: (0, i, 0)),
                  pl.BlockSpec((v.shape[0], lk, LANES), lambda i, j: (0, i, 0))]
        args += [k, v]
    return pl.pallas_call(
        functools.partial(_attn_kernel, rep=rep),
        grid=(b, nq),
        in_specs=specs,
        out_specs=pl.BlockSpec((None, tq, GROUP_W), lambda i, j: (i, j, 0)),
        out_shape=jax.ShapeDtypeStruct((b, l, GROUP_W), F32),
        compiler_params=_params("parallel", "parallel"),
        name="attention",
    )(*args)


def _ret_kernel(qs, ks, v_ref, g_ref, dl_ref, dc_ref, ng_ref, seg_ref, sf0_ref, sb0_ref,
                y_ref, sf_ref, sb_ref, sball, *, chunk, unroll):
    c_ = chunk
    w = GROUP_W
    n_chunks = qs.shape[0] // c_
    unroll = min(unroll, n_chunks)
    shift_c = int(math.log2(c_))
    shift_h = int(math.log2(HEAD_D))
    tn = (((0,), (0,)), ((), ()))
    nt = (((1,), (1,)), ((), ()))

    lgf = _log_sigmoid(dl_ref[0:1, :])
    lgb = _log_sigmoid(dl_ref[1:2, :])
    ii = lax.broadcasted_iota(jnp.int32, (c_, 1), 0).astype(F32)
    qdf = jnp.exp((ii + 1.0) * lgf)
    kdf = jnp.exp((c_ - 1.0 - ii) * lgf)
    cdf = jnp.exp(float(c_) * lgf)
    qdb = jnp.exp((c_ - ii) * lgb)
    kdb = jnp.exp(ii * lgb)
    cdb = jnp.exp(float(c_) * lgb)
    r_h = lax.shift_right_logical(lax.broadcasted_iota(jnp.int32, (w, w), 0), shift_h)
    c_h = lax.shift_right_logical(lax.broadcasted_iota(jnp.int32, (w, w), 1), shift_h)
    blockdiag = r_h == c_h
    lgc = _log_sigmoid(dc_ref[...])
    i4 = jnp.bitwise_and(lax.broadcasted_iota(jnp.int32, (4 * c_, c_), 0), c_ - 1)
    j4 = lax.broadcasted_iota(jnp.int32, (4 * c_, c_), 1)
    diff = (i4 - j4).astype(F32)
    dmat = (jnp.where(diff >= 0, jnp.exp(lgc[:, 0:1] * jnp.maximum(diff, 0.0)), 0.0)
            + jnp.where(diff <= 0, jnp.exp(lgc[:, 1:2] * jnp.maximum(-diff, 0.0)), 0.0))
    row_h = lax.shift_right_logical(lax.broadcasted_iota(jnp.int32, (4 * c_, w), 0), shift_c)
    lane_h = lax.shift_right_logical(lax.broadcasted_iota(jnp.int32, (4 * c_, w), 1), shift_h)
    headmask = row_h == lane_h
    seg = seg_ref[...]

    def rows(c):
        return pl.ds(pl.multiple_of(c * c_, c_), c_)

    def kv_outer(k, v, dec):
        a = lax.dot_general((k * dec).astype(BF16), v.astype(BF16), tn, preferred_element_type=F32)
        return jnp.where(blockdiag, a, 0.0)

    def bwd(t, s):
        c = n_chunks - 1 - t
        sball[c] = s
        sl = rows(c)
        return s * cdb + kv_outer(ks[sl, :], v_ref[sl, :], kdb)

    sb_ref[...] = lax.fori_loop(0, n_chunks, bwd, sb0_ref[...], unroll=unroll)

    def fwd(c, s):
        sl = rows(c)
        q, k, v = qs[sl, :], ks[sl, :], v_ref[sl, :]
        vb = v.astype(BF16)
        q4 = jnp.where(headmask, jnp.concatenate([q, q, q, q], axis=0), 0.0).astype(BF16)
        sc = lax.dot_general(q4, k.astype(BF16), nt, preferred_element_type=F32)
        o4 = jnp.dot((sc * dmat).astype(BF16), vb, preferred_element_type=F32)
        o4 = jnp.where(headmask, o4, 0.0)
        o = o4[0:c_] + o4[c_:2 * c_] + o4[2 * c_:3 * c_] + o4[3 * c_:4 * c_]
        o = o + jnp.dot((q * qdf).astype(BF16), s.astype(BF16), preferred_element_type=F32)
        o = o + jnp.dot((q * qdb).astype(BF16), sball[c].astype(BF16), preferred_element_type=F32)
        on = o * lax.rsqrt(_seg_mean(o * o, seg) + EPS) * ng_ref[...]
        g = g_ref[sl, :]
        y_ref[sl, :] = on * (g * jax.nn.sigmoid(g))
        return s * cdf + kv_outer(k, v, kdf)

    sf_ref[...] = lax.fori_loop(0, n_chunks, fwd, sf0_ref[...], unroll=unroll)


def _retention(ret, dl, dc, ng, seg, sf0, sb0):
    b, l, _ = ret.shape
    w = GROUP_W
    col = lambda j: pl.BlockSpec((None, l, w), lambda i: (i, 0, j))
    state = pl.BlockSpec((None, w, w), lambda i: (i, 0, 0))
    sshape = jax.ShapeDtypeStruct((b, w, w), F32)
    return pl.pallas_call(
        functools.partial(_ret_kernel, chunk=RET_CHUNK, unroll=RET_UNROLL),
        grid=(b,),
        in_specs=[col(0), col(1), col(2), col(3), _const_spec(dl), _const_spec(dc), _const_spec(ng),
                  _const_spec(seg), state, state],
        out_specs=[pl.BlockSpec((None, l, w), lambda i: (i, 0, 0)), state, state],
        out_shape=[jax.ShapeDtypeStruct((b, l, w), F32), sshape, sshape],
        scratch_shapes=[pltpu.VMEM((l // RET_CHUNK, w, w), F32)],
        compiler_params=_params("parallel"),
        name="retention",
    )(ret, ret, ret, ret, dl, dc, ng, seg, sf0, sb0)


def _mlp_kernel(x_ref, ya_ref, yb_ref, yc_ref, yd_ref, gout_ref, ga_ref, wout_ref,
                gffn_ref, sh_ref, sc_ref, gf_ref, wfi_ref, wfo_ref, fg_ref, o_ref,
                x1_a, h_a, x1_b, h_b, *, final_norm):
    t = pl.program_id(0)
    nff = wfo_ref.shape[0]

    @pl.when(t == 0)
    def _():
        x1_b[...] = jnp.zeros_like(x1_b)
        h_b[...] = jnp.zeros_like(h_b)

    def step(x1_w, h_w, x1_r, h_r):
        acc = 0.0
        for i, y_ref in enumerate((ya_ref, yb_ref, yc_ref, yd_ref)):
            rows = slice(i * GROUP_W, (i + 1) * GROUP_W)
            yn = (_rms(y_ref[...]) * gout_ref[:, rows]).astype(BF16)
            acc = acc + jnp.dot(yn, wout_ref[rows, :], preferred_element_type=F32)
        x1 = x_ref[...] + ga_ref[...] * acc
        x1_w[...] = x1
        h_w[...] = (_rms(x1) * gffn_ref[...] * (1.0 + sc_ref[...]) + sh_ref[...]).astype(BF16)

        h = h_r[...]
        ff = 0.0
        for c in range(nff):
            a = jnp.dot(h, wfi_ref[c], preferred_element_type=F32)
            g = jnp.dot(h, wfi_ref[c + nff], preferred_element_type=F32)
            u = (a * jax.nn.sigmoid(a) * g).astype(BF16)
            ff = ff + jnp.dot(u, wfo_ref[c], preferred_element_type=F32)
        out = x1_r[...] + gf_ref[...] * ff
        if final_norm:
            out = _rms(out) * fg_ref[...]
        o_ref[...] = out

    @pl.when(t % 2 == 0)
    def _():
        step(x1_a, h_a, x1_b, h_b)

    @pl.when(t % 2 == 1)
    def _():
        step(x1_b, h_b, x1_a, h_a)


def _mlp(x, ys, g_out, g_a, w_out, g_ffn, sh, sc, g_f, wfi, wfo, final_g, final_norm):
    b, l, d = x.shape
    tm = min(512, l)
    nt = l // tm
    n = b * nt
    cur = lambda t: jnp.minimum(t, n - 1)
    prv = lambda t: jnp.maximum(t - 1, 0)
    tok = lambda wd: pl.BlockSpec((None, tm, wd), lambda t: (cur(t) // nt, cur(t) % nt, 0))
    vec = pl.BlockSpec((None, 1, d), lambda t: (cur(t) // nt, 0, 0))
    return pl.pallas_call(
        functools.partial(_mlp_kernel, final_norm=final_norm),
        grid=(n + 1,),
        in_specs=[tok(d)] + [tok(GROUP_W)] * 4 + [
            _const_spec(g_out), vec, _const_spec(w_out), _const_spec(g_ffn), vec, vec,
            pl.BlockSpec((None, 1, d), lambda t: (prv(t) // nt, 0, 0)),
            _const_spec(wfi), _const_spec(wfo), _const_spec(final_g)],
        out_specs=pl.BlockSpec((None, tm, d), lambda t: (prv(t) // nt, prv(t) % nt, 0)),
        out_shape=jax.ShapeDtypeStruct((b, l, d), F32),
        scratch_shapes=[pltpu.VMEM((tm, d), F32), pltpu.VMEM((tm, d), BF16),
                        pltpu.VMEM((tm, d), F32), pltpu.VMEM((tm, d), BF16)],
        compiler_params=_params("arbitrary"),
        name="mlp",
    )(x, *ys, g_out, g_a, w_out, g_ffn, sh, sc, g_f, wfi, wfo, final_g)


def _pad_proj(w):
    z = jnp.zeros((w.shape[0], LANES - MLA_ROPE), w.dtype)
    return jnp.concatenate([w[:, :1184], z, w[:, 1184:]], axis=1).astype(BF16)


def _mla_weights(wuq, wukv):
    r = wuq.shape[0]
    q = wuq.reshape(r, 4, MLA_NOPE + MLA_ROPE)
    q = jnp.pad(q, ((0, 0), (0, 0), (0, LANES - MLA_NOPE - MLA_ROPE))).reshape(r, 4 * LANES)
    kv = wukv.reshape(wukv.shape[0], 4, MLA_NOPE + MLA_V)
    k = jnp.pad(kv[..., :MLA_NOPE], ((0, 0), (0, 0), (0, LANES - MLA_NOPE))).reshape(-1, 4 * LANES)
    v = jnp.concatenate([kv[..., MLA_NOPE:], kv[..., MLA_NOPE:]], axis=-1).reshape(-1, 4 * LANES)
    return q.astype(BF16), k.astype(BF16), v.astype(BF16)


def _ffn_weights(w_in, w_out):
    d = w_in.shape[0]
    nff = D_FF // FF_TILE
    wfi = w_in.astype(BF16).reshape(d, 2 * nff, FF_TILE).transpose(1, 0, 2)
    return wfi, w_out.astype(BF16).reshape(nff, FF_TILE, d)


def _seg_matrix(width):
    i = jnp.arange(width) // HEAD_D
    return jnp.where(i[:, None] == i[None, :], 1.0 / HEAD_D, 0.0).astype(BF16)


def kernel(x, c, ctx, c_ctx, w_mod, b_mod, norm_attn_g, norm_ffn_g, w_in, hy_conv_w, hy_conv_b, hy_w1, hy_b1, hy_w2, hy_b2, hy_w3, hy_b3, hy_freq, hy_w4, hy_bias, mla_q_norm_g, mla_w_uq, mla_kv_norm_g, mla_w_ukv, ret_decay, ret_norm_g, gqa_q_norm_g, gqa_k_norm_g, out_norm_g, w_out, w_ffn_in, w_ffn_out, final_norm_g):
    b, n_lat, d = x.shape
    n_ctx = ctx.shape[1]
    depth = w_mod.shape[0]

    rope_mla = _rope_tables(n_lat, MLA_ROPE, LANES, MLA_NOPE, 1)
    rope_hd2 = _rope_tables(n_lat, HEAD_D, HEAD_D, 0, 2)
    seg2, seg4 = _seg_matrix(LANES), _seg_matrix(GROUP_W)
    hy_consts = {n: (_hy_features(n),) + _fft_tables(n) for n in {n_lat, n_ctx}}
    zero_state = jnp.zeros((b, GROUP_W, GROUP_W), F32)

    rows = -(-(b + 1) // 8) * 8
    cond = jnp.zeros((rows, d), F32).at[:b].set(c).at[b].set(c_ctx)
    mod = _ada_mod(cond, w_mod, b_mod)

    xl, xc = x, ctx.reshape(1, b * n_ctx, d)
    for l in range(depth):
        update_ctx = l < depth - 1
        m_lat = [m[:b, None, :] for m in jnp.split(mod[l], 6, axis=-1)]
        m_ctx = [m[b][None, None, :] for m in jnp.split(mod[l], 6, axis=-1)]
        wp = _pad_proj(w_in[l])
        g_attn = norm_attn_g[l][None]
        mla_w = (mla_q_norm_g[l][None], mla_kv_norm_g[l][None]) + _mla_weights(mla_w_uq[l], mla_w_ukv[l])
        gqa_w = (jnp.tile(gqa_q_norm_g[l], 2)[None], jnp.tile(gqa_k_norm_g[l], 2)[None], seg2)
        hy_l, ret_l, q_l, k_l, v_l, gq_l, gk_l, gv_l = _front(
            xl, g_attn, m_lat[0], m_lat[1], wp, mla_w, gqa_w, (rope_mla, rope_hd2))
        hy_c, ret_c, q_c, k_c, v_c, gq_c, gk_c, gv_c = _front(
            xc, g_attn, m_ctx[0], m_ctx[1], wp, mla_w, gqa_w, None)
        hy_c = hy_c.reshape(b, n_ctx, -1)
        ret_c = ret_c.reshape(b, n_ctx, -1)

        hy_args = (hy_conv_w[l], hy_conv_b[l], hy_w1[l], hy_b1[l], hy_w2[l], hy_b2[l], hy_w3[l], hy_b3[l],
                   hy_freq[l], hy_w4[l], hy_bias[l])
        ya_l = _hyena(hy_l, *hy_consts[n_lat], *hy_args)
        yb_l = _attention(q_l, [(k_c, v_c, n_ctx), (k_l, v_l, n_lat)], 1, b, n_lat)

        dl = jnp.repeat(ret_decay[l], HEAD_D, axis=1)
        dc = jnp.repeat(ret_decay[l].T, RET_CHUNK, axis=0)
        ng = ret_norm_g[l][None]
        yc_c, s_f, s_b = _retention(ret_c, dl, dc, ng, seg4, zero_state, zero_state)
        yc_l, _, _ = _retention(ret_l, dl, dc, ng, seg4, s_f, s_b)
        yd_l = _attention(gq_l, [(gk_c, gv_c, n_ctx), (gk_l, gv_l, n_lat)], 2, b, n_lat)

        wo = w_out[l].astype(BF16)
        wfi, wfo = _ffn_weights(w_ffn_in[l], w_ffn_out[l])
        g_out, g_ffn, g_fin = out_norm_g[l][None], norm_ffn_g[l][None], final_norm_g[None]
        xl = _mlp(xl, (ya_l, yb_l, yc_l, yd_l), g_out, m_lat[2], wo, g_ffn, m_lat[3], m_lat[4], m_lat[5],
                  wfi, wfo, g_fin, l == depth - 1)

        if update_ctx:
            flat = lambda a: a.reshape(1, b * n_ctx, GROUP_W)
            ya_c = _hyena(hy_c, *hy_consts[n_ctx], *hy_args)
            yb_c = _attention(q_c, [(k_c, v_c, n_ctx)], 1, b, n_ctx)
            yd_c = _attention(gq_c, [(gk_c, gv_c, n_ctx)], 2, b, n_ctx)
            xc = _mlp(xc, tuple(flat(a) for a in (ya_c, yb_c, yc_c, yd_c)), g_out, m_ctx[2], wo,
                      g_ffn, m_ctx[3], m_ctx[4], m_ctx[5], wfi, wfo, g_fin, False)

    return xl
```

```python
import functools
import math

import numpy as np
import jax
import jax.numpy as jnp
from jax import lax
from jax.experimental import pallas as pl
from jax.experimental.pallas import tpu as pltpu

F32 = jnp.float32
BF16 = jnp.bfloat16
HI = lax.Precision.HIGHEST

D_MODEL = 1024
GRID_W = 64
EPS = 1e-6
ROPE_THETA = 10000.0
GROUP_W = 256
LANES = 128

HY_EMB = 33
HY_BANDS = 16
HY_FFN = 64
HY_NFILT = 1024
HY_DECAY_SHIFT = 0.05
HY_FAST_PCT = 0.3
HY_SLOW_PCT = 1.5
HY_TARGET = 1e-2
FFT_RADIX = 16
FFT_BINS = FFT_RADIX // 2 + 1
FFT_PIECE_ROWS = 32

MLA_NOPE = 64
MLA_ROPE = 32
MLA_V = 64
MLA_KV_RANK = 128
LOG2E = math.log2(math.e)
MLA_SCALE = (MLA_NOPE + MLA_ROPE) ** -0.5 * LOG2E

HEAD_D = 64
RET_CHUNK = 128
RET_UNROLL = 8
GQA_SCALE = HEAD_D ** -0.5 * LOG2E
RET_K_SCALE = HEAD_D ** -0.5

D_FF = 2816
FF_TILE = 256

PROJ_SPLITS = (768, 256, 256, 1024, 512)
PROJ_OFFSETS = (0, 768, 1024, 1280, 2304, 2816)

V7X_VMEM_BYTES = 64 * 1024 * 1024
VMEM_LIMIT = V7X_VMEM_BYTES - 8 * 1024 * 1024

FRONT_TILE = 1024
ATTN_TILE = 1024
MLP_TILE = 512


def _params(*sem):
    return pltpu.CompilerParams(dimension_semantics=sem, vmem_limit_bytes=VMEM_LIMIT)


def _const_spec(a):
    nd = a.ndim
    return pl.BlockSpec(a.shape, lambda *_: (0,) * nd, pipeline_mode=pl.Buffered(1))


def _rms(x):
    return x * lax.rsqrt(jnp.mean(x * x, axis=-1, keepdims=True) + EPS)


def _rope(x, cf, s1, s2, half):
    w = x.shape[-1]
    return x * cf + pltpu.roll(x, w - half, 1) * s1 + pltpu.roll(x, half, 1) * s2


def _seg_mean(sq, seg):
    hi = sq.astype(BF16)
    lo = (sq - hi.astype(F32)).astype(BF16)
    return (jnp.dot(hi, seg, preferred_element_type=F32) + jnp.dot(lo, seg, preferred_element_type=F32))


def _log_sigmoid(x):
    return -(jnp.maximum(-x, 0.0) + jnp.log(1.0 + jnp.exp(-jnp.abs(x))))


def _ada_kernel(c_ref, w_ref, b_ref, o_ref):
    c = c_ref[...]
    s = c * jax.nn.sigmoid(c)
    o_ref[...] = jnp.dot(s, w_ref[...], precision=HI, preferred_element_type=F32) + b_ref[...]


def _ada_mod(cond, w_mod, b_mod):
    depth, d, n = w_mod.shape
    r = cond.shape[0]
    tn = 1024
    return pl.pallas_call(
        _ada_kernel,
        grid=(depth, n // tn),
        in_specs=[pl.BlockSpec((r, d), lambda l, j: (0, 0)),
                  pl.BlockSpec((None, d, tn), lambda l, j: (l, 0, j)),
                  pl.BlockSpec((None, 1, tn), lambda l, j: (l, 0, j))],
        out_specs=pl.BlockSpec((None, r, tn), lambda l, j: (l, 0, j)),
        out_shape=jax.ShapeDtypeStruct((depth, r, n), F32),
        compiler_params=_params("arbitrary", "arbitrary"),
        name="ada_mod",
    )(cond, w_mod, b_mod.reshape(depth, 1, n))


def _value_with_ones(v2, head):
    lo = lax.broadcasted_iota(jnp.int32, v2.shape, 1) < HEAD_D
    keep = lo if head % 2 == 0 else jnp.logical_not(lo)
    return jnp.where(keep, v2, 1.0).astype(BF16)


def _mla_heads(cq, ckvr, gq, gkv, wuq_ref, wk_ref, wv_ref, tables, q_ref, k_ref, v_ref):
    half = MLA_ROPE // 2
    qn = (_rms(cq) * gq).astype(BF16)
    q = jnp.dot(qn, wuq_ref[...], preferred_element_type=F32)
    kvn = (_rms(ckvr[:, :MLA_KV_RANK]) * gkv).astype(BF16)
    k = jnp.dot(kvn, wk_ref[...], preferred_element_type=F32)
    v = jnp.dot(kvn, wv_ref[...], preferred_element_type=F32)
    kpe = pltpu.roll(ckvr[:, MLA_KV_RANK:], MLA_NOPE, 1)
    if tables is not None:
        kpe = _rope(kpe, *tables, half)
    for h in range(q_ref.shape[0]):
        cols = slice(h * LANES, (h + 1) * LANES)
        qh = q[:, cols]
        if tables is not None:
            qh = _rope(qh, *tables, half)
        q_ref[h] = (qh * MLA_SCALE).astype(BF16)
        k_ref[h] = (k[:, cols] + kpe).astype(BF16)
        v_ref[h] = _value_with_ones(v[:, cols], h)


def _gqa_heads(x, gq, gk, seg, tables, q_ref, k_ref, v_ref):
    half = HEAD_D // 2
    lo = lax.broadcasted_iota(jnp.int32, (x.shape[0], LANES), 1) < HEAD_D

    def normed(a, g):
        a = a * lax.rsqrt(_seg_mean(a * a, seg) + EPS) * g
        return _rope(a, *tables, half) if tables is not None else a

    def dup(a, g):
        a = jnp.where(lo if g == 0 else jnp.logical_not(lo), a, 0.0)
        return a + pltpu.roll(a, HEAD_D, 1)

    for p in range(2):
        qb = normed(x[:, p * LANES:(p + 1) * LANES], gq) * GQA_SCALE
        q_ref[2 * p] = jnp.where(lo, qb, 0.0).astype(BF16)
        q_ref[2 * p + 1] = jnp.where(lo, 0.0, qb).astype(BF16)
    kb = normed(x[:, 2 * LANES:3 * LANES], gk)
    vb = x[:, 3 * LANES:4 * LANES]
    for g in range(2):
        k_ref[g] = dup(kb, g).astype(BF16)
        for r in range(2):
            v_ref[2 * g + r] = _value_with_ones(dup(vb, g), 2 * g + r)


def _front_kernel(x_ref, g_ref, sh_ref, sc_ref, w_ref, gq_ref, gkv_ref, wuq_ref, wk_ref, wv_ref,
                  ggq_ref, ggk_ref, seg_ref, *rest, rope):
    if rope:
        mla_t = tuple(r[...] for r in rest[0:3])
        hd_t = tuple(r[...] for r in rest[3:6])
        rest = rest[6:]
    else:
        mla_t = hd_t = None
    hy_ref, ret_ref, mq_ref, mk_ref, mv_ref, gq_out, gk_out, gv_out = rest
    y = _rms(x_ref[...]) * g_ref[...]
    h = (y * (1.0 + sc_ref[...]) + sh_ref[...]).astype(BF16)
    o0, o1, o2, o3, o4, o5 = PROJ_OFFSETS

    def proj(a, b):
        return jnp.dot(h, w_ref[:, a:b], preferred_element_type=F32)

    _mla_heads(proj(o1, o2), proj(o2, o3), gq_ref[...], gkv_ref[...], wuq_ref, wk_ref, wv_ref, mla_t,
               mq_ref, mk_ref, mv_ref)
    _gqa_heads(proj(o4, o5), ggq_ref[...], ggk_ref[...], seg_ref[...], hd_t, gq_out, gk_out, gv_out)
    for i in range(2 * GROUP_W // LANES):
        cols = slice(i * LANES, (i + 1) * LANES)
        a = proj(o3 + i * LANES, o3 + (i + 1) * LANES)
        if i * LANES >= GROUP_W:
            a = a * RET_K_SCALE
        ret_ref[:, cols] = _rope(a, *hd_t, HEAD_D // 2) if rope else a
    ret_ref[:, 2 * GROUP_W:] = proj(o3 + 2 * GROUP_W, o4)
    hy_ref[...] = proj(o0, o1)


def _front(x, g, sh, sc, w, mla_w, gqa_w, tables):
    b, l, d = x.shape
    tm = min(FRONT_TILE, l)
    nt = l // tm
    rope = tables is not None
    tok = lambda wd: pl.BlockSpec((None, tm, wd), lambda i, j: (i, j, 0))
    vec = pl.BlockSpec((None, 1, d), lambda i, j: (i, 0, 0))
    head = lambda nh: pl.BlockSpec((nh, tm, LANES), lambda i, j: (0, i * nt + j, 0))
    hshape = lambda nh: jax.ShapeDtypeStruct((nh, b * l, LANES), BF16)
    consts = [w, *mla_w, *gqa_w]
    args = [x, g, sh, sc, *consts]
    specs = [tok(d), _const_spec(g), vec, vec] + [_const_spec(a) for a in consts]
    if rope:
        args += [*tables[0], *tables[1]]
        specs += [pl.BlockSpec((tm, LANES), lambda i, j: (j, 0))] * 6
    return pl.pallas_call(
        functools.partial(_front_kernel, rope=rope),
        grid=(b, nt),
        in_specs=specs,
        out_specs=[tok(PROJ_SPLITS[0]), tok(PROJ_SPLITS[3]), head(4), head(4), head(4), head(4), head(2), head(4)],
        out_shape=[jax.ShapeDtypeStruct((b, l, PROJ_SPLITS[0]), F32),
                   jax.ShapeDtypeStruct((b, l, PROJ_SPLITS[3]), F32),
                   hshape(4), hshape(4), hshape(4), hshape(4), hshape(2), hshape(4)],
        compiler_params=_params("parallel", "parallel"),
        name="front",
    )(*args)


def _hy_filter_kernel(z_ref, t_ref, w1_ref, b1_ref, w2_ref, b2_ref, w3_ref, b3_ref, fr_ref,
                      w4_ref, dl_ref, o_ref):
    fr = fr_ref[...]
    dot = functools.partial(jnp.dot, precision=HI, preferred_element_type=F32)
    h = jnp.sin(fr * (dot(z_ref[...], w1_ref[...]) + b1_ref[...]))
    h = jnp.sin(fr * (dot(h, w2_ref[...]) + b2_ref[...]))
    h = jnp.sin(fr * (dot(h, w3_ref[...]) + b3_ref[...]))
    filt = dot(h, w4_ref[...]) * (jnp.exp(-t_ref[...] * dl_ref[...]) + HY_DECAY_SHIFT)
    fwd = filt[:, :GROUP_W]
    bwd = filt[:, GROUP_W:]
    row = lax.broadcasted_iota(jnp.int32, bwd.shape, 0)
    bwd = jnp.where(row == 0, 0.0, bwd)
    nrm = (jnp.sum(jnp.abs(fwd), axis=0, keepdims=True) + jnp.sum(jnp.abs(bwd), axis=0, keepdims=True))
    o_ref[:, :GROUP_W] = fwd / nrm
    o_ref[:, GROUP_W:] = bwd / nrm


def _hy_filters(l, z, t, w1, b1, w2, b2, w3, b3, fr, w4, dl):
    full = lambda a: pl.BlockSpec(a.shape, lambda o: (0, 0))
    return pl.pallas_call(
        _hy_filter_kernel,
        grid=(2,),
        in_specs=[full(z), full(t), full(w1), full(b1), full(w2), full(b2), full(w3), full(b3), full(fr),
                  pl.BlockSpec((HY_FFN, 2 * GROUP_W), lambda o: (0, o)),
                  pl.BlockSpec((1, 2 * GROUP_W), lambda o: (0, o))],
        out_specs=pl.BlockSpec((l, 2 * GROUP_W), lambda o: (0, o)),
        out_shape=jax.ShapeDtypeStruct((l, HY_NFILT), F32),
        compiler_params=_params("arbitrary"),
        name="hy_filters",
    )(z, t, w1, b1, w2, b2, w3, b3, fr, w4, dl)


def _split_bf16(x):
    hi = x.astype(BF16)
    return hi, (x - hi.astype(F32)).astype(BF16)


def _dot_3pass(a_hi, a_lo, x):
    x_hi, x_lo = _split_bf16(x)
    dot = functools.partial(jnp.dot, preferred_element_type=F32)
    return dot(a_hi, x_hi) + (dot(a_hi, x_lo) + dot(a_lo, x_hi))


def _hy_spec_kernel(ch_ref, cl_ref, sh_ref, sl_ref, h_ref, re_ref, im_ref):
    for o in range(2):
        hf = h_ref[:, 2 * o * GROUP_W:(2 * o + 1) * GROUP_W]
        hb = h_ref[:, (2 * o + 1) * GROUP_W:(2 * o + 2) * GROUP_W]
        cols = slice(o * GROUP_W, (o + 1) * GROUP_W)
        re_ref[:, cols] = _dot_3pass(ch_ref[...], cl_ref[...], hf + hb)
        im_ref[:, cols] = -_dot_3pass(sh_ref[...], sl_ref[...], hf - hb)


def _hy_spectrum(h, tabs):
    rows, l = tabs[0].shape
    fb = rows // FFT_BINS if rows // FFT_BINS >= 256 else rows
    out = jax.ShapeDtypeStruct((rows, 2 * GROUP_W), F32)
    blk = pl.BlockSpec((fb, 2 * GROUP_W), lambda f: (f, 0))
    return pl.pallas_call(
        _hy_spec_kernel,
        grid=(rows // fb,),
        in_specs=[pl.BlockSpec((fb, l), lambda f: (f, 0))] * 4 + [_const_spec(h)],
        out_specs=[blk, blk],
        out_shape=[out, out],
        compiler_params=_params("arbitrary"),
        name="hy_spectrum",
    )(*tabs, h)


def _short_conv(u, w, b):
    n = u.shape[0]
    row = lax.broadcasted_iota(jnp.int32, u.shape, 0)
    up = jnp.where(row == 0, 0.0, pltpu.roll(u, 1, 0))
    dn = jnp.where(row == n - 1, 0.0, pltpu.roll(u, n - 1, 0))
    return w[0:1] * up + w[1:2] * u + w[2:3] * dn + b


_R2 = math.sqrt(0.5)


def _dft8_real4(x0, x1, x2, x3):
    s02, d02 = x0 + x2, x0 - x2
    s13, d13 = x1 + x3, x1 - x3
    p, q = _R2 * d13, _R2 * s13
    re = [s02 + s13, x0 + p, d02, x0 - p, s02 - s13]
    im = [None, -(q + x2), -d13, x2 - q, None]
    return re, im


def _fwd16(a):
    er, ei = _dft8_real4(a[0], a[2], a[4], a[6])
    orr, oi = _dft8_real4(a[1], a[3], a[5], a[7])

    def ext(re, im, k):
        if k <= 4:
            return re[k], im[k]
        return re[8 - k], (None if im[8 - k] is None else -im[8 - k])

    out_r, out_i = [], []
    for k in range(FFT_BINS):
        e_r, e_i = ext(er, ei, k)
        o_r, o_i = ext(orr, oi, k)
        c, s = math.cos(math.pi * k / 8), math.sin(math.pi * k / 8)
        if k == 0:
            out_r.append(e_r + o_r)
            out_i.append(None)
        elif k == 8:
            out_r.append(e_r - o_r)
            out_i.append(None)
        elif k == 4:
            out_r.append(e_r)
            out_i.append(-o_r)
        else:
            out_r.append(e_r + (c * o_r + s * o_i))
            out_i.append(e_i + (c * o_i - s * o_r))
    return out_r, out_i


def _dft8_real4_t(kr, ki):
    a = kr[0] + kr[4]
    b = kr[0] - kr[4]
    x0 = a + kr[2] + kr[1] + kr[3]
    x2 = a - kr[2] - ki[1] + ki[3]
    t1 = _R2 * (kr[1] - kr[3])
    t2 = _R2 * (ki[1] + ki[3])
    return x0, b - ki[2] + t1 - t2, x2, b + ki[2] - t1 - t2


def _inv16(br, bi):
    o_r, o_i = [None] * FFT_BINS, [None] * FFT_BINS
    for k in range(FFT_BINS):
        c, s = math.cos(math.pi * k / 8), math.sin(math.pi * k / 8)
        if k == 0:
            o_r[k] = br[k]
        elif k == 8:
            o_r[k] = -br[k]
        elif k == 4:
            o_r[k] = -bi[k]
        else:
            o_r[k] = c * br[k] - s * bi[k]
            o_i[k] = c * bi[k] + s * br[k]

    def fold(re, im):
        fr = [re[0] + re[8], re[1] + re[7], re[2] + re[6], re[3] + re[5], re[4]]
        fi = [None, im[1] - im[7], im[2] - im[6], im[3] - im[5], None]
        return fr, fi

    xe = _dft8_real4_t(*fold(br, bi))
    xo = _dft8_real4_t(*fold(o_r, o_i))
    return [xe[0], xo[0], xe[1], xo[1], xe[2], xo[2], xe[3], xo[3]]


def _hy_conv_kernel(sig_ref, gate_ref, cw_ref, cb_ref, kr_ref, ki_ref, bias_ref, m_ref, mt_ref,
                    twc_ref, tws_ref, twci_ref, twsi_ref, o_ref, u_scr, g_scr, z_scr, v_scr,
                    *, sig_col, gate_col, rows):
    w = GROUP_W
    l = u_scr.shape[0]
    n2 = l // (FFT_RADIX // 2)
    u = sig_ref[...]
    if sig_col is not None:
        u = _short_conv(u, cw_ref[:, sig_col * w:(sig_col + 1) * w], cb_ref[:, sig_col * w:(sig_col + 1) * w])
    u_scr[...] = u
    g_scr[...] = _short_conv(gate_ref[...], cw_ref[:, gate_col * w:(gate_col + 1) * w],
                             cb_ref[:, gate_col * w:(gate_col + 1) * w])

    def piece(base, r0):
        return pl.ds(pl.multiple_of(base + r0, rows), rows)

    def fwd_piece(i, carry):
        r0 = i * rows
        for hh in range(w // LANES):
            lanes = slice(hh * LANES, (hh + 1) * LANES)
            fr, fi = _fwd16([u_scr[piece(n1 * n2, r0), lanes] for n1 in range(FFT_RADIX // 2)])
            for k in range(FFT_BINS):
                cols = slice(k * w + hh * LANES, k * w + (hh + 1) * LANES)
                if k == 0:
                    zr, zi = fr[0], jnp.zeros_like(fr[0])
                else:
                    c, s = twc_ref[k, piece(0, r0), :], tws_ref[k, piece(0, r0), :]
                    if fi[k] is None:
                        zr, zi = fr[k] * c, -(fr[k] * s)
                    else:
                        zr, zi = fr[k] * c + fi[k] * s, fi[k] * c - fr[k] * s
                z_scr[piece(0, r0), cols] = zr.astype(BF16)
                z_scr[piece(n2, r0), cols] = zi.astype(BF16)
        return carry

    lax.fori_loop(0, n2 // rows, fwd_piece, 0)

    m, mt = m_ref[...], mt_ref[...]
    for k in range(FFT_BINS):
        cols = slice(k * w, (k + 1) * w)
        x = jnp.dot(m, z_scr[:, cols], preferred_element_type=F32)
        xr, xi = x[:n2], x[n2:]
        kr, ki = kr_ref[k * n2:(k + 1) * n2, :], ki_ref[k * n2:(k + 1) * n2, :]
        y = jnp.concatenate([xr * kr - xi * ki, xr * ki + xi * kr], axis=0).astype(BF16)
        v_scr[:, cols] = jnp.dot(mt, y, preferred_element_type=F32)

    def inv_piece(i, carry):
        r0 = i * rows
        for hh in range(w // LANES):
            lanes = slice(hh * LANES, (hh + 1) * LANES)
            br, bi = [], []
            for k in range(FFT_BINS):
                cols = slice(k * w + hh * LANES, k * w + (hh + 1) * LANES)
                vr, vi = v_scr[piece(0, r0), cols], v_scr[piece(n2, r0), cols]
                if k == 0:
                    br.append(vr * (0.5 / l))
                    bi.append(None)
                else:
                    c, s = twci_ref[k, piece(0, r0), :], twsi_ref[k, piece(0, r0), :]
                    br.append(vr * c - vi * s)
                    bi.append(None if k == FFT_BINS - 1 else vr * s + vi * c)
            y = _inv16(br, bi)
            for n1 in range(FFT_RADIX // 2):
                rws = piece(n1 * n2, r0)
                o_ref[rws, lanes] = g_scr[rws, lanes] * (y[n1] + u_scr[rws, lanes] * bias_ref[:, lanes])
        return carry

    lax.fori_loop(0, n2 // rows, inv_piece, 0)


def _hy_conv(sig, sig_col, hy, gate_col, cw, cb, spec, order, bias, tabs):
    b, l, _ = hy.shape
    w = GROUP_W
    n2 = 2 * l // FFT_RADIX
    rows = min(FFT_PIECE_ROWS, n2)
    kr, ki = spec
    sig_arr = hy if sig_col is not None else sig
    sig_blk = sig_col if sig_col is not None else 0
    seq = lambda col: pl.BlockSpec((None, l, w), lambda i: (i, 0, col))
    spec_blk = pl.BlockSpec((FFT_BINS * n2, w), lambda i: (0, order), pipeline_mode=pl.Buffered(1))
    kernel = functools.partial(_hy_conv_kernel, sig_col=sig_col, gate_col=gate_col, rows=rows)
    return pl.pallas_call(
        kernel,
        grid=(b,),
        in_specs=[seq(sig_blk), seq(gate_col), _const_spec(cw), _const_spec(cb), spec_blk, spec_blk,
                  _const_spec(bias)] + [_const_spec(t) for t in tabs],
        out_specs=seq(0),
        out_shape=jax.ShapeDtypeStruct((b, l, w), F32),
        scratch_shapes=[pltpu.VMEM((l, w), F32), pltpu.VMEM((l, w), F32),
                        pltpu.VMEM((2 * n2, FFT_BINS * w), BF16), pltpu.VMEM((2 * n2, FFT_BINS * w), F32)],
        compiler_params=_params("parallel"),
        name="hy_conv",
    )(sig_arr, hy, cw, cb, kr, ki, bias, *tabs)


@functools.lru_cache(maxsize=None)
def _fft_tables(l):
    n = 2 * l
    n2 = n // FFT_RADIX
    j = np.arange(n2, dtype=np.int64)
    k1 = np.arange(FFT_BINS, dtype=np.int64)
    f32 = lambda a: np.ascontiguousarray(a, dtype=np.float32)
    b16 = lambda a: f32(a).astype(BF16)
    ang2 = ((j[:, None] * j[None, :]) % n2).astype(np.float64) * (2.0 * math.pi / n2)
    c2, s2 = np.cos(ang2), np.sin(ang2)
    m = np.block([[c2, s2], [-s2, c2]])
    angt = (k1[:, None] * j[None, :]).astype(np.float64) * (2.0 * math.pi / n)
    wk = np.where((k1 == 0) | (k1 == FFT_BINS - 1), 1.0, 2.0)[:, None] / n
    lane = lambda a: f32(np.broadcast_to(a[:, :, None], a.shape + (LANES,)))
    bins = (k1[:, None] + FFT_RADIX * j[None, :]).reshape(-1)
    angf = ((bins[:, None] * np.arange(l, dtype=np.int64)[None, :]) % n).astype(np.float64) * (2.0 * math.pi / n)
    conv_tabs = (b16(m), b16(m.T), lane(np.cos(angt)), lane(np.sin(angt)),
                 lane(np.cos(angt) * wk), lane(np.sin(angt) * wk))
    def split(a):
        hi = b16(a)
        return hi, b16(a - hi.astype(np.float64))

    return split(np.cos(angf)) + split(np.sin(angf)), conv_tabs


@functools.lru_cache(maxsize=None)
def _hy_features(l):
    f = np.float32
    t = np.linspace(0.0, 1.0, l, dtype=f)[:, None]
    wpos = (f(2.0 * math.pi) * np.arange(l, dtype=f)[:, None] / f(l)).astype(f)
    fb = np.linspace(1e-4, HY_BANDS - 1, HY_BANDS, dtype=f)[None, :]
    arg = (fb * wpos).astype(f).astype(np.float64)
    z = np.concatenate([t, np.cos(arg), -np.sin(arg), np.zeros((l, HY_FFN - HY_EMB))], axis=-1).astype(f)
    deltas = np.linspace(math.log(HY_TARGET) / HY_FAST_PCT, math.log(HY_TARGET) / HY_SLOW_PCT, HY_NFILT, dtype=f)
    return z, t, np.abs(deltas)[None, :]


def _hyena(hy, feats, spec_tabs, conv_tabs, cw, cb, w1, b1, w2, b2, w3, b3, fr, w4, bias):
    l = hy.shape[1]
    z, t, dl = feats
    w1p = jnp.pad(w1, ((0, HY_FFN - HY_EMB), (0, 0)))
    h = _hy_filters(l, z, t, w1p, b1[None], w2, b2[None], w3, b3[None], fr[None], w4, dl)
    spec = _hy_spectrum(h, spec_tabs)
    cb2 = cb[None]
    zz = _hy_conv(None, 0, hy, 1, cw, cb2, spec, 0, bias[0:1], conv_tabs)
    return _hy_conv(zz, None, hy, 2, cw, cb2, spec, 1, bias[1:2], conv_tabs)


@functools.lru_cache(maxsize=None)
def _rope_tables(l, dim, width, off, reps):
    f = np.float32
    rows = np.repeat(np.arange(l // GRID_W), GRID_W).astype(f)
    cols = np.tile(np.arange(GRID_W), l // GRID_W).astype(f)
    quarter = dim // 4
    half = dim // 2
    inv = (f(ROPE_THETA) ** (-np.arange(quarter, dtype=f) / f(quarter))).astype(f)
    ang = np.concatenate([rows[:, None] * inv, cols[:, None] * inv], axis=-1).astype(f).astype(np.float64)
    c, s = np.cos(ang), np.sin(ang)
    cf = np.ones((l, width))
    s1 = np.zeros((l, width))
    s2 = np.zeros((l, width))
    cf[:, off:off + dim] = np.concatenate([c, c], axis=-1)
    s1[:, off:off + half] = -s
    s2[:, off + half:off + dim] = s
    return tuple(np.tile(a, (1, reps)).astype(f) for a in (cf, s1, s2))


def _attn_kernel(q_ref, *rest, rep):
    o_ref = rest[-1]
    kv = rest[:-1]
    ks, vs = kv[0::2], kv[1::2]
    tq = q_ref.shape[1]
    lo = lax.broadcasted_iota(jnp.int32, (tq, LANES), 1) < HEAD_D
    nt = (((1,), (1,)), ((), ()))
    n_heads = q_ref.shape[0]

    def scores(h):
        return [lax.dot_general(q_ref[h], k_ref[h // rep], nt, preferred_element_type=F32) for k_ref in ks]

    def weighted(h, ss):
        m = ss[0].max(axis=-1, keepdims=True)
        for s in ss[1:]:
            m = jnp.maximum(m, s.max(axis=-1, keepdims=True))
        acc = 0.0
        for s, v_ref in zip(ss, vs):
            acc = acc + jnp.dot(jnp.exp2(s - m).astype(BF16), v_ref[h], preferred_element_type=F32)
        return acc / pltpu.roll(acc, HEAD_D, 1)

    outs = []
    ss_next = scores(0)
    for h in range(n_heads):
        ss = ss_next
        if h + 1 < n_heads:
            ss_next = scores(h + 1)
        outs.append(weighted(h, ss))
    for p in range(n_heads // 2):
        o_ref[:, p * LANES:(p + 1) * LANES] = jnp.where(lo, outs[2 * p], outs[2 * p + 1])


def _attention(q, kvs, rep, b, l):
    nh = q.shape[0]
    tq = min(ATTN_TILE, l)
    nq = l // tq
    specs = [pl.BlockSpec((nh, tq, LANES), lambda i, j: (0, i * nq + j, 0))]
    args = [q]
    for k, v, lk in kvs:
        specs += [pl.BlockSpec((k.shape[0], lk, LANES), lambda i, j: (0, i, 0)),
                  pl.BlockSpec((v.shape[0], lk, LANES), lambda i, j: (0, i, 0))]
        args += [k, v]
    return pl.pallas_call(
        functools.partial(_attn_kernel, rep=rep),
        grid=(b, nq),
        in_specs=specs,
        out_specs=pl.BlockSpec((None, tq, GROUP_W), lambda i, j: (i, j, 0)),
        out_shape=jax.ShapeDtypeStruct((b, l, GROUP_W), F32),
        compiler_params=_params("parallel", "parallel"),
        name="attention",
    )(*args)


def _ret_kernel(qs, ks, v_ref, g_ref, dl_ref, dc_ref, ng_ref, seg_ref, sf0_ref, sb0_ref,
                y_ref, sf_ref, sb_ref, sball, *, chunk, unroll):
    c_ = chunk
    w = GROUP_W
    n_chunks = qs.shape[0] // c_
    unroll = min(unroll, n_chunks)
    shift_c = int(math.log2(c_))
    shift_h = int(math.log2(HEAD_D))
    tn = (((0,), (0,)), ((), ()))
    nt = (((1,), (1,)), ((), ()))

    lgf = _log_sigmoid(dl_ref[0:1, :])
    lgb = _log_sigmoid(dl_ref[1:2, :])
    ii = lax.broadcasted_iota(jnp.int32, (c_, 1), 0).astype(F32)
    qdf = jnp.exp((ii + 1.0) * lgf)
    kdf = jnp.exp((c_ - 1.0 - ii) * lgf)
    cdf = jnp.exp(float(c_) * lgf)
    qdb = jnp.exp((c_ - ii) * lgb)
    kdb = jnp.exp(ii * lgb)
    cdb = jnp.exp(float(c_) * lgb)
    r_h = lax.shift_right_logical(lax.broadcasted_iota(jnp.int32, (w, w), 0), shift_h)
    c_h = lax.shift_right_logical(lax.broadcasted_iota(jnp.int32, (w, w), 1), shift_h)
    blockdiag = r_h == c_h
    lgc = _log_sigmoid(dc_ref[...])
    i4 = jnp.bitwise_and(lax.broadcasted_iota(jnp.int32, (4 * c_, c_), 0), c_ - 1)
    j4 = lax.broadcasted_iota(jnp.int32, (4 * c_, c_), 1)
    diff = (i4 - j4).astype(F32)
    dmat = (jnp.where(diff >= 0, jnp.exp(lgc[:, 0:1] * jnp.maximum(diff, 0.0)), 0.0)
            + jnp.where(diff <= 0, jnp.exp(lgc[:, 1:2] * jnp.maximum(-diff, 0.0)), 0.0))
    row_h = lax.shift_right_logical(lax.broadcasted_iota(jnp.int32, (4 * c_, w), 0), shift_c)
    lane_h = lax.shift_right_logical(lax.broadcasted_iota(jnp.int32, (4 * c_, w), 1), shift_h)
    headmask = row_h == lane_h
    seg = seg_ref[...]

    def rows(c):
        return pl.ds(pl.multiple_of(c * c_, c_), c_)

    def kv_outer(k, v, dec):
        a = lax.dot_general((k * dec).astype(BF16), v.astype(BF16), tn, preferred_element_type=F32)
        return jnp.where(blockdiag, a, 0.0)

    def bwd(t, s):
        c = n_chunks - 1 - t
        sball[c] = s
        sl = rows(c)
        return s * cdb + kv_outer(ks[sl, :], v_ref[sl, :], kdb)

    sb_ref[...] = lax.fori_loop(0, n_chunks, bwd, sb0_ref[...], unroll=unroll)

    def fwd(c, s):
        sl = rows(c)
        q, k, v = qs[sl, :], ks[sl, :], v_ref[sl, :]
        vb = v.astype(BF16)
        q4 = jnp.where(headmask, jnp.concatenate([q, q, q, q], axis=0), 0.0).astype(BF16)
        sc = lax.dot_general(q4, k.astype(BF16), nt, preferred_element_type=F32)
        o4 = jnp.dot((sc * dmat).astype(BF16), vb, preferred_element_type=F32)
        o4 = jnp.where(headmask, o4, 0.0)
        o = o4[0:c_] + o4[c_:2 * c_] + o4[2 * c_:3 * c_] + o4[3 * c_:4 * c_]
        o = o + jnp.dot((q * qdf).astype(BF16), s.astype(BF16), preferred_element_type=F32)
        o = o + jnp.dot((q * qdb).astype(BF16), sball[c].astype(BF16), preferred_element_type=F32)
        on = o * lax.rsqrt(_seg_mean(o * o, seg) + EPS) * ng_ref[...]
        g = g_ref[sl, :]
        y_ref[sl, :] = on * (g * jax.nn.sigmoid(g))
        return s * cdf + kv_outer(k, v, kdf)

    sf_ref[...] = lax.fori_loop(0, n_chunks, fwd, sf0_ref[...], unroll=unroll)


def _retention(ret, dl, dc, ng, seg, sf0, sb0):
    b, l, _ = ret.shape
    w = GROUP_W
    col = lambda j: pl.BlockSpec((None, l, w), lambda i: (i, 0, j))
    state = pl.BlockSpec((None, w, w), lambda i: (i, 0, 0))
    sshape = jax.ShapeDtypeStruct((b, w, w), F32)
    return pl.pallas_call(
        functools.partial(_ret_kernel, chunk=RET_CHUNK, unroll=RET_UNROLL),
        grid=(b,),
        in_specs=[col(0), col(1), col(2), col(3), _const_spec(dl), _const_spec(dc), _const_spec(ng),
                  _const_spec(seg), state, state],
        out_specs=[pl.BlockSpec((None, l, w), lambda i: (i, 0, 0)), state, state],
        out_shape=[jax.ShapeDtypeStruct((b, l, w), F32), sshape, sshape],
        scratch_shapes=[pltpu.VMEM((l // RET_CHUNK, w, w), F32)],
        compiler_params=_params("parallel"),
        name="retention",
    )(ret, ret, ret, ret, dl, dc, ng, seg, sf0, sb0)


def _mlp_kernel(x_ref, ya_ref, yb_ref, yc_ref, yd_ref, gout_ref, ga_ref, wout_ref,
                gffn_ref, sh_ref, sc_ref, gf_ref, wfi_ref, wfo_ref, fg_ref, o_ref,
                x1_a, h_a, x1_b, h_b, *, final_norm):
    t = pl.program_id(0)
    nff = wfo_ref.shape[0]

    @pl.when(t == 0)
    def _():
        x1_b[...] = jnp.zeros_like(x1_b)
        h_b[...] = jnp.zeros_like(h_b)

    def step(x1_w, h_w, x1_r, h_r):
        acc = 0.0
        for i, y_ref in enumerate((ya_ref, yb_ref, yc_ref, yd_ref)):
            rows = slice(i * GROUP_W, (i + 1) * GROUP_W)
            yn = (_rms(y_ref[...]) * gout_ref[:, rows]).astype(BF16)
            acc = acc + jnp.dot(yn, wout_ref[rows, :], preferred_element_type=F32)
        x1 = x_ref[...] + ga_ref[...] * acc
        x1_w[...] = x1
        h_w[...] = (_rms(x1) * gffn_ref[...] * (1.0 + sc_ref[...]) + sh_ref[...]).astype(BF16)

        h = h_r[...]
        ff = 0.0
        for c in range(nff):
            a = jnp.dot(h, wfi_ref[c], preferred_element_type=F32)
            g = jnp.dot(h, wfi_ref[c + nff], preferred_element_type=F32)
            u = (a * jax.nn.sigmoid(a) * g).astype(BF16)
            ff = ff + jnp.dot(u, wfo_ref[c], preferred_element_type=F32)
        out = x1_r[...] + gf_ref[...] * ff
        if final_norm:
            out = _rms(out) * fg_ref[...]
        o_ref[...] = out

    @pl.when(t % 2 == 0)
    def _():
        step(x1_a, h_a, x1_b, h_b)

    @pl.when(t % 2 == 1)
    def _():
        step(x1_b, h_b, x1_a, h_a)


def _mlp(x, ys, g_out, g_a, w_out, g_ffn, sh, sc, g_f, wfi, wfo, final_g, final_norm):
    b, l, d = x.shape
    tm = min(MLP_TILE, l)
    nt = l // tm
    n = b * nt
    cur = lambda t: jnp.minimum(t, n - 1)
    prv = lambda t: jnp.maximum(t - 1, 0)
    tok = lambda wd: pl.BlockSpec((None, tm, wd), lambda t: (cur(t) // nt, cur(t) % nt, 0))
    vec = pl.BlockSpec((None, 1, d), lambda t: (cur(t) // nt, 0, 0))
    return pl.pallas_call(
        functools.partial(_mlp_kernel, final_norm=final_norm),
        grid=(n + 1,),
        in_specs=[tok(d)] + [tok(GROUP_W)] * 4 + [
            _const_spec(g_out), vec, _const_spec(w_out), _const_spec(g_ffn), vec, vec,
            pl.BlockSpec((None, 1, d), lambda t: (prv(t) // nt, 0, 0)),
            _const_spec(wfi), _const_spec(wfo), _const_spec(final_g)],
        out_specs=pl.BlockSpec((None, tm, d), lambda t: (prv(t) // nt, prv(t) % nt, 0)),
        out_shape=jax.ShapeDtypeStruct((b, l, d), F32),
        scratch_shapes=[pltpu.VMEM((tm, d), F32), pltpu.VMEM((tm, d), BF16),
                        pltpu.VMEM((tm, d), F32), pltpu.VMEM((tm, d), BF16)],
        compiler_params=_params("arbitrary"),
        name="mlp",
    )(x, *ys, g_out, g_a, w_out, g_ffn, sh, sc, g_f, wfi, wfo, final_g)


def _pad_proj(w):
    end = PROJ_OFFSETS[2] + MLA_KV_RANK + MLA_ROPE
    z = jnp.zeros((w.shape[0], LANES - MLA_ROPE), w.dtype)
    return jnp.concatenate([w[:, :end], z, w[:, end:]], axis=1).astype(BF16)


def _mla_weights(wuq, wukv):
    r = wuq.shape[0]
    q = wuq.reshape(r, 4, MLA_NOPE + MLA_ROPE)
    q = jnp.pad(q, ((0, 0), (0, 0), (0, LANES - MLA_NOPE - MLA_ROPE))).reshape(r, 4 * LANES)
    kv = wukv.reshape(wukv.shape[0], 4, MLA_NOPE + MLA_V)
    k = jnp.pad(kv[..., :MLA_NOPE], ((0, 0), (0, 0), (0, LANES - MLA_NOPE))).reshape(-1, 4 * LANES)
    v = jnp.concatenate([kv[..., MLA_NOPE:], kv[..., MLA_NOPE:]], axis=-1).reshape(-1, 4 * LANES)
    return q.astype(BF16), k.astype(BF16), v.astype(BF16)


def _ffn_weights(w_in, w_out):
    d = w_in.shape[0]
    nff = D_FF // FF_TILE
    wfi = w_in.astype(BF16).reshape(d, 2 * nff, FF_TILE).transpose(1, 0, 2)
    return wfi, w_out.astype(BF16).reshape(nff, FF_TILE, d)


def _seg_matrix(width):
    i = jnp.arange(width) // HEAD_D
    return jnp.where(i[:, None] == i[None, :], 1.0 / HEAD_D, 0.0).astype(BF16)


def kernel(x, c, ctx, c_ctx, w_mod, b_mod, norm_attn_g, norm_ffn_g, w_in, hy_conv_w, hy_conv_b, hy_w1, hy_b1, hy_w2, hy_b2, hy_w3, hy_b3, hy_freq, hy_w4, hy_bias, mla_q_norm_g, mla_w_uq, mla_kv_norm_g, mla_w_ukv, ret_decay, ret_norm_g, gqa_q_norm_g, gqa_k_norm_g, out_norm_g, w_out, w_ffn_in, w_ffn_out, final_norm_g):
    b, n_lat, d = x.shape
    n_ctx = ctx.shape[1]
    depth = w_mod.shape[0]

    rope_mla = _rope_tables(n_lat, MLA_ROPE, LANES, MLA_NOPE, 1)
    rope_hd2 = _rope_tables(n_lat, HEAD_D, HEAD_D, 0, 2)
    seg2, seg4 = _seg_matrix(LANES), _seg_matrix(GROUP_W)
    hy_consts = {n: (_hy_features(n),) + _fft_tables(n) for n in {n_lat, n_ctx}}
    zero_state = jnp.zeros((b, GROUP_W, GROUP_W), F32)

    rows = -(-(b + 1) // 8) * 8
    cond = jnp.zeros((rows, d), F32).at[:b].set(c).at[b].set(c_ctx)
    mod = _ada_mod(cond, w_mod, b_mod)

    xl, xc = x, ctx.reshape(1, b * n_ctx, d)
    for l in range(depth):
        update_ctx = l < depth - 1
        m_lat = [m[:b, None, :] for m in jnp.split(mod[l], 6, axis=-1)]
        m_ctx = [m[b][None, None, :] for m in jnp.split(mod[l], 6, axis=-1)]
        wp = _pad_proj(w_in[l])
        g_attn = norm_attn_g[l][None]
        mla_w = (mla_q_norm_g[l][None], mla_kv_norm_g[l][None]) + _mla_weights(mla_w_uq[l], mla_w_ukv[l])
        gqa_w = (jnp.tile(gqa_q_norm_g[l], 2)[None], jnp.tile(gqa_k_norm_g[l], 2)[None], seg2)
        hy_l, ret_l, q_l, k_l, v_l, gq_l, gk_l, gv_l = _front(
            xl, g_attn, m_lat[0], m_lat[1], wp, mla_w, gqa_w, (rope_mla, rope_hd2))
        hy_c, ret_c, q_c, k_c, v_c, gq_c, gk_c, gv_c = _front(
            xc, g_attn, m_ctx[0], m_ctx[1], wp, mla_w, gqa_w, None)
        hy_c = hy_c.reshape(b, n_ctx, -1)
        ret_c = ret_c.reshape(b, n_ctx, -1)

        hy_args = (hy_conv_w[l], hy_conv_b[l], hy_w1[l], hy_b1[l], hy_w2[l], hy_b2[l], hy_w3[l], hy_b3[l],
                   hy_freq[l], hy_w4[l], hy_bias[l])
        ya_l = _hyena(hy_l, *hy_consts[n_lat], *hy_args)
        yb_l = _attention(q_l, [(k_c, v_c, n_ctx), (k_l, v_l, n_lat)], 1, b, n_lat)

        dl = jnp.repeat(ret_decay[l], HEAD_D, axis=1)
        dc = jnp.repeat(ret_decay[l].T, RET_CHUNK, axis=0)
        ng = ret_norm_g[l][None]
        yc_c, s_f, s_b = _retention(ret_c, dl, dc, ng, seg4, zero_state, zero_state)
        yc_l, _, _ = _retention(ret_l, dl, dc, ng, seg4, s_f, s_b)
        yd_l = _attention(gq_l, [(gk_c, gv_c, n_ctx), (gk_l, gv_l, n_lat)], 2, b, n_lat)

        wo = w_out[l].astype(BF16)
        wfi, wfo = _ffn_weights(w_ffn_in[l], w_ffn_out[l])
        g_out, g_ffn, g_fin = out_norm_g[l][None], norm_ffn_g[l][None], final_norm_g[None]
        xl = _mlp(xl, (ya_l, yb_l, yc_l, yd_l), g_out, m_lat[2], wo, g_ffn, m_lat[3], m_lat[4], m_lat[5],
                  wfi, wfo, g_fin, l == depth - 1)

        if update_ctx:
            flat = lambda a: a.reshape(1, b * n_ctx, GROUP_W)
            ya_c = _hyena(hy_c, *hy_consts[n_ctx], *hy_args)
            yb_c = _attention(q_c, [(k_c, v_c, n_ctx)], 1, b, n_ctx)
            yd_c = _attention(gq_c, [(gk_c, gv_c, n_ctx)], 2, b, n_ctx)
            xc = _mlp(xc, tuple(flat(a) for a in (ya_c, yb_c, yc_c, yd_c)), g_out, m_ctx[2], wo,
                      g_ffn, m_ctx[3], m_ctx[4], m_ctx[5], wfi, wfo, g_fin, False)

    return xl
```

```python
import functools
import math

import numpy as np
import jax
import jax.numpy as jnp
from jax import lax
from jax.experimental import pallas as pl
from jax.experimental.pallas import tpu as pltpu

F32 = jnp.float32
BF16 = jnp.bfloat16
HI = lax.Precision.HIGHEST

D_MODEL = 1024
GRID_W = 64
EPS = 1e-6
ROPE_THETA = 10000.0
GROUP_W = 256
LANES = 128

HY_EMB = 33
HY_BANDS = 16
HY_FFN = 64
HY_NFILT = 1024
HY_DECAY_SHIFT = 0.05
HY_FAST_PCT = 0.3
HY_SLOW_PCT = 1.5
HY_TARGET = 1e-2
FFT_RADIX = 16
FFT_BINS = FFT_RADIX // 2 + 1
FFT_PIECE_ROWS = 32

MLA_NOPE = 64
MLA_ROPE = 32
MLA_V = 64
MLA_KV_RANK = 128
LOG2E = math.log2(math.e)
MLA_SCALE = (MLA_NOPE + MLA_ROPE) ** -0.5 * LOG2E

HEAD_D = 64
RET_CHUNK = 128
RET_UNROLL = 8
GQA_SCALE = HEAD_D ** -0.5 * LOG2E
RET_K_SCALE = HEAD_D ** -0.5

D_FF = 2816
FF_TILE = 256

PROJ_SPLITS = (768, 256, 256, 1024, 512)
PROJ_OFFSETS = (0, 768, 1024, 1280, 2304, 2816)

V7X_VMEM_BYTES = 64 * 1024 * 1024
VMEM_LIMIT = V7X_VMEM_BYTES - 8 * 1024 * 1024

FRONT_TILE = 1024
ATTN_TILE = 1024
MLP_TILE = 512


def _params(*sem):
    return pltpu.CompilerParams(dimension_semantics=sem, vmem_limit_bytes=VMEM_LIMIT)


def _const_spec(a):
    nd = a.ndim
    return pl.BlockSpec(a.shape, lambda *_: (0,) * nd, pipeline_mode=pl.Buffered(1))


def _rms(x):
    return x * lax.rsqrt(jnp.mean(x * x, axis=-1, keepdims=True) + EPS)


def _rope(x, cf, s1, s2, half):
    w = x.shape[-1]
    return x * cf + pltpu.roll(x, w - half, 1) * s1 + pltpu.roll(x, half, 1) * s2


def _seg_mean(sq, seg):
    hi = sq.astype(BF16)
    lo = (sq - hi.astype(F32)).astype(BF16)
    return (jnp.dot(hi, seg, preferred_element_type=F32) + jnp.dot(lo, seg, preferred_element_type=F32))


def _log_sigmoid(x):
    return -(jnp.maximum(-x, 0.0) + jnp.log(1.0 + jnp.exp(-jnp.abs(x))))


def _ada_kernel(c_ref, w_ref, b_ref, o_ref):
    c = c_ref[...]
    s = c * jax.nn.sigmoid(c)
    o_ref[...] = jnp.dot(s, w_ref[...], precision=HI, preferred_element_type=F32) + b_ref[...]


def _ada_mod(cond, w_mod, b_mod):
    depth, d, n = w_mod.shape
    r = cond.shape[0]
    tn = 1024
    return pl.pallas_call(
        _ada_kernel,
        grid=(depth, n // tn),
        in_specs=[pl.BlockSpec((r, d), lambda l, j: (0, 0)),
                  pl.BlockSpec((None, d, tn), lambda l, j: (l, 0, j)),
                  pl.BlockSpec((None, 1, tn), lambda l, j: (l, 0, j))],
        out_specs=pl.BlockSpec((None, r, tn), lambda l, j: (l, 0, j)),
        out_shape=jax.ShapeDtypeStruct((depth, r, n), F32),
        compiler_params=_params("arbitrary", "arbitrary"),
        name="ada_mod",
    )(cond, w_mod, b_mod.reshape(depth, 1, n))


def _value_with_ones(v2, head):
    lo = lax.broadcasted_iota(jnp.int32, v2.shape, 1) < HEAD_D
    keep = lo if head % 2 == 0 else jnp.logical_not(lo)
    return jnp.where(keep, v2, 1.0).astype(BF16)


def _mla_heads(cq, ckvr, gq, gkv, wuq_ref, wk_ref, wv_ref, tables, q_ref, k_ref, v_ref):
    half = MLA_ROPE // 2
    qn = (_rms(cq) * gq).astype(BF16)
    q = jnp.dot(qn, wuq_ref[...], preferred_element_type=F32)
    kvn = (_rms(ckvr[:, :MLA_KV_RANK]) * gkv).astype(BF16)
    k = jnp.dot(kvn, wk_ref[...], preferred_element_type=F32)
    v = jnp.dot(kvn, wv_ref[...], preferred_element_type=F32)
    kpe = pltpu.roll(ckvr[:, MLA_KV_RANK:], MLA_NOPE, 1)
    if tables is not None:
        kpe = _rope(kpe, *tables, half)
    for h in range(q_ref.shape[0]):
        cols = slice(h * LANES, (h + 1) * LANES)
        qh = q[:, cols]
        if tables is not None:
            qh = _rope(qh, *tables, half)
        q_ref[h] = (qh * MLA_SCALE).astype(BF16)
        k_ref[h] = (k[:, cols] + kpe).astype(BF16)
        v_ref[h] = _value_with_ones(v[:, cols], h)


def _gqa_heads(x, gq, gk, seg, tables, q_ref, k_ref, v_ref):
    half = HEAD_D // 2
    lo = lax.broadcasted_iota(jnp.int32, (x.shape[0], LANES), 1) < HEAD_D

    def normed(a, g):
        a = a * lax.rsqrt(_seg_mean(a * a, seg) + EPS) * g
        return _rope(a, *tables, half) if tables is not None else a

    def dup(a, g):
        a = jnp.where(lo if g == 0 else jnp.logical_not(lo), a, 0.0)
        return a + pltpu.roll(a, HEAD_D, 1)

    for p in range(2):
        qb = normed(x[:, p * LANES:(p + 1) * LANES], gq) * GQA_SCALE
        q_ref[2 * p] = jnp.where(lo, qb, 0.0).astype(BF16)
        q_ref[2 * p + 1] = jnp.where(lo, 0.0, qb).astype(BF16)
    kb = normed(x[:, 2 * LANES:3 * LANES], gk)
    vb = x[:, 3 * LANES:4 * LANES]
    for g in range(2):
        k_ref[g] = dup(kb, g).astype(BF16)
        for r in range(2):
            v_ref[2 * g + r] = _value_with_ones(dup(vb, g), 2 * g + r)


def _front_kernel(x_ref, g_ref, sh_ref, sc_ref, w_ref, gq_ref, gkv_ref, wuq_ref, wk_ref, wv_ref,
                  ggq_ref, ggk_ref, seg_ref, *rest, rope):
    if rope:
        mla_t = tuple(r[...] for r in rest[0:3])
        hd_t = tuple(r[...] for r in rest[3:6])
        rest = rest[6:]
    else:
        mla_t = hd_t = None
    hy_ref, ret_ref, mq_ref, mk_ref, mv_ref, gq_out, gk_out, gv_out = rest
    y = _rms(x_ref[...]) * g_ref[...]
    h = (y * (1.0 + sc_ref[...]) + sh_ref[...]).astype(BF16)
    o0, o1, o2, o3, o4, o5 = PROJ_OFFSETS

    def proj(a, b):
        return jnp.dot(h, w_ref[:, a:b], preferred_element_type=F32)

    _mla_heads(proj(o1, o2), proj(o2, o3), gq_ref[...], gkv_ref[...], wuq_ref, wk_ref, wv_ref, mla_t,
               mq_ref, mk_ref, mv_ref)
    _gqa_heads(proj(o4, o5), ggq_ref[...], ggk_ref[...], seg_ref[...], hd_t, gq_out, gk_out, gv_out)
    for i in range(2 * GROUP_W // LANES):
        cols = slice(i * LANES, (i + 1) * LANES)
        a = proj(o3 + i * LANES, o3 + (i + 1) * LANES)
        if i * LANES >= GROUP_W:
            a = a * RET_K_SCALE
        ret_ref[:, cols] = _rope(a, *hd_t, HEAD_D // 2) if rope else a
    ret_ref[:, 2 * GROUP_W:] = proj(o3 + 2 * GROUP_W, o4)
    hy_ref[...] = proj(o0, o1)


def _front(x, g, sh, sc, w, mla_w, gqa_w, tables):
    b, l, d = x.shape
    tm = min(FRONT_TILE, l)
    nt = l // tm
    rope = tables is not None
    tok = lambda wd: pl.BlockSpec((None, tm, wd), lambda i, j: (i, j, 0))
    vec = pl.BlockSpec((None, 1, d), lambda i, j: (i, 0, 0))
    head = lambda nh: pl.BlockSpec((nh, tm, LANES), lambda i, j: (0, i * nt + j, 0))
    hshape = lambda nh: jax.ShapeDtypeStruct((nh, b * l, LANES), BF16)
    consts = [w, *mla_w, *gqa_w]
    args = [x, g, sh, sc, *consts]
    specs = [tok(d), _const_spec(g), vec, vec] + [_const_spec(a) for a in consts]
    if rope:
        args += [*tables[0], *tables[1]]
        specs += [pl.BlockSpec((tm, LANES), lambda i, j: (j, 0))] * 6
    return pl.pallas_call(
        functools.partial(_front_kernel, rope=rope),
        grid=(b, nt),
        in_specs=specs,
        out_specs=[tok(PROJ_SPLITS[0]), tok(PROJ_SPLITS[3]), head(4), head(4), head(4), head(4), head(2), head(4)],
        out_shape=[jax.ShapeDtypeStruct((b, l, PROJ_SPLITS[0]), F32),
                   jax.ShapeDtypeStruct((b, l, PROJ_SPLITS[3]), F32),
                   hshape(4), hshape(4), hshape(4), hshape(4), hshape(2), hshape(4)],
        compiler_params=_params("parallel", "parallel"),
        name="front",
    )(*args)


def _hy_filter_kernel(z_ref, t_ref, w1_ref, b1_ref, w2_ref, b2_ref, w3_ref, b3_ref, fr_ref,
                      w4_ref, dl_ref, o_ref):
    fr = fr_ref[...]
    dot = functools.partial(jnp.dot, precision=HI, preferred_element_type=F32)
    h = jnp.sin(fr * (dot(z_ref[...], w1_ref[...]) + b1_ref[...]))
    h = jnp.sin(fr * (dot(h, w2_ref[...]) + b2_ref[...]))
    h = jnp.sin(fr * (dot(h, w3_ref[...]) + b3_ref[...]))
    filt = dot(h, w4_ref[...]) * (jnp.exp(-t_ref[...] * dl_ref[...]) + HY_DECAY_SHIFT)
    fwd = filt[:, :GROUP_W]
    bwd = filt[:, GROUP_W:]
    row = lax.broadcasted_iota(jnp.int32, bwd.shape, 0)
    bwd = jnp.where(row == 0, 0.0, bwd)
    nrm = (jnp.sum(jnp.abs(fwd), axis=0, keepdims=True) + jnp.sum(jnp.abs(bwd), axis=0, keepdims=True))
    o_ref[:, :GROUP_W] = fwd / nrm
    o_ref[:, GROUP_W:] = bwd / nrm


def _hy_filters(l, z, t, w1, b1, w2, b2, w3, b3, fr, w4, dl):
    full = lambda a: pl.BlockSpec(a.shape, lambda o: (0, 0))
    return pl.pallas_call(
        _hy_filter_kernel,
        grid=(2,),
        in_specs=[full(z), full(t), full(w1), full(b1), full(w2), full(b2), full(w3), full(b3), full(fr),
                  pl.BlockSpec((HY_FFN, 2 * GROUP_W), lambda o: (0, o)),
                  pl.BlockSpec((1, 2 * GROUP_W), lambda o: (0, o))],
        out_specs=pl.BlockSpec((l, 2 * GROUP_W), lambda o: (0, o)),
        out_shape=jax.ShapeDtypeStruct((l, HY_NFILT), F32),
        compiler_params=_params("arbitrary"),
        name="hy_filters",
    )(z, t, w1, b1, w2, b2, w3, b3, fr, w4, dl)


def _split_bf16(x):
    hi = x.astype(BF16)
    return hi, (x - hi.astype(F32)).astype(BF16)


def _dot_3pass(a_hi, a_lo, x):
    x_hi, x_lo = _split_bf16(x)
    dot = functools.partial(jnp.dot, preferred_element_type=F32)
    return dot(a_hi, x_hi) + (dot(a_hi, x_lo) + dot(a_lo, x_hi))


def _hy_spec_kernel(ch_ref, cl_ref, sh_ref, sl_ref, h_ref, re_ref, im_ref):
    for o in range(2):
        hf = h_ref[:, 2 * o * GROUP_W:(2 * o + 1) * GROUP_W]
        hb = h_ref[:, (2 * o + 1) * GROUP_W:(2 * o + 2) * GROUP_W]
        cols = slice(o * GROUP_W, (o + 1) * GROUP_W)
        re_ref[:, cols] = _dot_3pass(ch_ref[...], cl_ref[...], hf + hb)
        im_ref[:, cols] = -_dot_3pass(sh_ref[...], sl_ref[...], hf - hb)


def _hy_spectrum(h, tabs):
    rows, l = tabs[0].shape
    fb = rows // FFT_BINS if rows // FFT_BINS >= 256 else rows
    out = jax.ShapeDtypeStruct((rows, 2 * GROUP_W), F32)
    blk = pl.BlockSpec((fb, 2 * GROUP_W), lambda f: (f, 0))
    return pl.pallas_call(
        _hy_spec_kernel,
        grid=(rows // fb,),
        in_specs=[pl.BlockSpec((fb, l), lambda f: (f, 0))] * 4 + [_const_spec(h)],
        out_specs=[blk, blk],
        out_shape=[out, out],
        compiler_params=_params("arbitrary"),
        name="hy_spectrum",
    )(*tabs, h)


def _short_conv_to(dst_ref, u, w, b):
    n = u.shape[0]
    mid = w[1:2] * u + b
    dst_ref[...] = w[0:1] * pltpu.roll(u, 1, 0) + mid + w[2:3] * pltpu.roll(u, n - 1, 0)
    dst_ref[0:1, :] = mid[0:1] + w[2:3] * u[1:2]
    dst_ref[n - 1:n, :] = mid[n - 1:n] + w[0:1] * u[n - 2:n - 1]


_R2 = math.sqrt(0.5)


def _dft8_real4(x0, x1, x2, x3):
    s02, d02 = x0 + x2, x0 - x2
    s13, d13 = x1 + x3, x1 - x3
    p, q = _R2 * d13, _R2 * s13
    re = [s02 + s13, x0 + p, d02, x0 - p, s02 - s13]
    im = [None, -(q + x2), -d13, x2 - q, None]
    return re, im


def _fwd16(a):
    er, ei = _dft8_real4(a[0], a[2], a[4], a[6])
    orr, oi = _dft8_real4(a[1], a[3], a[5], a[7])

    def ext(re, im, k):
        if k <= 4:
            return re[k], im[k]
        return re[8 - k], (None if im[8 - k] is None else -im[8 - k])

    out_r, out_i = [], []
    for k in range(FFT_BINS):
        e_r, e_i = ext(er, ei, k)
        o_r, o_i = ext(orr, oi, k)
        c, s = math.cos(math.pi * k / 8), math.sin(math.pi * k / 8)
        if k == 0:
            out_r.append(e_r + o_r)
            out_i.append(None)
        elif k == 8:
            out_r.append(e_r - o_r)
            out_i.append(None)
        elif k == 4:
            out_r.append(e_r)
            out_i.append(-o_r)
        else:
            out_r.append(e_r + (c * o_r + s * o_i))
            out_i.append(e_i + (c * o_i - s * o_r))
    return out_r, out_i


def _dft8_real4_t(kr, ki):
    a = kr[0] + kr[4]
    b = kr[0] - kr[4]
    x0 = a + kr[2] + kr[1] + kr[3]
    x2 = a - kr[2] - ki[1] + ki[3]
    t1 = _R2 * (kr[1] - kr[3])
    t2 = _R2 * (ki[1] + ki[3])
    return x0, b - ki[2] + t1 - t2, x2, b + ki[2] - t1 - t2


def _inv16(br, bi):
    o_r, o_i = [None] * FFT_BINS, [None] * FFT_BINS
    for k in range(FFT_BINS):
        c, s = math.cos(math.pi * k / 8), math.sin(math.pi * k / 8)
        if k == 0:
            o_r[k] = br[k]
        elif k == 8:
            o_r[k] = -br[k]
        elif k == 4:
            o_r[k] = -bi[k]
        else:
            o_r[k] = c * br[k] - s * bi[k]
            o_i[k] = c * bi[k] + s * br[k]

    def fold(re, im):
        fr = [re[0] + re[8], re[1] + re[7], re[2] + re[6], re[3] + re[5], re[4]]
        fi = [None, im[1] - im[7], im[2] - im[6], im[3] - im[5], None]
        return fr, fi

    xe = _dft8_real4_t(*fold(br, bi))
    xo = _dft8_real4_t(*fold(o_r, o_i))
    return [xe[0], xo[0], xe[1], xo[1], xe[2], xo[2], xe[3], xo[3]]


def _hy_conv_kernel(sig_ref, gate_ref, cw_ref, cb_ref, kr_ref, ki_ref, bias_ref, m_ref, mt_ref,
                    twc_ref, tws_ref, twci_ref, twsi_ref, o_ref, u_a, g_a, v_a, u_b, g_b, v_b, z_scr,
                    *, sig_col, gate_col, rows):
    t = pl.program_id(0)

    @pl.when(t == 0)
    def _():
        u_b[...] = jnp.zeros_like(u_b)
        g_b[...] = jnp.zeros_like(g_b)
        v_b[...] = jnp.zeros_like(v_b)

    refs = (sig_ref, gate_ref, cw_ref, cb_ref, kr_ref, ki_ref, bias_ref, m_ref, mt_ref,
            twc_ref, tws_ref, twci_ref, twsi_ref, o_ref, z_scr)

    @pl.when(t % 2 == 0)
    def _():
        _hy_conv_step(*refs, (u_a, g_a, v_a), (u_b, g_b, v_b), sig_col, gate_col, rows)

    @pl.when(t % 2 == 1)
    def _():
        _hy_conv_step(*refs, (u_b, g_b, v_b), (u_a, g_a, v_a), sig_col, gate_col, rows)


def _hy_conv_step(sig_ref, gate_ref, cw_ref, cb_ref, kr_ref, ki_ref, bias_ref, m_ref, mt_ref,
                  twc_ref, tws_ref, twci_ref, twsi_ref, o_ref, z_scr, cur, prev, sig_col, gate_col, rows):
    u_scr, g_scr, v_cur = cur
    u_prev, g_prev, v_prev = prev
    w = GROUP_W
    l = u_scr.shape[0]
    n2 = l // (FFT_RADIX // 2)
    n_pieces = n2 // rows
    if sig_col is not None:
        _short_conv_to(u_scr, sig_ref[...], cw_ref[:, sig_col * w:(sig_col + 1) * w],
                       cb_ref[:, sig_col * w:(sig_col + 1) * w])
    else:
        u_scr[...] = sig_ref[...]
    _short_conv_to(g_scr, gate_ref[...], cw_ref[:, gate_col * w:(gate_col + 1) * w],
                   cb_ref[:, gate_col * w:(gate_col + 1) * w])

    def piece(base, r0):
        return pl.ds(pl.multiple_of(base + r0, rows), rows)

    def fwd_piece(i, carry):
        r0 = i * rows
        for hh in range(w // LANES):
            lanes = slice(hh * LANES, (hh + 1) * LANES)
            fr, fi = _fwd16([u_scr[piece(n1 * n2, r0), lanes] for n1 in range(FFT_RADIX // 2)])
            for k in range(FFT_BINS):
                if k == 0:
                    zr, zi = fr[0], jnp.zeros_like(fr[0])
                else:
                    c, s = twc_ref[k, piece(0, r0), :], tws_ref[k, piece(0, r0), :]
                    if fi[k] is None:
                        zr, zi = fr[k] * c, -(fr[k] * s)
                    else:
                        zr, zi = fr[k] * c + fi[k] * s, fi[k] * c - fr[k] * s
                z_scr[k, piece(0, r0), lanes] = zr.astype(BF16)
                z_scr[k, piece(n2, r0), lanes] = zi.astype(BF16)
        return carry

    lax.fori_loop(0, n_pieces, fwd_piece, 0)

    def slab(k):
        x = jnp.dot(m_ref[...], z_scr[k], preferred_element_type=F32)
        xr, xi = x[:n2], x[n2:]
        bins = pl.ds(pl.multiple_of(k * n2, n2), n2)
        kr, ki = kr_ref[bins, :], ki_ref[bins, :]
        y = jnp.concatenate([xr * kr - xi * ki, xr * ki + xi * kr], axis=0).astype(BF16)
        v_cur[k] = jnp.dot(mt_ref[...], y, preferred_element_type=F32)

    def inv_piece(i, carry):
        slab(i)
        r0 = i * rows
        for hh in range(w // LANES):
            lanes = slice(hh * LANES, (hh + 1) * LANES)
            br, bi = [], []
            for k in range(FFT_BINS):
                vr, vi = v_prev[k, piece(0, r0), lanes], v_prev[k, piece(n2, r0), lanes]
                if k == 0:
                    br.append(vr * (0.5 / l))
                    bi.append(None)
                else:
                    c, s = twci_ref[k, piece(0, r0), :], twsi_ref[k, piece(0, r0), :]
                    br.append(vr * c - vi * s)
                    bi.append(None if k == FFT_BINS - 1 else vr * s + vi * c)
            y = _inv16(br, bi)
            for n1 in range(FFT_RADIX // 2):
                rws = piece(n1 * n2, r0)
                o_ref[rws, lanes] = g_prev[rws, lanes] * (y[n1] + u_prev[rws, lanes] * bias_ref[:, lanes])
        return carry

    for i in range(n_pieces):
        inv_piece(i, 0)
    for k in range(n_pieces, FFT_BINS):
        slab(k)


def _hy_conv(sig, sig_col, hy, gate_col, cw, cb, spec, order, bias, tabs):
    b, l, _ = hy.shape
    w = GROUP_W
    n2 = 2 * l // FFT_RADIX
    rows = max(FFT_PIECE_ROWS, n2 // (FFT_BINS - 1))
    rows = min(rows, n2)
    kr, ki = spec
    sig_arr = hy if sig_col is not None else sig
    sig_blk = sig_col if sig_col is not None else 0
    seq = lambda col: pl.BlockSpec((None, l, w), lambda t: (jnp.minimum(t, b - 1), 0, col))
    spec_blk = pl.BlockSpec((FFT_BINS * n2, w), lambda t: (0, order), pipeline_mode=pl.Buffered(1))
    kernel = functools.partial(_hy_conv_kernel, sig_col=sig_col, gate_col=gate_col, rows=rows)
    stage = [pltpu.VMEM((l, w), F32), pltpu.VMEM((l, w), F32), pltpu.VMEM((FFT_BINS, 2 * n2, w), F32)]
    return pl.pallas_call(
        kernel,
        grid=(b + 1,),
        in_specs=[seq(sig_blk), seq(gate_col), _const_spec(cw), _const_spec(cb), spec_blk, spec_blk,
                  _const_spec(bias)] + [_const_spec(t) for t in tabs],
        out_specs=pl.BlockSpec((None, l, w), lambda t: (jnp.maximum(t - 1, 0), 0, 0)),
        out_shape=jax.ShapeDtypeStruct((b, l, w), F32),
        scratch_shapes=stage + stage + [pltpu.VMEM((FFT_BINS, 2 * n2, w), BF16)],
        compiler_params=_params("arbitrary"),
        name="hy_conv",
    )(sig_arr, hy, cw, cb, kr, ki, bias, *tabs)


@functools.lru_cache(maxsize=None)
def _fft_tables(l):
    n = 2 * l
    n2 = n // FFT_RADIX
    j = np.arange(n2, dtype=np.int64)
    k1 = np.arange(FFT_BINS, dtype=np.int64)
    f32 = lambda a: np.ascontiguousarray(a, dtype=np.float32)
    b16 = lambda a: f32(a).astype(BF16)
    ang2 = ((j[:, None] * j[None, :]) % n2).astype(np.float64) * (2.0 * math.pi / n2)
    c2, s2 = np.cos(ang2), np.sin(ang2)
    m = np.block([[c2, s2], [-s2, c2]])
    angt = (k1[:, None] * j[None, :]).astype(np.float64) * (2.0 * math.pi / n)
    wk = np.where((k1 == 0) | (k1 == FFT_BINS - 1), 1.0, 2.0)[:, None] / n
    lane = lambda a: f32(np.broadcast_to(a[:, :, None], a.shape + (LANES,)))
    bins = (k1[:, None] + FFT_RADIX * j[None, :]).reshape(-1)
    angf = ((bins[:, None] * np.arange(l, dtype=np.int64)[None, :]) % n).astype(np.float64) * (2.0 * math.pi / n)
    conv_tabs = (b16(m), b16(m.T), lane(np.cos(angt)), lane(np.sin(angt)),
                 lane(np.cos(angt) * wk), lane(np.sin(angt) * wk))
    def split(a):
        hi = b16(a)
        return hi, b16(a - hi.astype(np.float64))

    return split(np.cos(angf)) + split(np.sin(angf)), conv_tabs


@functools.lru_cache(maxsize=None)
def _hy_features(l):
    f = np.float32
    t = np.linspace(0.0, 1.0, l, dtype=f)[:, None]
    wpos = (f(2.0 * math.pi) * np.arange(l, dtype=f)[:, None] / f(l)).astype(f)
    fb = np.linspace(1e-4, HY_BANDS - 1, HY_BANDS, dtype=f)[None, :]
    arg = (fb * wpos).astype(f).astype(np.float64)
    z = np.concatenate([t, np.cos(arg), -np.sin(arg), np.zeros((l, HY_FFN - HY_EMB))], axis=-1).astype(f)
    deltas = np.linspace(math.log(HY_TARGET) / HY_FAST_PCT, math.log(HY_TARGET) / HY_SLOW_PCT, HY_NFILT, dtype=f)
    return z, t, np.abs(deltas)[None, :]


def _hyena(hy, feats, spec_tabs, conv_tabs, cw, cb, w1, b1, w2, b2, w3, b3, fr, w4, bias):
    l = hy.shape[1]
    z, t, dl = feats
    w1p = jnp.pad(w1, ((0, HY_FFN - HY_EMB), (0, 0)))
    h = _hy_filters(l, z, t, w1p, b1[None], w2, b2[None], w3, b3[None], fr[None], w4, dl)
    spec = _hy_spectrum(h, spec_tabs)
    cb2 = cb[None]
    zz = _hy_conv(None, 0, hy, 1, cw, cb2, spec, 0, bias[0:1], conv_tabs)
    return _hy_conv(zz, None, hy, 2, cw, cb2, spec, 1, bias[1:2], conv_tabs)


@functools.lru_cache(maxsize=None)
def _rope_tables(l, dim, width, off, reps):
    f = np.float32
    rows = np.repeat(np.arange(l // GRID_W), GRID_W).astype(f)
    cols = np.tile(np.arange(GRID_W), l // GRID_W).astype(f)
    quarter = dim // 4
    half = dim // 2
    inv = (f(ROPE_THETA) ** (-np.arange(quarter, dtype=f) / f(quarter))).astype(f)
    ang = np.concatenate([rows[:, None] * inv, cols[:, None] * inv], axis=-1).astype(f).astype(np.float64)
    c, s = np.cos(ang), np.sin(ang)
    cf = np.ones((l, width))
    s1 = np.zeros((l, width))
    s2 = np.zeros((l, width))
    cf[:, off:off + dim] = np.concatenate([c, c], axis=-1)
    s1[:, off:off + half] = -s
    s2[:, off + half:off + dim] = s
    return tuple(np.tile(a, (1, reps)).astype(f) for a in (cf, s1, s2))


def _attn_kernel(q_ref, *rest, rep):
    o_ref = rest[-1]
    kv = rest[:-1]
    ks, vs = kv[0::2], kv[1::2]
    tq = q_ref.shape[1]
    lo = lax.broadcasted_iota(jnp.int32, (tq, LANES), 1) < HEAD_D
    nt = (((1,), (1,)), ((), ()))
    n_heads = q_ref.shape[0]

    def scores(h):
        return [lax.dot_general(q_ref[h], k_ref[h // rep], nt, preferred_element_type=F32) for k_ref in ks]

    def weighted(h, ss):
        m = ss[0].max(axis=-1, keepdims=True)
        for s in ss[1:]:
            m = jnp.maximum(m, s.max(axis=-1, keepdims=True))
        acc = 0.0
        for s, v_ref in zip(ss, vs):
            acc = acc + jnp.dot(jnp.exp2(s - m).astype(BF16), v_ref[h], preferred_element_type=F32)
        return acc / pltpu.roll(acc, HEAD_D, 1)

    outs = []
    ss_next = scores(0)
    for h in range(n_heads):
        ss = ss_next
        if h + 1 < n_heads:
            ss_next = scores(h + 1)
        outs.append(weighted(h, ss))
    for p in range(n_heads // 2):
        o_ref[:, p * LANES:(p + 1) * LANES] = jnp.where(lo, outs[2 * p], outs[2 * p + 1])


def _attention(q, kvs, rep, b, l):
    nh = q.shape[0]
    tq = min(ATTN_TILE, l)
    nq = l // tq
    specs = [pl.BlockSpec((nh, tq, LANES), lambda i, j: (0, i * nq + j, 0))]
    args = [q]
    for k, v, lk in kvs:
        specs += [pl.BlockSpec((k.shape[0], lk, LANES), lambda i, j: (0, i, 0)),
                  pl.BlockSpec((v.shape[0], lk, LANES), lambda i, j: (0, i, 0))]
        args += [k, v]
    return pl.pallas_call(
        functools.partial(_attn_kernel, rep=rep),
        grid=(b, nq),
        in_specs=specs,
        out_specs=pl.BlockSpec((None, tq, GROUP_W), lambda i, j: (i, j, 0)),
        out_shape=jax.ShapeDtypeStruct((b, l, GROUP_W), F32),
        compiler_params=_params("parallel", "parallel"),
        name="attention",
    )(*args)


def _ret_kernel(qs, ks, v_ref, g_ref, dl_ref, dc_ref, ng_ref, seg_ref, sf0_ref, sb0_ref,
                y_ref, sf_ref, sb_ref, sball, *, chunk, unroll):
    c_ = chunk
    w = GROUP_W
    n_chunks = qs.shape[0] // c_
    unroll = min(unroll, n_chunks)
    shift_c = int(math.log2(c_))
    shift_h = int(math.log2(HEAD_D))
    tn = (((0,), (0,)), ((), ()))
    nt = (((1,), (1,)), ((), ()))

    lgf = _log_sigmoid(dl_ref[0:1, :])
    lgb = _log_sigmoid(dl_ref[1:2, :])
    ii = lax.broadcasted_iota(jnp.int32, (c_, 1), 0).astype(F32)
    qdf = jnp.exp((ii + 1.0) * lgf)
    kdf = jnp.exp((c_ - 1.0 - ii) * lgf)
    cdf = jnp.exp(float(c_) * lgf)
    qdb = jnp.exp((c_ - ii) * lgb)
    kdb = jnp.exp(ii * lgb)
    cdb = jnp.exp(float(c_) * lgb)
    r_h = lax.shift_right_logical(lax.broadcasted_iota(jnp.int32, (w, w), 0), shift_h)
    c_h = lax.shift_right_logical(lax.broadcasted_iota(jnp.int32, (w, w), 1), shift_h)
    blockdiag = r_h == c_h
    lgc = _log_sigmoid(dc_ref[...])
    i4 = jnp.bitwise_and(lax.broadcasted_iota(jnp.int32, (4 * c_, c_), 0), c_ - 1)
    j4 = lax.broadcasted_iota(jnp.int32, (4 * c_, c_), 1)
    diff = (i4 - j4).astype(F32)
    dmat = (jnp.where(diff >= 0, jnp.exp(lgc[:, 0:1] * jnp.maximum(diff, 0.0)), 0.0)
            + jnp.where(diff <= 0, jnp.exp(lgc[:, 1:2] * jnp.maximum(-diff, 0.0)), 0.0))
    row_h = lax.shift_right_logical(lax.broadcasted_iota(jnp.int32, (4 * c_, w), 0), shift_c)
    lane_h = lax.shift_right_logical(lax.broadcasted_iota(jnp.int32, (4 * c_, w), 1), shift_h)
    headmask = row_h == lane_h
    seg = seg_ref[...]

    def rows(c):
        return pl.ds(pl.multiple_of(c * c_, c_), c_)

    def kv_outer(k, v, dec):
        a = lax.dot_general((k * dec).astype(BF16), v.astype(BF16), tn, preferred_element_type=F32)
        return jnp.where(blockdiag, a, 0.0)

    def bwd(t, s):
        c = n_chunks - 1 - t
        sball[c] = s
        sl = rows(c)
        return s * cdb + kv_outer(ks[sl, :], v_ref[sl, :], kdb)

    sb_ref[...] = lax.fori_loop(0, n_chunks, bwd, sb0_ref[...], unroll=unroll)

    def fwd(c, s):
        sl = rows(c)
        q, k, v = qs[sl, :], ks[sl, :], v_ref[sl, :]
        vb = v.astype(BF16)
        q4 = jnp.where(headmask, jnp.concatenate([q, q, q, q], axis=0), 0.0).astype(BF16)
        sc = lax.dot_general(q4, k.astype(BF16), nt, preferred_element_type=F32)
        o4 = jnp.dot((sc * dmat).astype(BF16), vb, preferred_element_type=F32)
        o4 = jnp.where(headmask, o4, 0.0)
        o = o4[0:c_] + o4[c_:2 * c_] + o4[2 * c_:3 * c_] + o4[3 * c_:4 * c_]
        o = o + jnp.dot((q * qdf).astype(BF16), s.astype(BF16), preferred_element_type=F32)
        o = o + jnp.dot((q * qdb).astype(BF16), sball[c].astype(BF16), preferred_element_type=F32)
        on = o * lax.rsqrt(_seg_mean(o * o, seg) + EPS) * ng_ref[...]
        g = g_ref[sl, :]
        y_ref[sl, :] = on * (g * jax.nn.sigmoid(g))
        return s * cdf + kv_outer(k, v, kdf)

    sf_ref[...] = lax.fori_loop(0, n_chunks, fwd, sf0_ref[...], unroll=unroll)


def _retention(ret, dl, dc, ng, seg, sf0, sb0):
    b, l, _ = ret.shape
    w = GROUP_W
    col = lambda j: pl.BlockSpec((None, l, w), lambda i: (i, 0, j))
    state = pl.BlockSpec((None, w, w), lambda i: (i, 0, 0))
    sshape = jax.ShapeDtypeStruct((b, w, w), F32)
    return pl.pallas_call(
        functools.partial(_ret_kernel, chunk=RET_CHUNK, unroll=RET_UNROLL),
        grid=(b,),
        in_specs=[col(0), col(1), col(2), col(3), _const_spec(dl), _const_spec(dc), _const_spec(ng),
                  _const_spec(seg), state, state],
        out_specs=[pl.BlockSpec((None, l, w), lambda i: (i, 0, 0)), state, state],
        out_shape=[jax.ShapeDtypeStruct((b, l, w), F32), sshape, sshape],
        scratch_shapes=[pltpu.VMEM((l // RET_CHUNK, w, w), F32)],
        compiler_params=_params("parallel"),
        name="retention",
    )(ret, ret, ret, ret, dl, dc, ng, seg, sf0, sb0)


def _mlp_kernel(x_ref, ya_ref, yb_ref, yc_ref, yd_ref, gout_ref, ga_ref, wout_ref,
                gffn_ref, sh_ref, sc_ref, gf_ref, wfi_ref, wfo_ref, fg_ref, o_ref,
                x1_a, h_a, x1_b, h_b, *, final_norm):
    t = pl.program_id(0)
    nff = wfo_ref.shape[0]

    @pl.when(t == 0)
    def _():
        x1_b[...] = jnp.zeros_like(x1_b)
        h_b[...] = jnp.zeros_like(h_b)

    def step(x1_w, h_w, x1_r, h_r):
        acc = 0.0
        for i, y_ref in enumerate((ya_ref, yb_ref, yc_ref, yd_ref)):
            rows = slice(i * GROUP_W, (i + 1) * GROUP_W)
            yn = (_rms(y_ref[...]) * gout_ref[:, rows]).astype(BF16)
            acc = acc + jnp.dot(yn, wout_ref[rows, :], preferred_element_type=F32)
        x1 = x_ref[...] + ga_ref[...] * acc
        x1_w[...] = x1
        h_w[...] = (_rms(x1) * gffn_ref[...] * (1.0 + sc_ref[...]) + sh_ref[...]).astype(BF16)

        h = h_r[...]
        ff = 0.0
        for c in range(nff):
            a = jnp.dot(h, wfi_ref[c], preferred_element_type=F32)
            g = jnp.dot(h, wfi_ref[c + nff], preferred_element_type=F32)
            u = (a * jax.nn.sigmoid(a) * g).astype(BF16)
            ff = ff + jnp.dot(u, wfo_ref[c], preferred_element_type=F32)
        out = x1_r[...] + gf_ref[...] * ff
        if final_norm:
            out = _rms(out) * fg_ref[...]
        o_ref[...] = out

    @pl.when(t % 2 == 0)
    def _():
        step(x1_a, h_a, x1_b, h_b)

    @pl.when(t % 2 == 1)
    def _():
        step(x1_b, h_b, x1_a, h_a)


def _mlp(x, ys, g_out, g_a, w_out, g_ffn, sh, sc, g_f, wfi, wfo, final_g, final_norm):
    b, l, d = x.shape
    tm = min(MLP_TILE, l)
    nt = l // tm
    n = b * nt
    cur = lambda t: jnp.minimum(t, n - 1)
    prv = lambda t: jnp.maximum(t - 1, 0)
    tok = lambda wd: pl.BlockSpec((None, tm, wd), lambda t: (cur(t) // nt, cur(t) % nt, 0))
    vec = pl.BlockSpec((None, 1, d), lambda t: (cur(t) // nt, 0, 0))
    return pl.pallas_call(
        functools.partial(_mlp_kernel, final_norm=final_norm),
        grid=(n + 1,),
        in_specs=[tok(d)] + [tok(GROUP_W)] * 4 + [
            _const_spec(g_out), vec, _const_spec(w_out), _const_spec(g_ffn), vec, vec,
            pl.BlockSpec((None, 1, d), lambda t: (prv(t) // nt, 0, 0)),
            _const_spec(wfi), _const_spec(wfo), _const_spec(final_g)],
        out_specs=pl.BlockSpec((None, tm, d), lambda t: (prv(t) // nt, prv(t) % nt, 0)),
        out_shape=jax.ShapeDtypeStruct((b, l, d), F32),
        scratch_shapes=[pltpu.VMEM((tm, d), F32), pltpu.VMEM((tm, d), BF16),
                        pltpu.VMEM((tm, d), F32), pltpu.VMEM((tm, d), BF16)],
        compiler_params=_params("arbitrary"),
        name="mlp",
    )(x, *ys, g_out, g_a, w_out, g_ffn, sh, sc, g_f, wfi, wfo, final_g)


def _pad_proj(w):
    end = PROJ_OFFSETS[2] + MLA_KV_RANK + MLA_ROPE
    z = jnp.zeros((w.shape[0], LANES - MLA_ROPE), w.dtype)
    return jnp.concatenate([w[:, :end], z, w[:, end:]], axis=1).astype(BF16)


def _mla_weights(wuq, wukv):
    r = wuq.shape[0]
    q = wuq.reshape(r, 4, MLA_NOPE + MLA_ROPE)
    q = jnp.pad(q, ((0, 0), (0, 0), (0, LANES - MLA_NOPE - MLA_ROPE))).reshape(r, 4 * LANES)
    kv = wukv.reshape(wukv.shape[0], 4, MLA_NOPE + MLA_V)
    k = jnp.pad(kv[..., :MLA_NOPE], ((0, 0), (0, 0), (0, LANES - MLA_NOPE))).reshape(-1, 4 * LANES)
    v = jnp.concatenate([kv[..., MLA_NOPE:], kv[..., MLA_NOPE:]], axis=-1).reshape(-1, 4 * LANES)
    return q.astype(BF16), k.astype(BF16), v.astype(BF16)


def _ffn_weights(w_in, w_out):
    d = w_in.shape[0]
    nff = D_FF // FF_TILE
    wfi = w_in.astype(BF16).reshape(d, 2 * nff, FF_TILE).transpose(1, 0, 2)
    return wfi, w_out.astype(BF16).reshape(nff, FF_TILE, d)


def _seg_matrix(width):
    i = jnp.arange(width) // HEAD_D
    return jnp.where(i[:, None] == i[None, :], 1.0 / HEAD_D, 0.0).astype(BF16)


def kernel(x, c, ctx, c_ctx, w_mod, b_mod, norm_attn_g, norm_ffn_g, w_in, hy_conv_w, hy_conv_b, hy_w1, hy_b1, hy_w2, hy_b2, hy_w3, hy_b3, hy_freq, hy_w4, hy_bias, mla_q_norm_g, mla_w_uq, mla_kv_norm_g, mla_w_ukv, ret_decay, ret_norm_g, gqa_q_norm_g, gqa_k_norm_g, out_norm_g, w_out, w_ffn_in, w_ffn_out, final_norm_g):
    b, n_lat, d = x.shape
    n_ctx = ctx.shape[1]
    depth = w_mod.shape[0]

    rope_mla = _rope_tables(n_lat, MLA_ROPE, LANES, MLA_NOPE, 1)
    rope_hd2 = _rope_tables(n_lat, HEAD_D, HEAD_D, 0, 2)
    seg2, seg4 = _seg_matrix(LANES), _seg_matrix(GROUP_W)
    hy_consts = {n: (_hy_features(n),) + _fft_tables(n) for n in {n_lat, n_ctx}}
    zero_state = jnp.zeros((b, GROUP_W, GROUP_W), F32)

    rows = -(-(b + 1) // 8) * 8
    cond = jnp.zeros((rows, d), F32).at[:b].set(c).at[b].set(c_ctx)
    mod = _ada_mod(cond, w_mod, b_mod)

    xl, xc = x, ctx.reshape(1, b * n_ctx, d)
    for l in range(depth):
        update_ctx = l < depth - 1
        m_lat = [m[:b, None, :] for m in jnp.split(mod[l], 6, axis=-1)]
        m_ctx = [m[b][None, None, :] for m in jnp.split(mod[l], 6, axis=-1)]
        wp = _pad_proj(w_in[l])
        g_attn = norm_attn_g[l][None]
        mla_w = (mla_q_norm_g[l][None], mla_kv_norm_g[l][None]) + _mla_weights(mla_w_uq[l], mla_w_ukv[l])
        gqa_w = (jnp.tile(gqa_q_norm_g[l], 2)[None], jnp.tile(gqa_k_norm_g[l], 2)[None], seg2)
        hy_l, ret_l, q_l, k_l, v_l, gq_l, gk_l, gv_l = _front(
            xl, g_attn, m_lat[0], m_lat[1], wp, mla_w, gqa_w, (rope_mla, rope_hd2))
        hy_c, ret_c, q_c, k_c, v_c, gq_c, gk_c, gv_c = _front(
            xc, g_attn, m_ctx[0], m_ctx[1], wp, mla_w, gqa_w, None)
        hy_c = hy_c.reshape(b, n_ctx, -1)
        ret_c = ret_c.reshape(b, n_ctx, -1)

        hy_args = (hy_conv_w[l], hy_conv_b[l], hy_w1[l], hy_b1[l], hy_w2[l], hy_b2[l], hy_w3[l], hy_b3[l],
                   hy_freq[l], hy_w4[l], hy_bias[l])
        ya_l = _hyena(hy_l, *hy_consts[n_lat], *hy_args)
        yb_l = _attention(q_l, [(k_c, v_c, n_ctx), (k_l, v_l, n_lat)], 1, b, n_lat)

        dl = jnp.repeat(ret_decay[l], HEAD_D, axis=1)
        dc = jnp.repeat(ret_decay[l].T, RET_CHUNK, axis=0)
        ng = ret_norm_g[l][None]
        yc_c, s_f, s_b = _retention(ret_c, dl, dc, ng, seg4, zero_state, zero_state)
        yc_l, _, _ = _retention(ret_l, dl, dc, ng, seg4, s_f, s_b)
        yd_l = _attention(gq_l, [(gk_c, gv_c, n_ctx), (gk_l, gv_l, n_lat)], 2, b, n_lat)

        wo = w_out[l].astype(BF16)
        wfi, wfo = _ffn_weights(w_ffn_in[l], w_ffn_out[l])
        g_out, g_ffn, g_fin = out_norm_g[l][None], norm_ffn_g[l][None], final_norm_g[None]
        xl = _mlp(xl, (ya_l, yb_l, yc_l, yd_l), g_out, m_lat[2], wo, g_ffn, m_lat[3], m_lat[4], m_lat[5],
                  wfi, wfo, g_fin, l == depth - 1)

        if update_ctx:
            flat = lambda a: a.reshape(1, b * n_ctx, GROUP_W)
            ya_c = _hyena(hy_c, *hy_consts[n_ctx], *hy_args)
            yb_c = _attention(q_c, [(k_c, v_c, n_ctx)], 1, b, n_ctx)
            yd_c = _attention(gq_c, [(gk_c, gv_c, n_ctx)], 2, b, n_ctx)
            xc = _mlp(xc, tuple(flat(a) for a in (ya_c, yb_c, yc_c, yd_c)), g_out, m_ctx[2], wo,
                      g_ffn, m_ctx[3], m_ctx[4], m_ctx[5], wfi, wfo, g_fin, False)

    return xl
```

```python
import functools
import math

import numpy as np
import jax
import jax.numpy as jnp
from jax import lax
from jax.experimental import pallas as pl
from jax.experimental.pallas import tpu as pltpu

F32 = jnp.float32
BF16 = jnp.bfloat16
HI = lax.Precision.HIGHEST

D_MODEL = 1024
GRID_W = 64
EPS = 1e-6
ROPE_THETA = 10000.0
GROUP_W = 256
LANES = 128

HY_EMB = 33
HY_BANDS = 16
HY_FFN = 64
HY_NFILT = 1024
HY_DECAY_SHIFT = 0.05
HY_FAST_PCT = 0.3
HY_SLOW_PCT = 1.5
HY_TARGET = 1e-2
FFT_RADIX = 16
FFT_BINS = FFT_RADIX // 2 + 1
FFT_PIECE_ROWS = 32

MLA_NOPE = 64
MLA_ROPE = 32
MLA_V = 64
MLA_KV_RANK = 128
LOG2E = math.log2(math.e)
MLA_SCALE = (MLA_NOPE + MLA_ROPE) ** -0.5 * LOG2E

HEAD_D = 64
RET_CHUNK = 256
RET_UNROLL = 4
GQA_SCALE = HEAD_D ** -0.5 * LOG2E
RET_K_SCALE = HEAD_D ** -0.5

D_FF = 2816
FF_TILE = 256

PROJ_SPLITS = (768, 256, 256, 1024, 512)
PROJ_OFFSETS = (0, 768, 1024, 1280, 2304, 2816)

V7X_VMEM_BYTES = 64 * 1024 * 1024
VMEM_LIMIT = V7X_VMEM_BYTES - 8 * 1024 * 1024

FRONT_TILE = 1024
ATTN_TILE = 1024
MLP_TILE = 512


def _params(*sem):
    return pltpu.CompilerParams(dimension_semantics=sem, vmem_limit_bytes=VMEM_LIMIT)


def _const_spec(a):
    nd = a.ndim
    return pl.BlockSpec(a.shape, lambda *_: (0,) * nd, pipeline_mode=pl.Buffered(1))


def _rms(x):
    return x * lax.rsqrt(jnp.mean(x * x, axis=-1, keepdims=True) + EPS)


def _rope(x, cf, s1, s2, half):
    w = x.shape[-1]
    return x * cf + pltpu.roll(x, w - half, 1) * s1 + pltpu.roll(x, half, 1) * s2


def _seg_mean(sq, seg):
    hi = sq.astype(BF16)
    lo = (sq - hi.astype(F32)).astype(BF16)
    return (jnp.dot(hi, seg, preferred_element_type=F32) + jnp.dot(lo, seg, preferred_element_type=F32))


def _log_sigmoid(x):
    return -(jnp.maximum(-x, 0.0) + jnp.log(1.0 + jnp.exp(-jnp.abs(x))))


def _ada_kernel(c_ref, w_ref, b_ref, o_ref):
    c = c_ref[...]
    s = c * jax.nn.sigmoid(c)
    o_ref[...] = jnp.dot(s, w_ref[...], precision=HI, preferred_element_type=F32) + b_ref[...]


def _ada_mod(cond, w_mod, b_mod):
    depth, d, n = w_mod.shape
    r = cond.shape[0]
    tn = 1024
    return pl.pallas_call(
        _ada_kernel,
        grid=(depth, n // tn),
        in_specs=[pl.BlockSpec((r, d), lambda l, j: (0, 0)),
                  pl.BlockSpec((None, d, tn), lambda l, j: (l, 0, j)),
                  pl.BlockSpec((None, 1, tn), lambda l, j: (l, 0, j))],
        out_specs=pl.BlockSpec((None, r, tn), lambda l, j: (l, 0, j)),
        out_shape=jax.ShapeDtypeStruct((depth, r, n), F32),
        compiler_params=_params("arbitrary", "arbitrary"),
        name="ada_mod",
    )(cond, w_mod, b_mod.reshape(depth, 1, n))


def _value_with_ones(v2, head):
    lo = lax.broadcasted_iota(jnp.int32, v2.shape, 1) < HEAD_D
    keep = lo if head % 2 == 0 else jnp.logical_not(lo)
    return jnp.where(keep, v2, 1.0).astype(BF16)


def _mla_heads(cq, ckvr, gq, gkv, wuq_ref, wk_ref, wv_ref, tables, q_ref, k_ref, v_ref):
    half = MLA_ROPE // 2
    qn = (_rms(cq) * gq).astype(BF16)
    q = jnp.dot(qn, wuq_ref[...], preferred_element_type=F32)
    kvn = (_rms(ckvr[:, :MLA_KV_RANK]) * gkv).astype(BF16)
    k = jnp.dot(kvn, wk_ref[...], preferred_element_type=F32)
    v = jnp.dot(kvn, wv_ref[...], preferred_element_type=F32)
    kpe = pltpu.roll(ckvr[:, MLA_KV_RANK:], MLA_NOPE, 1)
    if tables is not None:
        kpe = _rope(kpe, *tables, half)
    for h in range(q_ref.shape[0]):
        cols = slice(h * LANES, (h + 1) * LANES)
        qh = q[:, cols]
        if tables is not None:
            qh = _rope(qh, *tables, half)
        q_ref[h] = (qh * MLA_SCALE).astype(BF16)
        k_ref[h] = (k[:, cols] + kpe).astype(BF16)
        v_ref[h] = _value_with_ones(v[:, cols], h)


def _gqa_heads(x, gq, gk, seg, tables, q_ref, k_ref, v_ref):
    half = HEAD_D // 2
    lo = lax.broadcasted_iota(jnp.int32, (x.shape[0], LANES), 1) < HEAD_D

    def normed(a, g):
        a = a * lax.rsqrt(_seg_mean(a * a, seg) + EPS) * g
        return _rope(a, *tables, half) if tables is not None else a

    def dup(a, g):
        a = jnp.where(lo if g == 0 else jnp.logical_not(lo), a, 0.0)
        return a + pltpu.roll(a, HEAD_D, 1)

    for p in range(2):
        qb = normed(x[:, p * LANES:(p + 1) * LANES], gq) * GQA_SCALE
        q_ref[2 * p] = jnp.where(lo, qb, 0.0).astype(BF16)
        q_ref[2 * p + 1] = jnp.where(lo, 0.0, qb).astype(BF16)
    kb = normed(x[:, 2 * LANES:3 * LANES], gk)
    vb = x[:, 3 * LANES:4 * LANES]
    for g in range(2):
        k_ref[g] = dup(kb, g).astype(BF16)
        for r in range(2):
            v_ref[2 * g + r] = _value_with_ones(dup(vb, g), 2 * g + r)


def _front_kernel(x_ref, g_ref, sh_ref, sc_ref, w_ref, gq_ref, gkv_ref, wuq_ref, wk_ref, wv_ref,
                  ggq_ref, ggk_ref, seg_ref, *rest, rope):
    if rope:
        mla_t = tuple(r[...] for r in rest[0:3])
        hd_t = tuple(r[...] for r in rest[3:6])
        rest = rest[6:]
    else:
        mla_t = hd_t = None
    hy_ref, ret_ref, mq_ref, mk_ref, mv_ref, gq_out, gk_out, gv_out = rest
    y = _rms(x_ref[...]) * g_ref[...]
    h = (y * (1.0 + sc_ref[...]) + sh_ref[...]).astype(BF16)
    o0, o1, o2, o3, o4, o5 = PROJ_OFFSETS

    def proj(a, b):
        return jnp.dot(h, w_ref[:, a:b], preferred_element_type=F32)

    _mla_heads(proj(o1, o2), proj(o2, o3), gq_ref[...], gkv_ref[...], wuq_ref, wk_ref, wv_ref, mla_t,
               mq_ref, mk_ref, mv_ref)
    _gqa_heads(proj(o4, o5), ggq_ref[...], ggk_ref[...], seg_ref[...], hd_t, gq_out, gk_out, gv_out)
    for i in range(2 * GROUP_W // LANES):
        cols = slice(i * LANES, (i + 1) * LANES)
        a = proj(o3 + i * LANES, o3 + (i + 1) * LANES)
        if i * LANES >= GROUP_W:
            a = a * RET_K_SCALE
        ret_ref[:, cols] = _rope(a, *hd_t, HEAD_D // 2) if rope else a
    ret_ref[:, 2 * GROUP_W:] = proj(o3 + 2 * GROUP_W, o4)
    hy_ref[...] = proj(o0, o1)


def _front(x, g, sh, sc, w, mla_w, gqa_w, tables):
    b, l, d = x.shape
    tm = min(FRONT_TILE, l)
    nt = l // tm
    rope = tables is not None
    tok = lambda wd: pl.BlockSpec((None, tm, wd), lambda i, j: (i, j, 0))
    vec = pl.BlockSpec((None, 1, d), lambda i, j: (i, 0, 0))
    head = lambda nh: pl.BlockSpec((nh, tm, LANES), lambda i, j: (0, i * nt + j, 0))
    hshape = lambda nh: jax.ShapeDtypeStruct((nh, b * l, LANES), BF16)
    consts = [w, *mla_w, *gqa_w]
    args = [x, g, sh, sc, *consts]
    specs = [tok(d), _const_spec(g), vec, vec] + [_const_spec(a) for a in consts]
    if rope:
        args += [*tables[0], *tables[1]]
        specs += [pl.BlockSpec((tm, LANES), lambda i, j: (j, 0))] * 6
    return pl.pallas_call(
        functools.partial(_front_kernel, rope=rope),
        grid=(b, nt),
        in_specs=specs,
        out_specs=[tok(PROJ_SPLITS[0]), tok(PROJ_SPLITS[3]), head(4), head(4), head(4), head(4), head(2), head(4)],
        out_shape=[jax.ShapeDtypeStruct((b, l, PROJ_SPLITS[0]), F32),
                   jax.ShapeDtypeStruct((b, l, PROJ_SPLITS[3]), F32),
                   hshape(4), hshape(4), hshape(4), hshape(4), hshape(2), hshape(4)],
        compiler_params=_params("parallel", "parallel"),
        name="front",
    )(*args)


def _hy_filter_kernel(z_ref, t_ref, w1_ref, b1_ref, w2_ref, b2_ref, w3_ref, b3_ref, fr_ref,
                      w4_ref, dl_ref, o_ref):
    fr = fr_ref[...]
    dot = functools.partial(jnp.dot, precision=HI, preferred_element_type=F32)
    h = jnp.sin(fr * (dot(z_ref[...], w1_ref[...]) + b1_ref[...]))
    h = jnp.sin(fr * (dot(h, w2_ref[...]) + b2_ref[...]))
    h = jnp.sin(fr * (dot(h, w3_ref[...]) + b3_ref[...]))
    filt = dot(h, w4_ref[...]) * (jnp.exp(-t_ref[...] * dl_ref[...]) + HY_DECAY_SHIFT)
    fwd = filt[:, :GROUP_W]
    bwd = filt[:, GROUP_W:]
    row = lax.broadcasted_iota(jnp.int32, bwd.shape, 0)
    bwd = jnp.where(row == 0, 0.0, bwd)
    nrm = (jnp.sum(jnp.abs(fwd), axis=0, keepdims=True) + jnp.sum(jnp.abs(bwd), axis=0, keepdims=True))
    o_ref[:, :GROUP_W] = fwd / nrm
    o_ref[:, GROUP_W:] = bwd / nrm


def _hy_filters(l, z, t, w1, b1, w2, b2, w3, b3, fr, w4, dl):
    full = lambda a: pl.BlockSpec(a.shape, lambda o: (0, 0))
    return pl.pallas_call(
        _hy_filter_kernel,
        grid=(2,),
        in_specs=[full(z), full(t), full(w1), full(b1), full(w2), full(b2), full(w3), full(b3), full(fr),
                  pl.BlockSpec((HY_FFN, 2 * GROUP_W), lambda o: (0, o)),
                  pl.BlockSpec((1, 2 * GROUP_W), lambda o: (0, o))],
        out_specs=pl.BlockSpec((l, 2 * GROUP_W), lambda o: (0, o)),
        out_shape=jax.ShapeDtypeStruct((l, HY_NFILT), F32),
        compiler_params=_params("arbitrary"),
        name="hy_filters",
    )(z, t, w1, b1, w2, b2, w3, b3, fr, w4, dl)


def _split_bf16(x):
    hi = x.astype(BF16)
    return hi, (x - hi.astype(F32)).astype(BF16)


def _dot_3pass(a_hi, a_lo, x):
    x_hi, x_lo = _split_bf16(x)
    dot = functools.partial(jnp.dot, preferred_element_type=F32)
    return dot(a_hi, x_hi) + (dot(a_hi, x_lo) + dot(a_lo, x_hi))


def _hy_spec_kernel(ch_ref, cl_ref, sh_ref, sl_ref, h_ref, re_ref, im_ref):
    for o in range(2):
        hf = h_ref[:, 2 * o * GROUP_W:(2 * o + 1) * GROUP_W]
        hb = h_ref[:, (2 * o + 1) * GROUP_W:(2 * o + 2) * GROUP_W]
        cols = slice(o * GROUP_W, (o + 1) * GROUP_W)
        re_ref[:, cols] = _dot_3pass(ch_ref[...], cl_ref[...], hf + hb)
        im_ref[:, cols] = -_dot_3pass(sh_ref[...], sl_ref[...], hf - hb)


def _hy_spectrum(h, tabs):
    rows, l = tabs[0].shape
    fb = rows // FFT_BINS if rows // FFT_BINS >= 256 else rows
    out = jax.ShapeDtypeStruct((rows, 2 * GROUP_W), F32)
    blk = pl.BlockSpec((fb, 2 * GROUP_W), lambda f: (f, 0))
    return pl.pallas_call(
        _hy_spec_kernel,
        grid=(rows // fb,),
        in_specs=[pl.BlockSpec((fb, l), lambda f: (f, 0))] * 4 + [_const_spec(h)],
        out_specs=[blk, blk],
        out_shape=[out, out],
        compiler_params=_params("arbitrary"),
        name="hy_spectrum",
    )(*tabs, h)


def _short_conv_to(dst_ref, u, w, b):
    n = u.shape[0]
    mid = w[1:2] * u + b
    dst_ref[...] = w[0:1] * pltpu.roll(u, 1, 0) + mid + w[2:3] * pltpu.roll(u, n - 1, 0)
    dst_ref[0:1, :] = mid[0:1] + w[2:3] * u[1:2]
    dst_ref[n - 1:n, :] = mid[n - 1:n] + w[0:1] * u[n - 2:n - 1]


_R2 = math.sqrt(0.5)


def _dft8_real4(x0, x1, x2, x3):
    s02, d02 = x0 + x2, x0 - x2
    s13, d13 = x1 + x3, x1 - x3
    p, q = _R2 * d13, _R2 * s13
    re = [s02 + s13, x0 + p, d02, x0 - p, s02 - s13]
    im = [None, -(q + x2), -d13, x2 - q, None]
    return re, im


def _fwd16(a):
    er, ei = _dft8_real4(a[0], a[2], a[4], a[6])
    orr, oi = _dft8_real4(a[1], a[3], a[5], a[7])

    def ext(re, im, k):
        if k <= 4:
            return re[k], im[k]
        return re[8 - k], (None if im[8 - k] is None else -im[8 - k])

    out_r, out_i = [], []
    for k in range(FFT_BINS):
        e_r, e_i = ext(er, ei, k)
        o_r, o_i = ext(orr, oi, k)
        c, s = math.cos(math.pi * k / 8), math.sin(math.pi * k / 8)
        if k == 0:
            out_r.append(e_r + o_r)
            out_i.append(None)
        elif k == 8:
            out_r.append(e_r - o_r)
            out_i.append(None)
        elif k == 4:
            out_r.append(e_r)
            out_i.append(-o_r)
        else:
            out_r.append(e_r + (c * o_r + s * o_i))
            out_i.append(e_i + (c * o_i - s * o_r))
    return out_r, out_i


def _dft8_real4_t(kr, ki):
    a = kr[0] + kr[4]
    b = kr[0] - kr[4]
    x0 = a + kr[2] + kr[1] + kr[3]
    x2 = a - kr[2] - ki[1] + ki[3]
    t1 = _R2 * (kr[1] - kr[3])
    t2 = _R2 * (ki[1] + ki[3])
    return x0, b - ki[2] + t1 - t2, x2, b + ki[2] - t1 - t2


def _inv16(br, bi):
    o_r, o_i = [None] * FFT_BINS, [None] * FFT_BINS
    for k in range(FFT_BINS):
        c, s = math.cos(math.pi * k / 8), math.sin(math.pi * k / 8)
        if k == 0:
            o_r[k] = br[k]
        elif k == 8:
            o_r[k] = -br[k]
        elif k == 4:
            o_r[k] = -bi[k]
        else:
            o_r[k] = c * br[k] - s * bi[k]
            o_i[k] = c * bi[k] + s * br[k]

    def fold(re, im):
        fr = [re[0] + re[8], re[1] + re[7], re[2] + re[6], re[3] + re[5], re[4]]
        fi = [None, im[1] - im[7], im[2] - im[6], im[3] - im[5], None]
        return fr, fi

    xe = _dft8_real4_t(*fold(br, bi))
    xo = _dft8_real4_t(*fold(o_r, o_i))
    return [xe[0], xo[0], xe[1], xo[1], xe[2], xo[2], xe[3], xo[3]]


def _hy_conv_kernel(sig_ref, gate_ref, cw_ref, cb_ref, kr_ref, ki_ref, bias_ref, m_ref, mt_ref,
                    twc_ref, tws_ref, twci_ref, twsi_ref, o_ref, u_a, g_a, v_a, u_b, g_b, v_b, z_scr,
                    *, sig_col, gate_col, rows):
    t = pl.program_id(0)

    @pl.when(t == 0)
    def _():
        u_b[...] = jnp.zeros_like(u_b)
        g_b[...] = jnp.zeros_like(g_b)
        v_b[...] = jnp.zeros_like(v_b)

    refs = (sig_ref, gate_ref, cw_ref, cb_ref, kr_ref, ki_ref, bias_ref, m_ref, mt_ref,
            twc_ref, tws_ref, twci_ref, twsi_ref, o_ref, z_scr)

    @pl.when(t % 2 == 0)
    def _():
        _hy_conv_step(*refs, (u_a, g_a, v_a), (u_b, g_b, v_b), sig_col, gate_col, rows)

    @pl.when(t % 2 == 1)
    def _():
        _hy_conv_step(*refs, (u_b, g_b, v_b), (u_a, g_a, v_a), sig_col, gate_col, rows)


def _hy_conv_step(sig_ref, gate_ref, cw_ref, cb_ref, kr_ref, ki_ref, bias_ref, m_ref, mt_ref,
                  twc_ref, tws_ref, twci_ref, twsi_ref, o_ref, z_scr, cur, prev, sig_col, gate_col, rows):
    u_scr, g_scr, v_cur = cur
    u_prev, g_prev, v_prev = prev
    w = GROUP_W
    l = u_scr.shape[0]
    n2 = l // (FFT_RADIX // 2)
    n_pieces = n2 // rows
    if sig_col is not None:
        _short_conv_to(u_scr, sig_ref[...], cw_ref[:, sig_col * w:(sig_col + 1) * w],
                       cb_ref[:, sig_col * w:(sig_col + 1) * w])
    else:
        u_scr[...] = sig_ref[...]
    _short_conv_to(g_scr, gate_ref[...], cw_ref[:, gate_col * w:(gate_col + 1) * w],
                   cb_ref[:, gate_col * w:(gate_col + 1) * w])

    def piece(base, r0):
        return pl.ds(pl.multiple_of(base + r0, rows), rows)

    def fwd_piece(i, carry):
        r0 = i * rows
        for hh in range(w // LANES):
            lanes = slice(hh * LANES, (hh + 1) * LANES)
            fr, fi = _fwd16([u_scr[piece(n1 * n2, r0), lanes] for n1 in range(FFT_RADIX // 2)])
            for k in range(FFT_BINS):
                if k == 0:
                    zr, zi = fr[0], jnp.zeros_like(fr[0])
                else:
                    c, s = twc_ref[k, piece(0, r0), :], tws_ref[k, piece(0, r0), :]
                    if fi[k] is None:
                        zr, zi = fr[k] * c, -(fr[k] * s)
                    else:
                        zr, zi = fr[k] * c + fi[k] * s, fi[k] * c - fr[k] * s
                z_scr[k, piece(0, r0), lanes] = zr.astype(BF16)
                z_scr[k, piece(n2, r0), lanes] = zi.astype(BF16)
        return carry

    lax.fori_loop(0, n_pieces, fwd_piece, 0)

    def slab(k):
        x = jnp.dot(m_ref[...], z_scr[k], preferred_element_type=F32)
        xr, xi = x[:n2], x[n2:]
        bins = pl.ds(pl.multiple_of(k * n2, n2), n2)
        kr, ki = kr_ref[bins, :], ki_ref[bins, :]
        y = jnp.concatenate([xr * kr - xi * ki, xr * ki + xi * kr], axis=0).astype(BF16)
        v_cur[k] = jnp.dot(mt_ref[...], y, preferred_element_type=F32)

    def inv_piece(i, carry):
        slab(i)
        r0 = i * rows
        for hh in range(w // LANES):
            lanes = slice(hh * LANES, (hh + 1) * LANES)
            br, bi = [], []
            for k in range(FFT_BINS):
                vr, vi = v_prev[k, piece(0, r0), lanes], v_prev[k, piece(n2, r0), lanes]
                if k == 0:
                    br.append(vr * (0.5 / l))
                    bi.append(None)
                else:
                    c, s = twci_ref[k, piece(0, r0), :], twsi_ref[k, piece(0, r0), :]
                    br.append(vr * c - vi * s)
                    bi.append(None if k == FFT_BINS - 1 else vr * s + vi * c)
            y = _inv16(br, bi)
            for n1 in range(FFT_RADIX // 2):
                rws = piece(n1 * n2, r0)
                o_ref[rws, lanes] = g_prev[rws, lanes] * (y[n1] + u_prev[rws, lanes] * bias_ref[:, lanes])
        return carry

    for i in range(n_pieces):
        inv_piece(i, 0)
    for k in range(n_pieces, FFT_BINS):
        slab(k)


def _hy_conv(sig, sig_col, hy, gate_col, cw, cb, spec, order, bias, tabs):
    b, l, _ = hy.shape
    w = GROUP_W
    n2 = 2 * l // FFT_RADIX
    rows = max(FFT_PIECE_ROWS, n2 // (FFT_BINS - 1))
    rows = min(rows, n2)
    kr, ki = spec
    sig_arr = hy if sig_col is not None else sig
    sig_blk = sig_col if sig_col is not None else 0
    seq = lambda col: pl.BlockSpec((None, l, w), lambda t: (jnp.minimum(t, b - 1), 0, col))
    spec_blk = pl.BlockSpec((FFT_BINS * n2, w), lambda t: (0, order), pipeline_mode=pl.Buffered(1))
    kernel = functools.partial(_hy_conv_kernel, sig_col=sig_col, gate_col=gate_col, rows=rows)
    stage = [pltpu.VMEM((l, w), F32), pltpu.VMEM((l, w), F32), pltpu.VMEM((FFT_BINS, 2 * n2, w), F32)]
    return pl.pallas_call(
        kernel,
        grid=(b + 1,),
        in_specs=[seq(sig_blk), seq(gate_col), _const_spec(cw), _const_spec(cb), spec_blk, spec_blk,
                  _const_spec(bias)] + [_const_spec(t) for t in tabs],
        out_specs=pl.BlockSpec((None, l, w), lambda t: (jnp.maximum(t - 1, 0), 0, 0)),
        out_shape=jax.ShapeDtypeStruct((b, l, w), F32),
        scratch_shapes=stage + stage + [pltpu.VMEM((FFT_BINS, 2 * n2, w), BF16)],
        compiler_params=_params("arbitrary"),
        name="hy_conv",
    )(sig_arr, hy, cw, cb, kr, ki, bias, *tabs)


@functools.lru_cache(maxsize=None)
def _fft_tables(l):
    n = 2 * l
    n2 = n // FFT_RADIX
    j = np.arange(n2, dtype=np.int64)
    k1 = np.arange(FFT_BINS, dtype=np.int64)
    f32 = lambda a: np.ascontiguousarray(a, dtype=np.float32)
    b16 = lambda a: f32(a).astype(BF16)
    ang2 = ((j[:, None] * j[None, :]) % n2).astype(np.float64) * (2.0 * math.pi / n2)
    c2, s2 = np.cos(ang2), np.sin(ang2)
    m = np.block([[c2, s2], [-s2, c2]])
    angt = (k1[:, None] * j[None, :]).astype(np.float64) * (2.0 * math.pi / n)
    wk = np.where((k1 == 0) | (k1 == FFT_BINS - 1), 1.0, 2.0)[:, None] / n
    lane = lambda a: f32(np.broadcast_to(a[:, :, None], a.shape + (LANES,)))
    bins = (k1[:, None] + FFT_RADIX * j[None, :]).reshape(-1)
    angf = ((bins[:, None] * np.arange(l, dtype=np.int64)[None, :]) % n).astype(np.float64) * (2.0 * math.pi / n)
    conv_tabs = (b16(m), b16(m.T), lane(np.cos(angt)), lane(np.sin(angt)),
                 lane(np.cos(angt) * wk), lane(np.sin(angt) * wk))
    def split(a):
        hi = b16(a)
        return hi, b16(a - hi.astype(np.float64))

    return split(np.cos(angf)) + split(np.sin(angf)), conv_tabs


@functools.lru_cache(maxsize=None)
def _hy_features(l):
    f = np.float32
    t = np.linspace(0.0, 1.0, l, dtype=f)[:, None]
    wpos = (f(2.0 * math.pi) * np.arange(l, dtype=f)[:, None] / f(l)).astype(f)
    fb = np.linspace(1e-4, HY_BANDS - 1, HY_BANDS, dtype=f)[None, :]
    arg = (fb * wpos).astype(f).astype(np.float64)
    z = np.concatenate([t, np.cos(arg), -np.sin(arg), np.zeros((l, HY_FFN - HY_EMB))], axis=-1).astype(f)
    deltas = np.linspace(math.log(HY_TARGET) / HY_FAST_PCT, math.log(HY_TARGET) / HY_SLOW_PCT, HY_NFILT, dtype=f)
    return z, t, np.abs(deltas)[None, :]


def _hyena(hy, feats, spec_tabs, conv_tabs, cw, cb, w1, b1, w2, b2, w3, b3, fr, w4, bias):
    l = hy.shape[1]
    z, t, dl = feats
    w1p = jnp.pad(w1, ((0, HY_FFN - HY_EMB), (0, 0)))
    h = _hy_filters(l, z, t, w1p, b1[None], w2, b2[None], w3, b3[None], fr[None], w4, dl)
    spec = _hy_spectrum(h, spec_tabs)
    cb2 = cb[None]
    zz = _hy_conv(None, 0, hy, 1, cw, cb2, spec, 0, bias[0:1], conv_tabs)
    return _hy_conv(zz, None, hy, 2, cw, cb2, spec, 1, bias[1:2], conv_tabs)


@functools.lru_cache(maxsize=None)
def _rope_tables(l, dim, width, off, reps):
    f = np.float32
    rows = np.repeat(np.arange(l // GRID_W), GRID_W).astype(f)
    cols = np.tile(np.arange(GRID_W), l // GRID_W).astype(f)
    quarter = dim // 4
    half = dim // 2
    inv = (f(ROPE_THETA) ** (-np.arange(quarter, dtype=f) / f(quarter))).astype(f)
    ang = np.concatenate([rows[:, None] * inv, cols[:, None] * inv], axis=-1).astype(f).astype(np.float64)
    c, s = np.cos(ang), np.sin(ang)
    cf = np.ones((l, width))
    s1 = np.zeros((l, width))
    s2 = np.zeros((l, width))
    cf[:, off:off + dim] = np.concatenate([c, c], axis=-1)
    s1[:, off:off + half] = -s
    s2[:, off + half:off + dim] = s
    return tuple(np.tile(a, (1, reps)).astype(f) for a in (cf, s1, s2))


def _attn_kernel(q_ref, *rest, rep):
    o_ref = rest[-1]
    kv = rest[:-1]
    ks, vs = kv[0::2], kv[1::2]
    tq = q_ref.shape[1]
    lo = lax.broadcasted_iota(jnp.int32, (tq, LANES), 1) < HEAD_D
    nt = (((1,), (1,)), ((), ()))
    n_heads = q_ref.shape[0]

    def scores(h):
        return [lax.dot_general(q_ref[h], k_ref[h // rep], nt, preferred_element_type=F32) for k_ref in ks]

    def weighted(h, ss):
        m = ss[0].max(axis=-1, keepdims=True)
        for s in ss[1:]:
            m = jnp.maximum(m, s.max(axis=-1, keepdims=True))
        acc = 0.0
        for s, v_ref in zip(ss, vs):
            acc = acc + jnp.dot(jnp.exp2(s - m).astype(BF16), v_ref[h], preferred_element_type=F32)
        return acc / pltpu.roll(acc, HEAD_D, 1)

    outs = []
    ss_next = scores(0)
    for h in range(n_heads):
        ss = ss_next
        if h + 1 < n_heads:
            ss_next = scores(h + 1)
        outs.append(weighted(h, ss))
    for p in range(n_heads // 2):
        o_ref[:, p * LANES:(p + 1) * LANES] = jnp.where(lo, outs[2 * p], outs[2 * p + 1])


def _attention(q, kvs, rep, b, l):
    nh = q.shape[0]
    tq = min(ATTN_TILE, l)
    nq = l // tq
    specs = [pl.BlockSpec((nh, tq, LANES), lambda i, j: (0, i * nq + j, 0))]
    args = [q]
    for k, v, lk in kvs:
        specs += [pl.BlockSpec((k.shape[0], lk, LANES), lambda i, j: (0, i, 0)),
                  pl.BlockSpec((v.shape[0], lk, LANES), lambda i, j: (0, i, 0))]
        args += [k, v]
    return pl.pallas_call(
        functools.partial(_attn_kernel, rep=rep),
        grid=(b, nq),
        in_specs=specs,
        out_specs=pl.BlockSpec((None, tq, GROUP_W), lambda i, j: (i, j, 0)),
        out_shape=jax.ShapeDtypeStruct((b, l, GROUP_W), F32),
        compiler_params=_params("parallel", "parallel"),
        name="attention",
    )(*args)


def _ret_kernel(qs, ks, v_ref, g_ref, dl_ref, dc_ref, ng_ref, seg_ref, sf0_ref, sb0_ref,
                y_ref, sf_ref, sb_ref, sball, dec_scr, cdec_scr, dmat_scr, *, chunk, unroll):
    c_ = chunk
    w = GROUP_W
    n_chunks = qs.shape[0] // c_
    unroll = min(unroll, n_chunks)
    shift_c = int(math.log2(c_))
    shift_h = int(math.log2(HEAD_D))
    tn = (((0,), (0,)), ((), ()))
    nt = (((1,), (1,)), ((), ()))

    @pl.when(pl.program_id(0) == 0)
    def _():
        lgf = _log_sigmoid(dl_ref[0:1, :])
        lgb = _log_sigmoid(dl_ref[1:2, :])
        ii = lax.broadcasted_iota(jnp.int32, (c_, 1), 0).astype(F32)
        dec_scr[0] = jnp.exp((ii + 1.0) * lgf)
        dec_scr[1] = jnp.exp((c_ - 1.0 - ii) * lgf)
        dec_scr[2] = jnp.exp((c_ - ii) * lgb)
        dec_scr[3] = jnp.exp(ii * lgb)
        cdec_scr[0] = jnp.exp(float(c_) * lgf)
        cdec_scr[1] = jnp.exp(float(c_) * lgb)
        lgc = _log_sigmoid(dc_ref[...])
        i4 = jnp.bitwise_and(lax.broadcasted_iota(jnp.int32, (4 * c_, c_), 0), c_ - 1)
        j4 = lax.broadcasted_iota(jnp.int32, (4 * c_, c_), 1)
        diff = (i4 - j4).astype(F32)
        dmat_scr[...] = (jnp.where(diff >= 0, jnp.exp(lgc[:, 0:1] * jnp.maximum(diff, 0.0)), 0.0)
                         + jnp.where(diff <= 0, jnp.exp(lgc[:, 1:2] * jnp.maximum(-diff, 0.0)), 0.0))

    qdf, kdf, qdb, kdb = dec_scr[0], dec_scr[1], dec_scr[2], dec_scr[3]
    cdf, cdb = cdec_scr[0], cdec_scr[1]
    dmat = dmat_scr[...]
    r_h = lax.shift_right_logical(lax.broadcasted_iota(jnp.int32, (w, w), 0), shift_h)
    c_h = lax.shift_right_logical(lax.broadcasted_iota(jnp.int32, (w, w), 1), shift_h)
    blockdiag = r_h == c_h
    row_h = lax.shift_right_logical(lax.broadcasted_iota(jnp.int32, (4 * c_, w), 0), shift_c)
    lane_h = lax.shift_right_logical(lax.broadcasted_iota(jnp.int32, (4 * c_, w), 1), shift_h)
    headmask = row_h == lane_h
    seg = seg_ref[...]

    def rows(c):
        return pl.ds(pl.multiple_of(c * c_, c_), c_)

    def kv_outer(k, v, dec):
        a = lax.dot_general((k * dec).astype(BF16), v.astype(BF16), tn, preferred_element_type=F32)
        return jnp.where(blockdiag, a, 0.0)

    def bwd(t, s):
        c = n_chunks - 1 - t
        sball[c] = s
        sl = rows(c)
        return s * cdb + kv_outer(ks[sl, :], v_ref[sl, :], kdb)

    sb_ref[...] = lax.fori_loop(0, n_chunks, bwd, sb0_ref[...], unroll=unroll)

    def fwd(c, s):
        sl = rows(c)
        q, k, v = qs[sl, :], ks[sl, :], v_ref[sl, :]
        vb = v.astype(BF16)
        q4 = jnp.where(headmask, jnp.concatenate([q, q, q, q], axis=0), 0.0).astype(BF16)
        sc = lax.dot_general(q4, k.astype(BF16), nt, preferred_element_type=F32)
        o4 = jnp.dot((sc * dmat).astype(BF16), vb, preferred_element_type=F32)
        o4 = jnp.where(headmask, o4, 0.0)
        o = o4[0:c_] + o4[c_:2 * c_] + o4[2 * c_:3 * c_] + o4[3 * c_:4 * c_]
        o = o + jnp.dot((q * qdf).astype(BF16), s.astype(BF16), preferred_element_type=F32)
        o = o + jnp.dot((q * qdb).astype(BF16), sball[c].astype(BF16), preferred_element_type=F32)
        on = o * lax.rsqrt(_seg_mean(o * o, seg) + EPS) * ng_ref[...]
        g = g_ref[sl, :]
        y_ref[sl, :] = on * (g * jax.nn.sigmoid(g))
        return s * cdf + kv_outer(k, v, kdf)

    sf_ref[...] = lax.fori_loop(0, n_chunks, fwd, sf0_ref[...], unroll=unroll)


def _retention(ret, decay, ng, seg, sf0, sb0):
    b, l, _ = ret.shape
    w = GROUP_W
    chunk = min(RET_CHUNK, l)
    dl = jnp.repeat(decay, HEAD_D, axis=1)
    dc = jnp.repeat(decay.T, chunk, axis=0)
    col = lambda j: pl.BlockSpec((None, l, w), lambda i: (i, 0, j))
    state = pl.BlockSpec((None, w, w), lambda i: (i, 0, 0))
    sshape = jax.ShapeDtypeStruct((b, w, w), F32)
    return pl.pallas_call(
        functools.partial(_ret_kernel, chunk=chunk, unroll=RET_UNROLL),
        grid=(b,),
        in_specs=[col(0), col(1), col(2), col(3), _const_spec(dl), _const_spec(dc), _const_spec(ng),
                  _const_spec(seg), state, state],
        out_specs=[pl.BlockSpec((None, l, w), lambda i: (i, 0, 0)), state, state],
        out_shape=[jax.ShapeDtypeStruct((b, l, w), F32), sshape, sshape],
        scratch_shapes=[pltpu.VMEM((l // chunk, w, w), F32), pltpu.VMEM((4, chunk, w), F32),
                        pltpu.VMEM((2, 1, w), F32), pltpu.VMEM((4 * chunk, chunk), F32)],
        compiler_params=_params("arbitrary"),
        name="retention",
    )(ret, ret, ret, ret, dl, dc, ng, seg, sf0, sb0)


def _mlp_kernel(x_ref, ya_ref, yb_ref, yc_ref, yd_ref, gout_ref, ga_ref, wout_ref,
                gffn_ref, sh_ref, sc_ref, gf_ref, wfi_ref, wfo_ref, fg_ref, o_ref,
                x1_a, h_a, x1_b, h_b, *, final_norm):
    t = pl.program_id(0)
    nff = wfo_ref.shape[0]

    @pl.when(t == 0)
    def _():
        x1_b[...] = jnp.zeros_like(x1_b)
        h_b[...] = jnp.zeros_like(h_b)

    def step(x1_w, h_w, x1_r, h_r):
        acc = 0.0
        for i, y_ref in enumerate((ya_ref, yb_ref, yc_ref, yd_ref)):
            rows = slice(i * GROUP_W, (i + 1) * GROUP_W)
            yn = (_rms(y_ref[...]) * gout_ref[:, rows]).astype(BF16)
            acc = acc + jnp.dot(yn, wout_ref[rows, :], preferred_element_type=F32)
        x1 = x_ref[...] + ga_ref[...] * acc
        x1_w[...] = x1
        h_w[...] = (_rms(x1) * gffn_ref[...] * (1.0 + sc_ref[...]) + sh_ref[...]).astype(BF16)

        h = h_r[...]
        ff = 0.0
        for c in range(nff):
            a = jnp.dot(h, wfi_ref[c], preferred_element_type=F32)
            g = jnp.dot(h, wfi_ref[c + nff], preferred_element_type=F32)
            u = (a * jax.nn.sigmoid(a) * g).astype(BF16)
            ff = ff + jnp.dot(u, wfo_ref[c], preferred_element_type=F32)
        out = x1_r[...] + gf_ref[...] * ff
        if final_norm:
            out = _rms(out) * fg_ref[...]
        o_ref[...] = out

    @pl.when(t % 2 == 0)
    def _():
        step(x1_a, h_a, x1_b, h_b)

    @pl.when(t % 2 == 1)
    def _():
        step(x1_b, h_b, x1_a, h_a)


def _mlp(x, ys, g_out, g_a, w_out, g_ffn, sh, sc, g_f, wfi, wfo, final_g, final_norm):
    b, l, d = x.shape
    tm = min(MLP_TILE, l)
    nt = l // tm
    n = b * nt
    cur = lambda t: jnp.minimum(t, n - 1)
    prv = lambda t: jnp.maximum(t - 1, 0)
    tok = lambda wd: pl.BlockSpec((None, tm, wd), lambda t: (cur(t) // nt, cur(t) % nt, 0))
    vec = pl.BlockSpec((None, 1, d), lambda t: (cur(t) // nt, 0, 0))
    return pl.pallas_call(
        functools.partial(_mlp_kernel, final_norm=final_norm),
        grid=(n + 1,),
        in_specs=[tok(d)] + [tok(GROUP_W)] * 4 + [
            _const_spec(g_out), vec, _const_spec(w_out), _const_spec(g_ffn), vec, vec,
            pl.BlockSpec((None, 1, d), lambda t: (prv(t) // nt, 0, 0)),
            _const_spec(wfi), _const_spec(wfo), _const_spec(final_g)],
        out_specs=pl.BlockSpec((None, tm, d), lambda t: (prv(t) // nt, prv(t) % nt, 0)),
        out_shape=jax.ShapeDtypeStruct((b, l, d), F32),
        scratch_shapes=[pltpu.VMEM((tm, d), F32), pltpu.VMEM((tm, d), BF16),
                        pltpu.VMEM((tm, d), F32), pltpu.VMEM((tm, d), BF16)],
        compiler_params=_params("arbitrary"),
        name="mlp",
    )(x, *ys, g_out, g_a, w_out, g_ffn, sh, sc, g_f, wfi, wfo, final_g)


def _pad_proj(w):
    end = PROJ_OFFSETS[2] + MLA_KV_RANK + MLA_ROPE
    z = jnp.zeros((w.shape[0], LANES - MLA_ROPE), w.dtype)
    return jnp.concatenate([w[:, :end], z, w[:, end:]], axis=1).astype(BF16)


def _mla_weights(wuq, wukv):
    r = wuq.shape[0]
    q = wuq.reshape(r, 4, MLA_NOPE + MLA_ROPE)
    q = jnp.pad(q, ((0, 0), (0, 0), (0, LANES - MLA_NOPE - MLA_ROPE))).reshape(r, 4 * LANES)
    kv = wukv.reshape(wukv.shape[0], 4, MLA_NOPE + MLA_V)
    k = jnp.pad(kv[..., :MLA_NOPE], ((0, 0), (0, 0), (0, LANES - MLA_NOPE))).reshape(-1, 4 * LANES)
    v = jnp.concatenate([kv[..., MLA_NOPE:], kv[..., MLA_NOPE:]], axis=-1).reshape(-1, 4 * LANES)
    return q.astype(BF16), k.astype(BF16), v.astype(BF16)


def _ffn_weights(w_in, w_out):
    d = w_in.shape[0]
    nff = D_FF // FF_TILE
    wfi = w_in.astype(BF16).reshape(d, 2 * nff, FF_TILE).transpose(1, 0, 2)
    return wfi, w_out.astype(BF16).reshape(nff, FF_TILE, d)


def _seg_matrix(width):
    i = jnp.arange(width) // HEAD_D
    return jnp.where(i[:, None] == i[None, :], 1.0 / HEAD_D, 0.0).astype(BF16)


def kernel(x, c, ctx, c_ctx, w_mod, b_mod, norm_attn_g, norm_ffn_g, w_in, hy_conv_w, hy_conv_b, hy_w1, hy_b1, hy_w2, hy_b2, hy_w3, hy_b3, hy_freq, hy_w4, hy_bias, mla_q_norm_g, mla_w_uq, mla_kv_norm_g, mla_w_ukv, ret_decay, ret_norm_g, gqa_q_norm_g, gqa_k_norm_g, out_norm_g, w_out, w_ffn_in, w_ffn_out, final_norm_g):
    b, n_lat, d = x.shape
    n_ctx = ctx.shape[1]
    depth = w_mod.shape[0]

    rope_mla = _rope_tables(n_lat, MLA_ROPE, LANES, MLA_NOPE, 1)
    rope_hd2 = _rope_tables(n_lat, HEAD_D, HEAD_D, 0, 2)
    seg2, seg4 = _seg_matrix(LANES), _seg_matrix(GROUP_W)
    hy_consts = {n: (_hy_features(n),) + _fft_tables(n) for n in {n_lat, n_ctx}}
    zero_state = jnp.zeros((b, GROUP_W, GROUP_W), F32)

    rows = -(-(b + 1) // 8) * 8
    cond = jnp.zeros((rows, d), F32).at[:b].set(c).at[b].set(c_ctx)
    mod = _ada_mod(cond, w_mod, b_mod)

    xl, xc = x, ctx.reshape(1, b * n_ctx, d)
    for l in range(depth):
        update_ctx = l < depth - 1
        m_lat = [m[:b, None, :] for m in jnp.split(mod[l], 6, axis=-1)]
        m_ctx = [m[b][None, None, :] for m in jnp.split(mod[l], 6, axis=-1)]
        wp = _pad_proj(w_in[l])
        g_attn = norm_attn_g[l][None]
        mla_w = (mla_q_norm_g[l][None], mla_kv_norm_g[l][None]) + _mla_weights(mla_w_uq[l], mla_w_ukv[l])
        gqa_w = (jnp.tile(gqa_q_norm_g[l], 2)[None], jnp.tile(gqa_k_norm_g[l], 2)[None], seg2)
        hy_l, ret_l, q_l, k_l, v_l, gq_l, gk_l, gv_l = _front(
            xl, g_attn, m_lat[0], m_lat[1], wp, mla_w, gqa_w, (rope_mla, rope_hd2))
        hy_c, ret_c, q_c, k_c, v_c, gq_c, gk_c, gv_c = _front(
            xc, g_attn, m_ctx[0], m_ctx[1], wp, mla_w, gqa_w, None)
        hy_c = hy_c.reshape(b, n_ctx, -1)
        ret_c = ret_c.reshape(b, n_ctx, -1)

        hy_args = (hy_conv_w[l], hy_conv_b[l], hy_w1[l], hy_b1[l], hy_w2[l], hy_b2[l], hy_w3[l], hy_b3[l],
                   hy_freq[l], hy_w4[l], hy_bias[l])
        ya_l = _hyena(hy_l, *hy_consts[n_lat], *hy_args)
        yb_l = _attention(q_l, [(k_c, v_c, n_ctx), (k_l, v_l, n_lat)], 1, b, n_lat)

        ng = ret_norm_g[l][None]
        yc_c, s_f, s_b = _retention(ret_c, ret_decay[l], ng, seg4, zero_state, zero_state)
        yc_l, _, _ = _retention(ret_l, ret_decay[l], ng, seg4, s_f, s_b)
        yd_l = _attention(gq_l, [(gk_c, gv_c, n_ctx), (gk_l, gv_l, n_lat)], 2, b, n_lat)

        wo = w_out[l].astype(BF16)
        wfi, wfo = _ffn_weights(w_ffn_in[l], w_ffn_out[l])
        g_out, g_ffn, g_fin = out_norm_g[l][None], norm_ffn_g[l][None], final_norm_g[None]
        xl = _mlp(xl, (ya_l, yb_l, yc_l, yd_l), g_out, m_lat[2], wo, g_ffn, m_lat[3], m_lat[4], m_lat[5],
                  wfi, wfo, g_fin, l == depth - 1)

        if update_ctx:
            flat = lambda a: a.reshape(1, b * n_ctx, GROUP_W)
            ya_c = _hyena(hy_c, *hy_consts[n_ctx], *hy_args)
            yb_c = _attention(q_c, [(k_c, v_c, n_ctx)], 1, b, n_ctx)
            yd_c = _attention(gq_c, [(gk_c, gv_c, n_ctx)], 2, b, n_ctx)
            xc = _mlp(xc, tuple(flat(a) for a in (ya_c, yb_c, yc_c, yd_c)), g_out, m_ctx[2], wo,
                      g_ffn, m_ctx[3], m_ctx[4], m_ctx[5], wfi, wfo, g_fin, False)

    return xl
```

```python
import functools
import math

import numpy as np
import jax
import jax.numpy as jnp
from jax import lax
from jax.experimental import pallas as pl
from jax.experimental.pallas import tpu as pltpu

F32 = jnp.float32
BF16 = jnp.bfloat16
HI = lax.Precision.HIGHEST

D_MODEL = 1024
GRID_W = 64
EPS = 1e-6
ROPE_THETA = 10000.0
GROUP_W = 256
LANES = 128

HY_EMB = 33
HY_BANDS = 16
HY_FFN = 64
HY_NFILT = 1024
HY_DECAY_SHIFT = 0.05
HY_FAST_PCT = 0.3
HY_SLOW_PCT = 1.5
HY_TARGET = 1e-2
FFT_RADIX = 16
FFT_BINS = FFT_RADIX // 2 + 1
FFT_PIECE_ROWS = 32

MLA_NOPE = 64
MLA_ROPE = 32
MLA_V = 64
MLA_KV_RANK = 128
LOG2E = math.log2(math.e)
MLA_SCALE = (MLA_NOPE + MLA_ROPE) ** -0.5 * LOG2E

HEAD_D = 64
RET_CHUNK = 256
RET_UNROLL = 8
GQA_SCALE = HEAD_D ** -0.5 * LOG2E
RET_K_SCALE = HEAD_D ** -0.5

D_FF = 2816
FF_TILE = 256

PROJ_SPLITS = (768, 256, 256, 1024, 512)
PROJ_OFFSETS = (0, 768, 1024, 1280, 2304, 2816)

V7X_VMEM_BYTES = 64 * 1024 * 1024
VMEM_LIMIT = V7X_VMEM_BYTES - 8 * 1024 * 1024

FRONT_TILE = 1024
ATTN_TILE = 1024
MLP_TILE = 512


def _params(*sem):
    return pltpu.CompilerParams(dimension_semantics=sem, vmem_limit_bytes=VMEM_LIMIT)


def _const_spec(a):
    nd = a.ndim
    return pl.BlockSpec(a.shape, lambda *_: (0,) * nd, pipeline_mode=pl.Buffered(1))


def _rms(x):
    return x * lax.rsqrt(jnp.mean(x * x, axis=-1, keepdims=True) + EPS)


def _rope(x, cf, s1, s2, half):
    w = x.shape[-1]
    return x * cf + pltpu.roll(x, w - half, 1) * s1 + pltpu.roll(x, half, 1) * s2


def _seg_mean(sq, seg):
    hi = sq.astype(BF16)
    lo = (sq - hi.astype(F32)).astype(BF16)
    return (jnp.dot(hi, seg, preferred_element_type=F32) + jnp.dot(lo, seg, preferred_element_type=F32))


def _log_sigmoid(x):
    return -(jnp.maximum(-x, 0.0) + jnp.log(1.0 + jnp.exp(-jnp.abs(x))))


def _ada_kernel(c_ref, w_ref, b_ref, o_ref):
    c = c_ref[...]
    s = c * jax.nn.sigmoid(c)
    o_ref[...] = jnp.dot(s, w_ref[...], precision=HI, preferred_element_type=F32) + b_ref[...]


def _ada_mod(cond, w_mod, b_mod):
    depth, d, n = w_mod.shape
    r = cond.shape[0]
    tn = 1024
    return pl.pallas_call(
        _ada_kernel,
        grid=(depth, n // tn),
        in_specs=[pl.BlockSpec((r, d), lambda l, j: (0, 0)),
                  pl.BlockSpec((None, d, tn), lambda l, j: (l, 0, j)),
                  pl.BlockSpec((None, 1, tn), lambda l, j: (l, 0, j))],
        out_specs=pl.BlockSpec((None, r, tn), lambda l, j: (l, 0, j)),
        out_shape=jax.ShapeDtypeStruct((depth, r, n), F32),
        compiler_params=_params("arbitrary", "arbitrary"),
        name="ada_mod",
    )(cond, w_mod, b_mod.reshape(depth, 1, n))


def _value_with_ones(v2, head):
    lo = lax.broadcasted_iota(jnp.int32, v2.shape, 1) < HEAD_D
    keep = lo if head % 2 == 0 else jnp.logical_not(lo)
    return jnp.where(keep, v2, 1.0).astype(BF16)


def _mla_heads(cq, ckvr, gq, gkv, wuq_ref, wk_ref, wv_ref, tables, q_ref, k_ref, v_ref):
    half = MLA_ROPE // 2
    qn = (_rms(cq) * gq).astype(BF16)
    q = jnp.dot(qn, wuq_ref[...], preferred_element_type=F32)
    kvn = (_rms(ckvr[:, :MLA_KV_RANK]) * gkv).astype(BF16)
    k = jnp.dot(kvn, wk_ref[...], preferred_element_type=F32)
    v = jnp.dot(kvn, wv_ref[...], preferred_element_type=F32)
    kpe = pltpu.roll(ckvr[:, MLA_KV_RANK:], MLA_NOPE, 1)
    if tables is not None:
        kpe = _rope(kpe, *tables, half)
    for h in range(q_ref.shape[0]):
        cols = slice(h * LANES, (h + 1) * LANES)
        qh = q[:, cols]
        if tables is not None:
            qh = _rope(qh, *tables, half)
        q_ref[h] = (qh * MLA_SCALE).astype(BF16)
        k_ref[h] = (k[:, cols] + kpe).astype(BF16)
        v_ref[h] = _value_with_ones(v[:, cols], h)


def _gqa_heads(x, gq, gk, seg, tables, q_ref, k_ref, v_ref):
    half = HEAD_D // 2
    lo = lax.broadcasted_iota(jnp.int32, (x.shape[0], LANES), 1) < HEAD_D

    def normed(a, g):
        a = a * lax.rsqrt(_seg_mean(a * a, seg) + EPS) * g
        return _rope(a, *tables, half) if tables is not None else a

    def dup(a, g):
        a = jnp.where(lo if g == 0 else jnp.logical_not(lo), a, 0.0)
        return a + pltpu.roll(a, HEAD_D, 1)

    for p in range(2):
        qb = normed(x[:, p * LANES:(p + 1) * LANES], gq) * GQA_SCALE
        q_ref[2 * p] = jnp.where(lo, qb, 0.0).astype(BF16)
        q_ref[2 * p + 1] = jnp.where(lo, 0.0, qb).astype(BF16)
    kb = normed(x[:, 2 * LANES:3 * LANES], gk)
    vb = x[:, 3 * LANES:4 * LANES]
    for g in range(2):
        k_ref[g] = dup(kb, g).astype(BF16)
        for r in range(2):
            v_ref[2 * g + r] = _value_with_ones(dup(vb, g), 2 * g + r)


def _front_kernel(x_ref, g_ref, sh_ref, sc_ref, w_ref, gq_ref, gkv_ref, wuq_ref, wk_ref, wv_ref,
                  ggq_ref, ggk_ref, seg_ref, *rest, rope):
    if rope:
        mla_t = tuple(r[...] for r in rest[0:3])
        hd_t = tuple(r[...] for r in rest[3:6])
        rest = rest[6:]
    else:
        mla_t = hd_t = None
    hy_ref, ret_ref, mq_ref, mk_ref, mv_ref, gq_out, gk_out, gv_out = rest
    y = _rms(x_ref[...]) * g_ref[...]
    h = (y * (1.0 + sc_ref[...]) + sh_ref[...]).astype(BF16)
    o0, o1, o2, o3, o4, o5 = PROJ_OFFSETS

    def proj(a, b):
        return jnp.dot(h, w_ref[:, a:b], preferred_element_type=F32)

    _mla_heads(proj(o1, o2), proj(o2, o3), gq_ref[...], gkv_ref[...], wuq_ref, wk_ref, wv_ref, mla_t,
               mq_ref, mk_ref, mv_ref)
    _gqa_heads(proj(o4, o5), ggq_ref[...], ggk_ref[...], seg_ref[...], hd_t, gq_out, gk_out, gv_out)
    for i in range(2 * GROUP_W // LANES):
        cols = slice(i * LANES, (i + 1) * LANES)
        a = proj(o3 + i * LANES, o3 + (i + 1) * LANES)
        if i * LANES >= GROUP_W:
            a = a * RET_K_SCALE
        ret_ref[:, cols] = _rope(a, *hd_t, HEAD_D // 2) if rope else a
    ret_ref[:, 2 * GROUP_W:] = proj(o3 + 2 * GROUP_W, o4)
    hy_ref[...] = proj(o0, o1)


def _front(x, g, sh, sc, w, mla_w, gqa_w, tables):
    b, l, d = x.shape
    tm = min(FRONT_TILE, l)
    nt = l // tm
    rope = tables is not None
    tok = lambda wd: pl.BlockSpec((None, tm, wd), lambda i, j: (i, j, 0))
    vec = pl.BlockSpec((None, 1, d), lambda i, j: (i, 0, 0))
    head = lambda nh: pl.BlockSpec((nh, tm, LANES), lambda i, j: (0, i * nt + j, 0))
    hshape = lambda nh: jax.ShapeDtypeStruct((nh, b * l, LANES), BF16)
    consts = [w, *mla_w, *gqa_w]
    args = [x, g, sh, sc, *consts]
    specs = [tok(d), _const_spec(g), vec, vec] + [_const_spec(a) for a in consts]
    if rope:
        args += [*tables[0], *tables[1]]
        specs += [pl.BlockSpec((tm, LANES), lambda i, j: (j, 0))] * 6
    return pl.pallas_call(
        functools.partial(_front_kernel, rope=rope),
        grid=(b, nt),
        in_specs=specs,
        out_specs=[tok(PROJ_SPLITS[0]), tok(PROJ_SPLITS[3]), head(4), head(4), head(4), head(4), head(2), head(4)],
        out_shape=[jax.ShapeDtypeStruct((b, l, PROJ_SPLITS[0]), F32),
                   jax.ShapeDtypeStruct((b, l, PROJ_SPLITS[3]), F32),
                   hshape(4), hshape(4), hshape(4), hshape(4), hshape(2), hshape(4)],
        compiler_params=_params("parallel", "parallel"),
        name="front",
    )(*args)


def _hy_filter_kernel(z_ref, t_ref, w1_ref, b1_ref, w2_ref, b2_ref, w3_ref, b3_ref, fr_ref,
                      w4_ref, dl_ref, o_ref, h_scr):
    dot = functools.partial(jnp.dot, precision=HI, preferred_element_type=F32)

    @pl.when(pl.program_id(0) == 0)
    def _():
        fr = fr_ref[...]
        h = jnp.sin(fr * (dot(z_ref[...], w1_ref[...]) + b1_ref[...]))
        h = jnp.sin(fr * (dot(h, w2_ref[...]) + b2_ref[...]))
        h_scr[...] = jnp.sin(fr * (dot(h, w3_ref[...]) + b3_ref[...]))

    filt = dot(h_scr[...], w4_ref[...]) * (jnp.exp(-t_ref[...] * dl_ref[...]) + HY_DECAY_SHIFT)
    fwd = filt[:, :GROUP_W]
    bwd = filt[:, GROUP_W:]
    row = lax.broadcasted_iota(jnp.int32, bwd.shape, 0)
    bwd = jnp.where(row == 0, 0.0, bwd)
    nrm = (jnp.sum(jnp.abs(fwd), axis=0, keepdims=True) + jnp.sum(jnp.abs(bwd), axis=0, keepdims=True))
    o_ref[:, :GROUP_W] = fwd / nrm
    o_ref[:, GROUP_W:] = bwd / nrm


def _hy_filters(l, z, t, w1, b1, w2, b2, w3, b3, fr, w4, dl):
    full = lambda a: pl.BlockSpec(a.shape, lambda o: (0, 0))
    return pl.pallas_call(
        _hy_filter_kernel,
        grid=(2,),
        in_specs=[full(z), full(t), full(w1), full(b1), full(w2), full(b2), full(w3), full(b3), full(fr),
                  pl.BlockSpec((HY_FFN, 2 * GROUP_W), lambda o: (0, o)),
                  pl.BlockSpec((1, 2 * GROUP_W), lambda o: (0, o))],
        out_specs=pl.BlockSpec((l, 2 * GROUP_W), lambda o: (0, o)),
        out_shape=jax.ShapeDtypeStruct((l, HY_NFILT), F32),
        scratch_shapes=[pltpu.VMEM((l, HY_FFN), F32)],
        compiler_params=_params("arbitrary"),
        name="hy_filters",
    )(z, t, w1, b1, w2, b2, w3, b3, fr, w4, dl)


def _split_bf16(x):
    hi = x.astype(BF16)
    return hi, (x - hi.astype(F32)).astype(BF16)


def _dot_3pass(a_hi, a_lo, x):
    x_hi, x_lo = _split_bf16(x)
    dot = functools.partial(jnp.dot, preferred_element_type=F32)
    return dot(a_hi, x_hi) + (dot(a_hi, x_lo) + dot(a_lo, x_hi))


def _hy_spec_kernel(ch_ref, cl_ref, sh_ref, sl_ref, h_ref, re_ref, im_ref):
    for o in range(2):
        hf = h_ref[:, 2 * o * GROUP_W:(2 * o + 1) * GROUP_W]
        hb = h_ref[:, (2 * o + 1) * GROUP_W:(2 * o + 2) * GROUP_W]
        cols = slice(o * GROUP_W, (o + 1) * GROUP_W)
        re_ref[:, cols] = _dot_3pass(ch_ref[...], cl_ref[...], hf + hb)
        im_ref[:, cols] = -_dot_3pass(sh_ref[...], sl_ref[...], hf - hb)


def _hy_spectrum(h, tabs):
    rows, l = tabs[0].shape
    fb = rows // FFT_BINS if rows // FFT_BINS >= 256 else rows
    out = jax.ShapeDtypeStruct((rows, 2 * GROUP_W), F32)
    blk = pl.BlockSpec((fb, 2 * GROUP_W), lambda f: (f, 0))
    return pl.pallas_call(
        _hy_spec_kernel,
        grid=(rows // fb,),
        in_specs=[pl.BlockSpec((fb, l), lambda f: (f, 0))] * 4 + [_const_spec(h)],
        out_specs=[blk, blk],
        out_shape=[out, out],
        compiler_params=_params("arbitrary"),
        name="hy_spectrum",
    )(*tabs, h)


def _short_conv_to(dst_ref, u, w, b):
    n = u.shape[0]
    mid = w[1:2] * u + b
    dst_ref[...] = w[0:1] * pltpu.roll(u, 1, 0) + mid + w[2:3] * pltpu.roll(u, n - 1, 0)
    dst_ref[0:1, :] = mid[0:1] + w[2:3] * u[1:2]
    dst_ref[n - 1:n, :] = mid[n - 1:n] + w[0:1] * u[n - 2:n - 1]


_R2 = math.sqrt(0.5)


def _dft8_real4(x0, x1, x2, x3):
    s02, d02 = x0 + x2, x0 - x2
    s13, d13 = x1 + x3, x1 - x3
    p, q = _R2 * d13, _R2 * s13
    re = [s02 + s13, x0 + p, d02, x0 - p, s02 - s13]
    im = [None, -(q + x2), -d13, x2 - q, None]
    return re, im


def _fwd16(a):
    er, ei = _dft8_real4(a[0], a[2], a[4], a[6])
    orr, oi = _dft8_real4(a[1], a[3], a[5], a[7])

    def ext(re, im, k):
        if k <= 4:
            return re[k], im[k]
        return re[8 - k], (None if im[8 - k] is None else -im[8 - k])

    out_r, out_i = [], []
    for k in range(FFT_BINS):
        e_r, e_i = ext(er, ei, k)
        o_r, o_i = ext(orr, oi, k)
        c, s = math.cos(math.pi * k / 8), math.sin(math.pi * k / 8)
        if k == 0:
            out_r.append(e_r + o_r)
            out_i.append(None)
        elif k == 8:
            out_r.append(e_r - o_r)
            out_i.append(None)
        elif k == 4:
            out_r.append(e_r)
            out_i.append(-o_r)
        else:
            out_r.append(e_r + (c * o_r + s * o_i))
            out_i.append(e_i + (c * o_i - s * o_r))
    return out_r, out_i


def _dft8_real4_t(kr, ki):
    a = kr[0] + kr[4]
    b = kr[0] - kr[4]
    x0 = a + kr[2] + kr[1] + kr[3]
    x2 = a - kr[2] - ki[1] + ki[3]
    t1 = _R2 * (kr[1] - kr[3])
    t2 = _R2 * (ki[1] + ki[3])
    return x0, b - ki[2] + t1 - t2, x2, b + ki[2] - t1 - t2


def _inv16(br, bi):
    o_r, o_i = [None] * FFT_BINS, [None] * FFT_BINS
    for k in range(FFT_BINS):
        c, s = math.cos(math.pi * k / 8), math.sin(math.pi * k / 8)
        if k == 0:
            o_r[k] = br[k]
        elif k == 8:
            o_r[k] = -br[k]
        elif k == 4:
            o_r[k] = -bi[k]
        else:
            o_r[k] = c * br[k] - s * bi[k]
            o_i[k] = c * bi[k] + s * br[k]

    def fold(re, im):
        fr = [re[0] + re[8], re[1] + re[7], re[2] + re[6], re[3] + re[5], re[4]]
        fi = [None, im[1] - im[7], im[2] - im[6], im[3] - im[5], None]
        return fr, fi

    xe = _dft8_real4_t(*fold(br, bi))
    xo = _dft8_real4_t(*fold(o_r, o_i))
    return [xe[0], xo[0], xe[1], xo[1], xe[2], xo[2], xe[3], xo[3]]


def _hy_conv_kernel(sig_ref, gate_ref, cw_ref, cb_ref, kr_ref, ki_ref, bias_ref, m_ref, mt_ref,
                    twc_ref, tws_ref, twci_ref, twsi_ref, o_ref, u_a, g_a, v_a, u_b, g_b, v_b, z_scr,
                    *, sig_col, gate_col, rows):
    t = pl.program_id(0)

    @pl.when(t == 0)
    def _():
        u_b[...] = jnp.zeros_like(u_b)
        g_b[...] = jnp.zeros_like(g_b)
        v_b[...] = jnp.zeros_like(v_b)

    refs = (sig_ref, gate_ref, cw_ref, cb_ref, kr_ref, ki_ref, bias_ref, m_ref, mt_ref,
            twc_ref, tws_ref, twci_ref, twsi_ref, o_ref, z_scr)

    @pl.when(t % 2 == 0)
    def _():
        _hy_conv_step(*refs, (u_a, g_a, v_a), (u_b, g_b, v_b), sig_col, gate_col, rows)

    @pl.when(t % 2 == 1)
    def _():
        _hy_conv_step(*refs, (u_b, g_b, v_b), (u_a, g_a, v_a), sig_col, gate_col, rows)


def _hy_conv_step(sig_ref, gate_ref, cw_ref, cb_ref, kr_ref, ki_ref, bias_ref, m_ref, mt_ref,
                  twc_ref, tws_ref, twci_ref, twsi_ref, o_ref, z_scr, cur, prev, sig_col, gate_col, rows):
    u_scr, g_scr, v_cur = cur
    u_prev, g_prev, v_prev = prev
    w = GROUP_W
    l = u_scr.shape[0]
    n2 = l // (FFT_RADIX // 2)
    n_pieces = n2 // rows
    if sig_col is not None:
        _short_conv_to(u_scr, sig_ref[...], cw_ref[:, sig_col * w:(sig_col + 1) * w],
                       cb_ref[:, sig_col * w:(sig_col + 1) * w])
    else:
        u_scr[...] = sig_ref[...]
    _short_conv_to(g_scr, gate_ref[...], cw_ref[:, gate_col * w:(gate_col + 1) * w],
                   cb_ref[:, gate_col * w:(gate_col + 1) * w])

    def piece(base, r0):
        return pl.ds(pl.multiple_of(base + r0, rows), rows)

    def fwd_piece(i, carry):
        r0 = i * rows
        for hh in range(w // LANES):
            lanes = slice(hh * LANES, (hh + 1) * LANES)
            fr, fi = _fwd16([u_scr[piece(n1 * n2, r0), lanes] for n1 in range(FFT_RADIX // 2)])
            for k in range(FFT_BINS):
                if k == 0:
                    zr, zi = fr[0], jnp.zeros_like(fr[0])
                else:
                    c, s = twc_ref[k, piece(0, r0), :], tws_ref[k, piece(0, r0), :]
                    if fi[k] is None:
                        zr, zi = fr[k] * c, -(fr[k] * s)
                    else:
                        zr, zi = fr[k] * c + fi[k] * s, fi[k] * c - fr[k] * s
                z_scr[k, piece(0, r0), lanes] = zr.astype(BF16)
                z_scr[k, piece(n2, r0), lanes] = zi.astype(BF16)
        return carry

    lax.fori_loop(0, n_pieces, fwd_piece, 0)

    def slab(k):
        x = jnp.dot(m_ref[...], z_scr[k], preferred_element_type=F32)
        xr, xi = x[:n2], x[n2:]
        bins = pl.ds(pl.multiple_of(k * n2, n2), n2)
        kr, ki = kr_ref[bins, :], ki_ref[bins, :]
        y = jnp.concatenate([xr * kr - xi * ki, xr * ki + xi * kr], axis=0).astype(BF16)
        v_cur[k] = jnp.dot(mt_ref[...], y, preferred_element_type=F32)

    def inv_piece(i, carry):
        slab(i)
        r0 = i * rows
        for hh in range(w // LANES):
            lanes = slice(hh * LANES, (hh + 1) * LANES)
            br, bi = [], []
            for k in range(FFT_BINS):
                vr, vi = v_prev[k, piece(0, r0), lanes], v_prev[k, piece(n2, r0), lanes]
                if k == 0:
                    br.append(vr * (0.5 / l))
                    bi.append(None)
                else:
                    c, s = twci_ref[k, piece(0, r0), :], twsi_ref[k, piece(0, r0), :]
                    br.append(vr * c - vi * s)
                    bi.append(None if k == FFT_BINS - 1 else vr * s + vi * c)
            y = _inv16(br, bi)
            for n1 in range(FFT_RADIX // 2):
                rws = piece(n1 * n2, r0)
                o_ref[rws, lanes] = g_prev[rws, lanes] * (y[n1] + u_prev[rws, lanes] * bias_ref[:, lanes])
        return carry

    for i in range(n_pieces):
        inv_piece(i, 0)
    for k in range(n_pieces, FFT_BINS):
        slab(k)


def _hy_conv(sig, sig_col, hy, gate_col, cw, cb, spec, order, bias, tabs):
    b, l, _ = hy.shape
    w = GROUP_W
    n2 = 2 * l // FFT_RADIX
    rows = max(FFT_PIECE_ROWS, n2 // (FFT_BINS - 1))
    rows = min(rows, n2)
    kr, ki = spec
    sig_arr = hy if sig_col is not None else sig
    sig_blk = sig_col if sig_col is not None else 0
    seq = lambda col: pl.BlockSpec((None, l, w), lambda t: (jnp.minimum(t, b - 1), 0, col))
    spec_blk = pl.BlockSpec((FFT_BINS * n2, w), lambda t: (0, order), pipeline_mode=pl.Buffered(1))
    kernel = functools.partial(_hy_conv_kernel, sig_col=sig_col, gate_col=gate_col, rows=rows)
    stage = [pltpu.VMEM((l, w), F32), pltpu.VMEM((l, w), F32), pltpu.VMEM((FFT_BINS, 2 * n2, w), F32)]
    return pl.pallas_call(
        kernel,
        grid=(b + 1,),
        in_specs=[seq(sig_blk), seq(gate_col), _const_spec(cw), _const_spec(cb), spec_blk, spec_blk,
                  _const_spec(bias)] + [_const_spec(t) for t in tabs],
        out_specs=pl.BlockSpec((None, l, w), lambda t: (jnp.maximum(t - 1, 0), 0, 0)),
        out_shape=jax.ShapeDtypeStruct((b, l, w), F32),
        scratch_shapes=stage + stage + [pltpu.VMEM((FFT_BINS, 2 * n2, w), BF16)],
        compiler_params=_params("arbitrary"),
        name="hy_conv",
    )(sig_arr, hy, cw, cb, kr, ki, bias, *tabs)


@functools.lru_cache(maxsize=None)
def _fft_tables(l):
    n = 2 * l
    n2 = n // FFT_RADIX
    j = np.arange(n2, dtype=np.int64)
    k1 = np.arange(FFT_BINS, dtype=np.int64)
    f32 = lambda a: np.ascontiguousarray(a, dtype=np.float32)
    b16 = lambda a: f32(a).astype(BF16)
    ang2 = ((j[:, None] * j[None, :]) % n2).astype(np.float64) * (2.0 * math.pi / n2)
    c2, s2 = np.cos(ang2), np.sin(ang2)
    m = np.block([[c2, s2], [-s2, c2]])
    angt = (k1[:, None] * j[None, :]).astype(np.float64) * (2.0 * math.pi / n)
    wk = np.where((k1 == 0) | (k1 == FFT_BINS - 1), 1.0, 2.0)[:, None] / n
    lane = lambda a: f32(np.broadcast_to(a[:, :, None], a.shape + (LANES,)))
    bins = (k1[:, None] + FFT_RADIX * j[None, :]).reshape(-1)
    angf = ((bins[:, None] * np.arange(l, dtype=np.int64)[None, :]) % n).astype(np.float64) * (2.0 * math.pi / n)
    conv_tabs = (b16(m), b16(m.T), lane(np.cos(angt)), lane(np.sin(angt)),
                 lane(np.cos(angt) * wk), lane(np.sin(angt) * wk))
    def split(a):
        hi = b16(a)
        return hi, b16(a - hi.astype(np.float64))

    return split(np.cos(angf)) + split(np.sin(angf)), conv_tabs


@functools.lru_cache(maxsize=None)
def _hy_features(l):
    f = np.float32
    t = np.linspace(0.0, 1.0, l, dtype=f)[:, None]
    wpos = (f(2.0 * math.pi) * np.arange(l, dtype=f)[:, None] / f(l)).astype(f)
    fb = np.linspace(1e-4, HY_BANDS - 1, HY_BANDS, dtype=f)[None, :]
    arg = (fb * wpos).astype(f).astype(np.float64)
    z = np.concatenate([t, np.cos(arg), -np.sin(arg), np.zeros((l, HY_FFN - HY_EMB))], axis=-1).astype(f)
    deltas = np.linspace(math.log(HY_TARGET) / HY_FAST_PCT, math.log(HY_TARGET) / HY_SLOW_PCT, HY_NFILT, dtype=f)
    return z, t, np.abs(deltas)[None, :]


def _hyena(hy, feats, spec_tabs, conv_tabs, cw, cb, w1, b1, w2, b2, w3, b3, fr, w4, bias):
    l = hy.shape[1]
    z, t, dl = feats
    w1p = jnp.pad(w1, ((0, HY_FFN - HY_EMB), (0, 0)))
    h = _hy_filters(l, z, t, w1p, b1[None], w2, b2[None], w3, b3[None], fr[None], w4, dl)
    spec = _hy_spectrum(h, spec_tabs)
    cb2 = cb[None]
    zz = _hy_conv(None, 0, hy, 1, cw, cb2, spec, 0, bias[0:1], conv_tabs)
    return _hy_conv(zz, None, hy, 2, cw, cb2, spec, 1, bias[1:2], conv_tabs)


@functools.lru_cache(maxsize=None)
def _rope_tables(l, dim, width, off, reps):
    f = np.float32
    rows = np.repeat(np.arange(l // GRID_W), GRID_W).astype(f)
    cols = np.tile(np.arange(GRID_W), l // GRID_W).astype(f)
    quarter = dim // 4
    half = dim // 2
    inv = (f(ROPE_THETA) ** (-np.arange(quarter, dtype=f) / f(quarter))).astype(f)
    ang = np.concatenate([rows[:, None] * inv, cols[:, None] * inv], axis=-1).astype(f).astype(np.float64)
    c, s = np.cos(ang), np.sin(ang)
    cf = np.ones((l, width))
    s1 = np.zeros((l, width))
    s2 = np.zeros((l, width))
    cf[:, off:off + dim] = np.concatenate([c, c], axis=-1)
    s1[:, off:off + half] = -s
    s2[:, off + half:off + dim] = s
    return tuple(np.tile(a, (1, reps)).astype(f) for a in (cf, s1, s2))


def _attn_kernel(q_ref, *rest, rep):
    o_ref = rest[-1]
    kv = rest[:-1]
    ks, vs = kv[0::2], kv[1::2]
    tq = q_ref.shape[1]
    lo = lax.broadcasted_iota(jnp.int32, (tq, LANES), 1) < HEAD_D
    nt = (((1,), (1,)), ((), ()))
    n_heads = q_ref.shape[0]

    def scores(h):
        return [lax.dot_general(q_ref[h], k_ref[h // rep], nt, preferred_element_type=F32) for k_ref in ks]

    def weighted(h, ss):
        m = ss[0].max(axis=-1, keepdims=True)
        for s in ss[1:]:
            m = jnp.maximum(m, s.max(axis=-1, keepdims=True))
        acc = 0.0
        for s, v_ref in zip(ss, vs):
            acc = acc + jnp.dot(jnp.exp2(s - m).astype(BF16), v_ref[h], preferred_element_type=F32)
        return acc / pltpu.roll(acc, HEAD_D, 1)

    outs = []
    ss_next = scores(0)
    for h in range(n_heads):
        ss = ss_next
        if h + 1 < n_heads:
            ss_next = scores(h + 1)
        outs.append(weighted(h, ss))
    for p in range(n_heads // 2):
        o_ref[:, p * LANES:(p + 1) * LANES] = jnp.where(lo, outs[2 * p], outs[2 * p + 1])


def _attention(q, kvs, rep, b, l):
    nh = q.shape[0]
    tq = min(ATTN_TILE, l)
    nq = l // tq
    specs = [pl.BlockSpec((nh, tq, LANES), lambda i, j: (0, i * nq + j, 0))]
    args = [q]
    for k, v, lk in kvs:
        specs += [pl.BlockSpec((k.shape[0], lk, LANES), lambda i, j: (0, i, 0)),
                  pl.BlockSpec((v.shape[0], lk, LANES), lambda i, j: (0, i, 0))]
        args += [k, v]
    return pl.pallas_call(
        functools.partial(_attn_kernel, rep=rep),
        grid=(b, nq),
        in_specs=specs,
        out_specs=pl.BlockSpec((None, tq, GROUP_W), lambda i, j: (i, j, 0)),
        out_shape=jax.ShapeDtypeStruct((b, l, GROUP_W), F32),
        compiler_params=_params("parallel", "parallel"),
        name="attention",
    )(*args)


def _ret_kernel(qs, ks, v_ref, g_ref, dl_ref, dc_ref, ng_ref, seg_ref, sf0_ref, sb0_ref,
                y_ref, sf_ref, sb_ref, sball, dec_scr, cdec_scr, dmat_scr, *, chunk, unroll):
    c_ = chunk
    w = GROUP_W
    n_chunks = qs.shape[0] // c_
    unroll = min(unroll, n_chunks)
    shift_c = int(math.log2(c_))
    shift_h = int(math.log2(HEAD_D))
    tn = (((0,), (0,)), ((), ()))
    nt = (((1,), (1,)), ((), ()))

    @pl.when(pl.program_id(0) == 0)
    def _():
        lgf = _log_sigmoid(dl_ref[0:1, :])
        lgb = _log_sigmoid(dl_ref[1:2, :])
        ii = lax.broadcasted_iota(jnp.int32, (c_, 1), 0).astype(F32)
        dec_scr[0] = jnp.exp((ii + 1.0) * lgf)
        dec_scr[1] = jnp.exp((c_ - 1.0 - ii) * lgf)
        dec_scr[2] = jnp.exp((c_ - ii) * lgb)
        dec_scr[3] = jnp.exp(ii * lgb)
        cdec_scr[0] = jnp.exp(float(c_) * lgf)
        cdec_scr[1] = jnp.exp(float(c_) * lgb)
        lgc = _log_sigmoid(dc_ref[...])
        i4 = jnp.bitwise_and(lax.broadcasted_iota(jnp.int32, (4 * c_, c_), 0), c_ - 1)
        j4 = lax.broadcasted_iota(jnp.int32, (4 * c_, c_), 1)
        diff = (i4 - j4).astype(F32)
        dmat_scr[...] = (jnp.where(diff >= 0, jnp.exp(lgc[:, 0:1] * jnp.maximum(diff, 0.0)), 0.0)
                         + jnp.where(diff <= 0, jnp.exp(lgc[:, 1:2] * jnp.maximum(-diff, 0.0)), 0.0))

    qdf, kdf, qdb, kdb = dec_scr[0], dec_scr[1], dec_scr[2], dec_scr[3]
    cdf, cdb = cdec_scr[0], cdec_scr[1]
    dmat = dmat_scr[...]
    r_h = lax.shift_right_logical(lax.broadcasted_iota(jnp.int32, (w, w), 0), shift_h)
    c_h = lax.shift_right_logical(lax.broadcasted_iota(jnp.int32, (w, w), 1), shift_h)
    blockdiag = r_h == c_h
    row_h = lax.shift_right_logical(lax.broadcasted_iota(jnp.int32, (4 * c_, w), 0), shift_c)
    lane_h = lax.shift_right_logical(lax.broadcasted_iota(jnp.int32, (4 * c_, w), 1), shift_h)
    headmask = row_h == lane_h
    seg = seg_ref[...]

    def rows(c):
        return pl.ds(pl.multiple_of(c * c_, c_), c_)

    def kv_outer(k, v, dec):
        a = lax.dot_general((k * dec).astype(BF16), v.astype(BF16), tn, preferred_element_type=F32)
        return jnp.where(blockdiag, a, 0.0)

    def bwd(t, s):
        c = n_chunks - 1 - t
        sball[c] = s
        sl = rows(c)
        return s * cdb + kv_outer(ks[sl, :], v_ref[sl, :], kdb)

    sb_ref[...] = lax.fori_loop(0, n_chunks, bwd, sb0_ref[...], unroll=unroll)

    def fwd(c, s):
        sl = rows(c)
        q, k, v = qs[sl, :], ks[sl, :], v_ref[sl, :]
        vb = v.astype(BF16)
        q4 = jnp.where(headmask, jnp.concatenate([q, q, q, q], axis=0), 0.0).astype(BF16)
        sc = lax.dot_general(q4, k.astype(BF16), nt, preferred_element_type=F32)
        o4 = jnp.dot((sc * dmat).astype(BF16), vb, preferred_element_type=F32)
        o4 = jnp.where(headmask, o4, 0.0)
        o = o4[0:c_] + o4[c_:2 * c_] + o4[2 * c_:3 * c_] + o4[3 * c_:4 * c_]
        o = o + jnp.dot((q * qdf).astype(BF16), s.astype(BF16), preferred_element_type=F32)
        o = o + jnp.dot((q * qdb).astype(BF16), sball[c].astype(BF16), preferred_element_type=F32)
        on = o * lax.rsqrt(_seg_mean(o * o, seg) + EPS) * ng_ref[...]
        g = g_ref[sl, :]
        y_ref[sl, :] = on * (g * jax.nn.sigmoid(g))
        return s * cdf + kv_outer(k, v, kdf)

    sf_ref[...] = lax.fori_loop(0, n_chunks, fwd, sf0_ref[...], unroll=unroll)


def _retention(ret, decay, ng, seg, sf0, sb0):
    b, l, _ = ret.shape
    w = GROUP_W
    chunk = min(RET_CHUNK, l)
    dl = jnp.repeat(decay, HEAD_D, axis=1)
    dc = jnp.repeat(decay.T, chunk, axis=0)
    col = lambda j: pl.BlockSpec((None, l, w), lambda i: (i, 0, j))
    state = pl.BlockSpec((None, w, w), lambda i: (i, 0, 0))
    sshape = jax.ShapeDtypeStruct((b, w, w), F32)
    return pl.pallas_call(
        functools.partial(_ret_kernel, chunk=chunk, unroll=RET_UNROLL),
        grid=(b,),
        in_specs=[col(0), col(1), col(2), col(3), _const_spec(dl), _const_spec(dc), _const_spec(ng),
                  _const_spec(seg), state, state],
        out_specs=[pl.BlockSpec((None, l, w), lambda i: (i, 0, 0)), state, state],
        out_shape=[jax.ShapeDtypeStruct((b, l, w), F32), sshape, sshape],
        scratch_shapes=[pltpu.VMEM((l // chunk, w, w), F32), pltpu.VMEM((4, chunk, w), F32),
                        pltpu.VMEM((2, 1, w), F32), pltpu.VMEM((4 * chunk, chunk), F32)],
        compiler_params=_params("arbitrary"),
        name="retention",
    )(ret, ret, ret, ret, dl, dc, ng, seg, sf0, sb0)


def _mlp_kernel(x_ref, ya_ref, yb_ref, yc_ref, yd_ref, gout_ref, ga_ref, wout_ref,
                gffn_ref, sh_ref, sc_ref, gf_ref, wfi_ref, wfo_ref, fg_ref, o_ref,
                x1_a, h_a, x1_b, h_b, *, final_norm):
    t = pl.program_id(0)
    nff = wfo_ref.shape[0]

    @pl.when(t == 0)
    def _():
        x1_b[...] = jnp.zeros_like(x1_b)
        h_b[...] = jnp.zeros_like(h_b)

    def step(x1_w, h_w, x1_r, h_r):
        acc = 0.0
        for i, y_ref in enumerate((ya_ref, yb_ref, yc_ref, yd_ref)):
            rows = slice(i * GROUP_W, (i + 1) * GROUP_W)
            yn = (_rms(y_ref[...]) * gout_ref[:, rows]).astype(BF16)
            acc = acc + jnp.dot(yn, wout_ref[rows, :], preferred_element_type=F32)
        x1 = x_ref[...] + ga_ref[...] * acc
        x1_w[...] = x1
        h_w[...] = (_rms(x1) * gffn_ref[...] * (1.0 + sc_ref[...]) + sh_ref[...]).astype(BF16)

        h = h_r[...]
        ff = 0.0
        for c in range(nff):
            a = jnp.dot(h, wfi_ref[c], preferred_element_type=F32)
            g = jnp.dot(h, wfi_ref[c + nff], preferred_element_type=F32)
            u = (a * jax.nn.sigmoid(a) * g).astype(BF16)
            ff = ff + jnp.dot(u, wfo_ref[c], preferred_element_type=F32)
        out = x1_r[...] + gf_ref[...] * ff
        if final_norm:
            out = _rms(out) * fg_ref[...]
        o_ref[...] = out

    @pl.when(t % 2 == 0)
    def _():
        step(x1_a, h_a, x1_b, h_b)

    @pl.when(t % 2 == 1)
    def _():
        step(x1_b, h_b, x1_a, h_a)


def _mlp(x, ys, g_out, g_a, w_out, g_ffn, sh, sc, g_f, wfi, wfo, final_g, final_norm):
    b, l, d = x.shape
    tm = min(MLP_TILE, l)
    nt = l // tm
    n = b * nt
    cur = lambda t: jnp.minimum(t, n - 1)
    prv = lambda t: jnp.maximum(t - 1, 0)
    tok = lambda wd: pl.BlockSpec((None, tm, wd), lambda t: (cur(t) // nt, cur(t) % nt, 0))
    vec = pl.BlockSpec((None, 1, d), lambda t: (cur(t) // nt, 0, 0))
    return pl.pallas_call(
        functools.partial(_mlp_kernel, final_norm=final_norm),
        grid=(n + 1,),
        in_specs=[tok(d)] + [tok(GROUP_W)] * 4 + [
            _const_spec(g_out), vec, _const_spec(w_out), _const_spec(g_ffn), vec, vec,
            pl.BlockSpec((None, 1, d), lambda t: (prv(t) // nt, 0, 0)),
            _const_spec(wfi), _const_spec(wfo), _const_spec(final_g)],
        out_specs=pl.BlockSpec((None, tm, d), lambda t: (prv(t) // nt, prv(t) % nt, 0)),
        out_shape=jax.ShapeDtypeStruct((b, l, d), F32),
        scratch_shapes=[pltpu.VMEM((tm, d), F32), pltpu.VMEM((tm, d), BF16),
                        pltpu.VMEM((tm, d), F32), pltpu.VMEM((tm, d), BF16)],
        compiler_params=_params("arbitrary"),
        name="mlp",
    )(x, *ys, g_out, g_a, w_out, g_ffn, sh, sc, g_f, wfi, wfo, final_g)


def _pad_proj(w):
    end = PROJ_OFFSETS[2] + MLA_KV_RANK + MLA_ROPE
    z = jnp.zeros((w.shape[0], LANES - MLA_ROPE), w.dtype)
    return jnp.concatenate([w[:, :end], z, w[:, end:]], axis=1).astype(BF16)


def _mla_weights(wuq, wukv):
    r = wuq.shape[0]
    q = wuq.reshape(r, 4, MLA_NOPE + MLA_ROPE)
    q = jnp.pad(q, ((0, 0), (0, 0), (0, LANES - MLA_NOPE - MLA_ROPE))).reshape(r, 4 * LANES)
    kv = wukv.reshape(wukv.shape[0], 4, MLA_NOPE + MLA_V)
    k = jnp.pad(kv[..., :MLA_NOPE], ((0, 0), (0, 0), (0, LANES - MLA_NOPE))).reshape(-1, 4 * LANES)
    v = jnp.concatenate([kv[..., MLA_NOPE:], kv[..., MLA_NOPE:]], axis=-1).reshape(-1, 4 * LANES)
    return q.astype(BF16), k.astype(BF16), v.astype(BF16)


def _ffn_weights(w_in, w_out):
    d = w_in.shape[0]
    nff = D_FF // FF_TILE
    wfi = w_in.astype(BF16).reshape(d, 2 * nff, FF_TILE).transpose(1, 0, 2)
    return wfi, w_out.astype(BF16).reshape(nff, FF_TILE, d)


def _seg_matrix(width):
    i = jnp.arange(width) // HEAD_D
    return jnp.where(i[:, None] == i[None, :], 1.0 / HEAD_D, 0.0).astype(BF16)


def kernel(x, c, ctx, c_ctx, w_mod, b_mod, norm_attn_g, norm_ffn_g, w_in, hy_conv_w, hy_conv_b, hy_w1, hy_b1, hy_w2, hy_b2, hy_w3, hy_b3, hy_freq, hy_w4, hy_bias, mla_q_norm_g, mla_w_uq, mla_kv_norm_g, mla_w_ukv, ret_decay, ret_norm_g, gqa_q_norm_g, gqa_k_norm_g, out_norm_g, w_out, w_ffn_in, w_ffn_out, final_norm_g):
    b, n_lat, d = x.shape
    n_ctx = ctx.shape[1]
    depth = w_mod.shape[0]

    rope_mla = _rope_tables(n_lat, MLA_ROPE, LANES, MLA_NOPE, 1)
    rope_hd2 = _rope_tables(n_lat, HEAD_D, HEAD_D, 0, 2)
    seg2, seg4 = _seg_matrix(LANES), _seg_matrix(GROUP_W)
    hy_consts = {n: (_hy_features(n),) + _fft_tables(n) for n in {n_lat, n_ctx}}
    zero_state = jnp.zeros((b, GROUP_W, GROUP_W), F32)

    rows = -(-(b + 1) // 8) * 8
    cond = jnp.zeros((rows, d), F32).at[:b].set(c).at[b].set(c_ctx)
    mod = _ada_mod(cond, w_mod, b_mod)

    xl, xc = x, ctx.reshape(1, b * n_ctx, d)
    for l in range(depth):
        update_ctx = l < depth - 1
        m_lat = [m[:b, None, :] for m in jnp.split(mod[l], 6, axis=-1)]
        m_ctx = [m[b][None, None, :] for m in jnp.split(mod[l], 6, axis=-1)]
        wp = _pad_proj(w_in[l])
        g_attn = norm_attn_g[l][None]
        mla_w = (mla_q_norm_g[l][None], mla_kv_norm_g[l][None]) + _mla_weights(mla_w_uq[l], mla_w_ukv[l])
        gqa_w = (jnp.tile(gqa_q_norm_g[l], 2)[None], jnp.tile(gqa_k_norm_g[l], 2)[None], seg2)
        hy_l, ret_l, q_l, k_l, v_l, gq_l, gk_l, gv_l = _front(
            xl, g_attn, m_lat[0], m_lat[1], wp, mla_w, gqa_w, (rope_mla, rope_hd2))
        hy_c, ret_c, q_c, k_c, v_c, gq_c, gk_c, gv_c = _front(
            xc, g_attn, m_ctx[0], m_ctx[1], wp, mla_w, gqa_w, None)
        hy_c = hy_c.reshape(b, n_ctx, -1)
        ret_c = ret_c.reshape(b, n_ctx, -1)

        hy_args = (hy_conv_w[l], hy_conv_b[l], hy_w1[l], hy_b1[l], hy_w2[l], hy_b2[l], hy_w3[l], hy_b3[l],
                   hy_freq[l], hy_w4[l], hy_bias[l])
        ya_l = _hyena(hy_l, *hy_consts[n_lat], *hy_args)
        yb_l = _attention(q_l, [(k_c, v_c, n_ctx), (k_l, v_l, n_lat)], 1, b, n_lat)

        ng = ret_norm_g[l][None]
        yc_c, s_f, s_b = _retention(ret_c, ret_decay[l], ng, seg4, zero_state, zero_state)
        yc_l, _, _ = _retention(ret_l, ret_decay[l], ng, seg4, s_f, s_b)
        yd_l = _attention(gq_l, [(gk_c, gv_c, n_ctx), (gk_l, gv_l, n_lat)], 2, b, n_lat)

        wo = w_out[l].astype(BF16)
        wfi, wfo = _ffn_weights(w_ffn_in[l], w_ffn_out[l])
        g_out, g_ffn, g_fin = out_norm_g[l][None], norm_ffn_g[l][None], final_norm_g[None]
        xl = _mlp(xl, (ya_l, yb_l, yc_l, yd_l), g_out, m_lat[2], wo, g_ffn, m_lat[3], m_lat[4], m_lat[5],
                  wfi, wfo, g_fin, l == depth - 1)

        if update_ctx:
            flat = lambda a: a.reshape(1, b * n_ctx, GROUP_W)
            ya_c = _hyena(hy_c, *hy_consts[n_ctx], *hy_args)
            yb_c = _attention(q_c, [(k_c, v_c, n_ctx)], 1, b, n_ctx)
            yd_c = _attention(gq_c, [(gk_c, gv_c, n_ctx)], 2, b, n_ctx)
            xc = _mlp(xc, tuple(flat(a) for a in (ya_c, yb_c, yc_c, yd_c)), g_out, m_ctx[2], wo,
                      g_ffn, m_ctx[3], m_ctx[4], m_ctx[5], wfi, wfo, g_fin, False)

    return xl
```

```python
import functools
import math

import numpy as np
import jax
import jax.numpy as jnp
from jax import lax
from jax.experimental import pallas as pl
from jax.experimental.pallas import tpu as pltpu

F32 = jnp.float32
BF16 = jnp.bfloat16
HI = lax.Precision.HIGHEST

D_MODEL = 1024
GRID_W = 64
EPS = 1e-6
ROPE_THETA = 10000.0
GROUP_W = 256
LANES = 128

HY_EMB = 33
HY_BANDS = 16
HY_FFN = 64
HY_NFILT = 1024
HY_DECAY_SHIFT = 0.05
HY_FAST_PCT = 0.3
HY_SLOW_PCT = 1.5
HY_TARGET = 1e-2
FFT_RADIX = 16
FFT_BINS = FFT_RADIX // 2 + 1
FFT_PIECE_ROWS = 32

MLA_NOPE = 64
MLA_ROPE = 32
MLA_V = 64
MLA_KV_RANK = 128
LOG2E = math.log2(math.e)
MLA_SCALE = (MLA_NOPE + MLA_ROPE) ** -0.5 * LOG2E

HEAD_D = 64
RET_CHUNK = 256
RET_UNROLL = 8
GQA_SCALE = HEAD_D ** -0.5 * LOG2E
RET_K_SCALE = HEAD_D ** -0.5

D_FF = 2816
FF_TILE = 256

PROJ_SPLITS = (768, 256, 256, 1024, 512)
PROJ_OFFSETS = (0, 768, 1024, 1280, 2304, 2816)

V7X_VMEM_BYTES = 64 * 1024 * 1024
VMEM_LIMIT = V7X_VMEM_BYTES - 8 * 1024 * 1024

FRONT_TILE = 1024
ATTN_TILE = 1024
MLP_TILE = 512


def _params(*sem):
    return pltpu.CompilerParams(dimension_semantics=sem, vmem_limit_bytes=VMEM_LIMIT)


def _const_spec(a):
    nd = a.ndim
    return pl.BlockSpec(a.shape, lambda *_: (0,) * nd, pipeline_mode=pl.Buffered(1))


def _rms(x):
    return x * lax.rsqrt(jnp.mean(x * x, axis=-1, keepdims=True) + EPS)


def _rope(x, cf, s1, s2, half):
    w = x.shape[-1]
    return x * cf + pltpu.roll(x, w - half, 1) * s1 + pltpu.roll(x, half, 1) * s2


def _seg_mean(sq, seg):
    hi = sq.astype(BF16)
    lo = (sq - hi.astype(F32)).astype(BF16)
    return (jnp.dot(hi, seg, preferred_element_type=F32) + jnp.dot(lo, seg, preferred_element_type=F32))


def _log_sigmoid(x):
    return -(jnp.maximum(-x, 0.0) + jnp.log(1.0 + jnp.exp(-jnp.abs(x))))


def _ada_kernel(c_ref, w_ref, b_ref, o_ref):
    c = c_ref[...]
    s = c * jax.nn.sigmoid(c)
    o_ref[...] = jnp.dot(s, w_ref[...], precision=HI, preferred_element_type=F32) + b_ref[...]


def _ada_mod(cond, w_mod, b_mod):
    depth, d, n = w_mod.shape
    r = cond.shape[0]
    tn = 1024
    return pl.pallas_call(
        _ada_kernel,
        grid=(depth, n // tn),
        in_specs=[pl.BlockSpec((r, d), lambda l, j: (0, 0)),
                  pl.BlockSpec((None, d, tn), lambda l, j: (l, 0, j)),
                  pl.BlockSpec((None, 1, tn), lambda l, j: (l, 0, j))],
        out_specs=pl.BlockSpec((None, r, tn), lambda l, j: (l, 0, j)),
        out_shape=jax.ShapeDtypeStruct((depth, r, n), F32),
        compiler_params=_params("arbitrary", "arbitrary"),
        name="ada_mod",
    )(cond, w_mod, b_mod.reshape(depth, 1, n))


def _value_with_ones(v2, head):
    lo = lax.broadcasted_iota(jnp.int32, v2.shape, 1) < HEAD_D
    keep = lo if head % 2 == 0 else jnp.logical_not(lo)
    return jnp.where(keep, v2, 1.0).astype(BF16)


def _mla_heads(cq, ckvr, gq, gkv, wuq_ref, wk_ref, wv_ref, tables, q_ref, k_ref, v_ref):
    half = MLA_ROPE // 2
    qn = (_rms(cq) * gq).astype(BF16)
    q = jnp.dot(qn, wuq_ref[...], preferred_element_type=F32)
    kvn = (_rms(ckvr[:, :MLA_KV_RANK]) * gkv).astype(BF16)
    k = jnp.dot(kvn, wk_ref[...], preferred_element_type=F32)
    v = jnp.dot(kvn, wv_ref[...], preferred_element_type=F32)
    kpe = pltpu.roll(ckvr[:, MLA_KV_RANK:], MLA_NOPE, 1)
    if tables is not None:
        kpe = _rope(kpe, *tables, half)
    for h in range(q_ref.shape[0]):
        cols = slice(h * LANES, (h + 1) * LANES)
        qh = q[:, cols]
        if tables is not None:
            qh = _rope(qh, *tables, half)
        q_ref[h] = (qh * MLA_SCALE).astype(BF16)
        k_ref[h] = (k[:, cols] + kpe).astype(BF16)
        v_ref[h] = _value_with_ones(v[:, cols], h)


def _gqa_heads(x, gq, gk, seg, tables, q_ref, k_ref, v_ref):
    half = HEAD_D // 2
    lo = lax.broadcasted_iota(jnp.int32, (x.shape[0], LANES), 1) < HEAD_D

    def normed(a, g):
        a = a * lax.rsqrt(_seg_mean(a * a, seg) + EPS) * g
        return _rope(a, *tables, half) if tables is not None else a

    def dup(a, g):
        a = jnp.where(lo if g == 0 else jnp.logical_not(lo), a, 0.0)
        return a + pltpu.roll(a, HEAD_D, 1)

    for p in range(2):
        qb = normed(x[:, p * LANES:(p + 1) * LANES], gq) * GQA_SCALE
        q_ref[2 * p] = jnp.where(lo, qb, 0.0).astype(BF16)
        q_ref[2 * p + 1] = jnp.where(lo, 0.0, qb).astype(BF16)
    kb = normed(x[:, 2 * LANES:3 * LANES], gk)
    vb = x[:, 3 * LANES:4 * LANES]
    for g in range(2):
        k_ref[g] = dup(kb, g).astype(BF16)
        for r in range(2):
            v_ref[2 * g + r] = _value_with_ones(dup(vb, g), 2 * g + r)


def _front_kernel(x_ref, g_ref, sh_ref, sc_ref, w_ref, gq_ref, gkv_ref, wuq_ref, wk_ref, wv_ref,
                  ggq_ref, ggk_ref, seg_ref, *rest, rope):
    if rope:
        mla_t = tuple(r[...] for r in rest[0:3])
        hd_t = tuple(r[...] for r in rest[3:6])
        rest = rest[6:]
    else:
        mla_t = hd_t = None
    hy_ref, ret_ref, mq_ref, mk_ref, mv_ref, gq_out, gk_out, gv_out = rest
    y = _rms(x_ref[...]) * g_ref[...]
    h = (y * (1.0 + sc_ref[...]) + sh_ref[...]).astype(BF16)
    o0, o1, o2, o3, o4, o5 = PROJ_OFFSETS

    def proj(a, b):
        return jnp.dot(h, w_ref[:, a:b], preferred_element_type=F32)

    _mla_heads(proj(o1, o2), proj(o2, o3), gq_ref[...], gkv_ref[...], wuq_ref, wk_ref, wv_ref, mla_t,
               mq_ref, mk_ref, mv_ref)
    _gqa_heads(proj(o4, o5), ggq_ref[...], ggk_ref[...], seg_ref[...], hd_t, gq_out, gk_out, gv_out)
    for i in range(2 * GROUP_W // LANES):
        cols = slice(i * LANES, (i + 1) * LANES)
        a = proj(o3 + i * LANES, o3 + (i + 1) * LANES)
        if i * LANES >= GROUP_W:
            a = a * RET_K_SCALE
        ret_ref[:, cols] = _rope(a, *hd_t, HEAD_D // 2) if rope else a
    ret_ref[:, 2 * GROUP_W:] = proj(o3 + 2 * GROUP_W, o4)
    hy_ref[...] = proj(o0, o1)


def _front(x, g, sh, sc, w, mla_w, gqa_w, tables):
    b, l, d = x.shape
    tm = min(FRONT_TILE, l)
    nt = l // tm
    rope = tables is not None
    tok = lambda wd: pl.BlockSpec((None, tm, wd), lambda i, j: (i, j, 0))
    vec = pl.BlockSpec((None, 1, d), lambda i, j: (i, 0, 0))
    head = lambda nh: pl.BlockSpec((nh, tm, LANES), lambda i, j: (0, i * nt + j, 0))
    hshape = lambda nh: jax.ShapeDtypeStruct((nh, b * l, LANES), BF16)
    consts = [w, *mla_w, *gqa_w]
    args = [x, g, sh, sc, *consts]
    specs = [tok(d), _const_spec(g), vec, vec] + [_const_spec(a) for a in consts]
    if rope:
        args += [*tables[0], *tables[1]]
        specs += [pl.BlockSpec((tm, LANES), lambda i, j: (j, 0))] * 6
    return pl.pallas_call(
        functools.partial(_front_kernel, rope=rope),
        grid=(b, nt),
        in_specs=specs,
        out_specs=[tok(PROJ_SPLITS[0]), tok(PROJ_SPLITS[3]), head(4), head(4), head(4), head(4), head(2), head(4)],
        out_shape=[jax.ShapeDtypeStruct((b, l, PROJ_SPLITS[0]), F32),
                   jax.ShapeDtypeStruct((b, l, PROJ_SPLITS[3]), F32),
                   hshape(4), hshape(4), hshape(4), hshape(4), hshape(2), hshape(4)],
        compiler_params=_params("parallel", "parallel"),
        name="front",
    )(*args)


def _hy_filter_kernel(z_ref, t_ref, w1_ref, b1_ref, w2_ref, b2_ref, w3_ref, b3_ref, fr_ref,
                      w4_ref, dl_ref, o_ref, h_scr):
    dot = functools.partial(jnp.dot, precision=HI, preferred_element_type=F32)

    @pl.when(pl.program_id(0) == 0)
    def _():
        fr = fr_ref[...]
        h = jnp.sin(fr * (dot(z_ref[...], w1_ref[...]) + b1_ref[...]))
        h = jnp.sin(fr * (dot(h, w2_ref[...]) + b2_ref[...]))
        h_scr[...] = jnp.sin(fr * (dot(h, w3_ref[...]) + b3_ref[...]))

    filt = dot(h_scr[...], w4_ref[...]) * (jnp.exp(-t_ref[...] * dl_ref[...]) + HY_DECAY_SHIFT)
    fwd = filt[:, :GROUP_W]
    bwd = filt[:, GROUP_W:]
    row = lax.broadcasted_iota(jnp.int32, bwd.shape, 0)
    bwd = jnp.where(row == 0, 0.0, bwd)
    nrm = (jnp.sum(jnp.abs(fwd), axis=0, keepdims=True) + jnp.sum(jnp.abs(bwd), axis=0, keepdims=True))
    o_ref[:, :GROUP_W] = fwd / nrm
    o_ref[:, GROUP_W:] = bwd / nrm


def _hy_filters(l, z, t, w1, b1, w2, b2, w3, b3, fr, w4, dl):
    full = lambda a: pl.BlockSpec(a.shape, lambda o: (0, 0))
    return pl.pallas_call(
        _hy_filter_kernel,
        grid=(2,),
        in_specs=[full(z), full(t), full(w1), full(b1), full(w2), full(b2), full(w3), full(b3), full(fr),
                  pl.BlockSpec((HY_FFN, 2 * GROUP_W), lambda o: (0, o)),
                  pl.BlockSpec((1, 2 * GROUP_W), lambda o: (0, o))],
        out_specs=pl.BlockSpec((l, 2 * GROUP_W), lambda o: (0, o)),
        out_shape=jax.ShapeDtypeStruct((l, HY_NFILT), F32),
        scratch_shapes=[pltpu.VMEM((l, HY_FFN), F32)],
        compiler_params=_params("arbitrary"),
        name="hy_filters",
    )(z, t, w1, b1, w2, b2, w3, b3, fr, w4, dl)


def _split_bf16(x):
    hi = x.astype(BF16)
    return hi, (x - hi.astype(F32)).astype(BF16)


def _dot_3pass(a_hi, a_lo, x):
    x_hi, x_lo = _split_bf16(x)
    dot = functools.partial(jnp.dot, preferred_element_type=F32)
    return dot(a_hi, x_hi) + (dot(a_hi, x_lo) + dot(a_lo, x_hi))


def _hy_spec_kernel(ch_ref, cl_ref, sh_ref, sl_ref, h_ref, re_ref, im_ref):
    for o in range(2):
        hf = h_ref[:, 2 * o * GROUP_W:(2 * o + 1) * GROUP_W]
        hb = h_ref[:, (2 * o + 1) * GROUP_W:(2 * o + 2) * GROUP_W]
        cols = slice(o * GROUP_W, (o + 1) * GROUP_W)
        re_ref[:, cols] = _dot_3pass(ch_ref[...], cl_ref[...], hf + hb)
        im_ref[:, cols] = -_dot_3pass(sh_ref[...], sl_ref[...], hf - hb)


def _hy_spectrum(h, tabs):
    rows, l = tabs[0].shape
    fb = rows // FFT_BINS if rows // FFT_BINS >= 256 else rows
    out = jax.ShapeDtypeStruct((rows, 2 * GROUP_W), F32)
    blk = pl.BlockSpec((fb, 2 * GROUP_W), lambda f: (f, 0))
    return pl.pallas_call(
        _hy_spec_kernel,
        grid=(rows // fb,),
        in_specs=[pl.BlockSpec((fb, l), lambda f: (f, 0))] * 4 + [_const_spec(h)],
        out_specs=[blk, blk],
        out_shape=[out, out],
        compiler_params=_params("arbitrary"),
        name="hy_spectrum",
    )(*tabs, h)


def _short_conv_to(dst_ref, u, w, b):
    n = u.shape[0]
    mid = w[1:2] * u + b
    dst_ref[...] = w[0:1] * pltpu.roll(u, 1, 0) + mid + w[2:3] * pltpu.roll(u, n - 1, 0)
    dst_ref[0:1, :] = mid[0:1] + w[2:3] * u[1:2]
    dst_ref[n - 1:n, :] = mid[n - 1:n] + w[0:1] * u[n - 2:n - 1]


_R2 = math.sqrt(0.5)


def _dft8_real4(x0, x1, x2, x3):
    s02, d02 = x0 + x2, x0 - x2
    s13, d13 = x1 + x3, x1 - x3
    p, q = _R2 * d13, _R2 * s13
    re = [s02 + s13, x0 + p, d02, x0 - p, s02 - s13]
    im = [None, -(q + x2), -d13, x2 - q, None]
    return re, im


def _fwd16(a):
    er, ei = _dft8_real4(a[0], a[2], a[4], a[6])
    orr, oi = _dft8_real4(a[1], a[3], a[5], a[7])

    def ext(re, im, k):
        if k <= 4:
            return re[k], im[k]
        return re[8 - k], (None if im[8 - k] is None else -im[8 - k])

    out_r, out_i = [], []
    for k in range(FFT_BINS):
        e_r, e_i = ext(er, ei, k)
        o_r, o_i = ext(orr, oi, k)
        c, s = math.cos(math.pi * k / 8), math.sin(math.pi * k / 8)
        if k == 0:
            out_r.append(e_r + o_r)
            out_i.append(None)
        elif k == 8:
            out_r.append(e_r - o_r)
            out_i.append(None)
        elif k == 4:
            out_r.append(e_r)
            out_i.append(-o_r)
        else:
            out_r.append(e_r + (c * o_r + s * o_i))
            out_i.append(e_i + (c * o_i - s * o_r))
    return out_r, out_i


def _dft8_real4_t(kr, ki):
    a = kr[0] + kr[4]
    b = kr[0] - kr[4]
    x0 = a + kr[2] + kr[1] + kr[3]
    x2 = a - kr[2] - ki[1] + ki[3]
    t1 = _R2 * (kr[1] - kr[3])
    t2 = _R2 * (ki[1] + ki[3])
    return x0, b - ki[2] + t1 - t2, x2, b + ki[2] - t1 - t2


def _inv16(br, bi):
    o_r, o_i = [None] * FFT_BINS, [None] * FFT_BINS
    for k in range(FFT_BINS):
        c, s = math.cos(math.pi * k / 8), math.sin(math.pi * k / 8)
        if k == 0:
            o_r[k] = br[k]
        elif k == 8:
            o_r[k] = -br[k]
        elif k == 4:
            o_r[k] = -bi[k]
        else:
            o_r[k] = c * br[k] - s * bi[k]
            o_i[k] = c * bi[k] + s * br[k]

    def fold(re, im):
        fr = [re[0] + re[8], re[1] + re[7], re[2] + re[6], re[3] + re[5], re[4]]
        fi = [None, im[1] - im[7], im[2] - im[6], im[3] - im[5], None]
        return fr, fi

    xe = _dft8_real4_t(*fold(br, bi))
    xo = _dft8_real4_t(*fold(o_r, o_i))
    return [xe[0], xo[0], xe[1], xo[1], xe[2], xo[2], xe[3], xo[3]]


def _hy_conv_kernel(sig_ref, gate_ref, cw_ref, cb_ref, kr_ref, ki_ref, bias_ref, m_ref, mt_ref,
                    twc_ref, tws_ref, twci_ref, twsi_ref, o_ref, u_a, g_a, v_a, u_b, g_b, v_b, z_scr,
                    *, sig_col, gate_col, rows):
    t = pl.program_id(0)

    @pl.when(t == 0)
    def _():
        u_b[...] = jnp.zeros_like(u_b)
        g_b[...] = jnp.zeros_like(g_b)
        v_b[...] = jnp.zeros_like(v_b)

    refs = (sig_ref, gate_ref, cw_ref, cb_ref, kr_ref, ki_ref, bias_ref, m_ref, mt_ref,
            twc_ref, tws_ref, twci_ref, twsi_ref, o_ref, z_scr)

    @pl.when(t % 2 == 0)
    def _():
        _hy_conv_step(*refs, (u_a, g_a, v_a), (u_b, g_b, v_b), sig_col, gate_col, rows)

    @pl.when(t % 2 == 1)
    def _():
        _hy_conv_step(*refs, (u_b, g_b, v_b), (u_a, g_a, v_a), sig_col, gate_col, rows)


def _hy_conv_step(sig_ref, gate_ref, cw_ref, cb_ref, kr_ref, ki_ref, bias_ref, m_ref, mt_ref,
                  twc_ref, tws_ref, twci_ref, twsi_ref, o_ref, z_scr, cur, prev, sig_col, gate_col, rows):
    u_scr, g_scr, v_cur = cur
    u_prev, g_prev, v_prev = prev
    w = GROUP_W
    l = u_scr.shape[0]
    n2 = l // (FFT_RADIX // 2)
    n_pieces = n2 // rows
    if sig_col is not None:
        _short_conv_to(u_scr, sig_ref[...], cw_ref[:, sig_col * w:(sig_col + 1) * w],
                       cb_ref[:, sig_col * w:(sig_col + 1) * w])
    else:
        u_scr[...] = sig_ref[...]
    _short_conv_to(g_scr, gate_ref[...], cw_ref[:, gate_col * w:(gate_col + 1) * w],
                   cb_ref[:, gate_col * w:(gate_col + 1) * w])

    def piece(base, r0):
        return pl.ds(pl.multiple_of(base + r0, rows), rows)

    def fwd_piece(i, carry):
        r0 = i * rows
        for hh in range(w // LANES):
            lanes = slice(hh * LANES, (hh + 1) * LANES)
            fr, fi = _fwd16([u_scr[piece(n1 * n2, r0), lanes] for n1 in range(FFT_RADIX // 2)])
            for k in range(FFT_BINS):
                if k == 0:
                    zr, zi = fr[0], jnp.zeros_like(fr[0])
                else:
                    c, s = twc_ref[k, piece(0, r0), :], tws_ref[k, piece(0, r0), :]
                    if fi[k] is None:
                        zr, zi = fr[k] * c, -(fr[k] * s)
                    else:
                        zr, zi = fr[k] * c + fi[k] * s, fi[k] * c - fr[k] * s
                z_scr[k, piece(0, r0), lanes] = zr.astype(BF16)
                z_scr[k, piece(n2, r0), lanes] = zi.astype(BF16)
        return carry

    lax.fori_loop(0, n_pieces, fwd_piece, 0)

    def slab(k):
        x = jnp.dot(m_ref[...], z_scr[k], preferred_element_type=F32)
        xr, xi = x[:n2], x[n2:]
        bins = pl.ds(pl.multiple_of(k * n2, n2), n2)
        kr, ki = kr_ref[bins, :], ki_ref[bins, :]
        y = jnp.concatenate([xr * kr - xi * ki, xr * ki + xi * kr], axis=0).astype(BF16)
        v_cur[k] = jnp.dot(mt_ref[...], y, preferred_element_type=F32)

    def inv_piece(i, carry):
        slab(i)
        r0 = i * rows
        for hh in range(w // LANES):
            lanes = slice(hh * LANES, (hh + 1) * LANES)
            br, bi = [], []
            for k in range(FFT_BINS):
                vr, vi = v_prev[k, piece(0, r0), lanes], v_prev[k, piece(n2, r0), lanes]
                if k == 0:
                    br.append(vr * (0.5 / l))
                    bi.append(None)
                else:
                    c, s = twci_ref[k, piece(0, r0), :], twsi_ref[k, piece(0, r0), :]
                    br.append(vr * c - vi * s)
                    bi.append(None if k == FFT_BINS - 1 else vr * s + vi * c)
            y = _inv16(br, bi)
            for n1 in range(FFT_RADIX // 2):
                rws = piece(n1 * n2, r0)
                o_ref[rws, lanes] = g_prev[rws, lanes] * (y[n1] + u_prev[rws, lanes] * bias_ref[:, lanes])
        return carry

    for i in range(n_pieces):
        inv_piece(i, 0)
    for k in range(n_pieces, FFT_BINS):
        slab(k)


def _hy_conv(sig, sig_col, hy, gate_col, cw, cb, spec, order, bias, tabs):
    b, l, _ = hy.shape
    w = GROUP_W
    n2 = 2 * l // FFT_RADIX
    rows = max(FFT_PIECE_ROWS, n2 // (FFT_BINS - 1))
    rows = min(rows, n2)
    kr, ki = spec
    sig_arr = hy if sig_col is not None else sig
    sig_blk = sig_col if sig_col is not None else 0
    seq = lambda col: pl.BlockSpec((None, l, w), lambda t: (jnp.minimum(t, b - 1), 0, col))
    spec_blk = pl.BlockSpec((FFT_BINS * n2, w), lambda t: (0, order), pipeline_mode=pl.Buffered(1))
    kernel = functools.partial(_hy_conv_kernel, sig_col=sig_col, gate_col=gate_col, rows=rows)
    stage = [pltpu.VMEM((l, w), F32), pltpu.VMEM((l, w), F32), pltpu.VMEM((FFT_BINS, 2 * n2, w), F32)]
    return pl.pallas_call(
        kernel,
        grid=(b + 1,),
        in_specs=[seq(sig_blk), seq(gate_col), _const_spec(cw), _const_spec(cb), spec_blk, spec_blk,
                  _const_spec(bias)] + [_const_spec(t) for t in tabs],
        out_specs=pl.BlockSpec((None, l, w), lambda t: (jnp.maximum(t - 1, 0), 0, 0)),
        out_shape=jax.ShapeDtypeStruct((b, l, w), F32),
        scratch_shapes=stage + stage + [pltpu.VMEM((FFT_BINS, 2 * n2, w), BF16)],
        compiler_params=_params("arbitrary"),
        name="hy_conv",
    )(sig_arr, hy, cw, cb, kr, ki, bias, *tabs)


@functools.lru_cache(maxsize=None)
def _fft_tables(l):
    n = 2 * l
    n2 = n // FFT_RADIX
    j = np.arange(n2, dtype=np.int64)
    k1 = np.arange(FFT_BINS, dtype=np.int64)
    f32 = lambda a: np.ascontiguousarray(a, dtype=np.float32)
    b16 = lambda a: f32(a).astype(BF16)
    ang2 = ((j[:, None] * j[None, :]) % n2).astype(np.float64) * (2.0 * math.pi / n2)
    c2, s2 = np.cos(ang2), np.sin(ang2)
    m = np.block([[c2, s2], [-s2, c2]])
    angt = (k1[:, None] * j[None, :]).astype(np.float64) * (2.0 * math.pi / n)
    wk = np.where((k1 == 0) | (k1 == FFT_BINS - 1), 1.0, 2.0)[:, None] / n
    lane = lambda a: f32(np.broadcast_to(a[:, :, None], a.shape + (LANES,)))
    bins = (k1[:, None] + FFT_RADIX * j[None, :]).reshape(-1)
    angf = ((bins[:, None] * np.arange(l, dtype=np.int64)[None, :]) % n).astype(np.float64) * (2.0 * math.pi / n)
    conv_tabs = (b16(m), b16(m.T), lane(np.cos(angt)), lane(np.sin(angt)),
                 lane(np.cos(angt) * wk), lane(np.sin(angt) * wk))
    def split(a):
        hi = b16(a)
        return hi, b16(a - hi.astype(np.float64))

    return split(np.cos(angf)) + split(np.sin(angf)), conv_tabs


@functools.lru_cache(maxsize=None)
def _hy_features(l):
    f = np.float32
    t = np.linspace(0.0, 1.0, l, dtype=f)[:, None]
    wpos = (f(2.0 * math.pi) * np.arange(l, dtype=f)[:, None] / f(l)).astype(f)
    fb = np.linspace(1e-4, HY_BANDS - 1, HY_BANDS, dtype=f)[None, :]
    arg = (fb * wpos).astype(f).astype(np.float64)
    z = np.concatenate([t, np.cos(arg), -np.sin(arg), np.zeros((l, HY_FFN - HY_EMB))], axis=-1).astype(f)
    deltas = np.linspace(math.log(HY_TARGET) / HY_FAST_PCT, math.log(HY_TARGET) / HY_SLOW_PCT, HY_NFILT, dtype=f)
    return z, t, np.abs(deltas)[None, :]


def _hyena(hy, feats, spec_tabs, conv_tabs, cw, cb, w1, b1, w2, b2, w3, b3, fr, w4, bias):
    l = hy.shape[1]
    z, t, dl = feats
    w1p = jnp.pad(w1, ((0, HY_FFN - HY_EMB), (0, 0)))
    h = _hy_filters(l, z, t, w1p, b1[None], w2, b2[None], w3, b3[None], fr[None], w4, dl)
    spec = _hy_spectrum(h, spec_tabs)
    cb2 = cb[None]
    zz = _hy_conv(None, 0, hy, 1, cw, cb2, spec, 0, bias[0:1], conv_tabs)
    return _hy_conv(zz, None, hy, 2, cw, cb2, spec, 1, bias[1:2], conv_tabs)


@functools.lru_cache(maxsize=None)
def _rope_tables(l, dim, width, off, reps):
    f = np.float32
    rows = np.repeat(np.arange(l // GRID_W), GRID_W).astype(f)
    cols = np.tile(np.arange(GRID_W), l // GRID_W).astype(f)
    quarter = dim // 4
    half = dim // 2
    inv = (f(ROPE_THETA) ** (-np.arange(quarter, dtype=f) / f(quarter))).astype(f)
    ang = np.concatenate([rows[:, None] * inv, cols[:, None] * inv], axis=-1).astype(f).astype(np.float64)
    c, s = np.cos(ang), np.sin(ang)
    cf = np.ones((l, width))
    s1 = np.zeros((l, width))
    s2 = np.zeros((l, width))
    cf[:, off:off + dim] = np.concatenate([c, c], axis=-1)
    s1[:, off:off + half] = -s
    s2[:, off + half:off + dim] = s
    return tuple(np.tile(a, (1, reps)).astype(f) for a in (cf, s1, s2))


def _attn_kernel(q_ref, *rest, rep):
    o_ref = rest[-1]
    kv = rest[:-1]
    ks, vs = kv[0::2], kv[1::2]
    tq = q_ref.shape[1]
    lo = lax.broadcasted_iota(jnp.int32, (tq, LANES), 1) < HEAD_D
    nt = (((1,), (1,)), ((), ()))
    n_heads = q_ref.shape[0]

    def scores(h):
        return [lax.dot_general(q_ref[h], k_ref[h // rep], nt, preferred_element_type=F32) for k_ref in ks]

    def weighted(h, ss):
        m = ss[0].max(axis=-1, keepdims=True)
        for s in ss[1:]:
            m = jnp.maximum(m, s.max(axis=-1, keepdims=True))
        acc = 0.0
        for s, v_ref in zip(ss, vs):
            acc = acc + jnp.dot(jnp.exp2(s - m).astype(BF16), v_ref[h], preferred_element_type=F32)
        return acc / pltpu.roll(acc, HEAD_D, 1)

    outs = []
    ss_next = scores(0)
    for h in range(n_heads):
        ss = ss_next
        if h + 1 < n_heads:
            ss_next = scores(h + 1)
        outs.append(weighted(h, ss))
    for p in range(n_heads // 2):
        o_ref[:, p * LANES:(p + 1) * LANES] = jnp.where(lo, outs[2 * p], outs[2 * p + 1])


def _attention(q, kvs, rep, b, l):
    nh = q.shape[0]
    tq = min(ATTN_TILE, l)
    nq = l // tq
    specs = [pl.BlockSpec((nh, tq, LANES), lambda i, j: (0, i * nq + j, 0))]
    args = [q]
    for k, v, lk in kvs:
        specs += [pl.BlockSpec((k.shape[0], lk, LANES), lambda i, j: (0, i, 0)),
                  pl.BlockSpec((v.shape[0], lk, LANES), lambda i, j: (0, i, 0))]
        args += [k, v]
    return pl.pallas_call(
        functools.partial(_attn_kernel, rep=rep),
        grid=(b, nq),
        in_specs=specs,
        out_specs=pl.BlockSpec((None, tq, GROUP_W), lambda i, j: (i, j, 0)),
        out_shape=jax.ShapeDtypeStruct((b, l, GROUP_W), F32),
        compiler_params=_params("parallel", "parallel"),
        name="attention",
    )(*args)


def _ret_kernel(qs, ks, v_ref, g_ref, dl_ref, dc_ref, ng_ref, seg_ref, sf0_ref, sb0_ref,
                y_ref, sf_ref, sb_ref, sball, dec_scr, cdec_scr, dmat_scr, *, chunk, unroll):
    c_ = chunk
    w = GROUP_W
    n_chunks = qs.shape[0] // c_
    unroll = min(unroll, n_chunks)
    shift_c = int(math.log2(c_))
    shift_h = int(math.log2(HEAD_D))
    tn = (((0,), (0,)), ((), ()))
    nt = (((1,), (1,)), ((), ()))

    @pl.when(pl.program_id(0) == 0)
    def _():
        lgf = _log_sigmoid(dl_ref[0:1, :])
        lgb = _log_sigmoid(dl_ref[1:2, :])
        ii = lax.broadcasted_iota(jnp.int32, (c_, 1), 0).astype(F32)
        dec_scr[0] = jnp.exp((ii + 1.0) * lgf)
        dec_scr[1] = jnp.exp((c_ - 1.0 - ii) * lgf)
        dec_scr[2] = jnp.exp((c_ - ii) * lgb)
        dec_scr[3] = jnp.exp(ii * lgb)
        cdec_scr[0] = jnp.exp(float(c_) * lgf)
        cdec_scr[1] = jnp.exp(float(c_) * lgb)
        lgc = _log_sigmoid(dc_ref[...])
        i4 = jnp.bitwise_and(lax.broadcasted_iota(jnp.int32, (4 * c_, c_), 0), c_ - 1)
        j4 = lax.broadcasted_iota(jnp.int32, (4 * c_, c_), 1)
        diff = (i4 - j4).astype(F32)
        dmat_scr[...] = (jnp.where(diff >= 0, jnp.exp(lgc[:, 0:1] * jnp.maximum(diff, 0.0)), 0.0)
                         + jnp.where(diff <= 0, jnp.exp(lgc[:, 1:2] * jnp.maximum(-diff, 0.0)), 0.0))

    qdf, kdf, qdb, kdb = dec_scr[0], dec_scr[1], dec_scr[2], dec_scr[3]
    cdf, cdb = cdec_scr[0], cdec_scr[1]
    dmat = dmat_scr[...]
    r_h = lax.shift_right_logical(lax.broadcasted_iota(jnp.int32, (w, w), 0), shift_h)
    c_h = lax.shift_right_logical(lax.broadcasted_iota(jnp.int32, (w, w), 1), shift_h)
    blockdiag = r_h == c_h
    row_h = lax.shift_right_logical(lax.broadcasted_iota(jnp.int32, (4 * c_, w), 0), shift_c)
    lane_h = lax.shift_right_logical(lax.broadcasted_iota(jnp.int32, (4 * c_, w), 1), shift_h)
    headmask = row_h == lane_h
    seg = seg_ref[...]

    def rows(c):
        return pl.ds(pl.multiple_of(c * c_, c_), c_)

    def kv_outer(k, v, dec):
        a = lax.dot_general((k * dec).astype(BF16), v.astype(BF16), tn, preferred_element_type=F32)
        return jnp.where(blockdiag, a, 0.0)

    def bwd(t, s):
        c = n_chunks - 1 - t
        sball[c] = s
        sl = rows(c)
        return s * cdb + kv_outer(ks[sl, :], v_ref[sl, :], kdb)

    sb_ref[...] = lax.fori_loop(0, n_chunks, bwd, sb0_ref[...], unroll=unroll)

    def fwd(c, s):
        sl = rows(c)
        q, k, v = qs[sl, :], ks[sl, :], v_ref[sl, :]
        vb = v.astype(BF16)
        q4 = jnp.where(headmask, jnp.concatenate([q, q, q, q], axis=0), 0.0).astype(BF16)
        sc = lax.dot_general(q4, k.astype(BF16), nt, preferred_element_type=F32)
        o4 = jnp.dot((sc * dmat).astype(BF16), vb, preferred_element_type=F32)
        o4 = jnp.where(headmask, o4, 0.0)
        o = o4[0:c_] + o4[c_:2 * c_] + o4[2 * c_:3 * c_] + o4[3 * c_:4 * c_]
        o = o + jnp.dot((q * qdf).astype(BF16), s.astype(BF16), preferred_element_type=F32)
        o = o + jnp.dot((q * qdb).astype(BF16), sball[c].astype(BF16), preferred_element_type=F32)
        on = o * lax.rsqrt(_seg_mean(o * o, seg) + EPS) * ng_ref[...]
        g = g_ref[sl, :]
        y_ref[sl, :] = on * (g * jax.nn.sigmoid(g))
        return s * cdf + kv_outer(k, v, kdf)

    sf_ref[...] = lax.fori_loop(0, n_chunks, fwd, sf0_ref[...], unroll=unroll)


def _retention(ret, decay, ng, seg, sf0, sb0):
    b, l, _ = ret.shape
    w = GROUP_W
    chunk = min(RET_CHUNK, l)
    dl = jnp.repeat(decay, HEAD_D, axis=1)
    dc = jnp.repeat(decay.T, chunk, axis=0)
    col = lambda j: pl.BlockSpec((None, l, w), lambda i: (i, 0, j))
    state = pl.BlockSpec((None, w, w), lambda i: (i, 0, 0))
    sshape = jax.ShapeDtypeStruct((b, w, w), F32)
    return pl.pallas_call(
        functools.partial(_ret_kernel, chunk=chunk, unroll=RET_UNROLL),
        grid=(b,),
        in_specs=[col(0), col(1), col(2), col(3), _const_spec(dl), _const_spec(dc), _const_spec(ng),
                  _const_spec(seg), state, state],
        out_specs=[pl.BlockSpec((None, l, w), lambda i: (i, 0, 0)), state, state],
        out_shape=[jax.ShapeDtypeStruct((b, l, w), F32), sshape, sshape],
        scratch_shapes=[pltpu.VMEM((l // chunk, w, w), F32), pltpu.VMEM((4, chunk, w), F32),
                        pltpu.VMEM((2, 1, w), F32), pltpu.VMEM((4 * chunk, chunk), F32)],
        compiler_params=_params("arbitrary"),
        name="retention",
    )(ret, ret, ret, ret, dl, dc, ng, seg, sf0, sb0)


def _mlp_kernel(x_ref, ya_ref, yb_ref, yc_ref, yd_ref, gout_ref, ga_ref, wout_ref,
                gffn_ref, sh_ref, sc_ref, gf_ref, wfi_ref, wfo_ref, fg_ref, o_ref,
                x1_a, h_a, x1_b, h_b, *, final_norm):
    t = pl.program_id(0)
    nff = D_FF // FF_TILE

    @pl.when(t == 0)
    def _():
        x1_b[...] = jnp.zeros_like(x1_b)
        h_b[...] = jnp.zeros_like(h_b)

    def step(x1_w, h_w, x1_r, h_r):
        acc = 0.0
        for i, y_ref in enumerate((ya_ref, yb_ref, yc_ref, yd_ref)):
            rows = slice(i * GROUP_W, (i + 1) * GROUP_W)
            yn = (_rms(y_ref[...]) * gout_ref[:, rows]).astype(BF16)
            acc = acc + jnp.dot(yn, wout_ref[rows, :], preferred_element_type=F32)
        x1 = x_ref[...] + ga_ref[...] * acc
        x1_w[...] = x1
        h_w[...] = (_rms(x1) * gffn_ref[...] * (1.0 + sc_ref[...]) + sh_ref[...]).astype(BF16)

        h = h_r[...]
        ff = 0.0
        for c in range(nff):
            cols = slice(c * FF_TILE, (c + 1) * FF_TILE)
            gate_cols = slice(D_FF + c * FF_TILE, D_FF + (c + 1) * FF_TILE)
            a = jnp.dot(h, wfi_ref[:, cols], preferred_element_type=F32)
            g = jnp.dot(h, wfi_ref[:, gate_cols], preferred_element_type=F32)
            u = (a * jax.nn.sigmoid(a) * g).astype(BF16)
            ff = ff + jnp.dot(u, wfo_ref[cols, :], preferred_element_type=F32)
        out = x1_r[...] + gf_ref[...] * ff
        if final_norm:
            out = _rms(out) * fg_ref[...]
        o_ref[...] = out

    @pl.when(t % 2 == 0)
    def _():
        step(x1_a, h_a, x1_b, h_b)

    @pl.when(t % 2 == 1)
    def _():
        step(x1_b, h_b, x1_a, h_a)


def _mlp(x, ys, g_out, g_a, w_out, g_ffn, sh, sc, g_f, wfi, wfo, final_g, final_norm):
    b, l, d = x.shape
    tm = min(MLP_TILE, l)
    nt = l // tm
    n = b * nt
    cur = lambda t: jnp.minimum(t, n - 1)
    prv = lambda t: jnp.maximum(t - 1, 0)
    tok = lambda wd: pl.BlockSpec((None, tm, wd), lambda t: (cur(t) // nt, cur(t) % nt, 0))
    vec = pl.BlockSpec((None, 1, d), lambda t: (cur(t) // nt, 0, 0))
    return pl.pallas_call(
        functools.partial(_mlp_kernel, final_norm=final_norm),
        grid=(n + 1,),
        in_specs=[tok(d)] + [tok(GROUP_W)] * 4 + [
            _const_spec(g_out), vec, _const_spec(w_out), _const_spec(g_ffn), vec, vec,
            pl.BlockSpec((None, 1, d), lambda t: (prv(t) // nt, 0, 0)),
            _const_spec(wfi), _const_spec(wfo), _const_spec(final_g)],
        out_specs=pl.BlockSpec((None, tm, d), lambda t: (prv(t) // nt, prv(t) % nt, 0)),
        out_shape=jax.ShapeDtypeStruct((b, l, d), F32),
        scratch_shapes=[pltpu.VMEM((tm, d), F32), pltpu.VMEM((tm, d), BF16),
                        pltpu.VMEM((tm, d), F32), pltpu.VMEM((tm, d), BF16)],
        compiler_params=_params("arbitrary"),
        name="mlp",
    )(x, *ys, g_out, g_a, w_out, g_ffn, sh, sc, g_f, wfi, wfo, final_g)


def _pad_proj(w):
    end = PROJ_OFFSETS[2] + MLA_KV_RANK + MLA_ROPE
    z = jnp.zeros((w.shape[0], LANES - MLA_ROPE), w.dtype)
    return jnp.concatenate([w[:, :end], z, w[:, end:]], axis=1).astype(BF16)


def _mla_weights(wuq, wukv):
    r = wuq.shape[0]
    q = wuq.reshape(r, 4, MLA_NOPE + MLA_ROPE)
    q = jnp.pad(q, ((0, 0), (0, 0), (0, LANES - MLA_NOPE - MLA_ROPE))).reshape(r, 4 * LANES)
    kv = wukv.reshape(wukv.shape[0], 4, MLA_NOPE + MLA_V)
    k = jnp.pad(kv[..., :MLA_NOPE], ((0, 0), (0, 0), (0, LANES - MLA_NOPE))).reshape(-1, 4 * LANES)
    v = jnp.concatenate([kv[..., MLA_NOPE:], kv[..., MLA_NOPE:]], axis=-1).reshape(-1, 4 * LANES)
    return q.astype(BF16), k.astype(BF16), v.astype(BF16)


def _seg_matrix(width):
    i = jnp.arange(width) // HEAD_D
    return jnp.where(i[:, None] == i[None, :], 1.0 / HEAD_D, 0.0).astype(BF16)


def kernel(x, c, ctx, c_ctx, w_mod, b_mod, norm_attn_g, norm_ffn_g, w_in, hy_conv_w, hy_conv_b, hy_w1, hy_b1, hy_w2, hy_b2, hy_w3, hy_b3, hy_freq, hy_w4, hy_bias, mla_q_norm_g, mla_w_uq, mla_kv_norm_g, mla_w_ukv, ret_decay, ret_norm_g, gqa_q_norm_g, gqa_k_norm_g, out_norm_g, w_out, w_ffn_in, w_ffn_out, final_norm_g):
    b, n_lat, d = x.shape
    n_ctx = ctx.shape[1]
    depth = w_mod.shape[0]

    rope_mla = _rope_tables(n_lat, MLA_ROPE, LANES, MLA_NOPE, 1)
    rope_hd2 = _rope_tables(n_lat, HEAD_D, HEAD_D, 0, 2)
    seg2, seg4 = _seg_matrix(LANES), _seg_matrix(GROUP_W)
    hy_consts = {n: (_hy_features(n),) + _fft_tables(n) for n in {n_lat, n_ctx}}
    zero_state = jnp.zeros((b, GROUP_W, GROUP_W), F32)

    rows = -(-(b + 1) // 8) * 8
    cond = jnp.zeros((rows, d), F32).at[:b].set(c).at[b].set(c_ctx)
    mod = _ada_mod(cond, w_mod, b_mod)

    xl, xc = x, ctx.reshape(1, b * n_ctx, d)
    for l in range(depth):
        update_ctx = l < depth - 1
        m_lat = [m[:b, None, :] for m in jnp.split(mod[l], 6, axis=-1)]
        m_ctx = [m[b][None, None, :] for m in jnp.split(mod[l], 6, axis=-1)]
        wp = _pad_proj(w_in[l])
        g_attn = norm_attn_g[l][None]
        mla_w = (mla_q_norm_g[l][None], mla_kv_norm_g[l][None]) + _mla_weights(mla_w_uq[l], mla_w_ukv[l])
        gqa_w = (jnp.tile(gqa_q_norm_g[l], 2)[None], jnp.tile(gqa_k_norm_g[l], 2)[None], seg2)
        hy_l, ret_l, q_l, k_l, v_l, gq_l, gk_l, gv_l = _front(
            xl, g_attn, m_lat[0], m_lat[1], wp, mla_w, gqa_w, (rope_mla, rope_hd2))
        hy_c, ret_c, q_c, k_c, v_c, gq_c, gk_c, gv_c = _front(
            xc, g_attn, m_ctx[0], m_ctx[1], wp, mla_w, gqa_w, None)
        hy_c = hy_c.reshape(b, n_ctx, -1)
        ret_c = ret_c.reshape(b, n_ctx, -1)

        hy_args = (hy_conv_w[l], hy_conv_b[l], hy_w1[l], hy_b1[l], hy_w2[l], hy_b2[l], hy_w3[l], hy_b3[l],
                   hy_freq[l], hy_w4[l], hy_bias[l])
        ya_l = _hyena(hy_l, *hy_consts[n_lat], *hy_args)
        yb_l = _attention(q_l, [(k_c, v_c, n_ctx), (k_l, v_l, n_lat)], 1, b, n_lat)

        ng = ret_norm_g[l][None]
        yc_c, s_f, s_b = _retention(ret_c, ret_decay[l], ng, seg4, zero_state, zero_state)
        yc_l, _, _ = _retention(ret_l, ret_decay[l], ng, seg4, s_f, s_b)
        yd_l = _attention(gq_l, [(gk_c, gv_c, n_ctx), (gk_l, gv_l, n_lat)], 2, b, n_lat)

        wo = w_out[l].astype(BF16)
        wfi, wfo = w_ffn_in[l].astype(BF16), w_ffn_out[l].astype(BF16)
        g_out, g_ffn, g_fin = out_norm_g[l][None], norm_ffn_g[l][None], final_norm_g[None]
        xl = _mlp(xl, (ya_l, yb_l, yc_l, yd_l), g_out, m_lat[2], wo, g_ffn, m_lat[3], m_lat[4], m_lat[5],
                  wfi, wfo, g_fin, l == depth - 1)

        if update_ctx:
            flat = lambda a: a.reshape(1, b * n_ctx, GROUP_W)
            ya_c = _hyena(hy_c, *hy_consts[n_ctx], *hy_args)
            yb_c = _attention(q_c, [(k_c, v_c, n_ctx)], 1, b, n_ctx)
            yd_c = _attention(gq_c, [(gk_c, gv_c, n_ctx)], 2, b, n_ctx)
            xc = _mlp(xc, tuple(flat(a) for a in (ya_c, yb_c, yc_c, yd_c)), g_out, m_ctx[2], wo,
                      g_ffn, m_ctx[3], m_ctx[4], m_ctx[5], wfi, wfo, g_fin, False)

    return xl
```

```python
import functools
import math

import numpy as np
import jax
import jax.numpy as jnp
from jax import lax
from jax.experimental import pallas as pl
from jax.experimental.pallas import tpu as pltpu

F32 = jnp.float32
BF16 = jnp.bfloat16
HI = lax.Precision.HIGHEST

D_MODEL = 1024
GRID_W = 64
EPS = 1e-6
ROPE_THETA = 10000.0
GROUP_W = 256
LANES = 128

HY_EMB = 33
HY_BANDS = 16
HY_FFN = 64
HY_NFILT = 1024
HY_DECAY_SHIFT = 0.05
HY_FAST_PCT = 0.3
HY_SLOW_PCT = 1.5
HY_TARGET = 1e-2
FFT_RADIX = 16
FFT_BINS = FFT_RADIX // 2 + 1
FFT_PIECE_ROWS = 32

MLA_NOPE = 64
MLA_ROPE = 32
MLA_V = 64
MLA_KV_RANK = 128
LOG2E = math.log2(math.e)
MLA_SCALE = (MLA_NOPE + MLA_ROPE) ** -0.5 * LOG2E

HEAD_D = 64
RET_CHUNK = 256
RET_UNROLL = 8
GQA_SCALE = HEAD_D ** -0.5 * LOG2E
RET_K_SCALE = HEAD_D ** -0.5

D_FF = 2816
FF_TILE = 256

PROJ_SPLITS = (768, 256, 256, 1024, 512)
PROJ_OFFSETS = (0, 768, 1024, 1280, 2304, 2816)

V7X_VMEM_BYTES = 64 * 1024 * 1024
VMEM_LIMIT = V7X_VMEM_BYTES - 8 * 1024 * 1024

FRONT_TILE = 1024
ATTN_TILE = 1024
MLP_TILE = 512


def _params(*sem):
    return pltpu.CompilerParams(dimension_semantics=sem, vmem_limit_bytes=VMEM_LIMIT)


def _const_spec(a):
    nd = a.ndim
    return pl.BlockSpec(a.shape, lambda *_: (0,) * nd, pipeline_mode=pl.Buffered(1))


def _rms(x):
    return x * lax.rsqrt(jnp.mean(x * x, axis=-1, keepdims=True) + EPS)


def _rope(x, cf, s1, s2, half):
    w = x.shape[-1]
    return x * cf + pltpu.roll(x, w - half, 1) * s1 + pltpu.roll(x, half, 1) * s2


def _seg_mean(sq, seg):
    hi = sq.astype(BF16)
    lo = (sq - hi.astype(F32)).astype(BF16)
    return (jnp.dot(hi, seg, preferred_element_type=F32) + jnp.dot(lo, seg, preferred_element_type=F32))


def _log_sigmoid(x):
    return -(jnp.maximum(-x, 0.0) + jnp.log(1.0 + jnp.exp(-jnp.abs(x))))


def _ada_kernel(c_ref, w_ref, b_ref, o_ref):
    c = c_ref[...]
    s = c * jax.nn.sigmoid(c)
    o_ref[...] = jnp.dot(s, w_ref[...], precision=HI, preferred_element_type=F32) + b_ref[...]


def _ada_mod(cond, w_mod, b_mod):
    depth, d, n = w_mod.shape
    r = cond.shape[0]
    tn = 1024
    return pl.pallas_call(
        _ada_kernel,
        grid=(depth, n // tn),
        in_specs=[pl.BlockSpec((r, d), lambda l, j: (0, 0)),
                  pl.BlockSpec((None, d, tn), lambda l, j: (l, 0, j)),
                  pl.BlockSpec((None, 1, tn), lambda l, j: (l, 0, j))],
        out_specs=pl.BlockSpec((None, r, tn), lambda l, j: (l, 0, j)),
        out_shape=jax.ShapeDtypeStruct((depth, r, n), F32),
        compiler_params=_params("arbitrary", "arbitrary"),
        name="ada_mod",
    )(cond, w_mod, b_mod.reshape(depth, 1, n))


def _value_with_ones(v2, head):
    lo = lax.broadcasted_iota(jnp.int32, v2.shape, 1) < HEAD_D
    keep = lo if head % 2 == 0 else jnp.logical_not(lo)
    return jnp.where(keep, v2, 1.0).astype(BF16)


def _mla_heads(cq, ckvr, gq, gkv, wuq_ref, wk_ref, wv_ref, tables, q_ref, k_ref, v_ref):
    half = MLA_ROPE // 2
    qn = (_rms(cq) * gq).astype(BF16)
    q = jnp.dot(qn, wuq_ref[...], preferred_element_type=F32)
    kvn = (_rms(ckvr[:, :MLA_KV_RANK]) * gkv).astype(BF16)
    k = jnp.dot(kvn, wk_ref[...], preferred_element_type=F32)
    v = jnp.dot(kvn, wv_ref[...], preferred_element_type=F32)
    kpe = pltpu.roll(ckvr[:, MLA_KV_RANK:], MLA_NOPE, 1)
    if tables is not None:
        kpe = _rope(kpe, *tables, half)
    for h in range(q_ref.shape[0]):
        cols = slice(h * LANES, (h + 1) * LANES)
        qh = q[:, cols]
        if tables is not None:
            qh = _rope(qh, *tables, half)
        q_ref[h] = (qh * MLA_SCALE).astype(BF16)
        k_ref[h] = (k[:, cols] + kpe).astype(BF16)
        v_ref[h] = _value_with_ones(v[:, cols], h)


def _gqa_heads(x, gq, gk, seg, tables, q_ref, k_ref, v_ref):
    half = HEAD_D // 2
    lo = lax.broadcasted_iota(jnp.int32, (x.shape[0], LANES), 1) < HEAD_D

    def normed(a, g):
        a = a * lax.rsqrt(_seg_mean(a * a, seg) + EPS) * g
        return _rope(a, *tables, half) if tables is not None else a

    def dup(a, g):
        a = jnp.where(lo if g == 0 else jnp.logical_not(lo), a, 0.0)
        return a + pltpu.roll(a, HEAD_D, 1)

    for p in range(2):
        qb = normed(x[:, p * LANES:(p + 1) * LANES], gq) * GQA_SCALE
        q_ref[2 * p] = jnp.where(lo, qb, 0.0).astype(BF16)
        q_ref[2 * p + 1] = jnp.where(lo, 0.0, qb).astype(BF16)
    kb = normed(x[:, 2 * LANES:3 * LANES], gk)
    vb = x[:, 3 * LANES:4 * LANES]
    for g in range(2):
        k_ref[g] = dup(kb, g).astype(BF16)
        for r in range(2):
            v_ref[2 * g + r] = _value_with_ones(dup(vb, g), 2 * g + r)


def _front_kernel(x_ref, g_ref, sh_ref, sc_ref, w_ref, gq_ref, gkv_ref, wuq_ref, wk_ref, wv_ref,
                  ggq_ref, ggk_ref, seg_ref, *rest, rope):
    if rope:
        mla_t = tuple(r[...] for r in rest[0:3])
        hd_t = tuple(r[...] for r in rest[3:6])
        rest = rest[6:]
    else:
        mla_t = hd_t = None
    hy_ref, ret_ref, mq_ref, mk_ref, mv_ref, gq_out, gk_out, gv_out = rest
    y = _rms(x_ref[...]) * g_ref[...]
    h = (y * (1.0 + sc_ref[...]) + sh_ref[...]).astype(BF16)
    o0, o1, o2, o3, o4, o5 = PROJ_OFFSETS

    def proj(a, b):
        return jnp.dot(h, w_ref[:, a:b], preferred_element_type=F32)

    _mla_heads(proj(o1, o2), proj(o2, o3), gq_ref[...], gkv_ref[...], wuq_ref, wk_ref, wv_ref, mla_t,
               mq_ref, mk_ref, mv_ref)
    _gqa_heads(proj(o4, o5), ggq_ref[...], ggk_ref[...], seg_ref[...], hd_t, gq_out, gk_out, gv_out)
    for i in range(2 * GROUP_W // LANES):
        cols = slice(i * LANES, (i + 1) * LANES)
        a = proj(o3 + i * LANES, o3 + (i + 1) * LANES)
        if i * LANES >= GROUP_W:
            a = a * RET_K_SCALE
        ret_ref[:, cols] = _rope(a, *hd_t, HEAD_D // 2) if rope else a
    ret_ref[:, 2 * GROUP_W:] = proj(o3 + 2 * GROUP_W, o4)
    hy_ref[...] = proj(o0, o1)


def _front(x, g, sh, sc, w, mla_w, gqa_w, tables):
    b, l, d = x.shape
    tm = min(FRONT_TILE, l)
    nt = l // tm
    rope = tables is not None
    tok = lambda wd: pl.BlockSpec((None, tm, wd), lambda i, j: (i, j, 0))
    vec = pl.BlockSpec((None, 1, d), lambda i, j: (i, 0, 0))
    head = lambda nh: pl.BlockSpec((nh, tm, LANES), lambda i, j: (0, i * nt + j, 0))
    hshape = lambda nh: jax.ShapeDtypeStruct((nh, b * l, LANES), BF16)
    consts = [w, *mla_w, *gqa_w]
    args = [x, g, sh, sc, *consts]
    specs = [tok(d), _const_spec(g), vec, vec] + [_const_spec(a) for a in consts]
    if rope:
        args += [*tables[0], *tables[1]]
        specs += [pl.BlockSpec((tm, LANES), lambda i, j: (j, 0))] * 6
    return pl.pallas_call(
        functools.partial(_front_kernel, rope=rope),
        grid=(b, nt),
        in_specs=specs,
        out_specs=[tok(PROJ_SPLITS[0]), tok(PROJ_SPLITS[3]), head(4), head(4), head(4), head(4), head(2), head(4)],
        out_shape=[jax.ShapeDtypeStruct((b, l, PROJ_SPLITS[0]), F32),
                   jax.ShapeDtypeStruct((b, l, PROJ_SPLITS[3]), F32),
                   hshape(4), hshape(4), hshape(4), hshape(4), hshape(2), hshape(4)],
        compiler_params=_params("parallel", "parallel"),
        name="front",
    )(*args)


def _hy_filter_kernel(z_ref, t_ref, w1_ref, b1_ref, w2_ref, b2_ref, w3_ref, b3_ref, fr_ref,
                      w4_ref, dl_ref, o_ref, h_scr):
    dot = functools.partial(jnp.dot, precision=HI, preferred_element_type=F32)

    @pl.when(pl.program_id(0) == 0)
    def _():
        fr = fr_ref[...]
        h = jnp.sin(fr * (dot(z_ref[...], w1_ref[...]) + b1_ref[...]))
        h = jnp.sin(fr * (dot(h, w2_ref[...]) + b2_ref[...]))
        h_scr[...] = jnp.sin(fr * (dot(h, w3_ref[...]) + b3_ref[...]))

    filt = dot(h_scr[...], w4_ref[...]) * (jnp.exp(-t_ref[...] * dl_ref[...]) + HY_DECAY_SHIFT)
    fwd = filt[:, :GROUP_W]
    bwd = filt[:, GROUP_W:]
    row = lax.broadcasted_iota(jnp.int32, bwd.shape, 0)
    bwd = jnp.where(row == 0, 0.0, bwd)
    nrm = (jnp.sum(jnp.abs(fwd), axis=0, keepdims=True) + jnp.sum(jnp.abs(bwd), axis=0, keepdims=True))
    o_ref[:, :GROUP_W] = fwd / nrm
    o_ref[:, GROUP_W:] = bwd / nrm


def _hy_filters(l, z, t, w1, b1, w2, b2, w3, b3, fr, w4, dl):
    full = lambda a: pl.BlockSpec(a.shape, lambda o: (0, 0))
    return pl.pallas_call(
        _hy_filter_kernel,
        grid=(2,),
        in_specs=[full(z), full(t), full(w1), full(b1), full(w2), full(b2), full(w3), full(b3), full(fr),
                  pl.BlockSpec((HY_FFN, 2 * GROUP_W), lambda o: (0, o)),
                  pl.BlockSpec((1, 2 * GROUP_W), lambda o: (0, o))],
        out_specs=pl.BlockSpec((l, 2 * GROUP_W), lambda o: (0, o)),
        out_shape=jax.ShapeDtypeStruct((l, HY_NFILT), F32),
        scratch_shapes=[pltpu.VMEM((l, HY_FFN), F32)],
        compiler_params=_params("arbitrary"),
        name="hy_filters",
    )(z, t, w1, b1, w2, b2, w3, b3, fr, w4, dl)


def _split_bf16(x):
    hi = x.astype(BF16)
    return hi, (x - hi.astype(F32)).astype(BF16)


def _dot_3pass(a_hi, a_lo, x):
    x_hi, x_lo = _split_bf16(x)
    dot = functools.partial(jnp.dot, preferred_element_type=F32)
    return dot(a_hi, x_hi) + (dot(a_hi, x_lo) + dot(a_lo, x_hi))


def _hy_spec_kernel(ch_ref, cl_ref, sh_ref, sl_ref, h_ref, re_ref, im_ref):
    for o in range(2):
        hf = h_ref[:, 2 * o * GROUP_W:(2 * o + 1) * GROUP_W]
        hb = h_ref[:, (2 * o + 1) * GROUP_W:(2 * o + 2) * GROUP_W]
        cols = slice(o * GROUP_W, (o + 1) * GROUP_W)
        re_ref[:, cols] = _dot_3pass(ch_ref[...], cl_ref[...], hf + hb)
        im_ref[:, cols] = -_dot_3pass(sh_ref[...], sl_ref[...], hf - hb)


def _hy_spectrum(h, tabs):
    rows, l = tabs[0].shape
    fb = rows // FFT_BINS if rows // FFT_BINS >= 256 else rows
    out = jax.ShapeDtypeStruct((rows, 2 * GROUP_W), F32)
    blk = pl.BlockSpec((fb, 2 * GROUP_W), lambda f: (f, 0))
    return pl.pallas_call(
        _hy_spec_kernel,
        grid=(rows // fb,),
        in_specs=[pl.BlockSpec((fb, l), lambda f: (f, 0))] * 4 + [_const_spec(h)],
        out_specs=[blk, blk],
        out_shape=[out, out],
        compiler_params=_params("arbitrary"),
        name="hy_spectrum",
    )(*tabs, h)


def _short_conv_to(dst_ref, u, w, b):
    n = u.shape[0]
    mid = w[1:2] * u + b
    dst_ref[...] = w[0:1] * pltpu.roll(u, 1, 0) + mid + w[2:3] * pltpu.roll(u, n - 1, 0)
    dst_ref[0:1, :] = mid[0:1] + w[2:3] * u[1:2]
    dst_ref[n - 1:n, :] = mid[n - 1:n] + w[0:1] * u[n - 2:n - 1]


_R2 = math.sqrt(0.5)


def _dft8_real4(x0, x1, x2, x3):
    s02, d02 = x0 + x2, x0 - x2
    s13, d13 = x1 + x3, x1 - x3
    p, q = _R2 * d13, _R2 * s13
    re = [s02 + s13, x0 + p, d02, x0 - p, s02 - s13]
    im = [None, -(q + x2), -d13, x2 - q, None]
    return re, im


def _fwd16(a):
    er, ei = _dft8_real4(a[0], a[2], a[4], a[6])
    orr, oi = _dft8_real4(a[1], a[3], a[5], a[7])

    def ext(re, im, k):
        if k <= 4:
            return re[k], im[k]
        return re[8 - k], (None if im[8 - k] is None else -im[8 - k])

    out_r, out_i = [], []
    for k in range(FFT_BINS):
        e_r, e_i = ext(er, ei, k)
        o_r, o_i = ext(orr, oi, k)
        c, s = math.cos(math.pi * k / 8), math.sin(math.pi * k / 8)
        if k == 0:
            out_r.append(e_r + o_r)
            out_i.append(None)
        elif k == 8:
            out_r.append(e_r - o_r)
            out_i.append(None)
        elif k == 4:
            out_r.append(e_r)
            out_i.append(-o_r)
        else:
            out_r.append(e_r + (c * o_r + s * o_i))
            out_i.append(e_i + (c * o_i - s * o_r))
    return out_r, out_i


def _dft8_real4_t(kr, ki):
    a = kr[0] + kr[4]
    b = kr[0] - kr[4]
    x0 = a + kr[2] + kr[1] + kr[3]
    x2 = a - kr[2] - ki[1] + ki[3]
    t1 = _R2 * (kr[1] - kr[3])
    t2 = _R2 * (ki[1] + ki[3])
    return x0, b - ki[2] + t1 - t2, x2, b + ki[2] - t1 - t2


def _inv16(br, bi):
    o_r, o_i = [None] * FFT_BINS, [None] * FFT_BINS
    for k in range(FFT_BINS):
        c, s = math.cos(math.pi * k / 8), math.sin(math.pi * k / 8)
        if k == 0:
            o_r[k] = br[k]
        elif k == 8:
            o_r[k] = -br[k]
        elif k == 4:
            o_r[k] = -bi[k]
        else:
            o_r[k] = c * br[k] - s * bi[k]
            o_i[k] = c * bi[k] + s * br[k]

    def fold(re, im):
        fr = [re[0] + re[8], re[1] + re[7], re[2] + re[6], re[3] + re[5], re[4]]
        fi = [None, im[1] - im[7], im[2] - im[6], im[3] - im[5], None]
        return fr, fi

    xe = _dft8_real4_t(*fold(br, bi))
    xo = _dft8_real4_t(*fold(o_r, o_i))
    return [xe[0], xo[0], xe[1], xo[1], xe[2], xo[2], xe[3], xo[3]]


def _hy_conv_kernel(sig_ref, gate_ref, cw_ref, cb_ref, kr_ref, ki_ref, bias_ref, m_ref, mt_ref,
                    twc_ref, tws_ref, twci_ref, twsi_ref, o_ref, u_a, g_a, v_a, u_b, g_b, v_b, z_scr,
                    *, sig_col, gate_col, rows):
    t = pl.program_id(0)

    @pl.when(t == 0)
    def _():
        u_b[...] = jnp.zeros_like(u_b)
        g_b[...] = jnp.zeros_like(g_b)
        v_b[...] = jnp.zeros_like(v_b)

    refs = (sig_ref, gate_ref, cw_ref, cb_ref, kr_ref, ki_ref, bias_ref, m_ref, mt_ref,
            twc_ref, tws_ref, twci_ref, twsi_ref, o_ref, z_scr)

    @pl.when(t % 2 == 0)
    def _():
        _hy_conv_step(*refs, (u_a, g_a, v_a), (u_b, g_b, v_b), sig_col, gate_col, rows)

    @pl.when(t % 2 == 1)
    def _():
        _hy_conv_step(*refs, (u_b, g_b, v_b), (u_a, g_a, v_a), sig_col, gate_col, rows)


def _hy_conv_step(sig_ref, gate_ref, cw_ref, cb_ref, kr_ref, ki_ref, bias_ref, m_ref, mt_ref,
                  twc_ref, tws_ref, twci_ref, twsi_ref, o_ref, z_scr, cur, prev, sig_col, gate_col, rows):
    u_scr, g_scr, v_cur = cur
    u_prev, g_prev, v_prev = prev
    w = GROUP_W
    l = u_scr.shape[0]
    n2 = l // (FFT_RADIX // 2)
    n_pieces = n2 // rows
    if sig_col is not None:
        _short_conv_to(u_scr, sig_ref[...], cw_ref[:, sig_col * w:(sig_col + 1) * w],
                       cb_ref[:, sig_col * w:(sig_col + 1) * w])
    else:
        u_scr[...] = sig_ref[...]
    _short_conv_to(g_scr, gate_ref[...], cw_ref[:, gate_col * w:(gate_col + 1) * w],
                   cb_ref[:, gate_col * w:(gate_col + 1) * w])

    def piece(base, r0):
        return pl.ds(pl.multiple_of(base + r0, rows), rows)

    def fwd_piece(i, carry):
        r0 = i * rows
        for hh in range(w // LANES):
            lanes = slice(hh * LANES, (hh + 1) * LANES)
            fr, fi = _fwd16([u_scr[piece(n1 * n2, r0), lanes] for n1 in range(FFT_RADIX // 2)])
            for k in range(FFT_BINS):
                if k == 0:
                    zr, zi = fr[0], jnp.zeros_like(fr[0])
                else:
                    c, s = twc_ref[k, piece(0, r0), :], tws_ref[k, piece(0, r0), :]
                    if fi[k] is None:
                        zr, zi = fr[k] * c, -(fr[k] * s)
                    else:
                        zr, zi = fr[k] * c + fi[k] * s, fi[k] * c - fr[k] * s
                z_scr[k, piece(0, r0), lanes] = zr.astype(BF16)
                z_scr[k, piece(n2, r0), lanes] = zi.astype(BF16)
        return carry

    lax.fori_loop(0, n_pieces, fwd_piece, 0)

    def slab(k):
        x = jnp.dot(m_ref[...], z_scr[k], preferred_element_type=F32)
        xr, xi = x[:n2], x[n2:]
        bins = pl.ds(pl.multiple_of(k * n2, n2), n2)
        kr, ki = kr_ref[bins, :], ki_ref[bins, :]
        y = jnp.concatenate([xr * kr - xi * ki, xr * ki + xi * kr], axis=0).astype(BF16)
        v_cur[k] = jnp.dot(mt_ref[...], y, preferred_element_type=F32)

    def inv_piece(i, carry):
        slab(i)
        r0 = i * rows
        for hh in range(w // LANES):
            lanes = slice(hh * LANES, (hh + 1) * LANES)
            br, bi = [], []
            for k in range(FFT_BINS):
                vr, vi = v_prev[k, piece(0, r0), lanes], v_prev[k, piece(n2, r0), lanes]
                if k == 0:
                    br.append(vr * (0.5 / l))
                    bi.append(None)
                else:
                    c, s = twci_ref[k, piece(0, r0), :], twsi_ref[k, piece(0, r0), :]
                    br.append(vr * c - vi * s)
                    bi.append(None if k == FFT_BINS - 1 else vr * s + vi * c)
            y = _inv16(br, bi)
            for n1 in range(FFT_RADIX // 2):
                rws = piece(n1 * n2, r0)
                o_ref[rws, lanes] = g_prev[rws, lanes] * (y[n1] + u_prev[rws, lanes] * bias_ref[:, lanes])
        return carry

    for i in range(n_pieces):
        inv_piece(i, 0)
    for k in range(n_pieces, FFT_BINS):
        slab(k)


def _hy_conv(sig, sig_col, hy, gate_col, cw, cb, spec, order, bias, tabs):
    b, l, _ = hy.shape
    w = GROUP_W
    n2 = 2 * l // FFT_RADIX
    rows = max(FFT_PIECE_ROWS, n2 // (FFT_BINS - 1))
    rows = min(rows, n2)
    kr, ki = spec
    sig_arr = hy if sig_col is not None else sig
    sig_blk = sig_col if sig_col is not None else 0
    seq = lambda col: pl.BlockSpec((None, l, w), lambda t: (jnp.minimum(t, b - 1), 0, col))
    spec_blk = pl.BlockSpec((FFT_BINS * n2, w), lambda t: (0, order), pipeline_mode=pl.Buffered(1))
    kernel = functools.partial(_hy_conv_kernel, sig_col=sig_col, gate_col=gate_col, rows=rows)
    stage = [pltpu.VMEM((l, w), F32), pltpu.VMEM((l, w), F32), pltpu.VMEM((FFT_BINS, 2 * n2, w), F32)]
    return pl.pallas_call(
        kernel,
        grid=(b + 1,),
        in_specs=[seq(sig_blk), seq(gate_col), _const_spec(cw), _const_spec(cb), spec_blk, spec_blk,
                  _const_spec(bias)] + [_const_spec(t) for t in tabs],
        out_specs=pl.BlockSpec((None, l, w), lambda t: (jnp.maximum(t - 1, 0), 0, 0)),
        out_shape=jax.ShapeDtypeStruct((b, l, w), F32),
        scratch_shapes=stage + stage + [pltpu.VMEM((FFT_BINS, 2 * n2, w), BF16)],
        compiler_params=_params("arbitrary"),
        name="hy_conv",
    )(sig_arr, hy, cw, cb, kr, ki, bias, *tabs)


@functools.lru_cache(maxsize=None)
def _fft_tables(l):
    n = 2 * l
    n2 = n // FFT_RADIX
    j = np.arange(n2, dtype=np.int64)
    k1 = np.arange(FFT_BINS, dtype=np.int64)
    f32 = lambda a: np.ascontiguousarray(a, dtype=np.float32)
    b16 = lambda a: f32(a).astype(BF16)
    ang2 = ((j[:, None] * j[None, :]) % n2).astype(np.float64) * (2.0 * math.pi / n2)
    c2, s2 = np.cos(ang2), np.sin(ang2)
    m = np.block([[c2, s2], [-s2, c2]])
    angt = (k1[:, None] * j[None, :]).astype(np.float64) * (2.0 * math.pi / n)
    wk = np.where((k1 == 0) | (k1 == FFT_BINS - 1), 1.0, 2.0)[:, None] / n
    lane = lambda a: f32(np.broadcast_to(a[:, :, None], a.shape + (LANES,)))
    bins = (k1[:, None] + FFT_RADIX * j[None, :]).reshape(-1)
    angf = ((bins[:, None] * np.arange(l, dtype=np.int64)[None, :]) % n).astype(np.float64) * (2.0 * math.pi / n)
    conv_tabs = (b16(m), b16(m.T), lane(np.cos(angt)), lane(np.sin(angt)),
                 lane(np.cos(angt) * wk), lane(np.sin(angt) * wk))
    def split(a):
        hi = b16(a)
        return hi, b16(a - hi.astype(np.float64))

    return split(np.cos(angf)) + split(np.sin(angf)), conv_tabs


@functools.lru_cache(maxsize=None)
def _hy_features(l):
    f = np.float32
    t = np.linspace(0.0, 1.0, l, dtype=f)[:, None]
    wpos = (f(2.0 * math.pi) * np.arange(l, dtype=f)[:, None] / f(l)).astype(f)
    fb = np.linspace(1e-4, HY_BANDS - 1, HY_BANDS, dtype=f)[None, :]
    arg = (fb * wpos).astype(f).astype(np.float64)
    z = np.concatenate([t, np.cos(arg), -np.sin(arg), np.zeros((l, HY_FFN - HY_EMB))], axis=-1).astype(f)
    deltas = np.linspace(math.log(HY_TARGET) / HY_FAST_PCT, math.log(HY_TARGET) / HY_SLOW_PCT, HY_NFILT, dtype=f)
    return z, t, np.abs(deltas)[None, :]


def _hyena(hy, feats, spec_tabs, conv_tabs, cw, cb, w1, b1, w2, b2, w3, b3, fr, w4, bias):
    l = hy.shape[1]
    z, t, dl = feats
    w1p = jnp.pad(w1, ((0, HY_FFN - HY_EMB), (0, 0)))
    h = _hy_filters(l, z, t, w1p, b1[None], w2, b2[None], w3, b3[None], fr[None], w4, dl)
    spec = _hy_spectrum(h, spec_tabs)
    cb2 = cb[None]
    zz = _hy_conv(None, 0, hy, 1, cw, cb2, spec, 0, bias[0:1], conv_tabs)
    return _hy_conv(zz, None, hy, 2, cw, cb2, spec, 1, bias[1:2], conv_tabs)


@functools.lru_cache(maxsize=None)
def _rope_tables(l, dim, width, off, reps):
    f = np.float32
    rows = np.repeat(np.arange(l // GRID_W), GRID_W).astype(f)
    cols = np.tile(np.arange(GRID_W), l // GRID_W).astype(f)
    quarter = dim // 4
    half = dim // 2
    inv = (f(ROPE_THETA) ** (-np.arange(quarter, dtype=f) / f(quarter))).astype(f)
    ang = np.concatenate([rows[:, None] * inv, cols[:, None] * inv], axis=-1).astype(f).astype(np.float64)
    c, s = np.cos(ang), np.sin(ang)
    cf = np.ones((l, width))
    s1 = np.zeros((l, width))
    s2 = np.zeros((l, width))
    cf[:, off:off + dim] = np.concatenate([c, c], axis=-1)
    s1[:, off:off + half] = -s
    s2[:, off + half:off + dim] = s
    return tuple(np.tile(a, (1, reps)).astype(f) for a in (cf, s1, s2))


def _attn_kernel(*refs, reps, n_src):
    per = 1 + 2 * n_src
    n_att = len(reps)
    tq = refs[0].shape[1]
    lo = lax.broadcasted_iota(jnp.int32, (tq, LANES), 1) < HEAD_D
    nt = (((1,), (1,)), ((), ()))
    n_heads = refs[0].shape[0]
    work = [(a, h) for a in range(n_att) for h in range(n_heads)]

    def scores(item):
        a, h = item
        q_ref, ks = refs[a * per], refs[a * per + 1:(a + 1) * per:2]
        return [lax.dot_general(q_ref[h], k_ref[h // reps[a]], nt, preferred_element_type=F32) for k_ref in ks]

    def weighted(item, ss):
        a, h = item
        vs = refs[a * per + 2:(a + 1) * per:2]
        m = ss[0].max(axis=-1, keepdims=True)
        for s in ss[1:]:
            m = jnp.maximum(m, s.max(axis=-1, keepdims=True))
        acc = 0.0
        for s, v_ref in zip(ss, vs):
            acc = acc + jnp.dot(jnp.exp2(s - m).astype(BF16), v_ref[h], preferred_element_type=F32)
        return acc / pltpu.roll(acc, HEAD_D, 1)

    outs = []
    ss_next = scores(work[0])
    for i, item in enumerate(work):
        ss = ss_next
        if i + 1 < len(work):
            ss_next = scores(work[i + 1])
        outs.append(weighted(item, ss))
    for a in range(n_att):
        o_ref = refs[n_att * per + a]
        for p in range(n_heads // 2):
            pair = outs[a * n_heads + 2 * p], outs[a * n_heads + 2 * p + 1]
            o_ref[:, p * LANES:(p + 1) * LANES] = jnp.where(lo, pair[0], pair[1])


def _attention(atts, b, l):
    tq = min(ATTN_TILE, l)
    nq = l // tq
    specs, args = [], []
    for q, kvs, _ in atts:
        specs.append(pl.BlockSpec((q.shape[0], tq, LANES), lambda i, j: (0, i * nq + j, 0)))
        args.append(q)
        for k, v, lk in kvs:
            specs += [pl.BlockSpec((k.shape[0], lk, LANES), lambda i, j: (0, i, 0)),
                      pl.BlockSpec((v.shape[0], lk, LANES), lambda i, j: (0, i, 0))]
            args += [k, v]
    out = jax.ShapeDtypeStruct((b, l, GROUP_W), F32)
    return pl.pallas_call(
        functools.partial(_attn_kernel, reps=tuple(rep for _, _, rep in atts), n_src=len(atts[0][1])),
        grid=(b, nq),
        in_specs=specs,
        out_specs=[pl.BlockSpec((None, tq, GROUP_W), lambda i, j: (i, j, 0))] * len(atts),
        out_shape=[out] * len(atts),
        compiler_params=_params("parallel", "parallel"),
        name="attention",
    )(*args)


def _ret_kernel(qs, ks, v_ref, g_ref, dl_ref, dc_ref, ng_ref, seg_ref, sf0_ref, sb0_ref,
                y_ref, sf_ref, sb_ref, sball, dec_scr, cdec_scr, dmat_scr, *, chunk, unroll):
    c_ = chunk
    w = GROUP_W
    n_chunks = qs.shape[0] // c_
    unroll = min(unroll, n_chunks)
    shift_c = int(math.log2(c_))
    shift_h = int(math.log2(HEAD_D))
    tn = (((0,), (0,)), ((), ()))
    nt = (((1,), (1,)), ((), ()))

    @pl.when(pl.program_id(0) == 0)
    def _():
        lgf = _log_sigmoid(dl_ref[0:1, :])
        lgb = _log_sigmoid(dl_ref[1:2, :])
        ii = lax.broadcasted_iota(jnp.int32, (c_, 1), 0).astype(F32)
        dec_scr[0] = jnp.exp((ii + 1.0) * lgf)
        dec_scr[1] = jnp.exp((c_ - 1.0 - ii) * lgf)
        dec_scr[2] = jnp.exp((c_ - ii) * lgb)
        dec_scr[3] = jnp.exp(ii * lgb)
        cdec_scr[0] = jnp.exp(float(c_) * lgf)
        cdec_scr[1] = jnp.exp(float(c_) * lgb)
        lgc = _log_sigmoid(dc_ref[...])
        i4 = jnp.bitwise_and(lax.broadcasted_iota(jnp.int32, (4 * c_, c_), 0), c_ - 1)
        j4 = lax.broadcasted_iota(jnp.int32, (4 * c_, c_), 1)
        diff = (i4 - j4).astype(F32)
        dmat_scr[...] = (jnp.where(diff >= 0, jnp.exp(lgc[:, 0:1] * jnp.maximum(diff, 0.0)), 0.0)
                         + jnp.where(diff <= 0, jnp.exp(lgc[:, 1:2] * jnp.maximum(-diff, 0.0)), 0.0))

    qdf, kdf, qdb, kdb = dec_scr[0], dec_scr[1], dec_scr[2], dec_scr[3]
    cdf, cdb = cdec_scr[0], cdec_scr[1]
    dmat = dmat_scr[...]
    r_h = lax.shift_right_logical(lax.broadcasted_iota(jnp.int32, (w, w), 0), shift_h)
    c_h = lax.shift_right_logical(lax.broadcasted_iota(jnp.int32, (w, w), 1), shift_h)
    blockdiag = r_h == c_h
    row_h = lax.shift_right_logical(lax.broadcasted_iota(jnp.int32, (4 * c_, w), 0), shift_c)
    lane_h = lax.shift_right_logical(lax.broadcasted_iota(jnp.int32, (4 * c_, w), 1), shift_h)
    headmask = row_h == lane_h
    seg = seg_ref[...]

    def rows(c):
        return pl.ds(pl.multiple_of(c * c_, c_), c_)

    def kv_outer(k, v, dec):
        a = lax.dot_general((k * dec).astype(BF16), v.astype(BF16), tn, preferred_element_type=F32)
        return jnp.where(blockdiag, a, 0.0)

    def bwd(t, s):
        c = n_chunks - 1 - t
        sball[c] = s
        sl = rows(c)
        return s * cdb + kv_outer(ks[sl, :], v_ref[sl, :], kdb)

    sb_ref[...] = lax.fori_loop(0, n_chunks, bwd, sb0_ref[...], unroll=unroll)

    def fwd(c, s):
        sl = rows(c)
        q, k, v = qs[sl, :], ks[sl, :], v_ref[sl, :]
        vb = v.astype(BF16)
        q4 = jnp.where(headmask, jnp.concatenate([q, q, q, q], axis=0), 0.0).astype(BF16)
        sc = lax.dot_general(q4, k.astype(BF16), nt, preferred_element_type=F32)
        o4 = jnp.dot((sc * dmat).astype(BF16), vb, preferred_element_type=F32)
        o4 = jnp.where(headmask, o4, 0.0)
        o = o4[0:c_] + o4[c_:2 * c_] + o4[2 * c_:3 * c_] + o4[3 * c_:4 * c_]
        o = o + jnp.dot((q * qdf).astype(BF16), s.astype(BF16), preferred_element_type=F32)
        o = o + jnp.dot((q * qdb).astype(BF16), sball[c].astype(BF16), preferred_element_type=F32)
        on = o * lax.rsqrt(_seg_mean(o * o, seg) + EPS) * ng_ref[...]
        g = g_ref[sl, :]
        y_ref[sl, :] = on * (g * jax.nn.sigmoid(g))
        return s * cdf + kv_outer(k, v, kdf)

    sf_ref[...] = lax.fori_loop(0, n_chunks, fwd, sf0_ref[...], unroll=unroll)


def _retention(ret, decay, ng, seg, sf0, sb0):
    b, l, _ = ret.shape
    w = GROUP_W
    chunk = min(RET_CHUNK, l)
    dl = jnp.repeat(decay, HEAD_D, axis=1)
    dc = jnp.repeat(decay.T, chunk, axis=0)
    col = lambda j: pl.BlockSpec((None, l, w), lambda i: (i, 0, j))
    state = pl.BlockSpec((None, w, w), lambda i: (i, 0, 0))
    sshape = jax.ShapeDtypeStruct((b, w, w), F32)
    return pl.pallas_call(
        functools.partial(_ret_kernel, chunk=chunk, unroll=RET_UNROLL),
        grid=(b,),
        in_specs=[col(0), col(1), col(2), col(3), _const_spec(dl), _const_spec(dc), _const_spec(ng),
                  _const_spec(seg), state, state],
        out_specs=[pl.BlockSpec((None, l, w), lambda i: (i, 0, 0)), state, state],
        out_shape=[jax.ShapeDtypeStruct((b, l, w), F32), sshape, sshape],
        scratch_shapes=[pltpu.VMEM((l // chunk, w, w), F32), pltpu.VMEM((4, chunk, w), F32),
                        pltpu.VMEM((2, 1, w), F32), pltpu.VMEM((4 * chunk, chunk), F32)],
        compiler_params=_params("arbitrary"),
        name="retention",
    )(ret, ret, ret, ret, dl, dc, ng, seg, sf0, sb0)


def _mlp_kernel(x_ref, ya_ref, yb_ref, yc_ref, yd_ref, gout_ref, ga_ref, wout_ref,
                gffn_ref, sh_ref, sc_ref, gf_ref, wfi_ref, wfo_ref, fg_ref, o_ref,
                x1_a, h_a, x1_b, h_b, *, final_norm):
    t = pl.program_id(0)
    nff = D_FF // FF_TILE

    @pl.when(t == 0)
    def _():
        x1_b[...] = jnp.zeros_like(x1_b)
        h_b[...] = jnp.zeros_like(h_b)

    def step(x1_w, h_w, x1_r, h_r):
        acc = 0.0
        for i, y_ref in enumerate((ya_ref, yb_ref, yc_ref, yd_ref)):
            rows = slice(i * GROUP_W, (i + 1) * GROUP_W)
            yn = (_rms(y_ref[...]) * gout_ref[:, rows]).astype(BF16)
            acc = acc + jnp.dot(yn, wout_ref[rows, :], preferred_element_type=F32)
        x1 = x_ref[...] + ga_ref[...] * acc
        x1_w[...] = x1
        h_w[...] = (_rms(x1) * gffn_ref[...] * (1.0 + sc_ref[...]) + sh_ref[...]).astype(BF16)

        h = h_r[...]
        ff = 0.0
        for c in range(nff):
            cols = slice(c * FF_TILE, (c + 1) * FF_TILE)
            gate_cols = slice(D_FF + c * FF_TILE, D_FF + (c + 1) * FF_TILE)
            a = jnp.dot(h, wfi_ref[:, cols], preferred_element_type=F32)
            g = jnp.dot(h, wfi_ref[:, gate_cols], preferred_element_type=F32)
            u = (a * jax.nn.sigmoid(a) * g).astype(BF16)
            ff = ff + jnp.dot(u, wfo_ref[cols, :], preferred_element_type=F32)
        out = x1_r[...] + gf_ref[...] * ff
        if final_norm:
            out = _rms(out) * fg_ref[...]
        o_ref[...] = out

    @pl.when(t % 2 == 0)
    def _():
        step(x1_a, h_a, x1_b, h_b)

    @pl.when(t % 2 == 1)
    def _():
        step(x1_b, h_b, x1_a, h_a)


def _mlp(x, ys, g_out, g_a, w_out, g_ffn, sh, sc, g_f, wfi, wfo, final_g, final_norm):
    b, l, d = x.shape
    tm = min(MLP_TILE, l)
    nt = l // tm
    n = b * nt
    cur = lambda t: jnp.minimum(t, n - 1)
    prv = lambda t: jnp.maximum(t - 1, 0)
    tok = lambda wd: pl.BlockSpec((None, tm, wd), lambda t: (cur(t) // nt, cur(t) % nt, 0))
    vec = pl.BlockSpec((None, 1, d), lambda t: (cur(t) // nt, 0, 0))
    return pl.pallas_call(
        functools.partial(_mlp_kernel, final_norm=final_norm),
        grid=(n + 1,),
        in_specs=[tok(d)] + [tok(GROUP_W)] * 4 + [
            _const_spec(g_out), vec, _const_spec(w_out), _const_spec(g_ffn), vec, vec,
            pl.BlockSpec((None, 1, d), lambda t: (prv(t) // nt, 0, 0)),
            _const_spec(wfi), _const_spec(wfo), _const_spec(final_g)],
        out_specs=pl.BlockSpec((None, tm, d), lambda t: (prv(t) // nt, prv(t) % nt, 0)),
        out_shape=jax.ShapeDtypeStruct((b, l, d), F32),
        scratch_shapes=[pltpu.VMEM((tm, d), F32), pltpu.VMEM((tm, d), BF16),
                        pltpu.VMEM((tm, d), F32), pltpu.VMEM((tm, d), BF16)],
        compiler_params=_params("arbitrary"),
        name="mlp",
    )(x, *ys, g_out, g_a, w_out, g_ffn, sh, sc, g_f, wfi, wfo, final_g)


def _pad_proj(w):
    end = PROJ_OFFSETS[2] + MLA_KV_RANK + MLA_ROPE
    z = jnp.zeros((w.shape[0], LANES - MLA_ROPE), w.dtype)
    return jnp.concatenate([w[:, :end], z, w[:, end:]], axis=1).astype(BF16)


def _mla_weights(wuq, wukv):
    r = wuq.shape[0]
    q = wuq.reshape(r, 4, MLA_NOPE + MLA_ROPE)
    q = jnp.pad(q, ((0, 0), (0, 0), (0, LANES - MLA_NOPE - MLA_ROPE))).reshape(r, 4 * LANES)
    kv = wukv.reshape(wukv.shape[0], 4, MLA_NOPE + MLA_V)
    k = jnp.pad(kv[..., :MLA_NOPE], ((0, 0), (0, 0), (0, LANES - MLA_NOPE))).reshape(-1, 4 * LANES)
    v = jnp.concatenate([kv[..., MLA_NOPE:], kv[..., MLA_NOPE:]], axis=-1).reshape(-1, 4 * LANES)
    return q.astype(BF16), k.astype(BF16), v.astype(BF16)


def _seg_matrix(width):
    i = jnp.arange(width) // HEAD_D
    return jnp.where(i[:, None] == i[None, :], 1.0 / HEAD_D, 0.0).astype(BF16)


def kernel(x, c, ctx, c_ctx, w_mod, b_mod, norm_attn_g, norm_ffn_g, w_in, hy_conv_w, hy_conv_b, hy_w1, hy_b1, hy_w2, hy_b2, hy_w3, hy_b3, hy_freq, hy_w4, hy_bias, mla_q_norm_g, mla_w_uq, mla_kv_norm_g, mla_w_ukv, ret_decay, ret_norm_g, gqa_q_norm_g, gqa_k_norm_g, out_norm_g, w_out, w_ffn_in, w_ffn_out, final_norm_g):
    b, n_lat, d = x.shape
    n_ctx = ctx.shape[1]
    depth = w_mod.shape[0]

    rope_mla = _rope_tables(n_lat, MLA_ROPE, LANES, MLA_NOPE, 1)
    rope_hd2 = _rope_tables(n_lat, HEAD_D, HEAD_D, 0, 2)
    seg2, seg4 = _seg_matrix(LANES), _seg_matrix(GROUP_W)
    hy_consts = {n: (_hy_features(n),) + _fft_tables(n) for n in {n_lat, n_ctx}}
    zero_state = jnp.zeros((b, GROUP_W, GROUP_W), F32)

    rows = -(-(b + 1) // 8) * 8
    cond = jnp.zeros((rows, d), F32).at[:b].set(c).at[b].set(c_ctx)
    mod = _ada_mod(cond, w_mod, b_mod)

    xl, xc = x, ctx.reshape(1, b * n_ctx, d)
    for l in range(depth):
        update_ctx = l < depth - 1
        m_lat = [m[:b, None, :] for m in jnp.split(mod[l], 6, axis=-1)]
        m_ctx = [m[b][None, None, :] for m in jnp.split(mod[l], 6, axis=-1)]
        wp = _pad_proj(w_in[l])
        g_attn = norm_attn_g[l][None]
        mla_w = (mla_q_norm_g[l][None], mla_kv_norm_g[l][None]) + _mla_weights(mla_w_uq[l], mla_w_ukv[l])
        gqa_w = (jnp.tile(gqa_q_norm_g[l], 2)[None], jnp.tile(gqa_k_norm_g[l], 2)[None], seg2)
        hy_l, ret_l, q_l, k_l, v_l, gq_l, gk_l, gv_l = _front(
            xl, g_attn, m_lat[0], m_lat[1], wp, mla_w, gqa_w, (rope_mla, rope_hd2))
        hy_c, ret_c, q_c, k_c, v_c, gq_c, gk_c, gv_c = _front(
            xc, g_attn, m_ctx[0], m_ctx[1], wp, mla_w, gqa_w, None)
        hy_c = hy_c.reshape(b, n_ctx, -1)
        ret_c = ret_c.reshape(b, n_ctx, -1)

        hy_args = (hy_conv_w[l], hy_conv_b[l], hy_w1[l], hy_b1[l], hy_w2[l], hy_b2[l], hy_w3[l], hy_b3[l],
                   hy_freq[l], hy_w4[l], hy_bias[l])
        ya_l = _hyena(hy_l, *hy_consts[n_lat], *hy_args)
        yb_l, yd_l = _attention([(q_l, [(k_c, v_c, n_ctx), (k_l, v_l, n_lat)], 1),
                                 (gq_l, [(gk_c, gv_c, n_ctx), (gk_l, gv_l, n_lat)], 2)], b, n_lat)

        ng = ret_norm_g[l][None]
        yc_c, s_f, s_b = _retention(ret_c, ret_decay[l], ng, seg4, zero_state, zero_state)
        yc_l, _, _ = _retention(ret_l, ret_decay[l], ng, seg4, s_f, s_b)

        wo = w_out[l].astype(BF16)
        wfi, wfo = w_ffn_in[l].astype(BF16), w_ffn_out[l].astype(BF16)
        g_out, g_ffn, g_fin = out_norm_g[l][None], norm_ffn_g[l][None], final_norm_g[None]
        xl = _mlp(xl, (ya_l, yb_l, yc_l, yd_l), g_out, m_lat[2], wo, g_ffn, m_lat[3], m_lat[4], m_lat[5],
                  wfi, wfo, g_fin, l == depth - 1)

        if update_ctx:
            flat = lambda a: a.reshape(1, b * n_ctx, GROUP_W)
            ya_c = _hyena(hy_c, *hy_consts[n_ctx], *hy_args)
            yb_c, yd_c = _attention([(q_c, [(k_c, v_c, n_ctx)], 1), (gq_c, [(gk_c, gv_c, n_ctx)], 2)], b, n_ctx)
            xc = _mlp(xc, tuple(flat(a) for a in (ya_c, yb_c, yc_c, yd_c)), g_out, m_ctx[2], wo,
                      g_ffn, m_ctx[3], m_ctx[4], m_ctx[5], wfi, wfo, g_fin, False)

    return xl
```

```python
import functools
import math

import numpy as np
import jax
import jax.numpy as jnp
from jax import lax
from jax.experimental import pallas as pl
from jax.experimental.pallas import tpu as pltpu

F32 = jnp.float32
BF16 = jnp.bfloat16
HI = lax.Precision.HIGHEST

D_MODEL = 1024
GRID_W = 64
EPS = 1e-6
ROPE_THETA = 10000.0
GROUP_W = 256
LANES = 128

HY_EMB = 33
HY_BANDS = 16
HY_FFN = 64
HY_NFILT = 1024
HY_DECAY_SHIFT = 0.05
HY_FAST_PCT = 0.3
HY_SLOW_PCT = 1.5
HY_TARGET = 1e-2
FFT_RADIX = 16
FFT_BINS = FFT_RADIX // 2 + 1
FFT_PIECE_ROWS = 32

MLA_NOPE = 64
MLA_ROPE = 32
MLA_V = 64
MLA_KV_RANK = 128
LOG2E = math.log2(math.e)
MLA_SCALE = (MLA_NOPE + MLA_ROPE) ** -0.5 * LOG2E

HEAD_D = 64
RET_CHUNK = 256
RET_UNROLL = 8
GQA_SCALE = HEAD_D ** -0.5 * LOG2E
RET_K_SCALE = HEAD_D ** -0.5

D_FF = 2816
FF_TILE = 256

PROJ_SPLITS = (768, 256, 256, 1024, 512)
PROJ_OFFSETS = (0, 768, 1024, 1280, 2304, 2816)

V7X_VMEM_BYTES = 64 * 1024 * 1024
VMEM_LIMIT = V7X_VMEM_BYTES - 8 * 1024 * 1024

FRONT_TILE = 1024
ATTN_TILE = 1024
MLP_TILE = 512


def _params(*sem):
    return pltpu.CompilerParams(dimension_semantics=sem, vmem_limit_bytes=VMEM_LIMIT)


def _const_spec(a):
    nd = a.ndim
    return pl.BlockSpec(a.shape, lambda *_: (0,) * nd, pipeline_mode=pl.Buffered(1))


def _rms(x):
    return x * lax.rsqrt(jnp.mean(x * x, axis=-1, keepdims=True) + EPS)


def _rope(x, cf, s1, s2, half):
    w = x.shape[-1]
    return x * cf + pltpu.roll(x, w - half, 1) * s1 + pltpu.roll(x, half, 1) * s2


def _seg_mean(sq, seg):
    hi = sq.astype(BF16)
    lo = (sq - hi.astype(F32)).astype(BF16)
    return (jnp.dot(hi, seg, preferred_element_type=F32) + jnp.dot(lo, seg, preferred_element_type=F32))


def _log_sigmoid(x):
    return -(jnp.maximum(-x, 0.0) + jnp.log(1.0 + jnp.exp(-jnp.abs(x))))


def _ada_kernel(c_ref, w_ref, b_ref, o_ref):
    c = c_ref[...]
    s = c * jax.nn.sigmoid(c)
    o_ref[...] = jnp.dot(s, w_ref[...], precision=HI, preferred_element_type=F32) + b_ref[...]


def _ada_mod(cond, w_mod, b_mod):
    depth, d, n = w_mod.shape
    r = cond.shape[0]
    tn = 1024
    return pl.pallas_call(
        _ada_kernel,
        grid=(depth, n // tn),
        in_specs=[pl.BlockSpec((r, d), lambda l, j: (0, 0)),
                  pl.BlockSpec((None, d, tn), lambda l, j: (l, 0, j)),
                  pl.BlockSpec((None, 1, tn), lambda l, j: (l, 0, j))],
        out_specs=pl.BlockSpec((None, r, tn), lambda l, j: (l, 0, j)),
        out_shape=jax.ShapeDtypeStruct((depth, r, n), F32),
        compiler_params=_params("arbitrary", "arbitrary"),
        name="ada_mod",
    )(cond, w_mod, b_mod.reshape(depth, 1, n))


def _value_with_ones(v2, head):
    lo = lax.broadcasted_iota(jnp.int32, v2.shape, 1) < HEAD_D
    keep = lo if head % 2 == 0 else jnp.logical_not(lo)
    return jnp.where(keep, v2, 1.0).astype(BF16)


def _mla_heads(cq, ckvr, gq, gkv, wuq_ref, wk_ref, wv_ref, tables, q_ref, k_ref, v_ref):
    half = MLA_ROPE // 2
    qn = (_rms(cq) * gq).astype(BF16)
    q = jnp.dot(qn, wuq_ref[...], preferred_element_type=F32)
    kvn = (_rms(ckvr[:, :MLA_KV_RANK]) * gkv).astype(BF16)
    k = jnp.dot(kvn, wk_ref[...], preferred_element_type=F32)
    v = jnp.dot(kvn, wv_ref[...], preferred_element_type=F32)
    kpe = pltpu.roll(ckvr[:, MLA_KV_RANK:], MLA_NOPE, 1)
    if tables is not None:
        kpe = _rope(kpe, *tables, half)
    for h in range(q_ref.shape[0]):
        cols = slice(h * LANES, (h + 1) * LANES)
        qh = q[:, cols]
        if tables is not None:
            qh = _rope(qh, *tables, half)
        q_ref[h] = (qh * MLA_SCALE).astype(BF16)
        k_ref[h] = (k[:, cols] + kpe).astype(BF16)
        v_ref[h] = _value_with_ones(v[:, cols], h)


def _gqa_heads(x, gq, gk, seg, tables, q_ref, k_ref, v_ref):
    half = HEAD_D // 2
    lo = lax.broadcasted_iota(jnp.int32, (x.shape[0], LANES), 1) < HEAD_D

    def normed(a, g):
        a = a * lax.rsqrt(_seg_mean(a * a, seg) + EPS) * g
        return _rope(a, *tables, half) if tables is not None else a

    def dup(a, g):
        a = jnp.where(lo if g == 0 else jnp.logical_not(lo), a, 0.0)
        return a + pltpu.roll(a, HEAD_D, 1)

    for p in range(2):
        qb = normed(x[:, p * LANES:(p + 1) * LANES], gq) * GQA_SCALE
        q_ref[2 * p] = jnp.where(lo, qb, 0.0).astype(BF16)
        q_ref[2 * p + 1] = jnp.where(lo, 0.0, qb).astype(BF16)
    kb = normed(x[:, 2 * LANES:3 * LANES], gk)
    vb = x[:, 3 * LANES:4 * LANES]
    for g in range(2):
        k_ref[g] = dup(kb, g).astype(BF16)
        for r in range(2):
            v_ref[2 * g + r] = _value_with_ones(dup(vb, g), 2 * g + r)


def _front_kernel(x_ref, g_ref, sh_ref, sc_ref, w_ref, gq_ref, gkv_ref, wuq_ref, wk_ref, wv_ref,
                  ggq_ref, ggk_ref, seg_ref, *rest, rope):
    if rope:
        mla_t = tuple(r[...] for r in rest[0:3])
        hd_t = tuple(r[...] for r in rest[3:6])
        rest = rest[6:]
    else:
        mla_t = hd_t = None
    hy_ref, ret_ref, mq_ref, mk_ref, mv_ref, gq_out, gk_out, gv_out = rest
    y = _rms(x_ref[...]) * g_ref[...]
    h = (y * (1.0 + sc_ref[...]) + sh_ref[...]).astype(BF16)
    o0, o1, o2, o3, o4, o5 = PROJ_OFFSETS

    def proj(a, b):
        return jnp.dot(h, w_ref[:, a:b], preferred_element_type=F32)

    _mla_heads(proj(o1, o2), proj(o2, o3), gq_ref[...], gkv_ref[...], wuq_ref, wk_ref, wv_ref, mla_t,
               mq_ref, mk_ref, mv_ref)
    _gqa_heads(proj(o4, o5), ggq_ref[...], ggk_ref[...], seg_ref[...], hd_t, gq_out, gk_out, gv_out)
    for i in range(2 * GROUP_W // LANES):
        cols = slice(i * LANES, (i + 1) * LANES)
        a = proj(o3 + i * LANES, o3 + (i + 1) * LANES)
        if i * LANES >= GROUP_W:
            a = a * RET_K_SCALE
        ret_ref[:, cols] = _rope(a, *hd_t, HEAD_D // 2) if rope else a
    ret_ref[:, 2 * GROUP_W:] = proj(o3 + 2 * GROUP_W, o4)
    hy_ref[...] = proj(o0, o1)


def _front(x, g, sh, sc, w, mla_w, gqa_w, tables):
    b, l, d = x.shape
    tm = min(FRONT_TILE, l)
    nt = l // tm
    rope = tables is not None
    tok = lambda wd: pl.BlockSpec((None, tm, wd), lambda i, j: (i, j, 0))
    vec = pl.BlockSpec((None, 1, d), lambda i, j: (i, 0, 0))
    head = lambda nh: pl.BlockSpec((nh, tm, LANES), lambda i, j: (0, i * nt + j, 0))
    hshape = lambda nh: jax.ShapeDtypeStruct((nh, b * l, LANES), BF16)
    consts = [w, *mla_w, *gqa_w]
    args = [x, g, sh, sc, *consts]
    specs = [tok(d), _const_spec(g), vec, vec] + [_const_spec(a) for a in consts]
    if rope:
        args += [*tables[0], *tables[1]]
        specs += [pl.BlockSpec((tm, LANES), lambda i, j: (j, 0))] * 6
    return pl.pallas_call(
        functools.partial(_front_kernel, rope=rope),
        grid=(b, nt),
        in_specs=specs,
        out_specs=[tok(PROJ_SPLITS[0]), tok(PROJ_SPLITS[3]), head(4), head(4), head(4), head(4), head(2), head(4)],
        out_shape=[jax.ShapeDtypeStruct((b, l, PROJ_SPLITS[0]), F32),
                   jax.ShapeDtypeStruct((b, l, PROJ_SPLITS[3]), F32),
                   hshape(4), hshape(4), hshape(4), hshape(4), hshape(2), hshape(4)],
        compiler_params=_params("parallel", "parallel"),
        name="front",
    )(*args)


def _hy_filter_kernel(z_ref, t_ref, w1_ref, b1_ref, w2_ref, b2_ref, w3_ref, b3_ref, fr_ref,
                      w4_ref, dl_ref, o_ref, h_scr):
    dot = functools.partial(jnp.dot, precision=HI, preferred_element_type=F32)

    @pl.when(pl.program_id(0) == 0)
    def _():
        fr = fr_ref[...]
        h = jnp.sin(fr * (dot(z_ref[...], w1_ref[...]) + b1_ref[...]))
        h = jnp.sin(fr * (dot(h, w2_ref[...]) + b2_ref[...]))
        h_scr[...] = jnp.sin(fr * (dot(h, w3_ref[...]) + b3_ref[...]))

    filt = dot(h_scr[...], w4_ref[...]) * (jnp.exp(-t_ref[...] * dl_ref[...]) + HY_DECAY_SHIFT)
    fwd = filt[:, :GROUP_W]
    bwd = filt[:, GROUP_W:]
    row = lax.broadcasted_iota(jnp.int32, bwd.shape, 0)
    bwd = jnp.where(row == 0, 0.0, bwd)
    nrm = (jnp.sum(jnp.abs(fwd), axis=0, keepdims=True) + jnp.sum(jnp.abs(bwd), axis=0, keepdims=True))
    o_ref[:, :GROUP_W] = fwd / nrm
    o_ref[:, GROUP_W:] = bwd / nrm


def _hy_filters(l, z, t, w1, b1, w2, b2, w3, b3, fr, w4, dl):
    full = lambda a: pl.BlockSpec(a.shape, lambda o: (0, 0))
    return pl.pallas_call(
        _hy_filter_kernel,
        grid=(2,),
        in_specs=[full(z), full(t), full(w1), full(b1), full(w2), full(b2), full(w3), full(b3), full(fr),
                  pl.BlockSpec((HY_FFN, 2 * GROUP_W), lambda o: (0, o)),
                  pl.BlockSpec((1, 2 * GROUP_W), lambda o: (0, o))],
        out_specs=pl.BlockSpec((l, 2 * GROUP_W), lambda o: (0, o)),
        out_shape=jax.ShapeDtypeStruct((l, HY_NFILT), F32),
        scratch_shapes=[pltpu.VMEM((l, HY_FFN), F32)],
        compiler_params=_params("arbitrary"),
        name="hy_filters",
    )(z, t, w1, b1, w2, b2, w3, b3, fr, w4, dl)


def _split_bf16(x):
    hi = x.astype(BF16)
    return hi, (x - hi.astype(F32)).astype(BF16)


def _dot_3pass(a_hi, a_lo, x):
    x_hi, x_lo = _split_bf16(x)
    dot = functools.partial(jnp.dot, preferred_element_type=F32)
    return dot(a_hi, x_hi) + (dot(a_hi, x_lo) + dot(a_lo, x_hi))


def _hy_spec_kernel(ch_ref, cl_ref, sh_ref, sl_ref, h_ref, re_ref, im_ref):
    for o in range(2):
        hf = h_ref[:, 2 * o * GROUP_W:(2 * o + 1) * GROUP_W]
        hb = h_ref[:, (2 * o + 1) * GROUP_W:(2 * o + 2) * GROUP_W]
        cols = slice(o * GROUP_W, (o + 1) * GROUP_W)
        re_ref[:, cols] = _dot_3pass(ch_ref[...], cl_ref[...], hf + hb)
        im_ref[:, cols] = -_dot_3pass(sh_ref[...], sl_ref[...], hf - hb)


def _hy_spectrum(h, tabs):
    rows, l = tabs[0].shape
    fb = rows // FFT_BINS if rows // FFT_BINS >= 256 else rows
    out = jax.ShapeDtypeStruct((rows, 2 * GROUP_W), F32)
    blk = pl.BlockSpec((fb, 2 * GROUP_W), lambda f: (f, 0))
    return pl.pallas_call(
        _hy_spec_kernel,
        grid=(rows // fb,),
        in_specs=[pl.BlockSpec((fb, l), lambda f: (f, 0))] * 4 + [_const_spec(h)],
        out_specs=[blk, blk],
        out_shape=[out, out],
        compiler_params=_params("arbitrary"),
        name="hy_spectrum",
    )(*tabs, h)


def _short_conv_to(dst_ref, u, w, b):
    n = u.shape[0]
    mid = w[1:2] * u + b
    dst_ref[...] = w[0:1] * pltpu.roll(u, 1, 0) + mid + w[2:3] * pltpu.roll(u, n - 1, 0)
    dst_ref[0:1, :] = mid[0:1] + w[2:3] * u[1:2]
    dst_ref[n - 1:n, :] = mid[n - 1:n] + w[0:1] * u[n - 2:n - 1]


_R2 = math.sqrt(0.5)


def _dft8_real4(x0, x1, x2, x3):
    s02, d02 = x0 + x2, x0 - x2
    s13, d13 = x1 + x3, x1 - x3
    p, q = _R2 * d13, _R2 * s13
    re = [s02 + s13, x0 + p, d02, x0 - p, s02 - s13]
    im = [None, -(q + x2), -d13, x2 - q, None]
    return re, im


def _fwd16(a):
    er, ei = _dft8_real4(a[0], a[2], a[4], a[6])
    orr, oi = _dft8_real4(a[1], a[3], a[5], a[7])

    def ext(re, im, k):
        if k <= 4:
            return re[k], im[k]
        return re[8 - k], (None if im[8 - k] is None else -im[8 - k])

    out_r, out_i = [], []
    for k in range(FFT_BINS):
        e_r, e_i = ext(er, ei, k)
        o_r, o_i = ext(orr, oi, k)
        c, s = math.cos(math.pi * k / 8), math.sin(math.pi * k / 8)
        if k == 0:
            out_r.append(e_r + o_r)
            out_i.append(None)
        elif k == 8:
            out_r.append(e_r - o_r)
            out_i.append(None)
        elif k == 4:
            out_r.append(e_r)
            out_i.append(-o_r)
        else:
            out_r.append(e_r + (c * o_r + s * o_i))
            out_i.append(e_i + (c * o_i - s * o_r))
    return out_r, out_i


def _dft8_real4_t(kr, ki):
    a = kr[0] + kr[4]
    b = kr[0] - kr[4]
    x0 = a + kr[2] + kr[1] + kr[3]
    x2 = a - kr[2] - ki[1] + ki[3]
    t1 = _R2 * (kr[1] - kr[3])
    t2 = _R2 * (ki[1] + ki[3])
    return x0, b - ki[2] + t1 - t2, x2, b + ki[2] - t1 - t2


def _inv16(br, bi):
    o_r, o_i = [None] * FFT_BINS, [None] * FFT_BINS
    for k in range(FFT_BINS):
        c, s = math.cos(math.pi * k / 8), math.sin(math.pi * k / 8)
        if k == 0:
            o_r[k] = br[k]
        elif k == 8:
            o_r[k] = -br[k]
        elif k == 4:
            o_r[k] = -bi[k]
        else:
            o_r[k] = c * br[k] - s * bi[k]
            o_i[k] = c * bi[k] + s * br[k]

    def fold(re, im):
        fr = [re[0] + re[8], re[1] + re[7], re[2] + re[6], re[3] + re[5], re[4]]
        fi = [None, im[1] - im[7], im[2] - im[6], im[3] - im[5], None]
        return fr, fi

    xe = _dft8_real4_t(*fold(br, bi))
    xo = _dft8_real4_t(*fold(o_r, o_i))
    return [xe[0], xo[0], xe[1], xo[1], xe[2], xo[2], xe[3], xo[3]]


def _hy_conv_kernel(sig_ref, gate_ref, cw_ref, cb_ref, kr_ref, ki_ref, bias_ref, m_ref, mt_ref,
                    twc_ref, tws_ref, twci_ref, twsi_ref, o_ref, u_a, g_a, v_a, u_b, g_b, v_b, z_scr,
                    *, sig_col, gate_col, rows):
    t = pl.program_id(0)

    @pl.when(t == 0)
    def _():
        u_b[...] = jnp.zeros_like(u_b)
        g_b[...] = jnp.zeros_like(g_b)
        v_b[...] = jnp.zeros_like(v_b)

    refs = (sig_ref, gate_ref, cw_ref, cb_ref, kr_ref, ki_ref, bias_ref, m_ref, mt_ref,
            twc_ref, tws_ref, twci_ref, twsi_ref, o_ref, z_scr)

    @pl.when(t % 2 == 0)
    def _():
        _hy_conv_step(*refs, (u_a, g_a, v_a), (u_b, g_b, v_b), sig_col, gate_col, rows)

    @pl.when(t % 2 == 1)
    def _():
        _hy_conv_step(*refs, (u_b, g_b, v_b), (u_a, g_a, v_a), sig_col, gate_col, rows)


def _hy_conv_step(sig_ref, gate_ref, cw_ref, cb_ref, kr_ref, ki_ref, bias_ref, m_ref, mt_ref,
                  twc_ref, tws_ref, twci_ref, twsi_ref, o_ref, z_scr, cur, prev, sig_col, gate_col, rows):
    u_scr, g_scr, v_cur = cur
    u_prev, g_prev, v_prev = prev
    w = GROUP_W
    l = u_scr.shape[0]
    n2 = l // (FFT_RADIX // 2)
    n_pieces = n2 // rows
    if sig_col is not None:
        _short_conv_to(u_scr, sig_ref[...], cw_ref[:, sig_col * w:(sig_col + 1) * w],
                       cb_ref[:, sig_col * w:(sig_col + 1) * w])
    else:
        u_scr[...] = sig_ref[...]
    _short_conv_to(g_scr, gate_ref[...], cw_ref[:, gate_col * w:(gate_col + 1) * w],
                   cb_ref[:, gate_col * w:(gate_col + 1) * w])

    def piece(base, r0):
        return pl.ds(pl.multiple_of(base + r0, rows), rows)

    def fwd_piece(i, carry):
        r0 = i * rows
        for hh in range(w // LANES):
            lanes = slice(hh * LANES, (hh + 1) * LANES)
            fr, fi = _fwd16([u_scr[piece(n1 * n2, r0), lanes] for n1 in range(FFT_RADIX // 2)])
            for k in range(FFT_BINS):
                if k == 0:
                    zr, zi = fr[0], jnp.zeros_like(fr[0])
                else:
                    c, s = twc_ref[k, piece(0, r0), :], tws_ref[k, piece(0, r0), :]
                    if fi[k] is None:
                        zr, zi = fr[k] * c, -(fr[k] * s)
                    else:
                        zr, zi = fr[k] * c + fi[k] * s, fi[k] * c - fr[k] * s
                z_scr[k, piece(0, r0), lanes] = zr.astype(BF16)
                z_scr[k, piece(n2, r0), lanes] = zi.astype(BF16)
        return carry

    lax.fori_loop(0, n_pieces, fwd_piece, 0)

    def slab(k):
        x = jnp.dot(m_ref[...], z_scr[k], preferred_element_type=F32)
        xr, xi = x[:n2], x[n2:]
        bins = pl.ds(pl.multiple_of(k * n2, n2), n2)
        kr, ki = kr_ref[bins, :], ki_ref[bins, :]
        y = jnp.concatenate([xr * kr - xi * ki, xr * ki + xi * kr], axis=0).astype(BF16)
        v_cur[k] = jnp.dot(mt_ref[...], y, preferred_element_type=F32)

    def inv_piece(i, carry):
        slab(i)
        r0 = i * rows
        for hh in range(w // LANES):
            lanes = slice(hh * LANES, (hh + 1) * LANES)
            br, bi = [], []
            for k in range(FFT_BINS):
                vr, vi = v_prev[k, piece(0, r0), lanes], v_prev[k, piece(n2, r0), lanes]
                if k == 0:
                    br.append(vr * (0.5 / l))
                    bi.append(None)
                else:
                    c, s = twci_ref[k, piece(0, r0), :], twsi_ref[k, piece(0, r0), :]
                    br.append(vr * c - vi * s)
                    bi.append(None if k == FFT_BINS - 1 else vr * s + vi * c)
            y = _inv16(br, bi)
            for n1 in range(FFT_RADIX // 2):
                rws = piece(n1 * n2, r0)
                o_ref[rws, lanes] = g_prev[rws, lanes] * (y[n1] + u_prev[rws, lanes] * bias_ref[:, lanes])
        return carry

    for i in range(n_pieces):
        inv_piece(i, 0)
    for k in range(n_pieces, FFT_BINS):
        slab(k)


def _hy_conv(sig, sig_col, hy, gate_col, cw, cb, spec, order, bias, tabs):
    b, l, _ = hy.shape
    w = GROUP_W
    n2 = 2 * l // FFT_RADIX
    rows = max(FFT_PIECE_ROWS, n2 // (FFT_BINS - 1))
    rows = min(rows, n2)
    kr, ki = spec
    sig_arr = hy if sig_col is not None else sig
    sig_blk = sig_col if sig_col is not None else 0
    seq = lambda col: pl.BlockSpec((None, l, w), lambda t: (jnp.minimum(t, b - 1), 0, col))
    spec_blk = pl.BlockSpec((FFT_BINS * n2, w), lambda t: (0, order), pipeline_mode=pl.Buffered(1))
    kernel = functools.partial(_hy_conv_kernel, sig_col=sig_col, gate_col=gate_col, rows=rows)
    stage = [pltpu.VMEM((l, w), F32), pltpu.VMEM((l, w), F32), pltpu.VMEM((FFT_BINS, 2 * n2, w), F32)]
    return pl.pallas_call(
        kernel,
        grid=(b + 1,),
        in_specs=[seq(sig_blk), seq(gate_col), _const_spec(cw), _const_spec(cb), spec_blk, spec_blk,
                  _const_spec(bias)] + [_const_spec(t) for t in tabs],
        out_specs=pl.BlockSpec((None, l, w), lambda t: (jnp.maximum(t - 1, 0), 0, 0)),
        out_shape=jax.ShapeDtypeStruct((b, l, w), F32),
        scratch_shapes=stage + stage + [pltpu.VMEM((FFT_BINS, 2 * n2, w), BF16)],
        compiler_params=_params("arbitrary"),
        name="hy_conv",
    )(sig_arr, hy, cw, cb, kr, ki, bias, *tabs)


@functools.lru_cache(maxsize=None)
def _fft_tables(l):
    n = 2 * l
    n2 = n // FFT_RADIX
    j = np.arange(n2, dtype=np.int64)
    k1 = np.arange(FFT_BINS, dtype=np.int64)
    f32 = lambda a: np.ascontiguousarray(a, dtype=np.float32)
    b16 = lambda a: f32(a).astype(BF16)
    ang2 = ((j[:, None] * j[None, :]) % n2).astype(np.float64) * (2.0 * math.pi / n2)
    c2, s2 = np.cos(ang2), np.sin(ang2)
    m = np.block([[c2, s2], [-s2, c2]])
    angt = (k1[:, None] * j[None, :]).astype(np.float64) * (2.0 * math.pi / n)
    wk = np.where((k1 == 0) | (k1 == FFT_BINS - 1), 1.0, 2.0)[:, None] / n
    lane = lambda a: f32(np.broadcast_to(a[:, :, None], a.shape + (LANES,)))
    bins = (k1[:, None] + FFT_RADIX * j[None, :]).reshape(-1)
    angf = ((bins[:, None] * np.arange(l, dtype=np.int64)[None, :]) % n).astype(np.float64) * (2.0 * math.pi / n)
    conv_tabs = (b16(m), b16(m.T), lane(np.cos(angt)), lane(np.sin(angt)),
                 lane(np.cos(angt) * wk), lane(np.sin(angt) * wk))
    def split(a):
        hi = b16(a)
        return hi, b16(a - hi.astype(np.float64))

    return split(np.cos(angf)) + split(np.sin(angf)), conv_tabs


@functools.lru_cache(maxsize=None)
def _hy_features(l):
    f = np.float32
    t = np.linspace(0.0, 1.0, l, dtype=f)[:, None]
    wpos = (f(2.0 * math.pi) * np.arange(l, dtype=f)[:, None] / f(l)).astype(f)
    fb = np.linspace(1e-4, HY_BANDS - 1, HY_BANDS, dtype=f)[None, :]
    arg = (fb * wpos).astype(f).astype(np.float64)
    z = np.concatenate([t, np.cos(arg), -np.sin(arg), np.zeros((l, HY_FFN - HY_EMB))], axis=-1).astype(f)
    deltas = np.linspace(math.log(HY_TARGET) / HY_FAST_PCT, math.log(HY_TARGET) / HY_SLOW_PCT, HY_NFILT, dtype=f)
    return z, t, np.abs(deltas)[None, :]


def _hyena(hy, feats, spec_tabs, conv_tabs, cw, cb, w1, b1, w2, b2, w3, b3, fr, w4, bias):
    l = hy.shape[1]
    z, t, dl = feats
    w1p = jnp.pad(w1, ((0, HY_FFN - HY_EMB), (0, 0)))
    h = _hy_filters(l, z, t, w1p, b1[None], w2, b2[None], w3, b3[None], fr[None], w4, dl)
    spec = _hy_spectrum(h, spec_tabs)
    cb2 = cb[None]
    zz = _hy_conv(None, 0, hy, 1, cw, cb2, spec, 0, bias[0:1], conv_tabs)
    return _hy_conv(zz, None, hy, 2, cw, cb2, spec, 1, bias[1:2], conv_tabs)


@functools.lru_cache(maxsize=None)
def _rope_tables(l, dim, width, off, reps):
    f = np.float32
    rows = np.repeat(np.arange(l // GRID_W), GRID_W).astype(f)
    cols = np.tile(np.arange(GRID_W), l // GRID_W).astype(f)
    quarter = dim // 4
    half = dim // 2
    inv = (f(ROPE_THETA) ** (-np.arange(quarter, dtype=f) / f(quarter))).astype(f)
    ang = np.concatenate([rows[:, None] * inv, cols[:, None] * inv], axis=-1).astype(f).astype(np.float64)
    c, s = np.cos(ang), np.sin(ang)
    cf = np.ones((l, width))
    s1 = np.zeros((l, width))
    s2 = np.zeros((l, width))
    cf[:, off:off + dim] = np.concatenate([c, c], axis=-1)
    s1[:, off:off + half] = -s
    s2[:, off + half:off + dim] = s
    return tuple(np.tile(a, (1, reps)).astype(f) for a in (cf, s1, s2))


def _attn_kernel(*refs, reps, n_src):
    per = 1 + 2 * n_src
    n_att = len(reps)
    tq = refs[0].shape[1]
    lo = lax.broadcasted_iota(jnp.int32, (tq, LANES), 1) < HEAD_D
    nt = (((1,), (1,)), ((), ()))
    n_heads = refs[0].shape[0]
    work = [(a, h) for a in range(n_att) for h in range(n_heads)]

    def scores(item):
        a, h = item
        q_ref, ks = refs[a * per], refs[a * per + 1:(a + 1) * per:2]
        return [lax.dot_general(q_ref[h], k_ref[h // reps[a]], nt, preferred_element_type=F32) for k_ref in ks]

    def weighted(item, ss):
        a, h = item
        vs = refs[a * per + 2:(a + 1) * per:2]
        m = ss[0].max(axis=-1, keepdims=True)
        for s in ss[1:]:
            m = jnp.maximum(m, s.max(axis=-1, keepdims=True))
        acc = 0.0
        for s, v_ref in zip(ss, vs):
            acc = acc + jnp.dot(jnp.exp2(s - m).astype(BF16), v_ref[h], preferred_element_type=F32)
        return acc / pltpu.roll(acc, HEAD_D, 1)

    outs = []
    ss_next = scores(work[0])
    for i, item in enumerate(work):
        ss = ss_next
        if i + 1 < len(work):
            ss_next = scores(work[i + 1])
        outs.append(weighted(item, ss))
    for a in range(n_att):
        o_ref = refs[n_att * per + a]
        for p in range(n_heads // 2):
            pair = outs[a * n_heads + 2 * p], outs[a * n_heads + 2 * p + 1]
            o_ref[:, p * LANES:(p + 1) * LANES] = jnp.where(lo, pair[0], pair[1])


def _attention(atts, b, l):
    tq = min(ATTN_TILE, l)
    nq = l // tq
    specs, args = [], []
    for q, kvs, _ in atts:
        specs.append(pl.BlockSpec((q.shape[0], tq, LANES), lambda i, j: (0, i * nq + j, 0)))
        args.append(q)
        for k, v, lk in kvs:
            specs += [pl.BlockSpec((k.shape[0], lk, LANES), lambda i, j: (0, i, 0)),
                      pl.BlockSpec((v.shape[0], lk, LANES), lambda i, j: (0, i, 0))]
            args += [k, v]
    out = jax.ShapeDtypeStruct((b, l, GROUP_W), F32)
    return pl.pallas_call(
        functools.partial(_attn_kernel, reps=tuple(rep for _, _, rep in atts), n_src=len(atts[0][1])),
        grid=(b, nq),
        in_specs=specs,
        out_specs=[pl.BlockSpec((None, tq, GROUP_W), lambda i, j: (i, j, 0))] * len(atts),
        out_shape=[out] * len(atts),
        compiler_params=_params("parallel", "parallel"),
        name="attention",
    )(*args)


def _ret_kernel(qs, ks, v_ref, g_ref, dl_ref, dc_ref, ng_ref, seg_ref, sf0_ref, sb0_ref,
                y_ref, sf_ref, sb_ref, sball, dec_scr, cdec_scr, dmat_scr, *, chunk, unroll):
    c_ = chunk
    w = GROUP_W
    n_chunks = qs.shape[0] // c_
    unroll = min(unroll, n_chunks)
    shift_c = int(math.log2(c_))
    shift_h = int(math.log2(HEAD_D))
    tn = (((0,), (0,)), ((), ()))
    nt = (((1,), (1,)), ((), ()))

    @pl.when(pl.program_id(0) == 0)
    def _():
        lgf = _log_sigmoid(dl_ref[0:1, :])
        lgb = _log_sigmoid(dl_ref[1:2, :])
        ii = lax.broadcasted_iota(jnp.int32, (c_, 1), 0).astype(F32)
        dec_scr[0] = jnp.exp((ii + 1.0) * lgf)
        dec_scr[1] = jnp.exp((c_ - 1.0 - ii) * lgf)
        dec_scr[2] = jnp.exp((c_ - ii) * lgb)
        dec_scr[3] = jnp.exp(ii * lgb)
        cdec_scr[0] = jnp.exp(float(c_) * lgf)
        cdec_scr[1] = jnp.exp(float(c_) * lgb)
        lgc = _log_sigmoid(dc_ref[...])
        i4 = jnp.bitwise_and(lax.broadcasted_iota(jnp.int32, (4 * c_, c_), 0), c_ - 1)
        j4 = lax.broadcasted_iota(jnp.int32, (4 * c_, c_), 1)
        diff = (i4 - j4).astype(F32)
        dmat_scr[...] = (jnp.where(diff >= 0, jnp.exp(lgc[:, 0:1] * jnp.maximum(diff, 0.0)), 0.0)
                         + jnp.where(diff <= 0, jnp.exp(lgc[:, 1:2] * jnp.maximum(-diff, 0.0)), 0.0))

    qdf, kdf, qdb, kdb = dec_scr[0], dec_scr[1], dec_scr[2], dec_scr[3]
    cdf, cdb = cdec_scr[0], cdec_scr[1]
    dmat = dmat_scr[...]
    r_h = lax.shift_right_logical(lax.broadcasted_iota(jnp.int32, (w, w), 0), shift_h)
    c_h = lax.shift_right_logical(lax.broadcasted_iota(jnp.int32, (w, w), 1), shift_h)
    blockdiag = r_h == c_h
    row_h = lax.shift_right_logical(lax.broadcasted_iota(jnp.int32, (4 * c_, w), 0), shift_c)
    lane_h = lax.shift_right_logical(lax.broadcasted_iota(jnp.int32, (4 * c_, w), 1), shift_h)
    headmask = row_h == lane_h
    seg = seg_ref[...]

    def rows(c):
        return pl.ds(pl.multiple_of(c * c_, c_), c_)

    def kv_outer(k, v, dec):
        a = lax.dot_general((k * dec).astype(BF16), v.astype(BF16), tn, preferred_element_type=F32)
        return jnp.where(blockdiag, a, 0.0)

    def bwd(t, s):
        c = n_chunks - 1 - t
        sball[c] = s
        sl = rows(c)
        return s * cdb + kv_outer(ks[sl, :], v_ref[sl, :], kdb)

    sb_ref[...] = lax.fori_loop(0, n_chunks, bwd, sb0_ref[...], unroll=unroll)

    def fwd(c, s):
        sl = rows(c)
        q, k, v = qs[sl, :], ks[sl, :], v_ref[sl, :]
        vb = v.astype(BF16)
        q4 = jnp.where(headmask, jnp.concatenate([q, q, q, q], axis=0), 0.0).astype(BF16)
        sc = lax.dot_general(q4, k.astype(BF16), nt, preferred_element_type=F32)
        o4 = jnp.dot((sc * dmat).astype(BF16), vb, preferred_element_type=F32)
        o4 = jnp.where(headmask, o4, 0.0)
        o = o4[0:c_] + o4[c_:2 * c_] + o4[2 * c_:3 * c_] + o4[3 * c_:4 * c_]
        o = o + jnp.dot((q * qdf).astype(BF16), s.astype(BF16), preferred_element_type=F32)
        o = o + jnp.dot((q * qdb).astype(BF16), sball[c].astype(BF16), preferred_element_type=F32)
        on = o * lax.rsqrt(_seg_mean(o * o, seg) + EPS) * ng_ref[...]
        g = g_ref[sl, :]
        y_ref[sl, :] = on * (g * jax.nn.sigmoid(g))
        return s * cdf + kv_outer(k, v, kdf)

    sf_ref[...] = lax.fori_loop(0, n_chunks, fwd, sf0_ref[...], unroll=unroll)


def _retention(ret, decay, ng, seg, sf0, sb0):
    b, l, _ = ret.shape
    w = GROUP_W
    chunk = min(RET_CHUNK, l)
    dl = jnp.repeat(decay, HEAD_D, axis=1)
    dc = jnp.repeat(decay.T, chunk, axis=0)
    col = lambda j: pl.BlockSpec((None, l, w), lambda i: (i, 0, j))
    state = pl.BlockSpec((None, w, w), lambda i: (i, 0, 0))
    sshape = jax.ShapeDtypeStruct((b, w, w), F32)
    return pl.pallas_call(
        functools.partial(_ret_kernel, chunk=chunk, unroll=RET_UNROLL),
        grid=(b,),
        in_specs=[col(0), col(1), col(2), col(3), _const_spec(dl), _const_spec(dc), _const_spec(ng),
                  _const_spec(seg), state, state],
        out_specs=[pl.BlockSpec((None, l, w), lambda i: (i, 0, 0)), state, state],
        out_shape=[jax.ShapeDtypeStruct((b, l, w), F32), sshape, sshape],
        scratch_shapes=[pltpu.VMEM((l // chunk, w, w), F32), pltpu.VMEM((4, chunk, w), F32),
                        pltpu.VMEM((2, 1, w), F32), pltpu.VMEM((4 * chunk, chunk), F32)],
        compiler_params=_params("arbitrary"),
        name="retention",
    )(ret, ret, ret, ret, dl, dc, ng, seg, sf0, sb0)


def _mlp_kernel(x_ref, ya_ref, yb_ref, yc_ref, yd_ref, gout_ref, ga_ref, wout_ref,
                gffn_ref, sh_ref, sc_ref, gf_ref, wfi_ref, wfo_ref, fg_ref, o_ref,
                x1_a, h_a, x1_b, h_b, *, final_norm):
    t = pl.program_id(0)
    nff = D_FF // FF_TILE

    @pl.when(t == 0)
    def _():
        x1_b[...] = jnp.zeros_like(x1_b)
        h_b[...] = jnp.zeros_like(h_b)

    def step(x1_w, h_w, x1_r, h_r):
        acc = 0.0
        for i, y_ref in enumerate((ya_ref, yb_ref, yc_ref, yd_ref)):
            rows = slice(i * GROUP_W, (i + 1) * GROUP_W)
            yn = (_rms(y_ref[...]) * gout_ref[:, rows]).astype(BF16)
            acc = acc + jnp.dot(yn, wout_ref[rows, :], preferred_element_type=F32)
        x1 = x_ref[...] + ga_ref[...] * acc
        x1_w[...] = x1
        h_w[...] = (_rms(x1) * gffn_ref[...] * (1.0 + sc_ref[...]) + sh_ref[...]).astype(BF16)

        h = h_r[...]
        ff = 0.0
        for c in range(nff):
            cols = slice(c * FF_TILE, (c + 1) * FF_TILE)
            gate_cols = slice(D_FF + c * FF_TILE, D_FF + (c + 1) * FF_TILE)
            a = jnp.dot(h, wfi_ref[:, cols], preferred_element_type=F32)
            g = jnp.dot(h, wfi_ref[:, gate_cols], preferred_element_type=F32)
            u = (a * jax.nn.sigmoid(a) * g).astype(BF16)
            ff = ff + jnp.dot(u, wfo_ref[cols, :], preferred_element_type=F32)
        out = x1_r[...] + gf_ref[...] * ff
        if final_norm:
            out = _rms(out) * fg_ref[...]
        o_ref[...] = out

    @pl.when(t % 2 == 0)
    def _():
        step(x1_a, h_a, x1_b, h_b)

    @pl.when(t % 2 == 1)
    def _():
        step(x1_b, h_b, x1_a, h_a)


def _mlp(x, ys, g_out, g_a, w_out, g_ffn, sh, sc, g_f, wfi, wfo, final_g, final_norm):
    b, l, d = x.shape
    tm = min(MLP_TILE, l)
    nt = l // tm
    n = b * nt
    cur = lambda t: jnp.minimum(t, n - 1)
    prv = lambda t: jnp.maximum(t - 1, 0)
    tok = lambda wd: pl.BlockSpec((None, tm, wd), lambda t: (cur(t) // nt, cur(t) % nt, 0))
    vec = pl.BlockSpec((None, 1, d), lambda t: (cur(t) // nt, 0, 0))
    return pl.pallas_call(
        functools.partial(_mlp_kernel, final_norm=final_norm),
        grid=(n + 1,),
        in_specs=[tok(d)] + [tok(GROUP_W)] * 4 + [
            _const_spec(g_out), vec, _const_spec(w_out), _const_spec(g_ffn), vec, vec,
            pl.BlockSpec((None, 1, d), lambda t: (prv(t) // nt, 0, 0)),
            _const_spec(wfi), _const_spec(wfo), _const_spec(final_g)],
        out_specs=pl.BlockSpec((None, tm, d), lambda t: (prv(t) // nt, prv(t) % nt, 0)),
        out_shape=jax.ShapeDtypeStruct((b, l, d), F32),
        scratch_shapes=[pltpu.VMEM((tm, d), F32), pltpu.VMEM((tm, d), BF16),
                        pltpu.VMEM((tm, d), F32), pltpu.VMEM((tm, d), BF16)],
        compiler_params=_params("arbitrary"),
        name="mlp",
    )(x, *ys, g_out, g_a, w_out, g_ffn, sh, sc, g_f, wfi, wfo, final_g)


def _pad_proj(w):
    end = PROJ_OFFSETS[2] + MLA_KV_RANK + MLA_ROPE
    z = jnp.zeros((w.shape[0], LANES - MLA_ROPE), w.dtype)
    return jnp.concatenate([w[:, :end], z, w[:, end:]], axis=1).astype(BF16)


def _mla_weights(wuq, wukv):
    r = wuq.shape[0]
    q = wuq.reshape(r, 4, MLA_NOPE + MLA_ROPE)
    q = jnp.pad(q, ((0, 0), (0, 0), (0, LANES - MLA_NOPE - MLA_ROPE))).reshape(r, 4 * LANES)
    kv = wukv.reshape(wukv.shape[0], 4, MLA_NOPE + MLA_V)
    k = jnp.pad(kv[..., :MLA_NOPE], ((0, 0), (0, 0), (0, LANES - MLA_NOPE))).reshape(-1, 4 * LANES)
    v = jnp.concatenate([kv[..., MLA_NOPE:], kv[..., MLA_NOPE:]], axis=-1).reshape(-1, 4 * LANES)
    return q.astype(BF16), k.astype(BF16), v.astype(BF16)


def _seg_matrix(width):
    i = jnp.arange(width) // HEAD_D
    return jnp.where(i[:, None] == i[None, :], 1.0 / HEAD_D, 0.0).astype(BF16)


def kernel(x, c, ctx, c_ctx, w_mod, b_mod, norm_attn_g, norm_ffn_g, w_in, hy_conv_w, hy_conv_b, hy_w1, hy_b1, hy_w2, hy_b2, hy_w3, hy_b3, hy_freq, hy_w4, hy_bias, mla_q_norm_g, mla_w_uq, mla_kv_norm_g, mla_w_ukv, ret_decay, ret_norm_g, gqa_q_norm_g, gqa_k_norm_g, out_norm_g, w_out, w_ffn_in, w_ffn_out, final_norm_g):
    b, n_lat, d = x.shape
    n_ctx = ctx.shape[1]
    depth = w_mod.shape[0]

    rope_mla = _rope_tables(n_lat, MLA_ROPE, LANES, MLA_NOPE, 1)
    rope_hd2 = _rope_tables(n_lat, HEAD_D, HEAD_D, 0, 2)
    seg2, seg4 = _seg_matrix(LANES), _seg_matrix(GROUP_W)
    hy_consts = {n: (_hy_features(n),) + _fft_tables(n) for n in {n_lat, n_ctx}}
    zero_state = jnp.zeros((b, GROUP_W, GROUP_W), F32)

    rows = -(-(b + 1) // 8) * 8
    cond = jnp.zeros((rows, d), F32).at[:b].set(c).at[b].set(c_ctx)
    mod = _ada_mod(cond, w_mod, b_mod)

    xl, xc = x, ctx.reshape(1, b * n_ctx, d)
    for l in range(depth):
        update_ctx = l < depth - 1
        m_lat = [m[:b, None, :] for m in jnp.split(mod[l], 6, axis=-1)]
        m_ctx = [m[b][None, None, :] for m in jnp.split(mod[l], 6, axis=-1)]
        wp = _pad_proj(w_in[l])
        g_attn = norm_attn_g[l][None]
        mla_w = (mla_q_norm_g[l][None], mla_kv_norm_g[l][None]) + _mla_weights(mla_w_uq[l], mla_w_ukv[l])
        gqa_w = (jnp.tile(gqa_q_norm_g[l], 2)[None], jnp.tile(gqa_k_norm_g[l], 2)[None], seg2)
        hy_l, ret_l, q_l, k_l, v_l, gq_l, gk_l, gv_l = _front(
            xl, g_attn, m_lat[0], m_lat[1], wp, mla_w, gqa_w, (rope_mla, rope_hd2))
        hy_c, ret_c, q_c, k_c, v_c, gq_c, gk_c, gv_c = _front(
            xc, g_attn, m_ctx[0], m_ctx[1], wp, mla_w, gqa_w, None)
        hy_c = hy_c.reshape(b, n_ctx, -1)
        ret_c = ret_c.reshape(b, n_ctx, -1)

        hy_args = (hy_conv_w[l], hy_conv_b[l], hy_w1[l], hy_b1[l], hy_w2[l], hy_b2[l], hy_w3[l], hy_b3[l],
                   hy_freq[l], hy_w4[l], hy_bias[l])
        ya_l = _hyena(hy_l, *hy_consts[n_lat], *hy_args)
        yb_l, yd_l = _attention([(q_l, [(k_c, v_c, n_ctx), (k_l, v_l, n_lat)], 1),
                                 (gq_l, [(gk_c, gv_c, n_ctx), (gk_l, gv_l, n_lat)], 2)], b, n_lat)

        ng = ret_norm_g[l][None]
        yc_c, s_f, s_b = _retention(ret_c, ret_decay[l], ng, seg4, zero_state, zero_state)
        yc_l, _, _ = _retention(ret_l, ret_decay[l], ng, seg4, s_f, s_b)

        wo = w_out[l].astype(BF16)
        wfi, wfo = w_ffn_in[l].astype(BF16), w_ffn_out[l].astype(BF16)
        g_out, g_ffn, g_fin = out_norm_g[l][None], norm_ffn_g[l][None], final_norm_g[None]
        xl = _mlp(xl, (ya_l, yb_l, yc_l, yd_l), g_out, m_lat[2], wo, g_ffn, m_lat[3], m_lat[4], m_lat[5],
                  wfi, wfo, g_fin, l == depth - 1)

        if update_ctx:
            flat = lambda a: a.reshape(1, b * n_ctx, GROUP_W)
            ya_c = _hyena(hy_c, *hy_consts[n_ctx], *hy_args)
            yb_c, = _attention([(q_c, [(k_c, v_c, n_ctx)], 1)], b, n_ctx)
            yd_c, = _attention([(gq_c, [(gk_c, gv_c, n_ctx)], 2)], b, n_ctx)
            xc = _mlp(xc, tuple(flat(a) for a in (ya_c, yb_c, yc_c, yd_c)), g_out, m_ctx[2], wo,
                      g_ffn, m_ctx[3], m_ctx[4], m_ctx[5], wfi, wfo, g_fin, False)

    return xl
```
